```python
import jax, jax.numpy as jnp
from jax import lax
import numpy as np

D_MODEL = 1024
BATCH = 4
SEQ = 8192
DEPTH = 1
DEC_BATCH = 128
DEC_SEQ = 8
PAST_LEN = 8192
PAGE_SIZE = 128

HEAD_DIM_A = 64
HEADS_PER_GROUP = 4
SWA_GROUPS = ((128, 1), (512, 4), (2048, 16))
N_GROUPS = len(SWA_GROUPS)
N_HEADS_A = HEADS_PER_GROUP * N_GROUPS
WIDTH_A = N_HEADS_A * HEAD_DIM_A
OUT_A = HEADS_PER_GROUP * HEAD_DIM_A
SWA_BLOCK = 128
ROPE_DIM = HEAD_DIM_A // 4
ROPE_THETA = 500000.0
CONV_CH = 512
CONV_WIDTH = 31
MEM_LEN = 256
MEM_HEADS = 4
MEM_HEAD_DIM = 128
MEM_WIDTH = MEM_HEADS * MEM_HEAD_DIM
N_BRANCH = 3
N_EXPERTS = 32
TOP_K = 4
D_FF = D_MODEL
SWIGLU_LIMIT = 7.0
SWIGLU_ALPHA = 1.702
MOE_BLOCK = 128
EPS = 1e-6

IN_SPLITS = (WIDTH_A, 2 * WIDTH_A, 3 * WIDTH_A, 3 * WIDTH_A + 2 * CONV_CH,
             3 * WIDTH_A + 2 * CONV_CH + MEM_WIDTH)
IN_COLS = 3 * WIDTH_A + 2 * CONV_CH + MEM_WIDTH + N_BRANCH * D_MODEL
F32 = jnp.float32

kernel_name = 'griffin_gated_dilated_swa_conformer_memxattn_moe_step'


def rms_norm(x, g):
    xf = x.astype(F32)
    y = xf * lax.rsqrt(jnp.mean(xf * xf, axis=-1, keepdims=True) + EPS)
    return (y * g.astype(F32)).astype(x.dtype)


def layer_norm(x, g, b):
    xf = x.astype(F32)
    xc = xf - jnp.mean(xf, axis=-1, keepdims=True)
    var = jnp.mean(xc * xc, axis=-1, keepdims=True)
    return (xc * lax.rsqrt(var + EPS) * g.astype(F32) + b.astype(F32)).astype(x.dtype)


def partial_rope(x, positions):
    half = ROPE_DIM // 2
    inv = jnp.power(jnp.float32(ROPE_THETA), -jnp.arange(half, dtype=F32) / half)
    ang = positions.astype(F32)[:, None] * inv[None, :]
    cos, sin = jnp.cos(ang)[:, None, :], jnp.sin(ang)[:, None, :]
    xf = x.astype(F32)
    x1, x2, rest = xf[..., :half], xf[..., half:ROPE_DIM], xf[..., ROPE_DIM:]
    out = jnp.concatenate([x1 * cos - x2 * sin, x1 * sin + x2 * cos, rest], axis=-1)
    return out.astype(x.dtype)


def _to_residue_blocks(t, dilation, s_pad):
    n, s = t.shape[:2]
    rest = t.shape[2:]
    t = jnp.pad(t, ((0, 0), (0, s_pad - s)) + ((0, 0),) * len(rest))
    L = s_pad // dilation
    t = jnp.moveaxis(t.reshape((n, L, dilation) + rest), 2, 1)
    return t.reshape((n, dilation, L // SWA_BLOCK, SWA_BLOCK) + rest)


def _from_residue_blocks(t, s):
    n, d, nb, blk = t.shape[:4]
    rest = t.shape[4:]
    t = jnp.moveaxis(t.reshape((n, d, nb * blk) + rest), 1, 2)
    return t.reshape((n, d * nb * blk) + rest)[:, :s]


def dilated_attn_prompt(q, k, v, window, dilation):
    n, s, h, dh = q.shape
    n_back = window // dilation
    unit = dilation * SWA_BLOCK
    s_pad = -(-s // unit) * unit
    qr, kr, vr = (_to_residue_blocks(t, dilation, s_pad) for t in (q, k, v))
    nb = qr.shape[2]

    def with_prev(t):
        prev = jnp.pad(t, ((0, 0), (0, 0), (1, 0), (0, 0), (0, 0), (0, 0)))[:, :, :-1]
        return jnp.concatenate([prev, t], axis=3)

    kk, vv = with_prev(kr), with_prev(vr)
    sc = jnp.einsum('bdnqhe,bdnkhe->bdnhqk', qr, kk, preferred_element_type=F32) * (dh ** -0.5)
    qi = jnp.arange(SWA_BLOCK)[:, None] + SWA_BLOCK
    kj = jnp.arange(2 * SWA_BLOCK)[None, :]
    off = qi - kj
    band = (off >= 0) & (off <= n_back)
    no_prev = (jnp.arange(nb) == 0)[:, None, None] & (kj < SWA_BLOCK)[None]
    mask = band[None] & ~no_prev
    sc = jnp.where(mask[None, None, :, None], sc, -jnp.inf)
    m = jnp.max(sc, axis=-1, keepdims=True)
    p = jnp.exp(sc - m)
    l = jnp.sum(p, axis=-1, keepdims=True)
    o = jnp.einsum('bdnhqk,bdnkhe->bdnqhe', p.astype(v.dtype), vv, preferred_element_type=F32)
    o = o / jnp.swapaxes(l, 3, 4)
    lse = jnp.swapaxes((m + jnp.log(l))[..., 0], 3, 4)
    return _from_residue_blocks(o, s), _from_residue_blocks(lse, s)


def dilated_attn_sample(q, k_new, v_new, k_buf, v_buf, window, dilation, positions):
    t = q.shape[1]
    dh = q.shape[-1]
    kk = jnp.concatenate([k_buf.astype(k_new.dtype), k_new], axis=1)
    vv = jnp.concatenate([v_buf.astype(v_new.dtype), v_new], axis=1)
    j = jnp.arange(window // dilation + 1)
    idx = (window + jnp.arange(t))[:, None] - dilation * j[None, :]
    valid = (positions[:, None] - dilation * j[None, :]) >= 0
    kg, vg = kk[:, idx], vv[:, idx]
    sc = jnp.einsum('bthe,btjhe->bthj', q, kg, preferred_element_type=F32) * (dh ** -0.5)
    sc = jnp.where(valid[None, :, None, :], sc, -jnp.inf)
    m = jnp.max(sc, axis=-1, keepdims=True)
    p = jnp.exp(sc - m)
    l = jnp.sum(p, axis=-1, keepdims=True)
    o = jnp.einsum('bthj,btjhe->bthe', p.astype(vg.dtype), vg, preferred_element_type=F32) / l
    lse = (m + jnp.log(l))[..., 0]
    return o, lse, kk[:, -window:], vv[:, -window:]


def causal_depthwise_conv(ctx, conv_w, conv_b):
    out = lax.conv_general_dilated(ctx, conv_w[:, None, :].astype(ctx.dtype), window_strides=(1,),
                                   padding='VALID', dimension_numbers=('NWC', 'WIO', 'NWC'),
                                   feature_group_count=CONV_CH)
    return out + conv_b.astype(out.dtype)


def memory_kv(mem, mem_norm_g, w_mem_kv, kn_m):
    n = mem.shape[0]
    kv = jnp.dot(rms_norm(mem, mem_norm_g), w_mem_kv)
    k, v = jnp.split(kv, 2, axis=-1)
    k = rms_norm(k.reshape(n, MEM_LEN, MEM_HEADS, MEM_HEAD_DIM), kn_m)
    return k, v.reshape(n, MEM_LEN, MEM_HEADS, MEM_HEAD_DIM)


def moe_ffn(h, w_router, b_router, w_gate, b_gate, w_up, b_up, w_down, b_down):
    lead = h.shape[:-1]
    xt = h.reshape(-1, D_MODEL)
    n = xt.shape[0]
    nk = n * TOP_K
    logits = jnp.dot(xt, w_router, preferred_element_type=F32) + b_router.astype(F32)
    top_val, top_idx = lax.top_k(logits, TOP_K)
    gate_w = jax.nn.softmax(top_val, axis=-1)
    e_flat = top_idx.reshape(-1).astype(jnp.int32)
    order = jnp.argsort(e_flat * nk + jnp.arange(nk, dtype=jnp.int32))
    e_sorted = e_flat[order]
    tok_sorted = (order // TOP_K).astype(jnp.int32)
    counts = jnp.zeros((N_EXPERTS,), jnp.int32).at[e_flat].add(1)
    padded = (counts + MOE_BLOCK - 1) // MOE_BLOCK * MOE_BLOCK
    starts = jnp.cumsum(counts) - counts
    pstarts = jnp.cumsum(padded) - padded
    pends = pstarts + padded
    dest_sorted = pstarts[e_sorted] + (jnp.arange(nk, dtype=jnp.int32) - starts[e_sorted])
    n_blocks = -(-nk // MOE_BLOCK) + N_EXPERTS
    n_slots = n_blocks * MOE_BLOCK
    slot_tok = jnp.full((n_slots,), n, jnp.int32).at[dest_sorted].set(tok_sorted)
    block_exp = jnp.minimum(jnp.searchsorted(pends, jnp.arange(n_blocks, dtype=jnp.int32) * MOE_BLOCK,
                                             side='right'), N_EXPERTS - 1)
    x_pad = jnp.concatenate([xt, jnp.zeros((1, D_MODEL), xt.dtype)], axis=0)
    xb = x_pad[slot_tok].reshape(n_blocks, MOE_BLOCK, D_MODEL)

    def expert_block(args):
        xblk, e = args
        g = jnp.dot(xblk, w_gate[e]) + b_gate[e]
        u = jnp.dot(xblk, w_up[e]) + b_up[e]
        g = jnp.minimum(g, SWIGLU_LIMIT)
        u = jnp.clip(u, -SWIGLU_LIMIT, SWIGLU_LIMIT)
        act = g * jax.nn.sigmoid(SWIGLU_ALPHA * g) * (u + 1.0)
        return jnp.dot(act, w_down[e]) + b_down[e]

    yb = lax.map(expert_block, (xb, block_exp)).reshape(n_slots, D_MODEL)
    dest = jnp.zeros((nk,), jnp.int32).at[order].set(dest_sorted).reshape(n, TOP_K)
    y = jnp.einsum('nk,nkd->nd', gate_w.astype(yb.dtype), yb[dest])
    return y.reshape(lead + (D_MODEL,))


def hybrid_layer(x, positions, mem_k, mem_v, swa_bufs, conv_state,
                 norm1_g, w_in, b_gates, qn_a, kn_a, conv_w, conv_b, cn_g, cn_b, qn_m,
                 w_proj_a, w_proj_b, w_proj_m, w_out, norm2_g, w_router, b_router,
                 w_gate, b_gate, w_up, b_up, w_down, b_down):
    n, t, _ = x.shape
    h = rms_norm(x, norm1_g)
    proj = jnp.dot(h, w_in)
    q_a, k_a, v_a, glu_in, q_m, gate_in = jnp.split(proj, IN_SPLITS, axis=-1)
    gates = jax.nn.sigmoid((gate_in + b_gates).astype(F32)).astype(x.dtype)
    g_a, g_b, g_m = jnp.split(gates, N_BRANCH, axis=-1)

    q_a = partial_rope(rms_norm(q_a.reshape(n, t, N_HEADS_A, HEAD_DIM_A), qn_a), positions)
    k_a = partial_rope(rms_norm(k_a.reshape(n, t, N_HEADS_A, HEAD_DIM_A), kn_a), positions)
    v_a = v_a.reshape(n, t, N_HEADS_A, HEAD_DIM_A)
    outs, lses, new_swa = [], [], []
    for gi, (win, dil) in enumerate(SWA_GROUPS):
        sl = slice(gi * HEADS_PER_GROUP, (gi + 1) * HEADS_PER_GROUP)
        qg, kg, vg = q_a[:, :, sl], k_a[:, :, sl], v_a[:, :, sl]
        if swa_bufs is None:
            o, lse = dilated_attn_prompt(qg, kg, vg, win, dil)
            front = max(win - t, 0)
            pad = ((0, 0), (front, 0), (0, 0), (0, 0))
            new_swa += [jnp.pad(kg, pad)[:, -win:], jnp.pad(vg, pad)[:, -win:]]
        else:
            o, lse, nk_buf, nv_buf = dilated_attn_sample(qg, kg, vg, swa_bufs[2 * gi], swa_bufs[2 * gi + 1],
                                                        win, dil, positions)
            new_swa += [nk_buf, nv_buf]
        outs.append(o)
        lses.append(lse)
    w_grp = jax.nn.softmax(jnp.stack(lses, axis=0), axis=0)
    o_a = jnp.sum(w_grp[..., None] * jnp.stack(outs, axis=0), axis=0)
    o_a = o_a.astype(x.dtype).reshape(n, t, OUT_A)

    a_in, a_gate = jnp.split(glu_in, 2, axis=-1)
    u = a_in * jax.nn.sigmoid(a_gate)
    if conv_state is None:
        ctx = jnp.pad(u, ((0, 0), (CONV_WIDTH - 1, 0), (0, 0)))
    else:
        ctx = jnp.concatenate([conv_state.astype(u.dtype), u], axis=1)
    new_conv = ctx[:, -(CONV_WIDTH - 1):]
    o_b = jax.nn.silu(layer_norm(causal_depthwise_conv(ctx, conv_w, conv_b), cn_g, cn_b))

    qm = rms_norm(q_m.reshape(n, t, MEM_HEADS, MEM_HEAD_DIM), qn_m)
    sm = jnp.einsum('bthe,bmhe->bhtm', qm, mem_k.astype(qm.dtype), preferred_element_type=F32) * (MEM_HEAD_DIM ** -0.5)
    pm = jax.nn.softmax(sm, axis=-1)
    o_m = jnp.einsum('bhtm,bmhe->bthe', pm.astype(x.dtype), mem_v.astype(x.dtype)).reshape(n, t, MEM_WIDTH)

    mixed = (g_a * jnp.dot(o_a, w_proj_a) + g_b * jnp.dot(o_b, w_proj_b) + g_m * jnp.dot(o_m, w_proj_m))
    x = x + jnp.dot(mixed, w_out)
    x = x + moe_ffn(rms_norm(x, norm2_g), w_router, b_router, w_gate, b_gate, w_up, b_up, w_down, b_down)
    return x, new_swa, new_conv


def setup_inputs(seed: int = 0) -> dict:
    key = jax.random.key(seed)
    ks = iter(jax.random.split(key, 64))

    def nrm(shape, scale):
        return scale * jax.random.normal(next(ks), shape, F32)

    def gain(shape):
        return 1.0 + 0.02 * jax.random.normal(next(ks), shape, F32)

    L = DEPTH

    def swa(w):
        return (L, DEC_BATCH, w, HEADS_PER_GROUP, HEAD_DIM_A)

    mem_shape = (L, DEC_BATCH, MEM_LEN, MEM_HEADS, MEM_HEAD_DIM)
    return {
        'x_prompt': nrm((BATCH, SEQ, D_MODEL), 1.0),
        'x_sample': nrm((DEC_BATCH, DEC_SEQ, D_MODEL), 1.0),
        'mem_prompt': nrm((BATCH, MEM_LEN, D_MODEL), 1.0),
        'cache_swa_k_w128': nrm(swa(128), 1.0),
        'cache_swa_v_w128': nrm(swa(128), 1.0),
        'cache_swa_k_w512': nrm(swa(512), 1.0),
        'cache_swa_v_w512': nrm(swa(512), 1.0),
        'cache_swa_k_w2048': nrm(swa(2048), 1.0),
        'cache_swa_v_w2048': nrm(swa(2048), 1.0),
        'state_conv': nrm((L, DEC_BATCH, CONV_WIDTH - 1, CONV_CH), 0.5),
        'cache_mem_k': nrm(mem_shape, 1.0),
        'cache_mem_v': nrm(mem_shape, 1.0),
        'norm1_g': gain((L, D_MODEL)),
        'w_in': nrm((L, D_MODEL, IN_COLS), D_MODEL ** -0.5),
        'b_gates': nrm((L, N_BRANCH * D_MODEL), 0.02),
        'qn_a': gain((L, HEAD_DIM_A)),
        'kn_a': gain((L, HEAD_DIM_A)),
        'conv_w': nrm((L, CONV_WIDTH, CONV_CH), CONV_WIDTH ** -0.5),
        'conv_b': nrm((L, CONV_CH), 0.02),
        'cn_g': gain((L, CONV_CH)),
        'cn_b': nrm((L, CONV_CH), 0.02),
        'mem_norm_g': gain((L, D_MODEL)),
        'w_mem_kv': nrm((L, D_MODEL, 2 * MEM_WIDTH), D_MODEL ** -0.5),
        'qn_m': gain((L, MEM_HEAD_DIM)),
        'kn_m': gain((L, MEM_HEAD_DIM)),
        'w_proj_a': nrm((L, OUT_A, D_MODEL), OUT_A ** -0.5),
        'w_proj_b': nrm((L, CONV_CH, D_MODEL), CONV_CH ** -0.5),
        'w_proj_m': nrm((L, MEM_WIDTH, D_MODEL), MEM_WIDTH ** -0.5),
        'w_out': nrm((L, D_MODEL, D_MODEL), D_MODEL ** -0.5),
        'norm2_g': gain((L, D_MODEL)),
        'w_router': nrm((L, D_MODEL, N_EXPERTS), D_MODEL ** -0.5),
        'b_router': nrm((L, N_EXPERTS), 0.01),
        'w_gate': nrm((L, N_EXPERTS, D_MODEL, D_FF), D_MODEL ** -0.5),
        'b_gate': nrm((L, N_EXPERTS, D_FF), 0.02),
        'w_up': nrm((L, N_EXPERTS, D_MODEL, D_FF), D_MODEL ** -0.5),
        'b_up': nrm((L, N_EXPERTS, D_FF), 0.02),
        'w_down': nrm((L, N_EXPERTS, D_FF, D_MODEL), D_FF ** -0.5),
        'b_down': nrm((L, N_EXPERTS, D_MODEL), 0.02),
    }


def reference(x_prompt, x_sample, mem_prompt, cache_swa_k_w128, cache_swa_v_w128, cache_swa_k_w512,
              cache_swa_v_w512, cache_swa_k_w2048, cache_swa_v_w2048, state_conv, cache_mem_k, cache_mem_v,
              norm1_g, w_in, b_gates, qn_a, kn_a, conv_w, conv_b, cn_g, cn_b, mem_norm_g, w_mem_kv, qn_m, kn_m,
              w_proj_a, w_proj_b, w_proj_m, w_out, norm2_g, w_router, b_router,
              w_gate, b_gate, w_up, b_up, w_down, b_down):
    pos_p = jnp.arange(SEQ, dtype=jnp.int32)
    pos_s = PAST_LEN + jnp.arange(DEC_SEQ, dtype=jnp.int32)
    y_p, y_s = x_prompt, x_sample
    st_p, st_s = [], []
    for l in range(DEPTH):
        lw = (norm1_g[l], w_in[l], b_gates[l], qn_a[l], kn_a[l], conv_w[l], conv_b[l], cn_g[l], cn_b[l], qn_m[l],
              w_proj_a[l], w_proj_b[l], w_proj_m[l], w_out[l], norm2_g[l], w_router[l], b_router[l],
              w_gate[l], b_gate[l], w_up[l], b_up[l], w_down[l], b_down[l])
        mk_p, mv_p = memory_kv(mem_prompt, mem_norm_g[l], w_mem_kv[l], kn_m[l])
        y_p, swa_p, conv_p = hybrid_layer(y_p, pos_p, mk_p, mv_p, None, None, *lw)
        bufs = (cache_swa_k_w128[l], cache_swa_v_w128[l], cache_swa_k_w512[l], cache_swa_v_w512[l],
                cache_swa_k_w2048[l], cache_swa_v_w2048[l])
        y_s, swa_s, conv_s = hybrid_layer(y_s, pos_s, cache_mem_k[l], cache_mem_v[l], bufs, state_conv[l], *lw)
        st_p.append((*swa_p, conv_p, mk_p, mv_p))
        st_s.append((*swa_s, conv_s))
    p_k128, p_v128, p_k512, p_v512, p_k2048, p_v2048, p_conv, p_mem_k, p_mem_v = [jnp.stack(a) for a in zip(*st_p)]
    s_k128, s_v128, s_k512, s_v512, s_k2048, s_v2048, s_conv = [jnp.stack(a) for a in zip(*st_s)]
    return (y_p, y_s, p_k128, p_v128, p_k512, p_v512, p_k2048, p_v2048, p_conv, p_mem_k, p_mem_v,
            s_k128, s_v128, s_k512, s_v512, s_k2048, s_v2048, s_conv)
```

```python
import functools

import jax
import jax.numpy as jnp
from jax import lax
from jax.experimental import pallas as pl
from jax.experimental.pallas import tpu as pltpu

F32 = jnp.float32
BF16 = jnp.bfloat16
I32 = jnp.int32

D_MODEL = 1024
HEAD_DIM_A = 64
GROUP_W = 256
N_GROUPS = 3
WIDTH_A = N_GROUPS * GROUP_W
SWA_GROUPS = ((128, 1), (512, 4), (2048, 16))
SWA_BLOCK = 128
ROPE_DIM = 16
ROPE_THETA = 500000.0
CONV_CH = 512
CONV_WIDTH = 31
CONV_HALO = 32
MEM_LEN = 256
MEM_HEADS = 4
MEM_HEAD_DIM = 128
MEM_WIDTH = 512
N_EXPERTS = 32
TOP_K = 4
D_FF = 1024
SWIGLU_LIMIT = 7.0
SWIGLU_ALPHA = 1.702
EPS = 1e-6
PAST_LEN = 8192
LANES = 128
W1_COLS = 3 * WIDTH_A + 2 * CONV_CH + MEM_WIDTH
NEG_INF = float("-inf")
MIB = 2 ** 20


def _params(semantics, vmem_mib):
    return pltpu.CompilerParams(dimension_semantics=semantics, vmem_limit_bytes=vmem_mib * MIB)


def _rms(x, gain):
    return x * lax.rsqrt(jnp.mean(x * x, axis=-1, keepdims=True) + EPS) * gain


def _dot(a, b):
    return jnp.dot(a, b, preferred_element_type=F32)


def _dot_nt(a, b):
    return lax.dot_general(a, b, (((1,), (1,)), ((), ())), preferred_element_type=F32)


def _in_proj_kernel(x_ref, g1_ref, w_ref, qg_ref, kg_ref, mg_ref, ra_ref, rp_ref, rm_ref, s64_ref, s128_ref,
                    q_ref, k_ref, v_ref, u_ref, qm_ref):
    hb = _rms(x_ref[...], g1_ref[...]).astype(BF16)
    ra, rp, rm = ra_ref[...], rp_ref[...], rm_ref[...]

    def proj(c0):
        return _dot(hb, w_ref[:, c0:c0 + 256])

    def head_norm(p, seg_ref, gain):
        ms = _dot((p * p).astype(BF16), seg_ref[...])
        return p * lax.rsqrt(ms + EPS) * gain

    def rope(p):
        outs = []
        for j in range(2):
            pj = p[:, j * LANES:(j + 1) * LANES]
            outs.append(pj * ra + pltpu.roll(pj, 8, 1) * rp + pltpu.roll(pj, LANES - 8, 1) * rm)
        return jnp.concatenate(outs, axis=1)

    for c in range(3):
        cs = slice(c * 256, (c + 1) * 256)
        q_ref[:, cs] = rope(head_norm(proj(c * 256), s64_ref, qg_ref[...])).astype(BF16)
        k_ref[:, cs] = rope(head_norm(proj(WIDTH_A + c * 256), s64_ref, kg_ref[...]))
        v_ref[:, cs] = proj(2 * WIDTH_A + c * 256)
    for c in range(2):
        cs = slice(c * 256, (c + 1) * 256)
        a = proj(3 * WIDTH_A + c * 256)
        gate = proj(3 * WIDTH_A + CONV_CH + c * 256)
        u_ref[:, cs] = a * jax.nn.sigmoid(gate)
        qm_ref[:, cs] = head_norm(proj(3 * WIDTH_A + 2 * CONV_CH + c * 256), s128_ref, mg_ref[...]).astype(BF16)


def _in_proj(x, g1, w1, qg, kg, mg, tabs, seg64, seg128, tm):
    n = x.shape[0]
    t_rows = tabs[0].shape[0]
    assert n % tm == 0 and t_rows % tm == 0
    t_blocks = t_rows // tm
    row = lambda i: (i, 0)
    const = lambda i: (0, 0)
    tab = lambda i: (i % t_blocks, 0)
    return pl.pallas_call(
        _in_proj_kernel,
        grid=(n // tm,),
        in_specs=[
            pl.BlockSpec((tm, D_MODEL), row),
            pl.BlockSpec((1, D_MODEL), const),
            pl.BlockSpec((D_MODEL, W1_COLS), const),
            pl.BlockSpec((1, 256), const), pl.BlockSpec((1, 256), const), pl.BlockSpec((1, 256), const),
            pl.BlockSpec((tm, LANES), tab), pl.BlockSpec((tm, LANES), tab), pl.BlockSpec((tm, LANES), tab),
            pl.BlockSpec((256, 256), const), pl.BlockSpec((256, 256), const),
        ],
        out_specs=[
            pl.BlockSpec((tm, WIDTH_A), row), pl.BlockSpec((tm, WIDTH_A), row), pl.BlockSpec((tm, WIDTH_A), row),
            pl.BlockSpec((tm, CONV_CH), row), pl.BlockSpec((tm, MEM_WIDTH), row),
        ],
        out_shape=[
            jax.ShapeDtypeStruct((n, WIDTH_A), BF16), jax.ShapeDtypeStruct((n, WIDTH_A), F32),
            jax.ShapeDtypeStruct((n, WIDTH_A), F32), jax.ShapeDtypeStruct((n, CONV_CH), F32),
            jax.ShapeDtypeStruct((n, MEM_WIDTH), BF16),
        ],
        compiler_params=_params(("arbitrary",), 48),
        name="in_proj",
    )(x, g1, w1, qg, kg, mg, *tabs, seg64, seg128)


def _swa_prompt_kernel(q_ref, kp_ref, kc_ref, vp_ref, vc_ref, o_ref, lse_ref):
    n = pl.program_id(2)
    q = q_ref[...]
    kk = jnp.concatenate([kp_ref[...], kc_ref[...]], axis=0).astype(BF16)
    vv = jnp.concatenate([vp_ref[...], vc_ref[...]], axis=0).astype(BF16)
    shape = (SWA_BLOCK, 2 * SWA_BLOCK)
    col = lax.broadcasted_iota(I32, shape, 1)
    off = lax.broadcasted_iota(I32, shape, 0) + SWA_BLOCK - col
    mask = (off >= 0) & (off <= SWA_BLOCK) & ((n > 0) | (col >= SWA_BLOCK))
    head = col // HEAD_DIM_A
    o_acc = jnp.zeros(shape, F32)
    lse_acc = jnp.zeros(shape, F32)
    for h in range(4):
        hm = head == h
        s = _dot_nt(jnp.where(hm, q, jnp.zeros_like(q)), kk) * (HEAD_DIM_A ** -0.5)
        s = jnp.where(mask, s, NEG_INF)
        m = jnp.max(s, axis=1, keepdims=True)
        p = jnp.exp(s - m)
        l = jnp.sum(p, axis=1, keepdims=True)
        oh = _dot(p.astype(BF16), vv) / l
        o_acc = jnp.where(hm, oh, o_acc)
        lse_acc = jnp.where(hm, m + jnp.log(l), lse_acc)
    o_ref[...] = o_acc
    lse_ref[...] = lse_acc


def _swa_prompt(q, k, v, g, dil):
    b, s, _ = q.shape
    assert s % (dil * SWA_BLOCK) == 0
    l = s // dil
    nb = l // SWA_BLOCK
    qv, kv, vv = (t.reshape(b, l, dil * WIDTH_A) for t in (q, k, v))
    blk = (None, SWA_BLOCK, GROUP_W)
    cur = lambda bi, r, n: (bi, n, r * N_GROUPS + g)
    prev = lambda bi, r, n: (bi, jnp.maximum(n - 1, 0), r * N_GROUPS + g)
    out = lambda bi, r, n: (bi, n, r)
    o, lse = pl.pallas_call(
        _swa_prompt_kernel,
        grid=(b, dil, nb),
        in_specs=[pl.BlockSpec(blk, cur), pl.BlockSpec(blk, prev), pl.BlockSpec(blk, cur),
                  pl.BlockSpec(blk, prev), pl.BlockSpec(blk, cur)],
        out_specs=[pl.BlockSpec(blk, out), pl.BlockSpec(blk, out)],
        out_shape=[jax.ShapeDtypeStruct((b, l, dil * GROUP_W), F32)] * 2,
        compiler_params=_params(("arbitrary",) * 3, 32),
        name=f"swa_prompt_d{dil}",
    )(qv, kv, kv, vv, vv)
    return o.reshape(b * s, GROUP_W), lse.reshape(b * s, GROUP_W)


def _swa_sample_kernel(q_ref, kn_ref, vn_ref, kb_ref, vb_ref, o_ref, lse_ref, ko_ref, vo_ref, *, win, dil):
    t_new = q_ref.shape[0]
    rows = win + LANES
    ko_ref[pl.ds(0, win - t_new), :] = kb_ref[pl.ds(t_new, win - t_new), :]
    ko_ref[pl.ds(win - t_new, t_new), :] = kn_ref[...]
    vo_ref[pl.ds(0, win - t_new), :] = vb_ref[pl.ds(t_new, win - t_new), :]
    vo_ref[pl.ds(win - t_new, t_new), :] = vn_ref[...]

    pad = jnp.zeros((LANES - t_new, GROUP_W), F32)
    kk = jnp.concatenate([kb_ref[...], kn_ref[...], pad], axis=0).astype(BF16)
    vv = jnp.concatenate([vb_ref[...], vn_ref[...], pad], axis=0).astype(BF16)
    qf = q_ref[...].astype(F32)
    ch_head = lax.broadcasted_iota(I32, (t_new, GROUP_W), 1) // HEAD_DIM_A
    qrows = [jnp.where(ch_head == h, qf, 0.0) for h in range(4)]
    qrows.append(jnp.zeros((LANES - 4 * t_new, GROUP_W), F32))
    qm = jnp.concatenate(qrows, axis=0).astype(BF16)
    st = _dot_nt(kk, qm) * (HEAD_DIM_A ** -0.5)
    i = lax.broadcasted_iota(I32, (rows, LANES), 0)
    x = lax.broadcasted_iota(I32, (rows, LANES), 1)
    delta = win + (x & (t_new - 1)) - i
    valid = (delta >= 0) & (delta <= win) & ((delta & (dil - 1)) == 0)
    valid = valid | (x >= 4 * t_new)
    st = jnp.where(valid, st, NEG_INF)
    m_row = jnp.max(st, axis=0, keepdims=True)
    p = jnp.exp(st - m_row)
    l_row = jnp.sum(p, axis=0, keepdims=True)
    o_all = _dot(p.T.astype(BF16), vv)
    eye = lax.broadcasted_iota(I32, (LANES, LANES), 0) == lax.broadcasted_iota(I32, (LANES, LANES), 1)
    m_col = jnp.sum(jnp.where(eye, m_row, 0.0), axis=1, keepdims=True)
    l_col = jnp.sum(jnp.where(eye, l_row, 0.0), axis=1, keepdims=True)
    o_n = o_all / l_col
    lse_col = m_col + jnp.log(l_col)
    o_acc = jnp.zeros((t_new, GROUP_W), F32)
    lse_acc = jnp.zeros((t_new, GROUP_W), F32)
    for h in range(4):
        hm = ch_head == h
        o_acc = jnp.where(hm, o_n[h * t_new:(h + 1) * t_new, :], o_acc)
        lse_acc = jnp.where(hm, lse_col[h * t_new:(h + 1) * t_new, :], lse_acc)
    o_ref[...] = o_acc
    lse_ref[...] = lse_acc


def _swa_sample(q, k_new, v_new, k_buf, v_buf, g, win, dil):
    b, t, _ = q.shape
    assert t == 8 and win % 8 == 0 and dil & (dil - 1) == 0
    new = pl.BlockSpec((None, t, GROUP_W), lambda bi: (bi, 0, g))
    buf = pl.BlockSpec((None, win, GROUP_W), lambda bi: (bi, 0, 0))
    out = pl.BlockSpec((None, t, GROUP_W), lambda bi: (bi, 0, 0))
    o, lse, ko, vo = pl.pallas_call(
        functools.partial(_swa_sample_kernel, win=win, dil=dil),
        grid=(b,),
        in_specs=[new, new, new, buf, buf],
        out_specs=[out, out, buf, buf],
        out_shape=[jax.ShapeDtypeStruct((b, t, GROUP_W), F32)] * 2
        + [jax.ShapeDtypeStruct((b, win, GROUP_W), F32)] * 2,
        compiler_params=_params(("arbitrary",), 48),
        name=f"swa_sample_w{win}",
    )(q, k_new, v_new, k_buf, v_buf)
    return o.reshape(b * t, GROUP_W), lse.reshape(b * t, GROUP_W), ko, vo


def _conv_kernel(halo_ref, cur_ref, w_ref, b_ref, g_ref, beta_ref, o_ref, ctx_ref, *, zero_first_halo, chunk):
    tm = cur_ref.shape[0]
    halo = halo_ref[...]
    if zero_first_halo:
        halo = jnp.where(pl.program_id(1) == 0, 0.0, halo)
    ctx_ref[pl.ds(0, CONV_HALO), :] = halo
    ctx_ref[pl.ds(CONV_HALO, tm), :] = cur_ref[...]
    first = CONV_HALO - (CONV_WIDTH - 1)
    for c0 in range(0, tm, chunk):
        acc = jnp.zeros((chunk, CONV_CH), F32) + b_ref[...]
        for w in range(CONV_WIDTH):
            acc = acc + ctx_ref[pl.ds(c0 + first + w, chunk), :] * w_ref[pl.ds(w, 1), :]
        mu = jnp.mean(acc, axis=-1, keepdims=True)
        xc = acc - mu
        var = jnp.mean(xc * xc, axis=-1, keepdims=True)
        y = xc * lax.rsqrt(var + EPS) * g_ref[...] + beta_ref[...]
        o_ref[pl.ds(c0, chunk), :] = (y * jax.nn.sigmoid(y)).astype(BF16)


def _conv_branch(u, halo, conv_w, conv_b, cn_g, cn_b, tm, halo_from_u):
    b, t, _ = u.shape
    assert t % tm == 0
    ratio = tm // CONV_HALO if halo_from_u else 0
    halo_map = (lambda bi, i: (bi, jnp.maximum(i * ratio - 1, 0), 0)) if halo_from_u else (lambda bi, i: (bi, 0, 0))
    const = lambda bi, i: (0, 0)
    return pl.pallas_call(
        functools.partial(_conv_kernel, zero_first_halo=halo_from_u, chunk=min(tm, 64)),
        grid=(b, t // tm),
        in_specs=[
            pl.BlockSpec((None, CONV_HALO, CONV_CH), halo_map),
            pl.BlockSpec((None, tm, CONV_CH), lambda bi, i: (bi, i, 0)),
            pl.BlockSpec((CONV_WIDTH, CONV_CH), const),
            pl.BlockSpec((1, CONV_CH), const), pl.BlockSpec((1, CONV_CH), const), pl.BlockSpec((1, CONV_CH), const),
        ],
        out_specs=pl.BlockSpec((None, tm, CONV_CH), lambda bi, i: (bi, i, 0)),
        out_shape=jax.ShapeDtypeStruct((b, t, CONV_CH), BF16),
        scratch_shapes=[pltpu.VMEM((CONV_HALO + tm, CONV_CH), F32)],
        compiler_params=_params(("arbitrary", "arbitrary"), 32),
        name="conv_module",
    )(halo, u, conv_w, conv_b, cn_g, cn_b).reshape(b * t, CONV_CH)


def _mem_kv_kernel(mem_ref, g_ref, w_ref, kg_ref, k_ref, v_ref):
    hb = _rms(mem_ref[...], g_ref[...]).astype(BF16)
    for h in range(MEM_HEADS):
        cs = slice(h * MEM_HEAD_DIM, (h + 1) * MEM_HEAD_DIM)
        k_ref[:, cs] = _rms(_dot(hb, w_ref[:, cs]), kg_ref[...])
        v_ref[:, cs] = _dot(hb, w_ref[:, MEM_WIDTH + h * MEM_HEAD_DIM:MEM_WIDTH + (h + 1) * MEM_HEAD_DIM])


def _mem_kv(mem, g, w, kg):
    b = mem.shape[0]
    const = lambda bi: (0, 0)
    blk = pl.BlockSpec((None, MEM_LEN, MEM_WIDTH), lambda bi: (bi, 0, 0))
    return pl.pallas_call(
        _mem_kv_kernel,
        grid=(b,),
        in_specs=[pl.BlockSpec((None, MEM_LEN, D_MODEL), lambda bi: (bi, 0, 0)), pl.BlockSpec((1, D_MODEL), const),
                  pl.BlockSpec((D_MODEL, 2 * MEM_WIDTH), const), pl.BlockSpec((1, MEM_HEAD_DIM), const)],
        out_specs=[blk, blk],
        out_shape=[jax.ShapeDtypeStruct((b, MEM_LEN, MEM_WIDTH), F32)] * 2,
        compiler_params=_params(("arbitrary",), 32),
        name="mem_kv",
    )(mem, g, w, kg)


def _mem_attn_kernel(q_ref, k_ref, v_ref, o_ref):
    for h in range(MEM_HEADS):
        cs = slice(h * MEM_HEAD_DIM, (h + 1) * MEM_HEAD_DIM)
        s = _dot_nt(q_ref[:, cs], k_ref[:, cs].astype(BF16)) * (MEM_HEAD_DIM ** -0.5)
        m = jnp.max(s, axis=1, keepdims=True)
        p = jnp.exp(s - m)
        l = jnp.sum(p, axis=1, keepdims=True)
        o_ref[:, cs] = (_dot(p.astype(BF16), v_ref[:, cs].astype(BF16)) / l).astype(BF16)


def _mem_attn(q, k, v, tm):
    b, t, _ = q.shape
    assert t % tm == 0
    kv = pl.BlockSpec((None, MEM_LEN, MEM_WIDTH), lambda bi, i: (bi, 0, 0))
    qo = pl.BlockSpec((None, tm, MEM_WIDTH), lambda bi, i: (bi, i, 0))
    return pl.pallas_call(
        _mem_attn_kernel,
        grid=(b, t // tm),
        in_specs=[qo, kv, kv],
        out_specs=qo,
        out_shape=jax.ShapeDtypeStruct((b, t, MEM_WIDTH), BF16),
        compiler_params=_params(("arbitrary", "arbitrary"), 32),
        name="mem_attn",
    )(q, k, v).reshape(b * t, MEM_WIDTH)


def _merge_kernel(x_ref, o1_ref, o2_ref, o3_ref, l1_ref, l2_ref, l3_ref, ob_ref, om_ref,
                  g1_ref, wg_ref, bg_ref, wa_ref, wb_ref, wm_ref, wo_ref, g2_ref, wrh_ref, wrl_ref, br_ref,
                  x1_ref, h2_ref, idx_ref, gw_ref, mix_ref):
    tm = x_ref.shape[0]
    x = x_ref[...]
    hb = _rms(x, g1_ref[...]).astype(BF16)
    la, lb, lc = l1_ref[...], l2_ref[...], l3_ref[...]
    m = jnp.maximum(la, jnp.maximum(lb, lc))
    ea, eb, ec = jnp.exp(la - m), jnp.exp(lb - m), jnp.exp(lc - m)
    oa = ((ea * o1_ref[...] + eb * o2_ref[...] + ec * o3_ref[...]) / (ea + eb + ec)).astype(BF16)
    ob, om = ob_ref[...], om_ref[...]
    for j in range(4):
        cs = slice(j * 256, (j + 1) * 256)

        def gate(branch):
            c0 = branch * D_MODEL + j * 256
            return jax.nn.sigmoid(_dot(hb, wg_ref[:, c0:c0 + 256]) + bg_ref[:, c0:c0 + 256])

        mixed = gate(0) * _dot(oa, wa_ref[:, cs]) + gate(1) * _dot(ob, wb_ref[:, cs]) + gate(2) * _dot(om, wm_ref[:, cs])
        mix_ref[:, cs] = mixed.astype(BF16)
    x1 = x + _dot(mix_ref[...], wo_ref[...])
    x1_ref[...] = x1
    h2 = _rms(x1, g2_ref[...])
    h2_ref[...] = h2
    hi = h2.astype(BF16)
    lo = (h2 - hi.astype(F32)).astype(BF16)
    logits = _dot(hi, wrh_ref[...]) + _dot(lo, wrh_ref[...]) + _dot(hi, wrl_ref[...]) + br_ref[...]
    lane = lax.broadcasted_iota(I32, (tm, LANES), 1)
    logits = jnp.where(lane < N_EXPERTS, logits, NEG_INF)
    vals, idxs = [], []
    for _ in range(TOP_K):
        mk = jnp.max(logits, axis=1, keepdims=True)
        ik = jnp.min(jnp.where(logits == mk, lane, LANES), axis=1, keepdims=True)
        vals.append(mk)
        idxs.append(ik)
        logits = jnp.where(lane == ik, NEG_INF, logits)
    es = [jnp.exp(v - vals[0]) for v in vals]
    den = es[0] + es[1] + es[2] + es[3]
    l4 = lax.broadcasted_iota(I32, (tm, TOP_K), 1)
    idx_out = jnp.zeros((tm, TOP_K), I32)
    gw_out = jnp.zeros((tm, TOP_K), F32)
    for k in range(TOP_K):
        idx_out = jnp.where(l4 == k, idxs[k], idx_out)
        gw_out = jnp.where(l4 == k, es[k] / den, gw_out)
    idx_ref[...] = idx_out
    gw_ref[...] = gw_out


def _merge(x, os_, lses, ob, om, g1, wg, bg, wa, wb, wm, wo, g2, wrh, wrl, br, tm):
    n = x.shape[0]
    assert n % tm == 0
    row = lambda i: (i, 0)
    const = lambda i: (0, 0)
    rows = lambda w: pl.BlockSpec((tm, w), row)
    full = lambda a: pl.BlockSpec(a.shape, const)
    return pl.pallas_call(
        _merge_kernel,
        grid=(n // tm,),
        in_specs=[rows(D_MODEL)] + [rows(GROUP_W)] * 6 + [rows(CONV_CH), rows(MEM_WIDTH)]
        + [full(a) for a in (g1, wg, bg, wa, wb, wm, wo, g2, wrh, wrl, br)],
        out_specs=[rows(D_MODEL), rows(D_MODEL), rows(TOP_K), rows(TOP_K)],
        out_shape=[jax.ShapeDtypeStruct((n, D_MODEL), F32), jax.ShapeDtypeStruct((n, D_MODEL), F32),
                   jax.ShapeDtypeStruct((n, TOP_K), I32), jax.ShapeDtypeStruct((n, TOP_K), F32)],
        scratch_shapes=[pltpu.VMEM((tm, D_MODEL), BF16)],
        compiler_params=_params(("arbitrary",), 56),
        name="merge_router",
    )(x, *os_, *lses, ob, om, g1, wg, bg, wa, wb, wm, wo, g2, wrh, wrl, br)


def _route_kernel(idx_ref, dest_ref, bexp_ref, cnt_ref, carry_ref, *, block_rows):
    phase, i = pl.program_id(0), pl.program_id(1)
    tq = idx_ref.shape[0]
    nb = bexp_ref.shape[0]
    idx = idx_ref[...]
    lane = lax.broadcasted_iota(I32, (tq, LANES), 1)
    member = jnp.zeros((tq, LANES), F32)
    for k in range(TOP_K):
        member = member + (lane == idx[:, k:k + 1]).astype(F32)
    tile_counts = jnp.sum(member, axis=0, keepdims=True)

    @pl.when(phase == 0)
    def _():
        @pl.when(i == 0)
        def _():
            cnt_ref[...] = jnp.zeros_like(cnt_ref)

        cnt_ref[...] += tile_counts
        dest_ref[...] = jnp.zeros_like(dest_ref)
        bexp_ref[...] = jnp.zeros_like(bexp_ref)

    @pl.when(phase == 1)
    def _():
        @pl.when(i == 0)
        def _():
            carry_ref[...] = jnp.zeros_like(carry_ref)

        counts = jnp.broadcast_to(cnt_ref[...], (8, LANES))
        padded = jnp.floor((counts + (block_rows - 1)) * (1.0 / block_rows)) * block_rows
        lane8 = lax.broadcasted_iota(I32, (8, LANES), 1)
        ends = padded
        for s in (1, 2, 4, 8, 16, 32, 64):
            ends = ends + jnp.where(lane8 >= s, pltpu.roll(ends, s, 1), 0.0)
        starts = (ends - padded)[0:1, :]
        r = lax.broadcasted_iota(I32, (tq, tq), 0)
        c = lax.broadcasted_iota(I32, (tq, tq), 1)
        earlier = _dot((c < r).astype(BF16), member.astype(BF16)) + carry_ref[...]
        carry_ref[...] += tile_counts
        base = earlier + starts
        l4 = lax.broadcasted_iota(I32, (tq, TOP_K), 1)
        dest = jnp.zeros((tq, TOP_K), F32)
        for k in range(TOP_K):
            dk = jnp.sum(jnp.where(lane == idx[:, k:k + 1], base, 0.0), axis=1, keepdims=True)
            dest = jnp.where(l4 == k, dk, dest)
        dest_ref[...] = dest.astype(I32)
        first_row = (lax.broadcasted_iota(I32, (nb, LANES), 0) * block_rows).astype(F32)
        lane_nb = lax.broadcasted_iota(I32, (nb, LANES), 1)
        done = jnp.where((ends[0:1, :] <= first_row) & (lane_nb < N_EXPERTS), 1.0, 0.0)
        bexp_ref[...] = jnp.minimum(jnp.sum(done, axis=1, keepdims=True), N_EXPERTS - 1.0).astype(I32)


def _route(idx, block_rows, n_blocks, tq):
    n = idx.shape[0]
    assert n % tq == 0
    nb_pad = -(-n_blocks // 8) * 8
    dest, bexp = pl.pallas_call(
        functools.partial(_route_kernel, block_rows=block_rows),
        grid=(2, n // tq),
        in_specs=[pl.BlockSpec((tq, TOP_K), lambda p, i: (i, 0))],
        out_specs=[pl.BlockSpec((tq, TOP_K), lambda p, i: (i * p, 0)),
                   pl.BlockSpec((nb_pad, 1), lambda p, i: (0, 0))],
        out_shape=[jax.ShapeDtypeStruct((n, TOP_K), I32), jax.ShapeDtypeStruct((nb_pad, 1), I32)],
        scratch_shapes=[pltpu.VMEM((1, LANES), F32), pltpu.VMEM((1, LANES), F32)],
        compiler_params=_params(("arbitrary", "arbitrary"), 32),
        name="route",
    )(idx)
    return dest.reshape(n * TOP_K), bexp.reshape(nb_pad)[:n_blocks]


def _dispatch_kernel(dest_ref, h_ref, xs_in_ref, xs_ref, sem):
    del xs_in_ref
    tm = h_ref.shape[0]

    def row_copy(r, d):
        return pltpu.make_async_copy(h_ref.at[pl.ds(r, 1)], xs_ref.at[pl.ds(d, 1)], sem)

    def issue(r, carry):
        for k in range(TOP_K):
            row_copy(r, dest_ref[r * TOP_K + k]).start()
        return carry

    def drain(r, carry):
        for k in range(TOP_K):
            row_copy(r, dest_ref[r * TOP_K + k]).wait()
        return carry

    lax.fori_loop(0, tm, issue, 0)
    lax.fori_loop(0, tm, drain, 0)


def _dispatch(h2, dest, n_slots, tm):
    n = h2.shape[0]
    assert n % tm == 0
    zeros = jnp.zeros((n_slots, D_MODEL), F32)
    return pl.pallas_call(
        _dispatch_kernel,
        grid=(n // tm,),
        in_specs=[pl.BlockSpec((tm * TOP_K,), lambda i: (i,), memory_space=pltpu.SMEM),
                  pl.BlockSpec((tm, D_MODEL), lambda i: (i, 0)),
                  pl.BlockSpec(memory_space=pl.ANY)],
        out_specs=pl.BlockSpec(memory_space=pl.ANY),
        out_shape=jax.ShapeDtypeStruct((n_slots, D_MODEL), F32),
        scratch_shapes=[pltpu.SemaphoreType.DMA],
        input_output_aliases={2: 0},
        compiler_params=_params(("arbitrary",), 32),
        name="dispatch",
    )(dest, h2, zeros)


def _moe_kernel(be_ref, xs_ref, wg_ref, bg_ref, wu_ref, bu_ref, wd_ref, bd_ref, yb_ref, wg_s, wu_s, wd_s):
    i = pl.program_id(0)
    changed = (i == 0) | (be_ref[i] != be_ref[jnp.maximum(i - 1, 0)])

    @pl.when(changed)
    def _():
        for r0 in range(0, D_MODEL, 128):
            rs = pl.ds(r0, 128)
            wg_s[rs, :] = wg_ref[rs, :].astype(BF16)
            wu_s[rs, :] = wu_ref[rs, :].astype(BF16)
            wd_s[rs, :] = wd_ref[rs, :].astype(BF16)

    xb = xs_ref[...].astype(BF16)
    acc = jnp.zeros(yb_ref.shape, F32) + bd_ref[...]
    for c in range(D_FF // 256):
        cs = slice(c * 256, (c + 1) * 256)
        g = jnp.minimum(_dot(xb, wg_s[:, cs]) + bg_ref[:, cs], SWIGLU_LIMIT)
        u = jnp.clip(_dot(xb, wu_s[:, cs]) + bu_ref[:, cs], -SWIGLU_LIMIT, SWIGLU_LIMIT)
        act = g * jax.nn.sigmoid(SWIGLU_ALPHA * g) * (u + 1.0)
        acc = acc + _dot(act.astype(BF16), wd_s[cs, :])
    yb_ref[...] = acc


def _moe(xs, bexp, wg, bg, wu, bu, wd, bd, bm):
    n_slots = xs.shape[0]
    assert n_slots % bm == 0
    wspec = pl.BlockSpec((None, D_MODEL, D_FF), lambda i, be: (be[i], 0, 0))
    bspec = pl.BlockSpec((None, 1, D_FF), lambda i, be: (be[i], 0, 0))
    rows = pl.BlockSpec((bm, D_MODEL), lambda i, be: (i, 0))
    grid_spec = pltpu.PrefetchScalarGridSpec(
        num_scalar_prefetch=1,
        grid=(n_slots // bm,),
        in_specs=[rows, wspec, bspec, wspec, bspec, wspec, bspec],
        out_specs=rows,
        scratch_shapes=[pltpu.VMEM((D_MODEL, D_FF), BF16)] * 3,
    )
    return pl.pallas_call(
        _moe_kernel,
        grid_spec=grid_spec,
        out_shape=jax.ShapeDtypeStruct((n_slots, D_MODEL), F32),
        compiler_params=_params(("arbitrary",), 56),
        name="moe_experts",
    )(bexp, xs, wg, bg.reshape(N_EXPERTS, 1, D_FF), wu, bu.reshape(N_EXPERTS, 1, D_FF),
      wd, bd.reshape(N_EXPERTS, 1, D_MODEL))


def _combine_kernel(dest_ref, x1_ref, gw_ref, yb_ref, y_ref, rows_ref, sem):
    tm = x1_ref.shape[0]

    def row_copy(r, k, d):
        return pltpu.make_async_copy(yb_ref.at[pl.ds(d, 1)], rows_ref.at[k, pl.ds(r, 1)], sem)

    def issue(r, carry):
        for k in range(TOP_K):
            row_copy(r, k, dest_ref[r * TOP_K + k]).start()
        return carry

    def drain(r, carry):
        for k in range(TOP_K):
            row_copy(r, k, dest_ref[r * TOP_K + k]).wait()
        return carry

    lax.fori_loop(0, tm, issue, 0)
    lax.fori_loop(0, tm, drain, 0)
    gw = gw_ref[...]
    y = x1_ref[...]
    for k in range(TOP_K):
        y = y + gw[:, k:k + 1] * rows_ref[k]
    y_ref[...] = y


def _combine(x1, gw, dest, yb, tm):
    n = x1.shape[0]
    assert n % tm == 0
    return pl.pallas_call(
        _combine_kernel,
        grid=(n // tm,),
        in_specs=[pl.BlockSpec((tm * TOP_K,), lambda i: (i,), memory_space=pltpu.SMEM),
                  pl.BlockSpec((tm, D_MODEL), lambda i: (i, 0)),
                  pl.BlockSpec((tm, TOP_K), lambda i: (i, 0)),
                  pl.BlockSpec(memory_space=pl.ANY)],
        out_specs=pl.BlockSpec((tm, D_MODEL), lambda i: (i, 0)),
        out_shape=jax.ShapeDtypeStruct((n, D_MODEL), F32),
        scratch_shapes=[pltpu.VMEM((TOP_K, tm, D_MODEL), F32), pltpu.SemaphoreType.DMA],
        compiler_params=_params(("arbitrary",), 32),
        name="combine",
    )(dest, x1, gw, yb)


def _rope_tables(positions):
    half = ROPE_DIM // 2
    inv = jnp.power(jnp.float32(ROPE_THETA), -jnp.arange(half, dtype=F32) / half)
    ang = positions.astype(F32)[:, None] * inv[None, :]
    cos, sin = jnp.cos(ang), jnp.sin(ang)
    t = positions.shape[0]
    z8 = jnp.zeros((t, half), F32)
    rest0 = jnp.zeros((t, HEAD_DIM_A - ROPE_DIM), F32)
    a = jnp.concatenate([cos, cos, jnp.ones((t, HEAD_DIM_A - ROPE_DIM), F32)], axis=1)
    bp = jnp.concatenate([z8, sin, rest0], axis=1)
    bm = jnp.concatenate([-sin, z8, rest0], axis=1)
    return tuple(jnp.tile(m, (1, LANES // HEAD_DIM_A)) for m in (a, bp, bm))


def _segment_mean_matrix(seg):
    i = jnp.arange(256)
    return jnp.where((i[:, None] // seg) == (i[None, :] // seg), 1.0 / seg, 0.0).astype(BF16)


def _layer_weights(l, norm1_g, w_in, b_gates, qn_a, kn_a, conv_w, conv_b, cn_g, cn_b, qn_m,
                   w_proj_a, w_proj_b, w_proj_m, w_out, norm2_g, w_router, b_router,
                   w_gate, b_gate, w_up, b_up, w_down, b_down):
    wr = jnp.pad(w_router[l], ((0, 0), (0, LANES - N_EXPERTS)))
    wrh = wr.astype(BF16)
    return dict(
        g1=norm1_g[l][None, :],
        w1=w_in[l][:, :W1_COLS].astype(BF16),
        wgates=w_in[l][:, W1_COLS:].astype(BF16),
        bgates=b_gates[l][None, :],
        qg=jnp.tile(qn_a[l], 256 // HEAD_DIM_A)[None, :],
        kg=jnp.tile(kn_a[l], 256 // HEAD_DIM_A)[None, :],
        mg=jnp.tile(qn_m[l], 256 // MEM_HEAD_DIM)[None, :],
        conv_w=conv_w[l], conv_b=conv_b[l][None, :], cn_g=cn_g[l][None, :], cn_b=cn_b[l][None, :],
        wa=w_proj_a[l].astype(BF16), wb=w_proj_b[l].astype(BF16), wm=w_proj_m[l].astype(BF16),
        wo=w_out[l].astype(BF16), g2=norm2_g[l][None, :],
        wrh=wrh, wrl=(wr - wrh.astype(F32)).astype(BF16),
        br=jnp.pad(b_router[l], (0, LANES - N_EXPERTS))[None, :],
        w_gate=w_gate[l], b_gate=b_gate[l], w_up=w_up[l], b_up=b_up[l], w_down=w_down[l], b_down=b_down[l],
        seg64=_segment_mean_matrix(HEAD_DIM_A), seg128=_segment_mean_matrix(MEM_HEAD_DIM),
    )


def _token_mixer_tail(x2d, os_, lses, ob, om, w, tm_merge, moe_rows, tm_route, tm_rows):
    n = x2d.shape[0]
    x1, h2, idx, gw = _merge(x2d, os_, lses, ob, om, w["g1"], w["wgates"], w["bgates"], w["wa"], w["wb"], w["wm"],
                             w["wo"], w["g2"], w["wrh"], w["wrl"], w["br"], tm_merge)
    n_blocks = -(-(n * TOP_K) // moe_rows) + N_EXPERTS
    dest, bexp = _route(idx, moe_rows, n_blocks, tm_route)
    xs = _dispatch(h2, dest, n_blocks * moe_rows, tm_rows)
    yb = _moe(xs, bexp, w["w_gate"], w["b_gate"], w["w_up"], w["b_up"], w["w_down"], w["b_down"], moe_rows)
    return _combine(x1, gw, dest, yb, min(tm_rows, 128))


def _prompt_layer(x, mem, w, mem_norm_g, w_mem_kv, kn_m):
    b, s, _ = x.shape
    n = b * s
    x2d = x.reshape(n, D_MODEL)
    tabs = _rope_tables(jnp.arange(s, dtype=I32))
    q, k, v, u, qm = _in_proj(x2d, w["g1"], w["w1"], w["qg"], w["kg"], w["mg"], tabs, w["seg64"], w["seg128"], 512)
    q3, k3, v3 = (t.reshape(b, s, WIDTH_A) for t in (q, k, v))
    os_, lses, caches = [], [], []
    for g, (win, dil) in enumerate(SWA_GROUPS):
        o, lse = _swa_prompt(q3, k3, v3, g, dil)
        os_.append(o)
        lses.append(lse)
        cs = slice(g * GROUP_W, (g + 1) * GROUP_W)
        caches += [k3[:, s - win:, cs].reshape(b, win, 4, HEAD_DIM_A), v3[:, s - win:, cs].reshape(b, win, 4, HEAD_DIM_A)]
    u3 = u.reshape(b, s, CONV_CH)
    ob = _conv_branch(u3, u3, w["conv_w"], w["conv_b"], w["cn_g"], w["cn_b"], 256, True)
    mk, mv = _mem_kv(mem, mem_norm_g[None, :], w_mem_kv.astype(BF16), kn_m[None, :])
    om = _mem_attn(qm.reshape(b, s, MEM_WIDTH), mk, mv, 512)
    y = _token_mixer_tail(x2d, os_, lses, ob, om, w, 512, 512, 512, 256)
    state = caches + [u3[:, s - (CONV_WIDTH - 1):], mk.reshape(b, MEM_LEN, MEM_HEADS, MEM_HEAD_DIM),
                      mv.reshape(b, MEM_LEN, MEM_HEADS, MEM_HEAD_DIM)]
    return y.reshape(b, s, D_MODEL), state


def _sample_layer(x, mem_k, mem_v, bufs, conv_state, w):
    b, t, _ = x.shape
    n = b * t
    x2d = x.reshape(n, D_MODEL)
    tabs = _rope_tables(jnp.tile(PAST_LEN + jnp.arange(t, dtype=I32), b))
    q, k, v, u, qm = _in_proj(x2d, w["g1"], w["w1"], w["qg"], w["kg"], w["mg"], tabs, w["seg64"], w["seg128"], 256)
    q3, k3, v3 = (a.reshape(b, t, WIDTH_A) for a in (q, k, v))
    os_, lses, caches = [], [], []
    for g, (win, dil) in enumerate(SWA_GROUPS):
        kb = bufs[2 * g].reshape(b, win, GROUP_W)
        vb = bufs[2 * g + 1].reshape(b, win, GROUP_W)
        o, lse, ko, vo = _swa_sample(q3, k3, v3, kb, vb, g, win, dil)
        os_.append(o)
        lses.append(lse)
        caches += [ko.reshape(b, win, 4, HEAD_DIM_A), vo.reshape(b, win, 4, HEAD_DIM_A)]
    u3 = u.reshape(b, t, CONV_CH)
    halo = jnp.pad(conv_state, ((0, 0), (CONV_HALO - (CONV_WIDTH - 1), 0), (0, 0)))
    ob = _conv_branch(u3, halo, w["conv_w"], w["conv_b"], w["cn_g"], w["cn_b"], t, False)
    om = _mem_attn(qm.reshape(b, t, MEM_WIDTH), mem_k.reshape(b, MEM_LEN, MEM_WIDTH),
                   mem_v.reshape(b, MEM_LEN, MEM_WIDTH), t)
    y = _token_mixer_tail(x2d, os_, lses, ob, om, w, 256, 128, 512, 128)
    new_conv = jnp.concatenate([conv_state, u3], axis=1)[:, t:]
    return y.reshape(b, t, D_MODEL), caches + [new_conv]


def kernel(x_prompt, x_sample, mem_prompt, cache_swa_k_w128, cache_swa_v_w128, cache_swa_k_w512, cache_swa_v_w512, cache_swa_k_w2048, cache_swa_v_w2048, state_conv, cache_mem_k, cache_mem_v, norm1_g, w_in, b_gates, qn_a, kn_a, conv_w, conv_b, cn_g, cn_b, mem_norm_g, w_mem_kv, qn_m, kn_m, w_proj_a, w_proj_b, w_proj_m, w_out, norm2_g, w_router, b_router, w_gate, b_gate, w_up, b_up, w_down, b_down):
    depth = norm1_g.shape[0]
    y_p, y_s = x_prompt, x_sample
    st_p, st_s = [], []
    for l in range(depth):
        w = _layer_weights(l, norm1_g, w_in, b_gates, qn_a, kn_a, conv_w, conv_b, cn_g, cn_b, qn_m,
                           w_proj_a, w_proj_b, w_proj_m, w_out, norm2_g, w_router, b_router,
                           w_gate, b_gate, w_up, b_up, w_down, b_down)
        y_p, state_p = _prompt_layer(y_p, mem_prompt, w, mem_norm_g[l], w_mem_kv[l], kn_m[l])
        bufs = (cache_swa_k_w128[l], cache_swa_v_w128[l], cache_swa_k_w512[l], cache_swa_v_w512[l],
                cache_swa_k_w2048[l], cache_swa_v_w2048[l])
        y_s, state_s = _sample_layer(y_s, cache_mem_k[l], cache_mem_v[l], bufs, state_conv[l], w)
        st_p.append(state_p)
        st_s.append(state_s)
    outs_p = [jnp.stack(a) for a in zip(*st_p)]
    outs_s = [jnp.stack(a) for a in zip(*st_s)]
    return (y_p, y_s, *outs_p, *outs_s)
```

```python
import functools

import jax
import jax.numpy as jnp
from jax import lax
from jax.experimental import pallas as pl
from jax.experimental.pallas import tpu as pltpu

F32 = jnp.float32
BF16 = jnp.bfloat16
I32 = jnp.int32

D_MODEL = 1024
HEAD_DIM_A = 64
GROUP_W = 256
N_GROUPS = 3
WIDTH_A = N_GROUPS * GROUP_W
SWA_GROUPS = ((128, 1), (512, 4), (2048, 16))
SWA_BLOCK = 128
ROPE_DIM = 16
ROPE_THETA = 500000.0
CONV_CH = 512
CONV_WIDTH = 31
CONV_HALO = 32
MEM_LEN = 256
MEM_HEADS = 4
MEM_HEAD_DIM = 128
MEM_WIDTH = 512
N_EXPERTS = 32
TOP_K = 4
D_FF = 1024
SWIGLU_LIMIT = 7.0
SWIGLU_ALPHA = 1.702
EPS = 1e-6
PAST_LEN = 8192
LANES = 128
W1_COLS = 3 * WIDTH_A + 2 * CONV_CH + MEM_WIDTH
NEG_INF = float("-inf")
MIB = 2 ** 20


def _params(semantics, vmem_mib):
    return pltpu.CompilerParams(dimension_semantics=semantics, vmem_limit_bytes=vmem_mib * MIB)


def _rms(x, gain):
    return x * lax.rsqrt(jnp.mean(x * x, axis=-1, keepdims=True) + EPS) * gain


def _dot(a, b):
    return jnp.dot(a, b, preferred_element_type=F32)


def _dot_nt(a, b):
    return lax.dot_general(a, b, (((1,), (1,)), ((), ())), preferred_element_type=F32)


def _in_proj_kernel(x_ref, g1_ref, w_ref, qg_ref, kg_ref, mg_ref, ra_ref, rp_ref, rm_ref, s64_ref, s128_ref,
                    q_ref, k_ref, v_ref, u_ref, qm_ref):
    hb = _rms(x_ref[...], g1_ref[...]).astype(BF16)
    ra, rp, rm = ra_ref[...], rp_ref[...], rm_ref[...]

    def proj(c0):
        return _dot(hb, w_ref[:, c0:c0 + 256])

    def head_norm(p, seg_ref, gain):
        ms = _dot((p * p).astype(BF16), seg_ref[...])
        return p * lax.rsqrt(ms + EPS) * gain

    def rope(p):
        outs = []
        for j in range(2):
            pj = p[:, j * LANES:(j + 1) * LANES]
            outs.append(pj * ra + pltpu.roll(pj, 8, 1) * rp + pltpu.roll(pj, LANES - 8, 1) * rm)
        return jnp.concatenate(outs, axis=1)

    for c in range(3):
        cs = slice(c * 256, (c + 1) * 256)
        q_ref[:, cs] = rope(head_norm(proj(c * 256), s64_ref, qg_ref[...])).astype(q_ref.dtype)
        k_ref[:, cs] = rope(head_norm(proj(WIDTH_A + c * 256), s64_ref, kg_ref[...]))
        v_ref[:, cs] = proj(2 * WIDTH_A + c * 256)
    for c in range(2):
        cs = slice(c * 256, (c + 1) * 256)
        a = proj(3 * WIDTH_A + c * 256)
        gate = proj(3 * WIDTH_A + CONV_CH + c * 256)
        u_ref[:, cs] = a * jax.nn.sigmoid(gate)
        qm_ref[:, cs] = head_norm(proj(3 * WIDTH_A + 2 * CONV_CH + c * 256), s128_ref, mg_ref[...]).astype(BF16)


def _in_proj(x, g1, w1, qg, kg, mg, tabs, seg64, seg128, tm, q_dtype):
    n = x.shape[0]
    t_rows = tabs[0].shape[0]
    assert n % tm == 0 and t_rows % tm == 0
    t_blocks = t_rows // tm
    row = lambda i: (i, 0)
    const = lambda i: (0, 0)
    tab = lambda i: (i % t_blocks, 0)
    return pl.pallas_call(
        _in_proj_kernel,
        grid=(n // tm,),
        in_specs=[
            pl.BlockSpec((tm, D_MODEL), row),
            pl.BlockSpec((1, D_MODEL), const),
            pl.BlockSpec((D_MODEL, W1_COLS), const),
            pl.BlockSpec((1, 256), const), pl.BlockSpec((1, 256), const), pl.BlockSpec((1, 256), const),
            pl.BlockSpec((tm, LANES), tab), pl.BlockSpec((tm, LANES), tab), pl.BlockSpec((tm, LANES), tab),
            pl.BlockSpec((256, 256), const), pl.BlockSpec((256, 256), const),
        ],
        out_specs=[
            pl.BlockSpec((tm, WIDTH_A), row), pl.BlockSpec((tm, WIDTH_A), row), pl.BlockSpec((tm, WIDTH_A), row),
            pl.BlockSpec((tm, CONV_CH), row), pl.BlockSpec((tm, MEM_WIDTH), row),
        ],
        out_shape=[
            jax.ShapeDtypeStruct((n, WIDTH_A), q_dtype), jax.ShapeDtypeStruct((n, WIDTH_A), F32),
            jax.ShapeDtypeStruct((n, WIDTH_A), F32), jax.ShapeDtypeStruct((n, CONV_CH), F32),
            jax.ShapeDtypeStruct((n, MEM_WIDTH), BF16),
        ],
        compiler_params=_params(("arbitrary",), 48),
        name="in_proj",
    )(x, g1, w1, qg, kg, mg, *tabs, seg64, seg128)


def _swa_prompt_kernel(q_ref, k_ref, v_ref, o_ref, lse_ref, q_s, k_s, v_s, o_s, l_s, *, dil, nsub, unroll):
    n = pl.program_id(1)
    span = SWA_BLOCK * dil
    cur = n % 2
    prv = 1 - cur
    for s in range(2):
        ls = slice(s * LANES, (s + 1) * LANES)
        q_s[s] = q_ref[:, ls]
        k_s[cur, s] = k_ref[:, ls]
        v_s[cur, s] = v_ref[:, ls]

    @pl.when(n == 0)
    def _():
        k_s[prv] = jnp.zeros(k_s.shape[1:], F32)
        v_s[prv] = jnp.zeros(v_s.shape[1:], F32)

    shape = (SWA_BLOCK, 2 * SWA_BLOCK)
    col = lax.broadcasted_iota(I32, shape, 1)
    off = lax.broadcasted_iota(I32, shape, 0) + SWA_BLOCK - col
    band = (off >= 0) & (off <= SWA_BLOCK)
    band_first = band & ((n > 0) | (col >= SWA_BLOCK))
    head = col // HEAD_DIM_A

    def residue(r, carry):
        def both(ref, lead, rows):
            return jnp.concatenate([ref[(*lead, s, rows, slice(None))] for s in range(2)], axis=1)

        for j in range(nsub):
            rows = pl.ds(j * span + r, SWA_BLOCK, stride=dil)
            before = (prv, pl.ds((nsub - 1) * span + r, SWA_BLOCK, stride=dil)) if j == 0 else \
                     (cur, pl.ds((j - 1) * span + r, SWA_BLOCK, stride=dil))
            q = both(q_s, (), rows).astype(BF16)
            kk = jnp.concatenate([both(k_s, before[:1], before[1]), both(k_s, (cur,), rows)], axis=0).astype(BF16)
            vv = jnp.concatenate([both(v_s, before[:1], before[1]), both(v_s, (cur,), rows)], axis=0).astype(BF16)
            mask = band_first if j == 0 else band
            hms = [head == h for h in range(4)]
            ss = [jnp.where(mask, _dot_nt(jnp.where(hm, q, jnp.zeros_like(q)), kk) * (HEAD_DIM_A ** -0.5), NEG_INF)
                  for hm in hms]
            ms = [jnp.max(s, axis=1, keepdims=True) for s in ss]
            ps = [jnp.exp(s - m) for s, m in zip(ss, ms)]
            ls = [jnp.sum(p, axis=1, keepdims=True) for p in ps]
            ohs = [_dot(p.astype(BF16), vv) for p in ps]
            o_acc = jnp.zeros(shape, F32)
            lse_acc = jnp.zeros(shape, F32)
            for hm, oh, m, l in zip(hms, ohs, ms, ls):
                o_acc = jnp.where(hm, oh / l, o_acc)
                lse_acc = jnp.where(hm, m + jnp.log(l), lse_acc)
            for s in range(2):
                ls = slice(s * LANES, (s + 1) * LANES)
                o_s[s, rows, :] = o_acc[:, ls]
                l_s[s, rows, :] = lse_acc[:, ls]
        return carry

    lax.fori_loop(0, dil, residue, 0, unroll=unroll)
    for s in range(2):
        ls = slice(s * LANES, (s + 1) * LANES)
        o_ref[:, ls] = o_s[s]
        lse_ref[:, ls] = l_s[s]


def _swa_prompt(q, k, v, g, dil):
    b, s, _ = q.shape
    nsub = max(1, 4 // dil)
    unroll = 2 if nsub == 1 else 1
    t = nsub * dil * SWA_BLOCK
    assert s % t == 0
    inp = pl.BlockSpec((None, t, GROUP_W), lambda bi, n: (bi, n, g))
    out = pl.BlockSpec((None, t, GROUP_W), lambda bi, n: (bi, n, 0))
    slab = lambda *lead: pltpu.VMEM((*lead, 2, t, LANES), F32)
    o, lse = pl.pallas_call(
        functools.partial(_swa_prompt_kernel, dil=dil, nsub=nsub, unroll=unroll),
        grid=(b, s // t),
        in_specs=[inp, inp, inp],
        out_specs=[out, out],
        out_shape=[jax.ShapeDtypeStruct((b, s, GROUP_W), F32)] * 2,
        scratch_shapes=[slab(), slab(2), slab(2), slab(), slab()],
        compiler_params=_params(("arbitrary", "arbitrary"), 48),
        name=f"swa_prompt_d{dil}",
    )(q, k, v)
    return o.reshape(b * s, GROUP_W), lse.reshape(b * s, GROUP_W)


def _swa_sample_kernel(q_ref, kn_ref, vn_ref, kt_ref, vt_ref, o_ref, lse_ref, kto_ref, vto_ref, *, win, dil):
    t_new = q_ref.shape[0]
    n_tiles = win // LANES
    pad = jnp.zeros((LANES - t_new, GROUP_W), F32)
    kn = jnp.concatenate([kn_ref[...], pad], axis=0)
    vn = jnp.concatenate([vn_ref[...], pad], axis=0)

    def channel_major(a):
        return jnp.concatenate([a[:, :LANES].T, a[:, LANES:].T], axis=0)

    knt, vnt = channel_major(kn), channel_major(vn)
    lane = lax.broadcasted_iota(I32, (GROUP_W, LANES), 1)
    keep = lane < LANES - t_new
    for src_ref, new_t, dst_ref in ((kt_ref, knt, kto_ref), (vt_ref, vnt, vto_ref)):
        nxt = pltpu.roll(src_ref[:, 0:LANES], LANES - t_new, 1)
        for j in range(n_tiles):
            this = nxt
            following = new_t if j + 1 == n_tiles else src_ref[:, (j + 1) * LANES:(j + 2) * LANES]
            nxt = pltpu.roll(following, LANES - t_new, 1)
            dst_ref[:, j * LANES:(j + 1) * LANES] = jnp.where(keep, this, nxt)

    qf = q_ref[...].astype(F32)
    ch_head = lax.broadcasted_iota(I32, (t_new, GROUP_W), 1) // HEAD_DIM_A
    qm = jnp.concatenate([jnp.where(ch_head == h, qf, 0.0) for h in range(4)], axis=0).astype(BF16)
    nq = 4 * t_new
    scale = HEAD_DIM_A ** -0.5
    s_c = _dot(qm, kt_ref[...].astype(BF16)) * scale
    s_n = _dot(qm, knt.astype(BF16)) * scale
    t_c = lax.broadcasted_iota(I32, (nq, win), 0) & (t_new - 1)
    d_c = win + t_c - lax.broadcasted_iota(I32, (nq, win), 1)
    s_c = jnp.where((d_c <= win) & ((d_c & (dil - 1)) == 0), s_c, NEG_INF)
    t_n = lax.broadcasted_iota(I32, (nq, LANES), 0) & (t_new - 1)
    d_n = t_n - lax.broadcasted_iota(I32, (nq, LANES), 1)
    s_n = jnp.where((d_n >= 0) & ((d_n & (dil - 1)) == 0), s_n, NEG_INF)
    m = jnp.maximum(jnp.max(s_c, axis=1, keepdims=True), jnp.max(s_n, axis=1, keepdims=True))
    p_c = jnp.exp(s_c - m)
    p_n = jnp.exp(s_n - m)
    l = jnp.sum(p_c, axis=1, keepdims=True) + jnp.sum(p_n, axis=1, keepdims=True)
    o_all = (_dot_nt(p_c.astype(BF16), vt_ref[...].astype(BF16)) + _dot(p_n.astype(BF16), vn.astype(BF16))) / l
    lse_all = m + jnp.log(l)
    o_acc = jnp.zeros((t_new, GROUP_W), F32)
    lse_acc = jnp.zeros((t_new, GROUP_W), F32)
    for h in range(4):
        hm = ch_head == h
        o_acc = jnp.where(hm, o_all[h * t_new:(h + 1) * t_new, :], o_acc)
        lse_acc = jnp.where(hm, lse_all[h * t_new:(h + 1) * t_new, :], lse_acc)
    o_ref[...] = o_acc
    lse_ref[...] = lse_acc


def _swa_sample(q, k_new, v_new, kt_buf, vt_buf, g, win, dil):
    b, t, _ = q.shape
    assert t == 8 and win % LANES == 0 and dil & (dil - 1) == 0
    new = pl.BlockSpec((None, t, GROUP_W), lambda bi: (bi, 0, g))
    buf = pl.BlockSpec((None, GROUP_W, win), lambda bi: (bi, 0, 0))
    out = pl.BlockSpec((None, t, GROUP_W), lambda bi: (bi, 0, 0))
    o, lse, kto, vto = pl.pallas_call(
        functools.partial(_swa_sample_kernel, win=win, dil=dil),
        grid=(b,),
        in_specs=[new, new, new, buf, buf],
        out_specs=[out, out, buf, buf],
        out_shape=[jax.ShapeDtypeStruct((b, t, GROUP_W), F32)] * 2
        + [jax.ShapeDtypeStruct((b, GROUP_W, win), F32)] * 2,
        compiler_params=_params(("arbitrary",), 48),
        name=f"swa_sample_w{win}",
    )(q, k_new, v_new, kt_buf, vt_buf)
    return o.reshape(b * t, GROUP_W), lse.reshape(b * t, GROUP_W), kto, vto


def _conv_kernel(halo_ref, cur_ref, w_ref, b_ref, g_ref, beta_ref, o_ref, ctx_ref, *, zero_first_halo, chunk):
    tm = cur_ref.shape[0]
    halo = halo_ref[...]
    if zero_first_halo:
        halo = jnp.where(pl.program_id(1) == 0, 0.0, halo)
    ctx_ref[pl.ds(0, CONV_HALO), :] = halo
    ctx_ref[pl.ds(CONV_HALO, tm), :] = cur_ref[...]
    first = CONV_HALO - (CONV_WIDTH - 1)
    for c0 in range(0, tm, chunk):
        acc = jnp.zeros((chunk, CONV_CH), F32) + b_ref[...]
        for phase in range(8):
            taps = range(phase, CONV_WIDTH, 8)
            start, shift = divmod(first + phase, 8)
            need = chunk + 8 * (len(taps) - 1)
            if shift == 0:
                window = ctx_ref[pl.ds(c0 + 8 * start, need), :]
            else:
                rows = ctx_ref[pl.ds(c0 + 8 * start, need + 8), :]
                window = pltpu.roll(rows, need + 8 - shift, 0)[0:need, :]
            for a, w in enumerate(taps):
                acc = acc + window[8 * a:8 * a + chunk, :] * w_ref[pl.ds(w, 1), :]
        mu = jnp.mean(acc, axis=-1, keepdims=True)
        xc = acc - mu
        var = jnp.mean(xc * xc, axis=-1, keepdims=True)
        y = xc * lax.rsqrt(var + EPS) * g_ref[...] + beta_ref[...]
        o_ref[pl.ds(c0, chunk), :] = (y * jax.nn.sigmoid(y)).astype(BF16)


def _conv_branch(u, halo, conv_w, conv_b, cn_g, cn_b, tm, halo_from_u):
    b, t, _ = u.shape
    assert t % tm == 0
    ratio = tm // CONV_HALO if halo_from_u else 0
    halo_map = (lambda bi, i: (bi, jnp.maximum(i * ratio - 1, 0), 0)) if halo_from_u else (lambda bi, i: (bi, 0, 0))
    const = lambda bi, i: (0, 0)
    return pl.pallas_call(
        functools.partial(_conv_kernel, zero_first_halo=halo_from_u, chunk=min(tm, 64)),
        grid=(b, t // tm),
        in_specs=[
            pl.BlockSpec((None, CONV_HALO, CONV_CH), halo_map),
            pl.BlockSpec((None, tm, CONV_CH), lambda bi, i: (bi, i, 0)),
            pl.BlockSpec((CONV_WIDTH, CONV_CH), const),
            pl.BlockSpec((1, CONV_CH), const), pl.BlockSpec((1, CONV_CH), const), pl.BlockSpec((1, CONV_CH), const),
        ],
        out_specs=pl.BlockSpec((None, tm, CONV_CH), lambda bi, i: (bi, i, 0)),
        out_shape=jax.ShapeDtypeStruct((b, t, CONV_CH), BF16),
        scratch_shapes=[pltpu.VMEM((CONV_HALO + tm, CONV_CH), F32)],
        compiler_params=_params(("arbitrary", "arbitrary"), 32),
        name="conv_module",
    )(halo, u, conv_w, conv_b, cn_g, cn_b).reshape(b * t, CONV_CH)


def _mem_kv_kernel(mem_ref, g_ref, w_ref, kg_ref, k_ref, v_ref):
    hb = _rms(mem_ref[...], g_ref[...]).astype(BF16)
    for h in range(MEM_HEADS):
        cs = slice(h * MEM_HEAD_DIM, (h + 1) * MEM_HEAD_DIM)
        k_ref[:, cs] = _rms(_dot(hb, w_ref[:, cs]), kg_ref[...])
        v_ref[:, cs] = _dot(hb, w_ref[:, MEM_WIDTH + h * MEM_HEAD_DIM:MEM_WIDTH + (h + 1) * MEM_HEAD_DIM])


def _mem_kv(mem, g, w, kg):
    b = mem.shape[0]
    const = lambda bi: (0, 0)
    blk = pl.BlockSpec((None, MEM_LEN, MEM_WIDTH), lambda bi: (bi, 0, 0))
    return pl.pallas_call(
        _mem_kv_kernel,
        grid=(b,),
        in_specs=[pl.BlockSpec((None, MEM_LEN, D_MODEL), lambda bi: (bi, 0, 0)), pl.BlockSpec((1, D_MODEL), const),
                  pl.BlockSpec((D_MODEL, 2 * MEM_WIDTH), const), pl.BlockSpec((1, MEM_HEAD_DIM), const)],
        out_specs=[blk, blk],
        out_shape=[jax.ShapeDtypeStruct((b, MEM_LEN, MEM_WIDTH), F32)] * 2,
        compiler_params=_params(("arbitrary",), 32),
        name="mem_kv",
    )(mem, g, w, kg)


def _mem_attn_kernel(q_ref, k_ref, v_ref, o_ref):
    for h in range(MEM_HEADS):
        cs = slice(h * MEM_HEAD_DIM, (h + 1) * MEM_HEAD_DIM)
        s = _dot_nt(q_ref[:, cs], k_ref[:, cs].astype(BF16)) * (MEM_HEAD_DIM ** -0.5)
        m = jnp.max(s, axis=1, keepdims=True)
        p = jnp.exp(s - m)
        l = jnp.sum(p, axis=1, keepdims=True)
        o_ref[:, cs] = (_dot(p.astype(BF16), v_ref[:, cs].astype(BF16)) / l).astype(BF16)


def _mem_attn(q, k, v, tm):
    b, t, _ = q.shape
    assert t % tm == 0
    kv = pl.BlockSpec((None, MEM_LEN, MEM_WIDTH), lambda bi, i: (bi, 0, 0))
    qo = pl.BlockSpec((None, tm, MEM_WIDTH), lambda bi, i: (bi, i, 0))
    return pl.pallas_call(
        _mem_attn_kernel,
        grid=(b, t // tm),
        in_specs=[qo, kv, kv],
        out_specs=qo,
        out_shape=jax.ShapeDtypeStruct((b, t, MEM_WIDTH), BF16),
        compiler_params=_params(("arbitrary", "arbitrary"), 32),
        name="mem_attn",
    )(q, k, v).reshape(b * t, MEM_WIDTH)


def _mem_attn_rows_kernel(q_ref, k_ref, v_ref, o_ref):
    t = q_ref.shape[0]
    qf = q_ref[...].astype(F32)
    qs = jnp.concatenate([qf[:, h * MEM_HEAD_DIM:(h + 1) * MEM_HEAD_DIM] for h in range(MEM_HEADS)], axis=0)
    s = _dot_nt(qs.astype(BF16), k_ref[...].astype(BF16)) * (MEM_HEAD_DIM ** -0.5)
    shape = s.shape
    same_head = (lax.broadcasted_iota(I32, shape, 0) // t) == (lax.broadcasted_iota(I32, shape, 1) & (MEM_HEADS - 1))
    s = jnp.where(same_head, s, NEG_INF)
    m = jnp.max(s, axis=1, keepdims=True)
    p = jnp.exp(s - m)
    l = jnp.sum(p, axis=1, keepdims=True)
    o = _dot(p.astype(BF16), v_ref[...].astype(BF16)) / l
    o_ref[...] = jnp.concatenate([o[h * t:(h + 1) * t, :] for h in range(MEM_HEADS)], axis=1).astype(BF16)


def _mem_attn_rows(q, k_rows, v_rows):
    b, t, _ = q.shape
    kv = pl.BlockSpec((None, MEM_LEN * MEM_HEADS, MEM_HEAD_DIM), lambda bi: (bi, 0, 0))
    qo = pl.BlockSpec((None, t, MEM_WIDTH), lambda bi: (bi, 0, 0))
    return pl.pallas_call(
        _mem_attn_rows_kernel,
        grid=(b,),
        in_specs=[qo, kv, kv],
        out_specs=qo,
        out_shape=jax.ShapeDtypeStruct((b, t, MEM_WIDTH), BF16),
        compiler_params=_params(("arbitrary",), 32),
        name="mem_attn_rows",
    )(q, k_rows, v_rows).reshape(b * t, MEM_WIDTH)


def _merge_kernel(x_ref, o1_ref, o2_ref, o3_ref, l1_ref, l2_ref, l3_ref, ob_ref, om_ref,
                  g1_ref, wg_ref, bg_ref, wa_ref, wb_ref, wm_ref, wo_ref, g2_ref, wrh_ref, wrl_ref, br_ref,
                  x1_ref, h2_ref, idx_ref, gw_ref, mix_ref):
    tm = x_ref.shape[0]
    x = x_ref[...]
    hb = _rms(x, g1_ref[...]).astype(BF16)
    la, lb, lc = l1_ref[...], l2_ref[...], l3_ref[...]
    m = jnp.maximum(la, jnp.maximum(lb, lc))
    ea, eb, ec = jnp.exp(la - m), jnp.exp(lb - m), jnp.exp(lc - m)
    oa = ((ea * o1_ref[...] + eb * o2_ref[...] + ec * o3_ref[...]) / (ea + eb + ec)).astype(BF16)
    ob, om = ob_ref[...], om_ref[...]
    for j in range(4):
        cs = slice(j * 256, (j + 1) * 256)

        def gate(branch):
            c0 = branch * D_MODEL + j * 256
            return jax.nn.sigmoid(_dot(hb, wg_ref[:, c0:c0 + 256]) + bg_ref[:, c0:c0 + 256])

        mixed = gate(0) * _dot(oa, wa_ref[:, cs]) + gate(1) * _dot(ob, wb_ref[:, cs]) + gate(2) * _dot(om, wm_ref[:, cs])
        mix_ref[:, cs] = mixed.astype(BF16)
    x1 = x + _dot(mix_ref[...], wo_ref[...])
    x1_ref[...] = x1
    h2 = _rms(x1, g2_ref[...])
    h2_ref[...] = h2
    hi = h2.astype(BF16)
    lo = (h2 - hi.astype(F32)).astype(BF16)
    logits = _dot(hi, wrh_ref[...]) + _dot(lo, wrh_ref[...]) + _dot(hi, wrl_ref[...]) + br_ref[...]
    lane = lax.broadcasted_iota(I32, (tm, LANES), 1)
    logits = jnp.where(lane < N_EXPERTS, logits, NEG_INF)
    vals, idxs = [], []
    for _ in range(TOP_K):
        mk = jnp.max(logits, axis=1, keepdims=True)
        ik = jnp.min(jnp.where(logits == mk, lane, LANES), axis=1, keepdims=True)
        vals.append(mk)
        idxs.append(ik)
        logits = jnp.where(lane == ik, NEG_INF, logits)
    es = [jnp.exp(v - vals[0]) for v in vals]
    den = es[0] + es[1] + es[2] + es[3]
    l4 = lax.broadcasted_iota(I32, (tm, TOP_K), 1)
    idx_out = jnp.zeros((tm, TOP_K), I32)
    gw_out = jnp.zeros((tm, TOP_K), F32)
    for k in range(TOP_K):
        idx_out = jnp.where(l4 == k, idxs[k], idx_out)
        gw_out = jnp.where(l4 == k, es[k] / den, gw_out)
    idx_ref[...] = idx_out
    gw_ref[...] = gw_out


def _merge(x, os_, lses, ob, om, g1, wg, bg, wa, wb, wm, wo, g2, wrh, wrl, br, tm):
    n = x.shape[0]
    assert n % tm == 0
    row = lambda i: (i, 0)
    const = lambda i: (0, 0)
    rows = lambda w: pl.BlockSpec((tm, w), row)
    full = lambda a: pl.BlockSpec(a.shape, const)
    return pl.pallas_call(
        _merge_kernel,
        grid=(n // tm,),
        in_specs=[rows(D_MODEL)] + [rows(GROUP_W)] * 6 + [rows(CONV_CH), rows(MEM_WIDTH)]
        + [full(a) for a in (g1, wg, bg, wa, wb, wm, wo, g2, wrh, wrl, br)],
        out_specs=[rows(D_MODEL), rows(D_MODEL), rows(TOP_K), rows(TOP_K)],
        out_shape=[jax.ShapeDtypeStruct((n, D_MODEL), F32), jax.ShapeDtypeStruct((n, D_MODEL), F32),
                   jax.ShapeDtypeStruct((n, TOP_K), I32), jax.ShapeDtypeStruct((n, TOP_K), F32)],
        scratch_shapes=[pltpu.VMEM((tm, D_MODEL), BF16)],
        compiler_params=_params(("arbitrary",), 56),
        name="merge_router",
    )(x, *os_, *lses, ob, om, g1, wg, bg, wa, wb, wm, wo, g2, wrh, wrl, br)


def _route_kernel(idx_ref, dest_ref, bexp_ref, cnt_ref, carry_ref, *, block_rows):
    phase, i = pl.program_id(0), pl.program_id(1)
    tq = idx_ref.shape[0]
    nb = bexp_ref.shape[0]
    idx = idx_ref[...]
    lane = lax.broadcasted_iota(I32, (tq, LANES), 1)
    member = jnp.zeros((tq, LANES), F32)
    for k in range(TOP_K):
        member = member + (lane == idx[:, k:k + 1]).astype(F32)
    tile_counts = jnp.sum(member, axis=0, keepdims=True)

    @pl.when(phase == 0)
    def _():
        @pl.when(i == 0)
        def _():
            cnt_ref[...] = jnp.zeros_like(cnt_ref)

        cnt_ref[...] += tile_counts
        dest_ref[...] = jnp.zeros_like(dest_ref)
        bexp_ref[...] = jnp.zeros_like(bexp_ref)

    @pl.when(phase == 1)
    def _():
        @pl.when(i == 0)
        def _():
            carry_ref[...] = jnp.zeros_like(carry_ref)

        counts = jnp.broadcast_to(cnt_ref[...], (8, LANES))
        padded = jnp.floor((counts + (block_rows - 1)) * (1.0 / block_rows)) * block_rows
        lane8 = lax.broadcasted_iota(I32, (8, LANES), 1)
        ends = padded
        for s in (1, 2, 4, 8, 16, 32, 64):
            ends = ends + jnp.where(lane8 >= s, pltpu.roll(ends, s, 1), 0.0)
        starts = (ends - padded)[0:1, :]
        r = lax.broadcasted_iota(I32, (tq, tq), 0)
        c = lax.broadcasted_iota(I32, (tq, tq), 1)
        earlier = _dot((c < r).astype(BF16), member.astype(BF16)) + carry_ref[...]
        carry_ref[...] += tile_counts
        base = earlier + starts
        l4 = lax.broadcasted_iota(I32, (tq, TOP_K), 1)
        dest = jnp.zeros((tq, TOP_K), F32)
        for k in range(TOP_K):
            dk = jnp.sum(jnp.where(lane == idx[:, k:k + 1], base, 0.0), axis=1, keepdims=True)
            dest = jnp.where(l4 == k, dk, dest)
        dest_ref[...] = dest.astype(I32)
        first_row = (lax.broadcasted_iota(I32, (nb, LANES), 0) * block_rows).astype(F32)
        lane_nb = lax.broadcasted_iota(I32, (nb, LANES), 1)
        done = jnp.where((ends[0:1, :] <= first_row) & (lane_nb < N_EXPERTS), 1.0, 0.0)
        bexp_ref[...] = jnp.minimum(jnp.sum(done, axis=1, keepdims=True), N_EXPERTS - 1.0).astype(I32)


def _route(idx, block_rows, n_blocks, tq):
    n = idx.shape[0]
    assert n % tq == 0
    nb_pad = -(-n_blocks // 8) * 8
    dest, bexp = pl.pallas_call(
        functools.partial(_route_kernel, block_rows=block_rows),
        grid=(2, n // tq),
        in_specs=[pl.BlockSpec((tq, TOP_K), lambda p, i: (i, 0))],
        out_specs=[pl.BlockSpec((tq, TOP_K), lambda p, i: (i * p, 0)),
                   pl.BlockSpec((nb_pad, 1), lambda p, i: (0, 0))],
        out_shape=[jax.ShapeDtypeStruct((n, TOP_K), I32), jax.ShapeDtypeStruct((nb_pad, 1), I32)],
        scratch_shapes=[pltpu.VMEM((1, LANES), F32), pltpu.VMEM((1, LANES), F32)],
        compiler_params=_params(("arbitrary", "arbitrary"), 32),
        name="route",
    )(idx)
    return dest.reshape(n * TOP_K), bexp.reshape(nb_pad)[:n_blocks]


def _dispatch_kernel(dest_ref, h_ref, xs_in_ref, xs_ref, sem, *, tm):
    del xs_in_ref
    i = pl.program_id(0)
    last = pl.num_programs(0) - 1

    def row_copy(n, d):
        return pltpu.make_async_copy(h_ref.at[pl.ds(n, 1)], xs_ref.at[pl.ds(d, 1)], sem)

    def issue(r, carry):
        for k in range(TOP_K):
            row_copy(i * tm + r, dest_ref[r * TOP_K + k]).start(priority=k % 2)
        return carry

    def drain(r, carry):
        for k in range(TOP_K):
            row_copy(0, 0).wait()
        return carry

    lax.fori_loop(0, tm, issue, 0, unroll=2)

    @pl.when(i > 0)
    def _():
        lax.fori_loop(0, tm, drain, 0)

    @pl.when(i == last)
    def _():
        lax.fori_loop(0, tm, drain, 0)


def _dispatch(h2, dest, n_slots, tm):
    n = h2.shape[0]
    assert n % tm == 0
    zeros = jnp.zeros((n_slots, D_MODEL), F32)
    return pl.pallas_call(
        functools.partial(_dispatch_kernel, tm=tm),
        grid=(n // tm,),
        in_specs=[pl.BlockSpec((tm * TOP_K,), lambda i: (i,), memory_space=pltpu.SMEM),
                  pl.BlockSpec(memory_space=pl.ANY),
                  pl.BlockSpec(memory_space=pl.ANY)],
        out_specs=pl.BlockSpec(memory_space=pl.ANY),
        out_shape=jax.ShapeDtypeStruct((n_slots, D_MODEL), F32),
        scratch_shapes=[pltpu.SemaphoreType.DMA],
        input_output_aliases={2: 0},
        compiler_params=_params(("arbitrary",), 32),
        name="dispatch",
    )(dest, h2, zeros)


def _moe_kernel(be_ref, xs_ref, wg_ref, bg_ref, wu_ref, bu_ref, wd_ref, bd_ref, yb_ref, wg_s, wu_s, wd_s):
    i = pl.program_id(0)
    changed = (i == 0) | (be_ref[i] != be_ref[jnp.maximum(i - 1, 0)])

    @pl.when(changed)
    def _():
        for r0 in range(0, D_MODEL, 128):
            rs = pl.ds(r0, 128)
            wg_s[rs, :] = wg_ref[rs, :].astype(BF16)
            wu_s[rs, :] = wu_ref[rs, :].astype(BF16)
            wd_s[rs, :] = wd_ref[rs, :].astype(BF16)

    xb = xs_ref[...].astype(BF16)
    acc = jnp.zeros(yb_ref.shape, F32) + bd_ref[...]
    for c in range(D_FF // 256):
        cs = slice(c * 256, (c + 1) * 256)
        g = jnp.minimum(_dot(xb, wg_s[:, cs]) + bg_ref[:, cs], SWIGLU_LIMIT)
        u = jnp.clip(_dot(xb, wu_s[:, cs]) + bu_ref[:, cs], -SWIGLU_LIMIT, SWIGLU_LIMIT)
        act = g * jax.nn.sigmoid(SWIGLU_ALPHA * g) * (u + 1.0)
        acc = acc + _dot(act.astype(BF16), wd_s[cs, :])
    yb_ref[...] = acc


def _moe(xs, bexp, wg, bg, wu, bu, wd, bd, bm):
    n_slots = xs.shape[0]
    assert n_slots % bm == 0
    wspec = pl.BlockSpec((None, D_MODEL, D_FF), lambda i, be: (be[i], 0, 0))
    bspec = pl.BlockSpec((None, 1, D_FF), lambda i, be: (be[i], 0, 0))
    rows = pl.BlockSpec((bm, D_MODEL), lambda i, be: (i, 0))
    grid_spec = pltpu.PrefetchScalarGridSpec(
        num_scalar_prefetch=1,
        grid=(n_slots // bm,),
        in_specs=[rows, wspec, bspec, wspec, bspec, wspec, bspec],
        out_specs=rows,
        scratch_shapes=[pltpu.VMEM((D_MODEL, D_FF), BF16)] * 3,
    )
    return pl.pallas_call(
        _moe_kernel,
        grid_spec=grid_spec,
        out_shape=jax.ShapeDtypeStruct((n_slots, D_MODEL), F32),
        compiler_params=_params(("arbitrary",), 56),
        name="moe_experts",
    )(bexp, xs, wg, bg.reshape(N_EXPERTS, 1, D_FF), wu, bu.reshape(N_EXPERTS, 1, D_FF),
      wd, bd.reshape(N_EXPERTS, 1, D_MODEL))


def _combine_kernel(dest_ref, dest_next_ref, x1_ref, gw_ref, yb_ref, y_ref, rows_ref, sems):
    tm = x1_ref.shape[0]
    i = pl.program_id(0)
    slot = i % 2

    def row_copy(buf, r, k, d):
        return pltpu.make_async_copy(yb_ref.at[pl.ds(d, 1)], rows_ref.at[buf, k, pl.ds(r, 1)], sems.at[buf])

    def issue(dref, buf):
        def body(r, carry):
            for k in range(TOP_K):
                row_copy(buf, r, k, dref[r * TOP_K + k]).start(priority=k % 2)
            return carry

        lax.fori_loop(0, tm, body, 0, unroll=2)

    def drain(r, carry):
        for k in range(TOP_K):
            row_copy(slot, 0, k, 0).wait()
        return carry

    @pl.when(i == 0)
    def _():
        issue(dest_ref, slot)

    @pl.when(i + 1 < pl.num_programs(0))
    def _():
        issue(dest_next_ref, 1 - slot)

    lax.fori_loop(0, tm, drain, 0)
    gw = gw_ref[...]
    y = x1_ref[...]
    for k in range(TOP_K):
        y = y + gw[:, k:k + 1] * rows_ref[slot, k]
    y_ref[...] = y


def _combine(x1, gw, dest, yb, tm):
    n = x1.shape[0]
    assert n % tm == 0
    steps = n // tm
    return pl.pallas_call(
        _combine_kernel,
        grid=(steps,),
        in_specs=[pl.BlockSpec((tm * TOP_K,), lambda i: (i,), memory_space=pltpu.SMEM),
                  pl.BlockSpec((tm * TOP_K,), lambda i: (jnp.minimum(i + 1, steps - 1),), memory_space=pltpu.SMEM),
                  pl.BlockSpec((tm, D_MODEL), lambda i: (i, 0)),
                  pl.BlockSpec((tm, TOP_K), lambda i: (i, 0)),
                  pl.BlockSpec(memory_space=pl.ANY)],
        out_specs=pl.BlockSpec((tm, D_MODEL), lambda i: (i, 0)),
        out_shape=jax.ShapeDtypeStruct((n, D_MODEL), F32),
        scratch_shapes=[pltpu.VMEM((2, TOP_K, tm, D_MODEL), F32), pltpu.SemaphoreType.DMA((2,))],
        compiler_params=_params(("arbitrary",), 32),
        name="combine",
    )(dest, dest, x1, gw, yb)


def _rope_tables(positions):
    half = ROPE_DIM // 2
    inv = jnp.power(jnp.float32(ROPE_THETA), -jnp.arange(half, dtype=F32) / half)
    ang = positions.astype(F32)[:, None] * inv[None, :]
    cos, sin = jnp.cos(ang), jnp.sin(ang)
    t = positions.shape[0]
    z8 = jnp.zeros((t, half), F32)
    rest0 = jnp.zeros((t, HEAD_DIM_A - ROPE_DIM), F32)
    a = jnp.concatenate([cos, cos, jnp.ones((t, HEAD_DIM_A - ROPE_DIM), F32)], axis=1)
    bp = jnp.concatenate([z8, sin, rest0], axis=1)
    bm = jnp.concatenate([-sin, z8, rest0], axis=1)
    return tuple(jnp.tile(m, (1, LANES // HEAD_DIM_A)) for m in (a, bp, bm))


def _segment_mean_matrix(seg):
    i = jnp.arange(256)
    return jnp.where((i[:, None] // seg) == (i[None, :] // seg), 1.0 / seg, 0.0).astype(BF16)


def _layer_weights(l, norm1_g, w_in, b_gates, qn_a, kn_a, conv_w, conv_b, cn_g, cn_b, qn_m,
                   w_proj_a, w_proj_b, w_proj_m, w_out, norm2_g, w_router, b_router,
                   w_gate, b_gate, w_up, b_up, w_down, b_down):
    wr = jnp.pad(w_router[l], ((0, 0), (0, LANES - N_EXPERTS)))
    wrh = wr.astype(BF16)
    return dict(
        g1=norm1_g[l][None, :],
        w1=w_in[l][:, :W1_COLS].astype(BF16),
        wgates=w_in[l][:, W1_COLS:].astype(BF16),
        bgates=b_gates[l][None, :],
        qg=jnp.tile(qn_a[l], 256 // HEAD_DIM_A)[None, :],
        kg=jnp.tile(kn_a[l], 256 // HEAD_DIM_A)[None, :],
        mg=jnp.tile(qn_m[l], 256 // MEM_HEAD_DIM)[None, :],
        conv_w=conv_w[l], conv_b=conv_b[l][None, :], cn_g=cn_g[l][None, :], cn_b=cn_b[l][None, :],
        wa=w_proj_a[l].astype(BF16), wb=w_proj_b[l].astype(BF16), wm=w_proj_m[l].astype(BF16),
        wo=w_out[l].astype(BF16), g2=norm2_g[l][None, :],
        wrh=wrh, wrl=(wr - wrh.astype(F32)).astype(BF16),
        br=jnp.pad(b_router[l], (0, LANES - N_EXPERTS))[None, :],
        w_gate=w_gate[l], b_gate=b_gate[l], w_up=w_up[l], b_up=b_up[l], w_down=w_down[l], b_down=b_down[l],
        seg64=_segment_mean_matrix(HEAD_DIM_A), seg128=_segment_mean_matrix(MEM_HEAD_DIM),
    )


def _token_mixer_tail(x2d, os_, lses, ob, om, w, tm_merge, moe_rows, tm_route, tm_rows):
    n = x2d.shape[0]
    x1, h2, idx, gw = _merge(x2d, os_, lses, ob, om, w["g1"], w["wgates"], w["bgates"], w["wa"], w["wb"], w["wm"],
                             w["wo"], w["g2"], w["wrh"], w["wrl"], w["br"], tm_merge)
    n_blocks = -(-(n * TOP_K) // moe_rows) + N_EXPERTS
    dest, bexp = _route(idx, moe_rows, n_blocks, tm_route)
    xs = _dispatch(h2, dest, n_blocks * moe_rows, tm_rows)
    yb = _moe(xs, bexp, w["w_gate"], w["b_gate"], w["w_up"], w["b_up"], w["w_down"], w["b_down"], moe_rows)
    return _combine(x1, gw, dest, yb, min(tm_rows, 128))


def _prompt_layer(x, mem, w, mem_norm_g, w_mem_kv, kn_m):
    b, s, _ = x.shape
    n = b * s
    x2d = x.reshape(n, D_MODEL)
    tabs = _rope_tables(jnp.arange(s, dtype=I32))
    q, k, v, u, qm = _in_proj(x2d, w["g1"], w["w1"], w["qg"], w["kg"], w["mg"], tabs, w["seg64"], w["seg128"], 512, F32)
    q3, k3, v3 = (t.reshape(b, s, WIDTH_A) for t in (q, k, v))
    os_, lses, caches = [], [], []
    for g, (win, dil) in enumerate(SWA_GROUPS):
        o, lse = _swa_prompt(q3, k3, v3, g, dil)
        os_.append(o)
        lses.append(lse)
        cs = slice(g * GROUP_W, (g + 1) * GROUP_W)
        caches += [k3[:, s - win:, cs].reshape(b, win, 4, HEAD_DIM_A), v3[:, s - win:, cs].reshape(b, win, 4, HEAD_DIM_A)]
    u3 = u.reshape(b, s, CONV_CH)
    ob = _conv_branch(u3, u3, w["conv_w"], w["conv_b"], w["cn_g"], w["cn_b"], 256, True)
    mk, mv = _mem_kv(mem, mem_norm_g[None, :], w_mem_kv.astype(BF16), kn_m[None, :])
    om = _mem_attn(qm.reshape(b, s, MEM_WIDTH), mk, mv, 512)
    y = _token_mixer_tail(x2d, os_, lses, ob, om, w, 512, 512, 512, 256)
    state = caches + [u3[:, s - (CONV_WIDTH - 1):], mk.reshape(b, MEM_LEN, MEM_HEADS, MEM_HEAD_DIM),
                      mv.reshape(b, MEM_LEN, MEM_HEADS, MEM_HEAD_DIM)]
    return y.reshape(b, s, D_MODEL), state


def _sample_layer(x, mem_k, mem_v, bufs, conv_state, w):
    b, t, _ = x.shape
    n = b * t
    x2d = x.reshape(n, D_MODEL)
    tabs = _rope_tables(jnp.tile(PAST_LEN + jnp.arange(t, dtype=I32), b))
    q, k, v, u, qm = _in_proj(x2d, w["g1"], w["w1"], w["qg"], w["kg"], w["mg"], tabs, w["seg64"], w["seg128"], 256, BF16)
    q3, k3, v3 = (a.reshape(b, t, WIDTH_A) for a in (q, k, v))
    os_, lses, caches = [], [], []
    for g, (win, dil) in enumerate(SWA_GROUPS):
        to_cm = lambda a: jnp.transpose(a, (0, 2, 3, 1)).reshape(b, GROUP_W, win)
        from_cm = lambda a: jnp.transpose(a.reshape(b, 4, HEAD_DIM_A, win), (0, 3, 1, 2))
        o, lse, kto, vto = _swa_sample(q3, k3, v3, to_cm(bufs[2 * g]), to_cm(bufs[2 * g + 1]), g, win, dil)
        os_.append(o)
        lses.append(lse)
        caches += [from_cm(kto), from_cm(vto)]
    u3 = u.reshape(b, t, CONV_CH)
    halo = jnp.pad(conv_state, ((0, 0), (CONV_HALO - (CONV_WIDTH - 1), 0), (0, 0)))
    ob = _conv_branch(u3, halo, w["conv_w"], w["conv_b"], w["cn_g"], w["cn_b"], t, False)
    om = _mem_attn_rows(qm.reshape(b, t, MEM_WIDTH), mem_k.reshape(b, MEM_LEN * MEM_HEADS, MEM_HEAD_DIM),
                        mem_v.reshape(b, MEM_LEN * MEM_HEADS, MEM_HEAD_DIM))
    y = _token_mixer_tail(x2d, os_, lses, ob, om, w, 256, 128, 512, 128)
    new_conv = jnp.concatenate([conv_state, u3], axis=1)[:, t:]
    return y.reshape(b, t, D_MODEL), caches + [new_conv]


def kernel(x_prompt, x_sample, mem_prompt, cache_swa_k_w128, cache_swa_v_w128, cache_swa_k_w512, cache_swa_v_w512, cache_swa_k_w2048, cache_swa_v_w2048, state_conv, cache_mem_k, cache_mem_v, norm1_g, w_in, b_gates, qn_a, kn_a, conv_w, conv_b, cn_g, cn_b, mem_norm_g, w_mem_kv, qn_m, kn_m, w_proj_a, w_proj_b, w_proj_m, w_out, norm2_g, w_router, b_router, w_gate, b_gate, w_up, b_up, w_down, b_down):
    depth = norm1_g.shape[0]
    y_p, y_s = x_prompt, x_sample
    st_p, st_s = [], []
    for l in range(depth):
        w = _layer_weights(l, norm1_g, w_in, b_gates, qn_a, kn_a, conv_w, conv_b, cn_g, cn_b, qn_m,
                           w_proj_a, w_proj_b, w_proj_m, w_out, norm2_g, w_router, b_router,
                           w_gate, b_gate, w_up, b_up, w_down, b_down)
        y_p, state_p = _prompt_layer(y_p, mem_prompt, w, mem_norm_g[l], w_mem_kv[l], kn_m[l])
        bufs = (cache_swa_k_w128[l], cache_swa_v_w128[l], cache_swa_k_w512[l], cache_swa_v_w512[l],
                cache_swa_k_w2048[l], cache_swa_v_w2048[l])
        y_s, state_s = _sample_layer(y_s, cache_mem_k[l], cache_mem_v[l], bufs, state_conv[l], w)
        st_p.append(state_p)
        st_s.append(state_s)
    outs_p = [jnp.stack(a) for a in zip(*st_p)]
    outs_s = [jnp.stack(a) for a in zip(*st_s)]
    return (y_p, y_s, *outs_p, *outs_s)
```

```python
import functools

import jax
import jax.numpy as jnp
from jax import lax
from jax.experimental import pallas as pl
from jax.experimental.pallas import tpu as pltpu

F32 = jnp.float32
BF16 = jnp.bfloat16
I32 = jnp.int32

D_MODEL = 1024
HEAD_DIM_A = 64
GROUP_W = 256
N_GROUPS = 3
WIDTH_A = N_GROUPS * GROUP_W
SWA_GROUPS = ((128, 1), (512, 4), (2048, 16))
SWA_BLOCK = 128
ROPE_DIM = 16
ROPE_THETA = 500000.0
CONV_CH = 512
CONV_WIDTH = 31
CONV_HALO = 32
MEM_LEN = 256
MEM_HEADS = 4
MEM_HEAD_DIM = 128
MEM_WIDTH = 512
N_EXPERTS = 32
TOP_K = 4
D_FF = 1024
SWIGLU_LIMIT = 7.0
SWIGLU_ALPHA = 1.702
EPS = 1e-6
PAST_LEN = 8192
LANES = 128
ROW_TILE = D_MODEL // LANES
W1_COLS = 3 * WIDTH_A + 2 * CONV_CH + MEM_WIDTH
NEG_INF = float("-inf")
MIB = 2 ** 20


def _params(semantics, vmem_mib):
    return pltpu.CompilerParams(dimension_semantics=semantics, vmem_limit_bytes=vmem_mib * MIB)


def _rms(x, gain):
    return x * lax.rsqrt(jnp.mean(x * x, axis=-1, keepdims=True) + EPS) * gain


def _dot(a, b):
    return jnp.dot(a, b, preferred_element_type=F32)


def _dot_nt(a, b):
    return lax.dot_general(a, b, (((1,), (1,)), ((), ())), preferred_element_type=F32)


def _in_proj_kernel(x_ref, g1_ref, w_ref, qg_ref, kg_ref, mg_ref, ra_ref, rp_ref, rm_ref, s64_ref, s128_ref,
                    q_ref, k_ref, v_ref, u_ref, qm_ref):
    hb = _rms(x_ref[...], g1_ref[...]).astype(BF16)
    ra, rp, rm = ra_ref[...], rp_ref[...], rm_ref[...]

    def proj(c0):
        return _dot(hb, w_ref[:, c0:c0 + 256])

    def head_norm(p, seg_ref, gain):
        ms = _dot((p * p).astype(BF16), seg_ref[...])
        return p * lax.rsqrt(ms + EPS) * gain

    def rope(p):
        outs = []
        for j in range(2):
            pj = p[:, j * LANES:(j + 1) * LANES]
            outs.append(pj * ra + pltpu.roll(pj, 8, 1) * rp + pltpu.roll(pj, LANES - 8, 1) * rm)
        return jnp.concatenate(outs, axis=1)

    for c in range(3):
        cs = slice(c * 256, (c + 1) * 256)
        q_ref[:, cs] = rope(head_norm(proj(c * 256), s64_ref, qg_ref[...])).astype(q_ref.dtype)
        k_ref[:, cs] = rope(head_norm(proj(WIDTH_A + c * 256), s64_ref, kg_ref[...]))
        v_ref[:, cs] = proj(2 * WIDTH_A + c * 256)
    for c in range(2):
        cs = slice(c * 256, (c + 1) * 256)
        a = proj(3 * WIDTH_A + c * 256)
        gate = proj(3 * WIDTH_A + CONV_CH + c * 256)
        u_ref[:, cs] = a * jax.nn.sigmoid(gate)
        qm_ref[:, cs] = head_norm(proj(3 * WIDTH_A + 2 * CONV_CH + c * 256), s128_ref, mg_ref[...]).astype(BF16)


def _in_proj(x, g1, w1, qg, kg, mg, tabs, seg64, seg128, tm, q_dtype):
    n = x.shape[0]
    t_rows = tabs[0].shape[0]
    assert n % tm == 0 and t_rows % tm == 0
    t_blocks = t_rows // tm
    row = lambda i: (i, 0)
    const = lambda i: (0, 0)
    tab = lambda i: (i % t_blocks, 0)
    return pl.pallas_call(
        _in_proj_kernel,
        grid=(n // tm,),
        in_specs=[
            pl.BlockSpec((tm, D_MODEL), row),
            pl.BlockSpec((1, D_MODEL), const),
            pl.BlockSpec((D_MODEL, W1_COLS), const),
            pl.BlockSpec((1, 256), const), pl.BlockSpec((1, 256), const), pl.BlockSpec((1, 256), const),
            pl.BlockSpec((tm, LANES), tab), pl.BlockSpec((tm, LANES), tab), pl.BlockSpec((tm, LANES), tab),
            pl.BlockSpec((256, 256), const), pl.BlockSpec((256, 256), const),
        ],
        out_specs=[
            pl.BlockSpec((tm, WIDTH_A), row), pl.BlockSpec((tm, WIDTH_A), row), pl.BlockSpec((tm, WIDTH_A), row),
            pl.BlockSpec((tm, CONV_CH), row), pl.BlockSpec((tm, MEM_WIDTH), row),
        ],
        out_shape=[
            jax.ShapeDtypeStruct((n, WIDTH_A), q_dtype), jax.ShapeDtypeStruct((n, WIDTH_A), F32),
            jax.ShapeDtypeStruct((n, WIDTH_A), F32), jax.ShapeDtypeStruct((n, CONV_CH), F32),
            jax.ShapeDtypeStruct((n, MEM_WIDTH), BF16),
        ],
        compiler_params=_params(("arbitrary",), 48),
        name="in_proj",
    )(x, g1, w1, qg, kg, mg, *tabs, seg64, seg128)


def _swa_prompt_kernel(q_ref, k_ref, v_ref, o_ref, lse_ref, q_s, k_s, v_s, o_s, l_s, *, dil, nsub, unroll):
    n = pl.program_id(1)
    span = SWA_BLOCK * dil
    cur = n % 2
    prv = 1 - cur
    for s in range(2):
        ls = slice(s * LANES, (s + 1) * LANES)
        q_s[s] = q_ref[:, ls]
        k_s[cur, s] = k_ref[:, ls]
        v_s[cur, s] = v_ref[:, ls]

    @pl.when(n == 0)
    def _():
        k_s[prv] = jnp.zeros(k_s.shape[1:], F32)
        v_s[prv] = jnp.zeros(v_s.shape[1:], F32)

    shape = (SWA_BLOCK, 2 * SWA_BLOCK)
    col = lax.broadcasted_iota(I32, shape, 1)
    off = lax.broadcasted_iota(I32, shape, 0) + SWA_BLOCK - col
    band = (off >= 0) & (off <= SWA_BLOCK)
    band_first = band & ((n > 0) | (col >= SWA_BLOCK))
    head = col // HEAD_DIM_A

    def residue(r, carry):
        def both(ref, lead, rows):
            return jnp.concatenate([ref[(*lead, s, rows, slice(None))] for s in range(2)], axis=1)

        for j in range(nsub):
            rows = pl.ds(j * span + r, SWA_BLOCK, stride=dil)
            before = (prv, pl.ds((nsub - 1) * span + r, SWA_BLOCK, stride=dil)) if j == 0 else \
                     (cur, pl.ds((j - 1) * span + r, SWA_BLOCK, stride=dil))
            q = both(q_s, (), rows).astype(BF16)
            kk = jnp.concatenate([both(k_s, before[:1], before[1]), both(k_s, (cur,), rows)], axis=0).astype(BF16)
            vv = jnp.concatenate([both(v_s, before[:1], before[1]), both(v_s, (cur,), rows)], axis=0).astype(BF16)
            mask = band_first if j == 0 else band
            hms = [head == h for h in range(4)]
            ss = [jnp.where(mask, _dot_nt(jnp.where(hm, q, jnp.zeros_like(q)), kk) * (HEAD_DIM_A ** -0.5), NEG_INF)
                  for hm in hms]
            ms = [jnp.max(s, axis=1, keepdims=True) for s in ss]
            ps = [jnp.exp(s - m) for s, m in zip(ss, ms)]
            ls = [jnp.sum(p, axis=1, keepdims=True) for p in ps]
            ohs = [_dot(p.astype(BF16), vv) for p in ps]
            o_acc = jnp.zeros(shape, F32)
            lse_acc = jnp.zeros(shape, F32)
            for hm, oh, m, l in zip(hms, ohs, ms, ls):
                o_acc = jnp.where(hm, oh / l, o_acc)
                lse_acc = jnp.where(hm, m + jnp.log(l), lse_acc)
            for s in range(2):
                ls = slice(s * LANES, (s + 1) * LANES)
                o_s[s, rows, :] = o_acc[:, ls]
                l_s[s, rows, :] = lse_acc[:, ls]
        return carry

    lax.fori_loop(0, dil, residue, 0, unroll=unroll)
    for s in range(2):
        ls = slice(s * LANES, (s + 1) * LANES)
        o_ref[:, ls] = o_s[s]
        lse_ref[:, ls] = l_s[s]


def _swa_prompt(q, k, v, g, dil):
    b, s, _ = q.shape
    nsub = max(1, 4 // dil)
    unroll = 2 if nsub == 1 else 1
    t = nsub * dil * SWA_BLOCK
    assert s % t == 0
    inp = pl.BlockSpec((None, t, GROUP_W), lambda bi, n: (bi, n, g))
    out = pl.BlockSpec((None, t, GROUP_W), lambda bi, n: (bi, n, 0))
    slab = lambda *lead: pltpu.VMEM((*lead, 2, t, LANES), F32)
    o, lse = pl.pallas_call(
        functools.partial(_swa_prompt_kernel, dil=dil, nsub=nsub, unroll=unroll),
        grid=(b, s // t),
        in_specs=[inp, inp, inp],
        out_specs=[out, out],
        out_shape=[jax.ShapeDtypeStruct((b, s, GROUP_W), F32)] * 2,
        scratch_shapes=[slab(), slab(2), slab(2), slab(), slab()],
        compiler_params=_params(("arbitrary", "arbitrary"), 48),
        name=f"swa_prompt_d{dil}",
    )(q, k, v)
    return o.reshape(b * s, GROUP_W), lse.reshape(b * s, GROUP_W)


def _swa_sample_kernel(q_ref, kn_ref, vn_ref, kt_ref, vt_ref, o_ref, lse_ref, kto_ref, vto_ref, *, win, dil):
    t_new = q_ref.shape[0]
    n_tiles = win // LANES
    pad = jnp.zeros((LANES - t_new, GROUP_W), F32)
    kn = jnp.concatenate([kn_ref[...], pad], axis=0)
    vn = jnp.concatenate([vn_ref[...], pad], axis=0)

    def channel_major(a):
        return jnp.concatenate([a[:, :LANES].T, a[:, LANES:].T], axis=0)

    knt, vnt = channel_major(kn), channel_major(vn)
    lane = lax.broadcasted_iota(I32, (GROUP_W, LANES), 1)
    keep = lane < LANES - t_new
    for src_ref, new_t, dst_ref in ((kt_ref, knt, kto_ref), (vt_ref, vnt, vto_ref)):
        nxt = pltpu.roll(src_ref[:, 0:LANES], LANES - t_new, 1)
        for j in range(n_tiles):
            this = nxt
            following = new_t if j + 1 == n_tiles else src_ref[:, (j + 1) * LANES:(j + 2) * LANES]
            nxt = pltpu.roll(following, LANES - t_new, 1)
            dst_ref[:, j * LANES:(j + 1) * LANES] = jnp.where(keep, this, nxt)

    qf = q_ref[...].astype(F32)
    ch_head = lax.broadcasted_iota(I32, (t_new, GROUP_W), 1) // HEAD_DIM_A
    qm = jnp.concatenate([jnp.where(ch_head == h, qf, 0.0) for h in range(4)], axis=0).astype(BF16)
    nq = 4 * t_new
    scale = HEAD_DIM_A ** -0.5
    s_c = _dot(qm, kt_ref[...].astype(BF16)) * scale
    s_n = _dot(qm, knt.astype(BF16)) * scale
    t_c = lax.broadcasted_iota(I32, (nq, win), 0) & (t_new - 1)
    d_c = win + t_c - lax.broadcasted_iota(I32, (nq, win), 1)
    s_c = jnp.where((d_c <= win) & ((d_c & (dil - 1)) == 0), s_c, NEG_INF)
    t_n = lax.broadcasted_iota(I32, (nq, LANES), 0) & (t_new - 1)
    d_n = t_n - lax.broadcasted_iota(I32, (nq, LANES), 1)
    s_n = jnp.where((d_n >= 0) & ((d_n & (dil - 1)) == 0), s_n, NEG_INF)
    m = jnp.maximum(jnp.max(s_c, axis=1, keepdims=True), jnp.max(s_n, axis=1, keepdims=True))
    p_c = jnp.exp(s_c - m)
    p_n = jnp.exp(s_n - m)
    l = jnp.sum(p_c, axis=1, keepdims=True) + jnp.sum(p_n, axis=1, keepdims=True)
    o_all = (_dot_nt(p_c.astype(BF16), vt_ref[...].astype(BF16)) + _dot(p_n.astype(BF16), vn.astype(BF16))) / l
    lse_all = m + jnp.log(l)
    o_acc = jnp.zeros((t_new, GROUP_W), F32)
    lse_acc = jnp.zeros((t_new, GROUP_W), F32)
    for h in range(4):
        hm = ch_head == h
        o_acc = jnp.where(hm, o_all[h * t_new:(h + 1) * t_new, :], o_acc)
        lse_acc = jnp.where(hm, lse_all[h * t_new:(h + 1) * t_new, :], lse_acc)
    o_ref[...] = o_acc
    lse_ref[...] = lse_acc


def _swa_sample(q, k_new, v_new, kt_buf, vt_buf, g, win, dil):
    b, t, _ = q.shape
    assert t == 8 and win % LANES == 0 and dil & (dil - 1) == 0
    new = pl.BlockSpec((None, t, GROUP_W), lambda bi: (bi, 0, g))
    buf = pl.BlockSpec((None, GROUP_W, win), lambda bi: (bi, 0, 0))
    out = pl.BlockSpec((None, t, GROUP_W), lambda bi: (bi, 0, 0))
    o, lse, kto, vto = pl.pallas_call(
        functools.partial(_swa_sample_kernel, win=win, dil=dil),
        grid=(b,),
        in_specs=[new, new, new, buf, buf],
        out_specs=[out, out, buf, buf],
        out_shape=[jax.ShapeDtypeStruct((b, t, GROUP_W), F32)] * 2
        + [jax.ShapeDtypeStruct((b, GROUP_W, win), F32)] * 2,
        compiler_params=_params(("arbitrary",), 48),
        name=f"swa_sample_w{win}",
    )(q, k_new, v_new, kt_buf, vt_buf)
    return o.reshape(b * t, GROUP_W), lse.reshape(b * t, GROUP_W), kto, vto


def _conv_kernel(halo_ref, cur_ref, w_ref, b_ref, g_ref, beta_ref, o_ref, ctx_ref, *, zero_first_halo, chunk):
    tm = cur_ref.shape[0]
    halo = halo_ref[...]
    if zero_first_halo:
        halo = jnp.where(pl.program_id(1) == 0, 0.0, halo)
    ctx_ref[pl.ds(0, CONV_HALO), :] = halo
    ctx_ref[pl.ds(CONV_HALO, tm), :] = cur_ref[...]
    first = CONV_HALO - (CONV_WIDTH - 1)
    for c0 in range(0, tm, chunk):
        acc = jnp.zeros((chunk, CONV_CH), F32) + b_ref[...]
        for phase in range(8):
            taps = range(phase, CONV_WIDTH, 8)
            start, shift = divmod(first + phase, 8)
            need = chunk + 8 * (len(taps) - 1)
            if shift == 0:
                window = ctx_ref[pl.ds(c0 + 8 * start, need), :]
            else:
                rows = ctx_ref[pl.ds(c0 + 8 * start, need + 8), :]
                window = pltpu.roll(rows, need + 8 - shift, 0)[0:need, :]
            for a, w in enumerate(taps):
                acc = acc + window[8 * a:8 * a + chunk, :] * w_ref[pl.ds(w, 1), :]
        mu = jnp.mean(acc, axis=-1, keepdims=True)
        xc = acc - mu
        var = jnp.mean(xc * xc, axis=-1, keepdims=True)
        y = xc * lax.rsqrt(var + EPS) * g_ref[...] + beta_ref[...]
        o_ref[pl.ds(c0, chunk), :] = (y * jax.nn.sigmoid(y)).astype(BF16)


def _conv_branch(u, halo, conv_w, conv_b, cn_g, cn_b, tm, halo_from_u):
    b, t, _ = u.shape
    assert t % tm == 0
    ratio = tm // CONV_HALO if halo_from_u else 0
    halo_map = (lambda bi, i: (bi, jnp.maximum(i * ratio - 1, 0), 0)) if halo_from_u else (lambda bi, i: (bi, 0, 0))
    const = lambda bi, i: (0, 0)
    return pl.pallas_call(
        functools.partial(_conv_kernel, zero_first_halo=halo_from_u, chunk=min(tm, 64)),
        grid=(b, t // tm),
        in_specs=[
            pl.BlockSpec((None, CONV_HALO, CONV_CH), halo_map),
            pl.BlockSpec((None, tm, CONV_CH), lambda bi, i: (bi, i, 0)),
            pl.BlockSpec((CONV_WIDTH, CONV_CH), const),
            pl.BlockSpec((1, CONV_CH), const), pl.BlockSpec((1, CONV_CH), const), pl.BlockSpec((1, CONV_CH), const),
        ],
        out_specs=pl.BlockSpec((None, tm, CONV_CH), lambda bi, i: (bi, i, 0)),
        out_shape=jax.ShapeDtypeStruct((b, t, CONV_CH), BF16),
        scratch_shapes=[pltpu.VMEM((CONV_HALO + tm, CONV_CH), F32)],
        compiler_params=_params(("arbitrary", "arbitrary"), 32),
        name="conv_module",
    )(halo, u, conv_w, conv_b, cn_g, cn_b).reshape(b * t, CONV_CH)


def _mem_kv_kernel(mem_ref, g_ref, w_ref, kg_ref, k_ref, v_ref):
    hb = _rms(mem_ref[...], g_ref[...]).astype(BF16)
    for h in range(MEM_HEADS):
        cs = slice(h * MEM_HEAD_DIM, (h + 1) * MEM_HEAD_DIM)
        k_ref[:, cs] = _rms(_dot(hb, w_ref[:, cs]), kg_ref[...])
        v_ref[:, cs] = _dot(hb, w_ref[:, MEM_WIDTH + h * MEM_HEAD_DIM:MEM_WIDTH + (h + 1) * MEM_HEAD_DIM])


def _mem_kv(mem, g, w, kg):
    b = mem.shape[0]
    const = lambda bi: (0, 0)
    blk = pl.BlockSpec((None, MEM_LEN, MEM_WIDTH), lambda bi: (bi, 0, 0))
    return pl.pallas_call(
        _mem_kv_kernel,
        grid=(b,),
        in_specs=[pl.BlockSpec((None, MEM_LEN, D_MODEL), lambda bi: (bi, 0, 0)), pl.BlockSpec((1, D_MODEL), const),
                  pl.BlockSpec((D_MODEL, 2 * MEM_WIDTH), const), pl.BlockSpec((1, MEM_HEAD_DIM), const)],
        out_specs=[blk, blk],
        out_shape=[jax.ShapeDtypeStruct((b, MEM_LEN, MEM_WIDTH), F32)] * 2,
        compiler_params=_params(("arbitrary",), 32),
        name="mem_kv",
    )(mem, g, w, kg)


def _mem_attn_kernel(q_ref, k_ref, v_ref, o_ref):
    for h in range(MEM_HEADS):
        cs = slice(h * MEM_HEAD_DIM, (h + 1) * MEM_HEAD_DIM)
        s = _dot_nt(q_ref[:, cs], k_ref[:, cs].astype(BF16)) * (MEM_HEAD_DIM ** -0.5)
        m = jnp.max(s, axis=1, keepdims=True)
        p = jnp.exp(s - m)
        l = jnp.sum(p, axis=1, keepdims=True)
        o_ref[:, cs] = (_dot(p.astype(BF16), v_ref[:, cs].astype(BF16)) / l).astype(BF16)


def _mem_attn(q, k, v, tm):
    b, t, _ = q.shape
    assert t % tm == 0
    kv = pl.BlockSpec((None, MEM_LEN, MEM_WIDTH), lambda bi, i: (bi, 0, 0))
    qo = pl.BlockSpec((None, tm, MEM_WIDTH), lambda bi, i: (bi, i, 0))
    return pl.pallas_call(
        _mem_attn_kernel,
        grid=(b, t // tm),
        in_specs=[qo, kv, kv],
        out_specs=qo,
        out_shape=jax.ShapeDtypeStruct((b, t, MEM_WIDTH), BF16),
        compiler_params=_params(("arbitrary", "arbitrary"), 32),
        name="mem_attn",
    )(q, k, v).reshape(b * t, MEM_WIDTH)


def _mem_attn_rows_kernel(q_ref, k_ref, v_ref, o_ref):
    t = q_ref.shape[0]
    qf = q_ref[...].astype(F32)
    qs = jnp.concatenate([qf[:, h * MEM_HEAD_DIM:(h + 1) * MEM_HEAD_DIM] for h in range(MEM_HEADS)], axis=0)
    s = _dot_nt(qs.astype(BF16), k_ref[...].astype(BF16)) * (MEM_HEAD_DIM ** -0.5)
    shape = s.shape
    same_head = (lax.broadcasted_iota(I32, shape, 0) // t) == (lax.broadcasted_iota(I32, shape, 1) & (MEM_HEADS - 1))
    s = jnp.where(same_head, s, NEG_INF)
    m = jnp.max(s, axis=1, keepdims=True)
    p = jnp.exp(s - m)
    l = jnp.sum(p, axis=1, keepdims=True)
    o = _dot(p.astype(BF16), v_ref[...].astype(BF16)) / l
    o_ref[...] = jnp.concatenate([o[h * t:(h + 1) * t, :] for h in range(MEM_HEADS)], axis=1).astype(BF16)


def _mem_attn_rows(q, k_rows, v_rows):
    b, t, _ = q.shape
    kv = pl.BlockSpec((None, MEM_LEN * MEM_HEADS, MEM_HEAD_DIM), lambda bi: (bi, 0, 0))
    qo = pl.BlockSpec((None, t, MEM_WIDTH), lambda bi: (bi, 0, 0))
    return pl.pallas_call(
        _mem_attn_rows_kernel,
        grid=(b,),
        in_specs=[qo, kv, kv],
        out_specs=qo,
        out_shape=jax.ShapeDtypeStruct((b, t, MEM_WIDTH), BF16),
        compiler_params=_params(("arbitrary",), 32),
        name="mem_attn_rows",
    )(q, k_rows, v_rows).reshape(b * t, MEM_WIDTH)


def _merge_kernel(x_ref, o1_ref, o2_ref, o3_ref, l1_ref, l2_ref, l3_ref, ob_ref, om_ref,
                  g1_ref, wg_ref, bg_ref, wa_ref, wb_ref, wm_ref, wo_ref, g2_ref, wrh_ref, wrl_ref, br_ref,
                  x1_ref, h2_ref, idx_ref, gw_ref, mix_ref):
    tm = x_ref.shape[0]
    x = x_ref[...]
    hb = _rms(x, g1_ref[...]).astype(BF16)
    la, lb, lc = l1_ref[...], l2_ref[...], l3_ref[...]
    m = jnp.maximum(la, jnp.maximum(lb, lc))
    ea, eb, ec = jnp.exp(la - m), jnp.exp(lb - m), jnp.exp(lc - m)
    oa = ((ea * o1_ref[...] + eb * o2_ref[...] + ec * o3_ref[...]) / (ea + eb + ec)).astype(BF16)
    ob, om = ob_ref[...], om_ref[...]
    for j in range(4):
        cs = slice(j * 256, (j + 1) * 256)

        def gate(branch):
            c0 = branch * D_MODEL + j * 256
            return jax.nn.sigmoid(_dot(hb, wg_ref[:, c0:c0 + 256]) + bg_ref[:, c0:c0 + 256])

        mixed = gate(0) * _dot(oa, wa_ref[:, cs]) + gate(1) * _dot(ob, wb_ref[:, cs]) + gate(2) * _dot(om, wm_ref[:, cs])
        mix_ref[:, cs] = mixed.astype(BF16)
    x1 = x + _dot(mix_ref[...], wo_ref[...])
    x1_ref[...] = x1
    h2 = _rms(x1, g2_ref[...])
    for j in range(ROW_TILE):
        h2_ref[pl.ds(j, tm, stride=ROW_TILE), :] = h2[:, j * LANES:(j + 1) * LANES]
    hi = h2.astype(BF16)
    lo = (h2 - hi.astype(F32)).astype(BF16)
    logits = _dot(hi, wrh_ref[...]) + _dot(lo, wrh_ref[...]) + _dot(hi, wrl_ref[...]) + br_ref[...]
    lane = lax.broadcasted_iota(I32, (tm, LANES), 1)
    logits = jnp.where(lane < N_EXPERTS, logits, NEG_INF)
    vals, idxs = [], []
    for _ in range(TOP_K):
        mk = jnp.max(logits, axis=1, keepdims=True)
        ik = jnp.min(jnp.where(logits == mk, lane, LANES), axis=1, keepdims=True)
        vals.append(mk)
        idxs.append(ik)
        logits = jnp.where(lane == ik, NEG_INF, logits)
    es = [jnp.exp(v - vals[0]) for v in vals]
    den = es[0] + es[1] + es[2] + es[3]
    l4 = lax.broadcasted_iota(I32, (tm, TOP_K), 1)
    idx_out = jnp.zeros((tm, TOP_K), I32)
    gw_out = jnp.zeros((tm, TOP_K), F32)
    for k in range(TOP_K):
        idx_out = jnp.where(l4 == k, idxs[k], idx_out)
        gw_out = jnp.where(l4 == k, es[k] / den, gw_out)
    idx_ref[...] = idx_out
    gw_ref[...] = gw_out


def _merge(x, os_, lses, ob, om, g1, wg, bg, wa, wb, wm, wo, g2, wrh, wrl, br, tm):
    n = x.shape[0]
    assert n % tm == 0
    row = lambda i: (i, 0)
    const = lambda i: (0, 0)
    rows = lambda w: pl.BlockSpec((tm, w), row)
    full = lambda a: pl.BlockSpec(a.shape, const)
    return pl.pallas_call(
        _merge_kernel,
        grid=(n // tm,),
        in_specs=[rows(D_MODEL)] + [rows(GROUP_W)] * 6 + [rows(CONV_CH), rows(MEM_WIDTH)]
        + [full(a) for a in (g1, wg, bg, wa, wb, wm, wo, g2, wrh, wrl, br)],
        out_specs=[rows(D_MODEL), pl.BlockSpec((tm * ROW_TILE, LANES), row), rows(TOP_K), rows(TOP_K)],
        out_shape=[jax.ShapeDtypeStruct((n, D_MODEL), F32), jax.ShapeDtypeStruct((n * ROW_TILE, LANES), F32),
                   jax.ShapeDtypeStruct((n, TOP_K), I32), jax.ShapeDtypeStruct((n, TOP_K), F32)],
        scratch_shapes=[pltpu.VMEM((tm, D_MODEL), BF16)],
        compiler_params=_params(("arbitrary",), 56),
        name="merge_router",
    )(x, *os_, *lses, ob, om, g1, wg, bg, wa, wb, wm, wo, g2, wrh, wrl, br)


def _route_kernel(idx_ref, dest_ref, bexp_ref, cnt_ref, carry_ref, *, block_rows):
    phase, i = pl.program_id(0), pl.program_id(1)
    tq = idx_ref.shape[0]
    nb = bexp_ref.shape[0]
    idx = idx_ref[...]
    lane = lax.broadcasted_iota(I32, (tq, LANES), 1)
    member = jnp.zeros((tq, LANES), F32)
    for k in range(TOP_K):
        member = member + (lane == idx[:, k:k + 1]).astype(F32)
    tile_counts = jnp.sum(member, axis=0, keepdims=True)

    @pl.when(phase == 0)
    def _():
        @pl.when(i == 0)
        def _():
            cnt_ref[...] = jnp.zeros_like(cnt_ref)

        cnt_ref[...] += tile_counts
        dest_ref[...] = jnp.zeros_like(dest_ref)
        bexp_ref[...] = jnp.zeros_like(bexp_ref)

    @pl.when(phase == 1)
    def _():
        @pl.when(i == 0)
        def _():
            carry_ref[...] = jnp.zeros_like(carry_ref)

        counts = jnp.broadcast_to(cnt_ref[...], (8, LANES))
        padded = jnp.floor((counts + (block_rows - 1)) * (1.0 / block_rows)) * block_rows
        lane8 = lax.broadcasted_iota(I32, (8, LANES), 1)
        ends = padded
        for s in (1, 2, 4, 8, 16, 32, 64):
            ends = ends + jnp.where(lane8 >= s, pltpu.roll(ends, s, 1), 0.0)
        starts = (ends - padded)[0:1, :]
        r = lax.broadcasted_iota(I32, (tq, tq), 0)
        c = lax.broadcasted_iota(I32, (tq, tq), 1)
        earlier = _dot((c < r).astype(BF16), member.astype(BF16)) + carry_ref[...]
        carry_ref[...] += tile_counts
        base = earlier + starts
        l4 = lax.broadcasted_iota(I32, (tq, TOP_K), 1)
        dest = jnp.zeros((tq, TOP_K), F32)
        for k in range(TOP_K):
            dk = jnp.sum(jnp.where(lane == idx[:, k:k + 1], base, 0.0), axis=1, keepdims=True)
            dest = jnp.where(l4 == k, dk, dest)
        dest_ref[...] = dest.astype(I32)
        first_row = (lax.broadcasted_iota(I32, (nb, LANES), 0) * block_rows).astype(F32)
        lane_nb = lax.broadcasted_iota(I32, (nb, LANES), 1)
        done = jnp.where((ends[0:1, :] <= first_row) & (lane_nb < N_EXPERTS), 1.0, 0.0)
        bexp_ref[...] = jnp.minimum(jnp.sum(done, axis=1, keepdims=True), N_EXPERTS - 1.0).astype(I32)


def _route(idx, block_rows, n_blocks, tq):
    n = idx.shape[0]
    assert n % tq == 0
    nb_pad = -(-n_blocks // 8) * 8
    dest, bexp = pl.pallas_call(
        functools.partial(_route_kernel, block_rows=block_rows),
        grid=(2, n // tq),
        in_specs=[pl.BlockSpec((tq, TOP_K), lambda p, i: (i, 0))],
        out_specs=[pl.BlockSpec((tq, TOP_K), lambda p, i: (i * p, 0)),
                   pl.BlockSpec((nb_pad, 1), lambda p, i: (0, 0))],
        out_shape=[jax.ShapeDtypeStruct((n, TOP_K), I32), jax.ShapeDtypeStruct((nb_pad, 1), I32)],
        scratch_shapes=[pltpu.VMEM((1, LANES), F32), pltpu.VMEM((1, LANES), F32)],
        compiler_params=_params(("arbitrary", "arbitrary"), 32),
        name="route",
    )(idx)
    return dest.reshape(n * TOP_K), bexp.reshape(nb_pad)[:n_blocks]


def _tile_rows(index):
    return pl.ds(pl.multiple_of(index * ROW_TILE, ROW_TILE), ROW_TILE)


def _dispatch_kernel(dest_ref, h_ref, xs_in_ref, xs_ref, sem):
    del xs_in_ref
    tm = h_ref.shape[0] // ROW_TILE

    def issue(r, carry):
        for k in range(TOP_K):
            pltpu.make_async_copy(h_ref.at[_tile_rows(r)], xs_ref.at[_tile_rows(dest_ref[r * TOP_K + k])],
                                  sem).start(priority=k % 2)
        return carry

    lax.fori_loop(0, tm, issue, 0, unroll=4)
    for k in range(TOP_K):
        pltpu.make_async_copy(h_ref, xs_ref.at[pl.ds(0, tm * ROW_TILE)], sem).wait()


def _dispatch(h2t, dest, n_slots, tm):
    n = h2t.shape[0] // ROW_TILE
    assert n % tm == 0
    zeros = jnp.zeros((n_slots * ROW_TILE, LANES), F32)
    return pl.pallas_call(
        _dispatch_kernel,
        grid=(n // tm,),
        in_specs=[pl.BlockSpec((tm * TOP_K,), lambda i: (i,), memory_space=pltpu.SMEM),
                  pl.BlockSpec((tm * ROW_TILE, LANES), lambda i: (i, 0)),
                  pl.BlockSpec(memory_space=pl.ANY)],
        out_specs=pl.BlockSpec(memory_space=pl.ANY),
        out_shape=jax.ShapeDtypeStruct((n_slots * ROW_TILE, LANES), F32),
        scratch_shapes=[pltpu.SemaphoreType.DMA],
        input_output_aliases={2: 0},
        compiler_params=_params(("arbitrary",), 32),
        name="dispatch",
    )(dest, h2t, zeros)


def _moe_kernel(be_ref, xs_ref, wg_ref, bg_ref, wu_ref, bu_ref, wd_ref, bd_ref, yb_ref, wg_s, wu_s, wd_s):
    i = pl.program_id(0)
    changed = (i == 0) | (be_ref[i] != be_ref[jnp.maximum(i - 1, 0)])

    @pl.when(changed)
    def _():
        for r0 in range(0, D_MODEL, 128):
            rs = pl.ds(r0, 128)
            wg_s[rs, :] = wg_ref[rs, :].astype(BF16)
            wu_s[rs, :] = wu_ref[rs, :].astype(BF16)
            wd_s[rs, :] = wd_ref[rs, :].astype(BF16)

    bm = xs_ref.shape[0] // ROW_TILE
    xb = jnp.concatenate([xs_ref[pl.ds(j, bm, stride=ROW_TILE), :] for j in range(ROW_TILE)], axis=1).astype(BF16)
    acc = jnp.zeros((bm, D_MODEL), F32) + bd_ref[...]
    for c in range(D_FF // 256):
        cs = slice(c * 256, (c + 1) * 256)
        g = jnp.minimum(_dot(xb, wg_s[:, cs]) + bg_ref[:, cs], SWIGLU_LIMIT)
        u = jnp.clip(_dot(xb, wu_s[:, cs]) + bu_ref[:, cs], -SWIGLU_LIMIT, SWIGLU_LIMIT)
        act = g * jax.nn.sigmoid(SWIGLU_ALPHA * g) * (u + 1.0)
        acc = acc + _dot(act.astype(BF16), wd_s[cs, :])
    for j in range(ROW_TILE):
        yb_ref[pl.ds(j, bm, stride=ROW_TILE), :] = acc[:, j * LANES:(j + 1) * LANES]


def _moe(xs, bexp, wg, bg, wu, bu, wd, bd, bm):
    n_slots = xs.shape[0] // ROW_TILE
    assert n_slots % bm == 0
    wspec = pl.BlockSpec((None, D_MODEL, D_FF), lambda i, be: (be[i], 0, 0))
    bspec = pl.BlockSpec((None, 1, D_FF), lambda i, be: (be[i], 0, 0))
    rows = pl.BlockSpec((bm * ROW_TILE, LANES), lambda i, be: (i, 0))
    grid_spec = pltpu.PrefetchScalarGridSpec(
        num_scalar_prefetch=1,
        grid=(n_slots // bm,),
        in_specs=[rows, wspec, bspec, wspec, bspec, wspec, bspec],
        out_specs=rows,
        scratch_shapes=[pltpu.VMEM((D_MODEL, D_FF), BF16)] * 3,
    )
    return pl.pallas_call(
        _moe_kernel,
        grid_spec=grid_spec,
        out_shape=jax.ShapeDtypeStruct((n_slots * ROW_TILE, LANES), F32),
        compiler_params=_params(("arbitrary",), 56),
        name="moe_experts",
    )(bexp, xs, wg, bg.reshape(N_EXPERTS, 1, D_FF), wu, bu.reshape(N_EXPERTS, 1, D_FF),
      wd, bd.reshape(N_EXPERTS, 1, D_MODEL))


def _combine_kernel(dest_ref, dest_next_ref, x1_ref, gw_ref, yb_ref, y_ref, rows_ref, sems):
    tm = x1_ref.shape[0]
    i = pl.program_id(0)
    slot = i % 2

    def issue(dref, buf):
        def body(r, carry):
            for k in range(TOP_K):
                pltpu.make_async_copy(yb_ref.at[_tile_rows(dref[r * TOP_K + k])], rows_ref.at[buf, k, _tile_rows(r)],
                                      sems.at[buf]).start(priority=k % 2)
            return carry

        lax.fori_loop(0, tm, body, 0, unroll=4)

    @pl.when(i == 0)
    def _():
        issue(dest_ref, slot)

    @pl.when(i + 1 < pl.num_programs(0))
    def _():
        issue(dest_next_ref, 1 - slot)

    for k in range(TOP_K):
        pltpu.make_async_copy(yb_ref.at[pl.ds(0, tm * ROW_TILE)], rows_ref.at[slot, k], sems.at[slot]).wait()
    gw = gw_ref[...]
    for j in range(ROW_TILE):
        ls = slice(j * LANES, (j + 1) * LANES)
        y = x1_ref[:, ls]
        for k in range(TOP_K):
            y = y + gw[:, k:k + 1] * rows_ref[slot, k, pl.ds(j, tm, stride=ROW_TILE), :]
        y_ref[:, ls] = y


def _combine(x1, gw, dest, yb, tm):
    n = x1.shape[0]
    assert n % tm == 0
    steps = n // tm
    return pl.pallas_call(
        _combine_kernel,
        grid=(steps,),
        in_specs=[pl.BlockSpec((tm * TOP_K,), lambda i: (i,), memory_space=pltpu.SMEM),
                  pl.BlockSpec((tm * TOP_K,), lambda i: (jnp.minimum(i + 1, steps - 1),), memory_space=pltpu.SMEM),
                  pl.BlockSpec((tm, D_MODEL), lambda i: (i, 0)),
                  pl.BlockSpec((tm, TOP_K), lambda i: (i, 0)),
                  pl.BlockSpec(memory_space=pl.ANY)],
        out_specs=pl.BlockSpec((tm, D_MODEL), lambda i: (i, 0)),
        out_shape=jax.ShapeDtypeStruct((n, D_MODEL), F32),
        scratch_shapes=[pltpu.VMEM((2, TOP_K, tm * ROW_TILE, LANES), F32), pltpu.SemaphoreType.DMA((2,))],
        compiler_params=_params(("arbitrary",), 32),
        name="combine",
    )(dest, dest, x1, gw, yb)


def _rope_tables(positions):
    half = ROPE_DIM // 2
    inv = jnp.power(jnp.float32(ROPE_THETA), -jnp.arange(half, dtype=F32) / half)
    ang = positions.astype(F32)[:, None] * inv[None, :]
    cos, sin = jnp.cos(ang), jnp.sin(ang)
    t = positions.shape[0]
    z8 = jnp.zeros((t, half), F32)
    rest0 = jnp.zeros((t, HEAD_DIM_A - ROPE_DIM), F32)
    a = jnp.concatenate([cos, cos, jnp.ones((t, HEAD_DIM_A - ROPE_DIM), F32)], axis=1)
    bp = jnp.concatenate([z8, sin, rest0], axis=1)
    bm = jnp.concatenate([-sin, z8, rest0], axis=1)
    return tuple(jnp.tile(m, (1, LANES // HEAD_DIM_A)) for m in (a, bp, bm))


def _segment_mean_matrix(seg):
    i = jnp.arange(256)
    return jnp.where((i[:, None] // seg) == (i[None, :] // seg), 1.0 / seg, 0.0).astype(BF16)


def _layer_weights(l, norm1_g, w_in, b_gates, qn_a, kn_a, conv_w, conv_b, cn_g, cn_b, qn_m,
                   w_proj_a, w_proj_b, w_proj_m, w_out, norm2_g, w_router, b_router,
                   w_gate, b_gate, w_up, b_up, w_down, b_down):
    wr = jnp.pad(w_router[l], ((0, 0), (0, LANES - N_EXPERTS)))
    wrh = wr.astype(BF16)
    return dict(
        g1=norm1_g[l][None, :],
        w1=w_in[l][:, :W1_COLS].astype(BF16),
        wgates=w_in[l][:, W1_COLS:].astype(BF16),
        bgates=b_gates[l][None, :],
        qg=jnp.tile(qn_a[l], 256 // HEAD_DIM_A)[None, :],
        kg=jnp.tile(kn_a[l], 256 // HEAD_DIM_A)[None, :],
        mg=jnp.tile(qn_m[l], 256 // MEM_HEAD_DIM)[None, :],
        conv_w=conv_w[l], conv_b=conv_b[l][None, :], cn_g=cn_g[l][None, :], cn_b=cn_b[l][None, :],
        wa=w_proj_a[l].astype(BF16), wb=w_proj_b[l].astype(BF16), wm=w_proj_m[l].astype(BF16),
        wo=w_out[l].astype(BF16), g2=norm2_g[l][None, :],
        wrh=wrh, wrl=(wr - wrh.astype(F32)).astype(BF16),
        br=jnp.pad(b_router[l], (0, LANES - N_EXPERTS))[None, :],
        w_gate=w_gate[l], b_gate=b_gate[l], w_up=w_up[l], b_up=b_up[l], w_down=w_down[l], b_down=b_down[l],
        seg64=_segment_mean_matrix(HEAD_DIM_A), seg128=_segment_mean_matrix(MEM_HEAD_DIM),
    )


def _token_mixer_tail(x2d, os_, lses, ob, om, w, tm_merge, moe_rows, tm_route, tm_rows):
    n = x2d.shape[0]
    x1, h2, idx, gw = _merge(x2d, os_, lses, ob, om, w["g1"], w["wgates"], w["bgates"], w["wa"], w["wb"], w["wm"],
                             w["wo"], w["g2"], w["wrh"], w["wrl"], w["br"], tm_merge)
    n_blocks = -(-(n * TOP_K) // moe_rows) + N_EXPERTS
    dest, bexp = _route(idx, moe_rows, n_blocks, tm_route)
    xs = _dispatch(h2, dest, n_blocks * moe_rows, tm_rows)
    yb = _moe(xs, bexp, w["w_gate"], w["b_gate"], w["w_up"], w["b_up"], w["w_down"], w["b_down"], moe_rows)
    return _combine(x1, gw, dest, yb, min(tm_rows, 128))


def _prompt_layer(x, mem, w, mem_norm_g, w_mem_kv, kn_m):
    b, s, _ = x.shape
    n = b * s
    x2d = x.reshape(n, D_MODEL)
    tabs = _rope_tables(jnp.arange(s, dtype=I32))
    q, k, v, u, qm = _in_proj(x2d, w["g1"], w["w1"], w["qg"], w["kg"], w["mg"], tabs, w["seg64"], w["seg128"], 512, F32)
    q3, k3, v3 = (t.reshape(b, s, WIDTH_A) for t in (q, k, v))
    os_, lses, caches = [], [], []
    for g, (win, dil) in enumerate(SWA_GROUPS):
        o, lse = _swa_prompt(q3, k3, v3, g, dil)
        os_.append(o)
        lses.append(lse)
        cs = slice(g * GROUP_W, (g + 1) * GROUP_W)
        caches += [k3[:, s - win:, cs].reshape(b, win, 4, HEAD_DIM_A), v3[:, s - win:, cs].reshape(b, win, 4, HEAD_DIM_A)]
    u3 = u.reshape(b, s, CONV_CH)
    ob = _conv_branch(u3, u3, w["conv_w"], w["conv_b"], w["cn_g"], w["cn_b"], 256, True)
    mk, mv = _mem_kv(mem, mem_norm_g[None, :], w_mem_kv.astype(BF16), kn_m[None, :])
    om = _mem_attn(qm.reshape(b, s, MEM_WIDTH), mk, mv, 512)
    y = _token_mixer_tail(x2d, os_, lses, ob, om, w, 512, 512, 512, 256)
    state = caches + [u3[:, s - (CONV_WIDTH - 1):], mk.reshape(b, MEM_LEN, MEM_HEADS, MEM_HEAD_DIM),
                      mv.reshape(b, MEM_LEN, MEM_HEADS, MEM_HEAD_DIM)]
    return y.reshape(b, s, D_MODEL), state


def _sample_layer(x, mem_k, mem_v, bufs, conv_state, w):
    b, t, _ = x.shape
    n = b * t
    x2d = x.reshape(n, D_MODEL)
    tabs = _rope_tables(jnp.tile(PAST_LEN + jnp.arange(t, dtype=I32), b))
    q, k, v, u, qm = _in_proj(x2d, w["g1"], w["w1"], w["qg"], w["kg"], w["mg"], tabs, w["seg64"], w["seg128"], 256, BF16)
    q3, k3, v3 = (a.reshape(b, t, WIDTH_A) for a in (q, k, v))
    os_, lses, caches = [], [], []
    for g, (win, dil) in enumerate(SWA_GROUPS):
        to_cm = lambda a: jnp.transpose(a, (0, 2, 3, 1)).reshape(b, GROUP_W, win)
        from_cm = lambda a: jnp.transpose(a.reshape(b, 4, HEAD_DIM_A, win), (0, 3, 1, 2))
        o, lse, kto, vto = _swa_sample(q3, k3, v3, to_cm(bufs[2 * g]), to_cm(bufs[2 * g + 1]), g, win, dil)
        os_.append(o)
        lses.append(lse)
        caches += [from_cm(kto), from_cm(vto)]
    u3 = u.reshape(b, t, CONV_CH)
    halo = jnp.pad(conv_state, ((0, 0), (CONV_HALO - (CONV_WIDTH - 1), 0), (0, 0)))
    ob = _conv_branch(u3, halo, w["conv_w"], w["conv_b"], w["cn_g"], w["cn_b"], t, False)
    om = _mem_attn_rows(qm.reshape(b, t, MEM_WIDTH), mem_k.reshape(b, MEM_LEN * MEM_HEADS, MEM_HEAD_DIM),
                        mem_v.reshape(b, MEM_LEN * MEM_HEADS, MEM_HEAD_DIM))
    y = _token_mixer_tail(x2d, os_, lses, ob, om, w, 256, 128, 512, 128)
    new_conv = jnp.concatenate([conv_state, u3], axis=1)[:, t:]
    return y.reshape(b, t, D_MODEL), caches + [new_conv]


def kernel(x_prompt, x_sample, mem_prompt, cache_swa_k_w128, cache_swa_v_w128, cache_swa_k_w512, cache_swa_v_w512, cache_swa_k_w2048, cache_swa_v_w2048, state_conv, cache_mem_k, cache_mem_v, norm1_g, w_in, b_gates, qn_a, kn_a, conv_w, conv_b, cn_g, cn_b, mem_norm_g, w_mem_kv, qn_m, kn_m, w_proj_a, w_proj_b, w_proj_m, w_out, norm2_g, w_router, b_router, w_gate, b_gate, w_up, b_up, w_down, b_down):
    depth = norm1_g.shape[0]
    y_p, y_s = x_prompt, x_sample
    st_p, st_s = [], []
    for l in range(depth):
        w = _layer_weights(l, norm1_g, w_in, b_gates, qn_a, kn_a, conv_w, conv_b, cn_g, cn_b, qn_m,
                           w_proj_a, w_proj_b, w_proj_m, w_out, norm2_g, w_router, b_router,
                           w_gate, b_gate, w_up, b_up, w_down, b_down)
        y_p, state_p = _prompt_layer(y_p, mem_prompt, w, mem_norm_g[l], w_mem_kv[l], kn_m[l])
        bufs = (cache_swa_k_w128[l], cache_swa_v_w128[l], cache_swa_k_w512[l], cache_swa_v_w512[l],
                cache_swa_k_w2048[l], cache_swa_v_w2048[l])
        y_s, state_s = _sample_layer(y_s, cache_mem_k[l], cache_mem_v[l], bufs, state_conv[l], w)
        st_p.append(state_p)
        st_s.append(state_s)
    outs_p = [jnp.stack(a) for a in zip(*st_p)]
    outs_s = [jnp.stack(a) for a in zip(*st_s)]
    return (y_p, y_s, *outs_p, *outs_s)
```

```python
import functools

import jax
import jax.numpy as jnp
from jax import lax
from jax.experimental import pallas as pl
from jax.experimental.pallas import tpu as pltpu

F32 = jnp.float32
BF16 = jnp.bfloat16
I32 = jnp.int32

D_MODEL = 1024
HEAD_DIM_A = 64
GROUP_W = 256
N_GROUPS = 3
WIDTH_A = N_GROUPS * GROUP_W
SWA_GROUPS = ((128, 1), (512, 4), (2048, 16))
SWA_BLOCK = 128
ROPE_DIM = 16
ROPE_THETA = 500000.0
CONV_CH = 512
CONV_WIDTH = 31
CONV_HALO = 32
MEM_LEN = 256
MEM_HEADS = 4
MEM_HEAD_DIM = 128
MEM_WIDTH = 512
N_EXPERTS = 32
TOP_K = 4
D_FF = 1024
SWIGLU_LIMIT = 7.0
SWIGLU_ALPHA = 1.702
EPS = 1e-6
PAST_LEN = 8192
LANES = 128
ROW_TILE = D_MODEL // LANES
W1_COLS = 3 * WIDTH_A + 2 * CONV_CH + MEM_WIDTH
NEG_INF = float("-inf")
MIB = 2 ** 20


def _params(semantics, vmem_mib):
    return pltpu.CompilerParams(dimension_semantics=semantics, vmem_limit_bytes=vmem_mib * MIB)


def _rms(x, gain):
    return x * lax.rsqrt(jnp.mean(x * x, axis=-1, keepdims=True) + EPS) * gain


def _dot(a, b):
    return jnp.dot(a, b, preferred_element_type=F32)


def _dot_nt(a, b):
    return lax.dot_general(a, b, (((1,), (1,)), ((), ())), preferred_element_type=F32)


def _in_proj_kernel(x_ref, g1_ref, w_ref, qg_ref, kg_ref, mg_ref, ra_ref, rp_ref, rm_ref, s64_ref, s128_ref,
                    q_ref, k_ref, v_ref, u_ref, qm_ref):
    hb = _rms(x_ref[...], g1_ref[...]).astype(BF16)
    ra, rp, rm = ra_ref[...], rp_ref[...], rm_ref[...]

    def proj(c0):
        return _dot(hb, w_ref[:, c0:c0 + 256])

    def head_norm(p, seg_ref, gain):
        ms = _dot((p * p).astype(BF16), seg_ref[...])
        return p * lax.rsqrt(ms + EPS) * gain

    def rope(p):
        outs = []
        for j in range(2):
            pj = p[:, j * LANES:(j + 1) * LANES]
            outs.append(pj * ra + pltpu.roll(pj, 8, 1) * rp + pltpu.roll(pj, LANES - 8, 1) * rm)
        return jnp.concatenate(outs, axis=1)

    for c in range(3):
        cs = slice(c * 256, (c + 1) * 256)
        q_ref[:, cs] = rope(head_norm(proj(c * 256), s64_ref, qg_ref[...])).astype(q_ref.dtype)
        k_ref[:, cs] = rope(head_norm(proj(WIDTH_A + c * 256), s64_ref, kg_ref[...]))
        v_ref[:, cs] = proj(2 * WIDTH_A + c * 256)
    for c in range(2):
        cs = slice(c * 256, (c + 1) * 256)
        a = proj(3 * WIDTH_A + c * 256)
        gate = proj(3 * WIDTH_A + CONV_CH + c * 256)
        u_ref[:, cs] = a * jax.nn.sigmoid(gate)
        qm_ref[:, cs] = head_norm(proj(3 * WIDTH_A + 2 * CONV_CH + c * 256), s128_ref, mg_ref[...]).astype(BF16)


def _in_proj(x, g1, w1, qg, kg, mg, tabs, seg64, seg128, tm, q_dtype):
    n = x.shape[0]
    t_rows = tabs[0].shape[0]
    assert n % tm == 0 and t_rows % tm == 0
    t_blocks = t_rows // tm
    row = lambda i: (i, 0)
    const = lambda i: (0, 0)
    tab = lambda i: (i % t_blocks, 0)
    return pl.pallas_call(
        _in_proj_kernel,
        grid=(n // tm,),
        in_specs=[
            pl.BlockSpec((tm, D_MODEL), row),
            pl.BlockSpec((1, D_MODEL), const),
            pl.BlockSpec((D_MODEL, W1_COLS), const),
            pl.BlockSpec((1, 256), const), pl.BlockSpec((1, 256), const), pl.BlockSpec((1, 256), const),
            pl.BlockSpec((tm, LANES), tab), pl.BlockSpec((tm, LANES), tab), pl.BlockSpec((tm, LANES), tab),
            pl.BlockSpec((256, 256), const), pl.BlockSpec((256, 256), const),
        ],
        out_specs=[
            pl.BlockSpec((tm, WIDTH_A), row), pl.BlockSpec((tm, WIDTH_A), row), pl.BlockSpec((tm, WIDTH_A), row),
            pl.BlockSpec((tm, CONV_CH), row), pl.BlockSpec((tm, MEM_WIDTH), row),
        ],
        out_shape=[
            jax.ShapeDtypeStruct((n, WIDTH_A), q_dtype), jax.ShapeDtypeStruct((n, WIDTH_A), F32),
            jax.ShapeDtypeStruct((n, WIDTH_A), F32), jax.ShapeDtypeStruct((n, CONV_CH), F32),
            jax.ShapeDtypeStruct((n, MEM_WIDTH), BF16),
        ],
        compiler_params=_params(("arbitrary",), 48),
        name="in_proj",
    )(x, g1, w1, qg, kg, mg, *tabs, seg64, seg128)


def _swa_prompt_kernel(q_ref, k_ref, v_ref, o_ref, lse_ref, q_s, k_s, v_s, o_s, l_s, *, dil, nsub, unroll):
    n = pl.program_id(1)
    span = SWA_BLOCK * dil
    cur = n % 2
    prv = 1 - cur
    for s in range(2):
        ls = slice(s * LANES, (s + 1) * LANES)
        q_s[s] = q_ref[:, ls]
        k_s[cur, s] = k_ref[:, ls]
        v_s[cur, s] = v_ref[:, ls]

    @pl.when(n == 0)
    def _():
        k_s[prv] = jnp.zeros(k_s.shape[1:], F32)
        v_s[prv] = jnp.zeros(v_s.shape[1:], F32)

    shape = (SWA_BLOCK, 2 * SWA_BLOCK)
    col = lax.broadcasted_iota(I32, shape, 1)
    off = lax.broadcasted_iota(I32, shape, 0) + SWA_BLOCK - col
    band = (off >= 0) & (off <= SWA_BLOCK)
    band_first = band & ((n > 0) | (col >= SWA_BLOCK))
    head = col // HEAD_DIM_A

    def residue(r, carry):
        def both(ref, lead, rows):
            return jnp.concatenate([ref[(*lead, s, rows, slice(None))] for s in range(2)], axis=1)

        for j in range(nsub):
            rows = pl.ds(j * span + r, SWA_BLOCK, stride=dil)
            before = (prv, pl.ds((nsub - 1) * span + r, SWA_BLOCK, stride=dil)) if j == 0 else \
                     (cur, pl.ds((j - 1) * span + r, SWA_BLOCK, stride=dil))
            q = both(q_s, (), rows).astype(BF16)
            kk = jnp.concatenate([both(k_s, before[:1], before[1]), both(k_s, (cur,), rows)], axis=0).astype(BF16)
            vv = jnp.concatenate([both(v_s, before[:1], before[1]), both(v_s, (cur,), rows)], axis=0).astype(BF16)
            mask = band_first if j == 0 else band
            hms = [head == h for h in range(4)]
            ss = [jnp.where(mask, _dot_nt(jnp.where(hm, q, jnp.zeros_like(q)), kk) * (HEAD_DIM_A ** -0.5), NEG_INF)
                  for hm in hms]
            ms = [jnp.max(s, axis=1, keepdims=True) for s in ss]
            ps = [jnp.exp(s - m) for s, m in zip(ss, ms)]
            ls = [jnp.sum(p, axis=1, keepdims=True) for p in ps]
            ohs = [_dot(p.astype(BF16), vv) for p in ps]
            o_acc = jnp.zeros(shape, F32)
            lse_acc = jnp.zeros(shape, F32)
            for hm, oh, m, l in zip(hms, ohs, ms, ls):
                o_acc = jnp.where(hm, oh / l, o_acc)
                lse_acc = jnp.where(hm, m + jnp.log(l), lse_acc)
            for s in range(2):
                ls = slice(s * LANES, (s + 1) * LANES)
                o_s[s, rows, :] = o_acc[:, ls]
                l_s[s, rows, :] = lse_acc[:, ls]
        return carry

    lax.fori_loop(0, dil, residue, 0, unroll=unroll)
    for s in range(2):
        ls = slice(s * LANES, (s + 1) * LANES)
        o_ref[:, ls] = o_s[s]
        lse_ref[:, ls] = l_s[s]


def _swa_prompt(q, k, v, g, dil):
    b, s, _ = q.shape
    nsub = max(1, 4 // dil)
    unroll = 2 if nsub == 1 else 1
    t = nsub * dil * SWA_BLOCK
    assert s % t == 0
    inp = pl.BlockSpec((None, t, GROUP_W), lambda bi, n: (bi, n, g))
    out = pl.BlockSpec((None, t, GROUP_W), lambda bi, n: (bi, n, 0))
    slab = lambda *lead: pltpu.VMEM((*lead, 2, t, LANES), F32)
    o, lse = pl.pallas_call(
        functools.partial(_swa_prompt_kernel, dil=dil, nsub=nsub, unroll=unroll),
        grid=(b, s // t),
        in_specs=[inp, inp, inp],
        out_specs=[out, out],
        out_shape=[jax.ShapeDtypeStruct((b, s, GROUP_W), F32)] * 2,
        scratch_shapes=[slab(), slab(2), slab(2), slab(), slab()],
        compiler_params=_params(("arbitrary", "arbitrary"), 48),
        name=f"swa_prompt_d{dil}",
    )(q, k, v)
    return o.reshape(b * s, GROUP_W), lse.reshape(b * s, GROUP_W)


def _swa_sample_kernel(*refs, win, dil):
    def one(i, carry):
        _swa_sample_one(*(r.at[i] for r in refs), win=win, dil=dil)
        return carry

    lax.fori_loop(0, refs[0].shape[0], one, 0)


def _swa_sample_one(q_ref, kn_ref, vn_ref, kt_ref, vt_ref, o_ref, lse_ref, kto_ref, vto_ref, *, win, dil):
    t_new = q_ref.shape[0]
    n_tiles = win // LANES
    pad = jnp.zeros((LANES - t_new, GROUP_W), F32)
    kn = jnp.concatenate([kn_ref[...], pad], axis=0)
    vn = jnp.concatenate([vn_ref[...], pad], axis=0)

    def channel_major(a):
        return jnp.concatenate([a[:, :LANES].T, a[:, LANES:].T], axis=0)

    knt, vnt = channel_major(kn), channel_major(vn)
    lane = lax.broadcasted_iota(I32, (GROUP_W, LANES), 1)
    keep = lane < LANES - t_new
    for src_ref, new_t, dst_ref in ((kt_ref, knt, kto_ref), (vt_ref, vnt, vto_ref)):
        nxt = pltpu.roll(src_ref[:, 0:LANES], LANES - t_new, 1)
        for j in range(n_tiles):
            this = nxt
            following = new_t if j + 1 == n_tiles else src_ref[:, (j + 1) * LANES:(j + 2) * LANES]
            nxt = pltpu.roll(following, LANES - t_new, 1)
            dst_ref[:, j * LANES:(j + 1) * LANES] = jnp.where(keep, this, nxt)

    qf = q_ref[...].astype(F32)
    ch_head = lax.broadcasted_iota(I32, (t_new, GROUP_W), 1) // HEAD_DIM_A
    qm = jnp.concatenate([jnp.where(ch_head == h, qf, 0.0) for h in range(4)], axis=0).astype(BF16)
    nq = 4 * t_new
    scale = HEAD_DIM_A ** -0.5
    s_c = _dot(qm, kt_ref[...].astype(BF16)) * scale
    s_n = _dot(qm, knt.astype(BF16)) * scale
    t_c = lax.broadcasted_iota(I32, (nq, win), 0) & (t_new - 1)
    d_c = win + t_c - lax.broadcasted_iota(I32, (nq, win), 1)
    s_c = jnp.where((d_c <= win) & ((d_c & (dil - 1)) == 0), s_c, NEG_INF)
    t_n = lax.broadcasted_iota(I32, (nq, LANES), 0) & (t_new - 1)
    d_n = t_n - lax.broadcasted_iota(I32, (nq, LANES), 1)
    s_n = jnp.where((d_n >= 0) & ((d_n & (dil - 1)) == 0), s_n, NEG_INF)
    m = jnp.maximum(jnp.max(s_c, axis=1, keepdims=True), jnp.max(s_n, axis=1, keepdims=True))
    p_c = jnp.exp(s_c - m)
    p_n = jnp.exp(s_n - m)
    l = jnp.sum(p_c, axis=1, keepdims=True) + jnp.sum(p_n, axis=1, keepdims=True)
    o_all = (_dot_nt(p_c.astype(BF16), vt_ref[...].astype(BF16)) + _dot(p_n.astype(BF16), vn.astype(BF16))) / l
    lse_all = m + jnp.log(l)
    o_acc = jnp.zeros((t_new, GROUP_W), F32)
    lse_acc = jnp.zeros((t_new, GROUP_W), F32)
    for h in range(4):
        hm = ch_head == h
        o_acc = jnp.where(hm, o_all[h * t_new:(h + 1) * t_new, :], o_acc)
        lse_acc = jnp.where(hm, lse_all[h * t_new:(h + 1) * t_new, :], lse_acc)
    o_ref[...] = o_acc
    lse_ref[...] = lse_acc


def _swa_sample(q, k_new, v_new, kt_buf, vt_buf, g, win, dil):
    b, t, _ = q.shape
    assert t == 8 and win % LANES == 0 and dil & (dil - 1) == 0
    bb = max(1, 2048 // win)
    assert b % bb == 0
    new = pl.BlockSpec((bb, t, GROUP_W), lambda bi: (bi, 0, g))
    buf = pl.BlockSpec((bb, GROUP_W, win), lambda bi: (bi, 0, 0))
    out = pl.BlockSpec((bb, t, GROUP_W), lambda bi: (bi, 0, 0))
    o, lse, kto, vto = pl.pallas_call(
        functools.partial(_swa_sample_kernel, win=win, dil=dil),
        grid=(b // bb,),
        in_specs=[new, new, new, buf, buf],
        out_specs=[out, out, buf, buf],
        out_shape=[jax.ShapeDtypeStruct((b, t, GROUP_W), F32)] * 2
        + [jax.ShapeDtypeStruct((b, GROUP_W, win), F32)] * 2,
        compiler_params=_params(("arbitrary",), 48),
        name=f"swa_sample_w{win}",
    )(q, k_new, v_new, kt_buf, vt_buf)
    return o.reshape(b * t, GROUP_W), lse.reshape(b * t, GROUP_W), kto, vto


def _conv_kernel(halo_ref, cur_ref, w_ref, b_ref, g_ref, beta_ref, o_ref, ctx_ref, *, zero_first_halo, chunk):
    tm = cur_ref.shape[0]
    halo = halo_ref[...]
    if zero_first_halo:
        halo = jnp.where(pl.program_id(1) == 0, 0.0, halo)
    ctx_ref[pl.ds(0, CONV_HALO), :] = halo
    ctx_ref[pl.ds(CONV_HALO, tm), :] = cur_ref[...]
    first = CONV_HALO - (CONV_WIDTH - 1)
    for c0 in range(0, tm, chunk):
        acc = jnp.zeros((chunk, CONV_CH), F32) + b_ref[...]
        for phase in range(8):
            taps = range(phase, CONV_WIDTH, 8)
            start, shift = divmod(first + phase, 8)
            need = chunk + 8 * (len(taps) - 1)
            if shift == 0:
                window = ctx_ref[pl.ds(c0 + 8 * start, need), :]
            else:
                rows = ctx_ref[pl.ds(c0 + 8 * start, need + 8), :]
                window = pltpu.roll(rows, need + 8 - shift, 0)[0:need, :]
            for a, w in enumerate(taps):
                acc = acc + window[8 * a:8 * a + chunk, :] * w_ref[pl.ds(w, 1), :]
        mu = jnp.mean(acc, axis=-1, keepdims=True)
        xc = acc - mu
        var = jnp.mean(xc * xc, axis=-1, keepdims=True)
        y = xc * lax.rsqrt(var + EPS) * g_ref[...] + beta_ref[...]
        o_ref[pl.ds(c0, chunk), :] = (y * jax.nn.sigmoid(y)).astype(BF16)


def _conv_branch(u, halo, conv_w, conv_b, cn_g, cn_b, tm, halo_from_u):
    b, t, _ = u.shape
    assert t % tm == 0
    ratio = tm // CONV_HALO if halo_from_u else 0
    halo_map = (lambda bi, i: (bi, jnp.maximum(i * ratio - 1, 0), 0)) if halo_from_u else (lambda bi, i: (bi, 0, 0))
    const = lambda bi, i: (0, 0)
    return pl.pallas_call(
        functools.partial(_conv_kernel, zero_first_halo=halo_from_u, chunk=min(tm, 64)),
        grid=(b, t // tm),
        in_specs=[
            pl.BlockSpec((None, CONV_HALO, CONV_CH), halo_map),
            pl.BlockSpec((None, tm, CONV_CH), lambda bi, i: (bi, i, 0)),
            pl.BlockSpec((CONV_WIDTH, CONV_CH), const),
            pl.BlockSpec((1, CONV_CH), const), pl.BlockSpec((1, CONV_CH), const), pl.BlockSpec((1, CONV_CH), const),
        ],
        out_specs=pl.BlockSpec((None, tm, CONV_CH), lambda bi, i: (bi, i, 0)),
        out_shape=jax.ShapeDtypeStruct((b, t, CONV_CH), BF16),
        scratch_shapes=[pltpu.VMEM((CONV_HALO + tm, CONV_CH), F32)],
        compiler_params=_params(("arbitrary", "arbitrary"), 32),
        name="conv_module",
    )(halo, u, conv_w, conv_b, cn_g, cn_b).reshape(b * t, CONV_CH)


def _mem_kv_kernel(mem_ref, g_ref, w_ref, kg_ref, k_ref, v_ref):
    hb = _rms(mem_ref[...], g_ref[...]).astype(BF16)
    for h in range(MEM_HEADS):
        cs = slice(h * MEM_HEAD_DIM, (h + 1) * MEM_HEAD_DIM)
        k_ref[:, cs] = _rms(_dot(hb, w_ref[:, cs]), kg_ref[...])
        v_ref[:, cs] = _dot(hb, w_ref[:, MEM_WIDTH + h * MEM_HEAD_DIM:MEM_WIDTH + (h + 1) * MEM_HEAD_DIM])


def _mem_kv(mem, g, w, kg):
    b = mem.shape[0]
    const = lambda bi: (0, 0)
    blk = pl.BlockSpec((None, MEM_LEN, MEM_WIDTH), lambda bi: (bi, 0, 0))
    return pl.pallas_call(
        _mem_kv_kernel,
        grid=(b,),
        in_specs=[pl.BlockSpec((None, MEM_LEN, D_MODEL), lambda bi: (bi, 0, 0)), pl.BlockSpec((1, D_MODEL), const),
                  pl.BlockSpec((D_MODEL, 2 * MEM_WIDTH), const), pl.BlockSpec((1, MEM_HEAD_DIM), const)],
        out_specs=[blk, blk],
        out_shape=[jax.ShapeDtypeStruct((b, MEM_LEN, MEM_WIDTH), F32)] * 2,
        compiler_params=_params(("arbitrary",), 32),
        name="mem_kv",
    )(mem, g, w, kg)


def _mem_attn_kernel(q_ref, k_ref, v_ref, o_ref):
    for h in range(MEM_HEADS):
        cs = slice(h * MEM_HEAD_DIM, (h + 1) * MEM_HEAD_DIM)
        s = _dot_nt(q_ref[:, cs], k_ref[:, cs].astype(BF16)) * (MEM_HEAD_DIM ** -0.5)
        m = jnp.max(s, axis=1, keepdims=True)
        p = jnp.exp(s - m)
        l = jnp.sum(p, axis=1, keepdims=True)
        o_ref[:, cs] = (_dot(p.astype(BF16), v_ref[:, cs].astype(BF16)) / l).astype(BF16)


def _mem_attn(q, k, v, tm):
    b, t, _ = q.shape
    assert t % tm == 0
    kv = pl.BlockSpec((None, MEM_LEN, MEM_WIDTH), lambda bi, i: (bi, 0, 0))
    qo = pl.BlockSpec((None, tm, MEM_WIDTH), lambda bi, i: (bi, i, 0))
    return pl.pallas_call(
        _mem_attn_kernel,
        grid=(b, t // tm),
        in_specs=[qo, kv, kv],
        out_specs=qo,
        out_shape=jax.ShapeDtypeStruct((b, t, MEM_WIDTH), BF16),
        compiler_params=_params(("arbitrary", "arbitrary"), 32),
        name="mem_attn",
    )(q, k, v).reshape(b * t, MEM_WIDTH)


def _mem_attn_rows_kernel(*refs):
    def one(i, carry):
        _mem_attn_rows_one(*(r.at[i] for r in refs))
        return carry

    lax.fori_loop(0, refs[0].shape[0], one, 0)


def _mem_attn_rows_one(q_ref, k_ref, v_ref, o_ref):
    t = q_ref.shape[0]
    qf = q_ref[...].astype(F32)
    qs = jnp.concatenate([qf[:, h * MEM_HEAD_DIM:(h + 1) * MEM_HEAD_DIM] for h in range(MEM_HEADS)], axis=0)
    s = _dot_nt(qs.astype(BF16), k_ref[...].astype(BF16)) * (MEM_HEAD_DIM ** -0.5)
    shape = s.shape
    same_head = (lax.broadcasted_iota(I32, shape, 0) // t) == (lax.broadcasted_iota(I32, shape, 1) & (MEM_HEADS - 1))
    s = jnp.where(same_head, s, NEG_INF)
    m = jnp.max(s, axis=1, keepdims=True)
    p = jnp.exp(s - m)
    l = jnp.sum(p, axis=1, keepdims=True)
    o = _dot(p.astype(BF16), v_ref[...].astype(BF16)) / l
    o_ref[...] = jnp.concatenate([o[h * t:(h + 1) * t, :] for h in range(MEM_HEADS)], axis=1).astype(BF16)


def _mem_attn_rows(q, k_rows, v_rows):
    b, t, _ = q.shape
    bb = 4
    assert b % bb == 0
    kv = pl.BlockSpec((bb, MEM_LEN * MEM_HEADS, MEM_HEAD_DIM), lambda bi: (bi, 0, 0))
    qo = pl.BlockSpec((bb, t, MEM_WIDTH), lambda bi: (bi, 0, 0))
    return pl.pallas_call(
        _mem_attn_rows_kernel,
        grid=(b // bb,),
        in_specs=[qo, kv, kv],
        out_specs=qo,
        out_shape=jax.ShapeDtypeStruct((b, t, MEM_WIDTH), BF16),
        compiler_params=_params(("arbitrary",), 32),
        name="mem_attn_rows",
    )(q, k_rows, v_rows).reshape(b * t, MEM_WIDTH)


def _merge_kernel(x_ref, o1_ref, o2_ref, o3_ref, l1_ref, l2_ref, l3_ref, ob_ref, om_ref,
                  g1_ref, wg_ref, bg_ref, wa_ref, wb_ref, wm_ref, wo_ref, g2_ref, wrh_ref, wrl_ref, br_ref,
                  x1_ref, h2_ref, idx_ref, gw_ref, mix_ref):
    tm = x_ref.shape[0]
    x = x_ref[...]
    hb = _rms(x, g1_ref[...]).astype(BF16)
    la, lb, lc = l1_ref[...], l2_ref[...], l3_ref[...]
    m = jnp.maximum(la, jnp.maximum(lb, lc))
    ea, eb, ec = jnp.exp(la - m), jnp.exp(lb - m), jnp.exp(lc - m)
    oa = ((ea * o1_ref[...] + eb * o2_ref[...] + ec * o3_ref[...]) / (ea + eb + ec)).astype(BF16)
    ob, om = ob_ref[...], om_ref[...]
    for j in range(4):
        cs = slice(j * 256, (j + 1) * 256)

        def gate(branch):
            c0 = branch * D_MODEL + j * 256
            return jax.nn.sigmoid(_dot(hb, wg_ref[:, c0:c0 + 256]) + bg_ref[:, c0:c0 + 256])

        mixed = gate(0) * _dot(oa, wa_ref[:, cs]) + gate(1) * _dot(ob, wb_ref[:, cs]) + gate(2) * _dot(om, wm_ref[:, cs])
        mix_ref[:, cs] = mixed.astype(BF16)
    x1 = x + _dot(mix_ref[...], wo_ref[...])
    x1_ref[...] = x1
    h2 = _rms(x1, g2_ref[...])
    for j in range(ROW_TILE):
        h2_ref[pl.ds(j, tm, stride=ROW_TILE), :] = h2[:, j * LANES:(j + 1) * LANES]
    hi = h2.astype(BF16)
    lo = (h2 - hi.astype(F32)).astype(BF16)
    logits = _dot(hi, wrh_ref[...]) + _dot(lo, wrh_ref[...]) + _dot(hi, wrl_ref[...]) + br_ref[...]
    lane = lax.broadcasted_iota(I32, (tm, LANES), 1)
    logits = jnp.where(lane < N_EXPERTS, logits, NEG_INF)
    vals, idxs = [], []
    for _ in range(TOP_K):
        mk = jnp.max(logits, axis=1, keepdims=True)
        ik = jnp.min(jnp.where(logits == mk, lane, LANES), axis=1, keepdims=True)
        vals.append(mk)
        idxs.append(ik)
        logits = jnp.where(lane == ik, NEG_INF, logits)
    es = [jnp.exp(v - vals[0]) for v in vals]
    den = es[0] + es[1] + es[2] + es[3]
    l4 = lax.broadcasted_iota(I32, (tm, TOP_K), 1)
    idx_out = jnp.zeros((tm, TOP_K), I32)
    gw_out = jnp.zeros((tm, TOP_K), F32)
    for k in range(TOP_K):
        idx_out = jnp.where(l4 == k, idxs[k], idx_out)
        gw_out = jnp.where(l4 == k, es[k] / den, gw_out)
    idx_ref[...] = idx_out
    gw_ref[...] = gw_out


def _merge(x, os_, lses, ob, om, g1, wg, bg, wa, wb, wm, wo, g2, wrh, wrl, br, tm):
    n = x.shape[0]
    assert n % tm == 0
    row = lambda i: (i, 0)
    const = lambda i: (0, 0)
    rows = lambda w: pl.BlockSpec((tm, w), row)
    full = lambda a: pl.BlockSpec(a.shape, const)
    return pl.pallas_call(
        _merge_kernel,
        grid=(n // tm,),
        in_specs=[rows(D_MODEL)] + [rows(GROUP_W)] * 6 + [rows(CONV_CH), rows(MEM_WIDTH)]
        + [full(a) for a in (g1, wg, bg, wa, wb, wm, wo, g2, wrh, wrl, br)],
        out_specs=[rows(D_MODEL), pl.BlockSpec((tm * ROW_TILE, LANES), row), rows(TOP_K), rows(TOP_K)],
        out_shape=[jax.ShapeDtypeStruct((n, D_MODEL), F32), jax.ShapeDtypeStruct((n * ROW_TILE, LANES), F32),
                   jax.ShapeDtypeStruct((n, TOP_K), I32), jax.ShapeDtypeStruct((n, TOP_K), F32)],
        scratch_shapes=[pltpu.VMEM((tm, D_MODEL), BF16)],
        compiler_params=_params(("arbitrary",), 56),
        name="merge_router",
    )(x, *os_, *lses, ob, om, g1, wg, bg, wa, wb, wm, wo, g2, wrh, wrl, br)


def _route_kernel(idx_ref, dest_ref, bexp_ref, seg_ref, cnt_ref, carry_ref, *, block_rows):
    phase, i = pl.program_id(0), pl.program_id(1)
    tq = idx_ref.shape[0]
    nb = bexp_ref.shape[0]
    idx = idx_ref[...]
    lane = lax.broadcasted_iota(I32, (tq, LANES), 1)
    member = jnp.zeros((tq, LANES), F32)
    for k in range(TOP_K):
        member = member + (lane == idx[:, k:k + 1]).astype(F32)
    tile_counts = jnp.sum(member, axis=0, keepdims=True)

    @pl.when(phase == 0)
    def _():
        @pl.when(i == 0)
        def _():
            cnt_ref[...] = jnp.zeros_like(cnt_ref)

        cnt_ref[...] += tile_counts
        dest_ref[...] = jnp.zeros_like(dest_ref)
        bexp_ref[...] = jnp.zeros_like(bexp_ref)
        seg_ref[...] = jnp.zeros_like(seg_ref)

    @pl.when(phase == 1)
    def _():
        @pl.when(i == 0)
        def _():
            carry_ref[...] = jnp.zeros_like(carry_ref)

        counts = jnp.broadcast_to(cnt_ref[...], (8, LANES))
        padded = jnp.floor((counts + (block_rows - 1)) * (1.0 / block_rows)) * block_rows
        lane8 = lax.broadcasted_iota(I32, (8, LANES), 1)
        ends = padded
        for s in (1, 2, 4, 8, 16, 32, 64):
            ends = ends + jnp.where(lane8 >= s, pltpu.roll(ends, s, 1), 0.0)
        starts = (ends - padded)[0:1, :]
        r = lax.broadcasted_iota(I32, (tq, tq), 0)
        c = lax.broadcasted_iota(I32, (tq, tq), 1)
        earlier = _dot((c < r).astype(BF16), member.astype(BF16)) + carry_ref[...]
        carry_ref[...] += tile_counts
        base = earlier + starts
        l4 = lax.broadcasted_iota(I32, (tq, TOP_K), 1)
        dest = jnp.zeros((tq, TOP_K), F32)
        for k in range(TOP_K):
            dk = jnp.sum(jnp.where(lane == idx[:, k:k + 1], base, 0.0), axis=1, keepdims=True)
            dest = jnp.where(l4 == k, dk, dest)
        dest_ref[...] = dest.astype(I32)
        first_row = (lax.broadcasted_iota(I32, (nb, LANES), 0) * block_rows).astype(F32)
        lane_nb = lax.broadcasted_iota(I32, (nb, LANES), 1)
        done = jnp.where((ends[0:1, :] <= first_row) & (lane_nb < N_EXPERTS), 1.0, 0.0)
        bexp_ref[...] = jnp.minimum(jnp.sum(done, axis=1, keepdims=True), N_EXPERTS - 1.0).astype(I32)
        row8 = lax.broadcasted_iota(I32, (8, LANES), 0)
        seg = jnp.where(row8 == 0, ends - padded + counts, jnp.where(row8 == 1, ends, 0.0))
        seg_ref[...] = seg.astype(I32)


def _route(idx, block_rows, n_blocks, tq):
    n = idx.shape[0]
    assert n % tq == 0
    nb_pad = -(-n_blocks // 8) * 8
    dest, bexp, seg = pl.pallas_call(
        functools.partial(_route_kernel, block_rows=block_rows),
        grid=(2, n // tq),
        in_specs=[pl.BlockSpec((tq, TOP_K), lambda p, i: (i, 0))],
        out_specs=[pl.BlockSpec((tq, TOP_K), lambda p, i: (i * p, 0)),
                   pl.BlockSpec((nb_pad, 1), lambda p, i: (0, 0)),
                   pl.BlockSpec((8, LANES), lambda p, i: (0, 0))],
        out_shape=[jax.ShapeDtypeStruct((n, TOP_K), I32), jax.ShapeDtypeStruct((nb_pad, 1), I32),
                   jax.ShapeDtypeStruct((8, LANES), I32)],
        scratch_shapes=[pltpu.VMEM((1, LANES), F32), pltpu.VMEM((1, LANES), F32)],
        compiler_params=_params(("arbitrary", "arbitrary"), 40),
        name="route",
    )(idx)
    return dest.reshape(n * TOP_K), bexp.reshape(nb_pad)[:n_blocks], seg


def _tile_rows(index):
    return pl.ds(pl.multiple_of(index * ROW_TILE, ROW_TILE), ROW_TILE)


def _dispatch_kernel(seg_ref, dest_ref, h_ref, xs_ref, zero_ref, sem, zero_sem):
    tm = h_ref.shape[0] // ROW_TILE

    @pl.when(pl.program_id(0) == 0)
    def _():
        zero_ref[...] = jnp.zeros_like(zero_ref)

        def zero_copy(slot):
            return pltpu.make_async_copy(zero_ref, xs_ref.at[_tile_rows(slot)], zero_sem)

        def expert(e, carry):
            first_pad, end = seg_ref[0, e], seg_ref[1, e]
            lax.fori_loop(first_pad, end, lambda s, c: (zero_copy(s).start(), c)[1], 0)
            lax.fori_loop(first_pad, end, lambda s, c: (zero_copy(s).wait(), c)[1], 0)
            return carry

        lax.fori_loop(0, N_EXPERTS, expert, 0)

    def issue(r, carry):
        for k in range(TOP_K):
            pltpu.make_async_copy(h_ref.at[_tile_rows(r)], xs_ref.at[_tile_rows(dest_ref[r * TOP_K + k])],
                                  sem).start(priority=k % 2)
        return carry

    lax.fori_loop(0, tm, issue, 0, unroll=4)
    for k in range(TOP_K):
        pltpu.make_async_copy(h_ref, xs_ref.at[pl.ds(0, tm * ROW_TILE)], sem).wait()


def _dispatch(h2t, dest, seg, n_slots, tm):
    n = h2t.shape[0] // ROW_TILE
    assert n % tm == 0
    return pl.pallas_call(
        _dispatch_kernel,
        grid=(n // tm,),
        in_specs=[pl.BlockSpec(memory_space=pltpu.SMEM),
                  pl.BlockSpec((tm * TOP_K,), lambda i: (i,), memory_space=pltpu.SMEM),
                  pl.BlockSpec((tm * ROW_TILE, LANES), lambda i: (i, 0))],
        out_specs=pl.BlockSpec(memory_space=pl.ANY),
        out_shape=jax.ShapeDtypeStruct((n_slots * ROW_TILE, LANES), F32),
        scratch_shapes=[pltpu.VMEM((ROW_TILE, LANES), F32), pltpu.SemaphoreType.DMA, pltpu.SemaphoreType.DMA],
        compiler_params=_params(("arbitrary",), 32),
        name="dispatch",
    )(seg, dest, h2t)


def _moe_kernel(be_ref, seg_ref, xs_ref, wg_ref, bg_ref, wu_ref, bu_ref, wd_ref, bd_ref, yb_ref, wg_s, wu_s, wd_s):
    i = pl.program_id(0)
    bm = xs_ref.shape[0] // ROW_TILE
    used_rows = seg_ref[LANES + N_EXPERTS - 1]

    @pl.when(i * bm < used_rows)
    def _():
        changed = (i == 0) | (be_ref[i] != be_ref[jnp.maximum(i - 1, 0)])

        @pl.when(changed)
        def _():
            for r0 in range(0, D_MODEL, 128):
                rs = pl.ds(r0, 128)
                wg_s[rs, :] = wg_ref[rs, :].astype(BF16)
                wu_s[rs, :] = wu_ref[rs, :].astype(BF16)
                wd_s[rs, :] = wd_ref[rs, :].astype(BF16)

        xb = jnp.concatenate([xs_ref[pl.ds(j, bm, stride=ROW_TILE), :] for j in range(ROW_TILE)],
                             axis=1).astype(BF16)
        chunks = [slice(c * 256, (c + 1) * 256) for c in range(D_FF // 256)]
        gs = [_dot(xb, wg_s[:, cs]) + bg_ref[:, cs] for cs in chunks]
        us = [_dot(xb, wu_s[:, cs]) + bu_ref[:, cs] for cs in chunks]
        acts = []
        for g, u in zip(gs, us):
            g = jnp.minimum(g, SWIGLU_LIMIT)
            u = jnp.clip(u, -SWIGLU_LIMIT, SWIGLU_LIMIT)
            acts.append((g * jax.nn.sigmoid(SWIGLU_ALPHA * g) * (u + 1.0)).astype(BF16))
        acc = jnp.zeros((bm, D_MODEL), F32) + bd_ref[...]
        for act, cs in zip(acts, chunks):
            acc = acc + _dot(act, wd_s[cs, :])
        for j in range(ROW_TILE):
            yb_ref[pl.ds(j, bm, stride=ROW_TILE), :] = acc[:, j * LANES:(j + 1) * LANES]


def _moe(xs, bexp, seg, wg, bg, wu, bu, wd, bd, bm):
    n_slots = xs.shape[0] // ROW_TILE
    assert n_slots % bm == 0
    wspec = pl.BlockSpec((None, D_MODEL, D_FF), lambda i, be, sg: (be[i], 0, 0))
    bspec = pl.BlockSpec((None, 1, D_FF), lambda i, be, sg: (be[i], 0, 0))
    rows = pl.BlockSpec((bm * ROW_TILE, LANES), lambda i, be, sg: (i, 0))
    grid_spec = pltpu.PrefetchScalarGridSpec(
        num_scalar_prefetch=2,
        grid=(n_slots // bm,),
        in_specs=[rows, wspec, bspec, wspec, bspec, wspec, bspec],
        out_specs=rows,
        scratch_shapes=[pltpu.VMEM((D_MODEL, D_FF), BF16)] * 3,
    )
    return pl.pallas_call(
        _moe_kernel,
        grid_spec=grid_spec,
        out_shape=jax.ShapeDtypeStruct((n_slots * ROW_TILE, LANES), F32),
        compiler_params=_params(("arbitrary",), 56),
        name="moe_experts",
    )(bexp, seg.reshape(-1), xs, wg, bg.reshape(N_EXPERTS, 1, D_FF), wu, bu.reshape(N_EXPERTS, 1, D_FF),
      wd, bd.reshape(N_EXPERTS, 1, D_MODEL))


def _combine_kernel(dest_ref, dest_next_ref, x1_ref, gw_ref, yb_ref, y_ref, rows_ref, sems):
    tm = x1_ref.shape[0]
    i = pl.program_id(0)
    slot = i % 2

    def issue(dref, buf):
        def body(r, carry):
            for k in range(TOP_K):
                pltpu.make_async_copy(yb_ref.at[_tile_rows(dref[r * TOP_K + k])], rows_ref.at[buf, k, _tile_rows(r)],
                                      sems.at[buf]).start(priority=k % 2)
            return carry

        lax.fori_loop(0, tm, body, 0, unroll=4)

    @pl.when(i == 0)
    def _():
        issue(dest_ref, slot)

    @pl.when(i + 1 < pl.num_programs(0))
    def _():
        issue(dest_next_ref, 1 - slot)

    for k in range(TOP_K):
        pltpu.make_async_copy(yb_ref.at[pl.ds(0, tm * ROW_TILE)], rows_ref.at[slot, k], sems.at[slot]).wait()
    gw = gw_ref[...]
    for j in range(ROW_TILE):
        ls = slice(j * LANES, (j + 1) * LANES)
        y = x1_ref[:, ls]
        for k in range(TOP_K):
            y = y + gw[:, k:k + 1] * rows_ref[slot, k, pl.ds(j, tm, stride=ROW_TILE), :]
        y_ref[:, ls] = y


def _combine(x1, gw, dest, yb, tm):
    n = x1.shape[0]
    assert n % tm == 0
    steps = n // tm
    return pl.pallas_call(
        _combine_kernel,
        grid=(steps,),
        in_specs=[pl.BlockSpec((tm * TOP_K,), lambda i: (i,), memory_space=pltpu.SMEM),
                  pl.BlockSpec((tm * TOP_K,), lambda i: (jnp.minimum(i + 1, steps - 1),), memory_space=pltpu.SMEM),
                  pl.BlockSpec((tm, D_MODEL), lambda i: (i, 0)),
                  pl.BlockSpec((tm, TOP_K), lambda i: (i, 0)),
                  pl.BlockSpec(memory_space=pl.ANY)],
        out_specs=pl.BlockSpec((tm, D_MODEL), lambda i: (i, 0)),
        out_shape=jax.ShapeDtypeStruct((n, D_MODEL), F32),
        scratch_shapes=[pltpu.VMEM((2, TOP_K, tm * ROW_TILE, LANES), F32), pltpu.SemaphoreType.DMA((2,))],
        compiler_params=_params(("arbitrary",), 32),
        name="combine",
    )(dest, dest, x1, gw, yb)


def _rope_tables(positions):
    half = ROPE_DIM // 2
    inv = jnp.power(jnp.float32(ROPE_THETA), -jnp.arange(half, dtype=F32) / half)
    ang = positions.astype(F32)[:, None] * inv[None, :]
    cos, sin = jnp.cos(ang), jnp.sin(ang)
    t = positions.shape[0]
    z8 = jnp.zeros((t, half), F32)
    rest0 = jnp.zeros((t, HEAD_DIM_A - ROPE_DIM), F32)
    a = jnp.concatenate([cos, cos, jnp.ones((t, HEAD_DIM_A - ROPE_DIM), F32)], axis=1)
    bp = jnp.concatenate([z8, sin, rest0], axis=1)
    bm = jnp.concatenate([-sin, z8, rest0], axis=1)
    return tuple(jnp.tile(m, (1, LANES // HEAD_DIM_A)) for m in (a, bp, bm))


def _segment_mean_matrix(seg):
    i = jnp.arange(256)
    return jnp.where((i[:, None] // seg) == (i[None, :] // seg), 1.0 / seg, 0.0).astype(BF16)


def _layer_weights(l, norm1_g, w_in, b_gates, qn_a, kn_a, conv_w, conv_b, cn_g, cn_b, qn_m,
                   w_proj_a, w_proj_b, w_proj_m, w_out, norm2_g, w_router, b_router,
                   w_gate, b_gate, w_up, b_up, w_down, b_down):
    wr = jnp.pad(w_router[l], ((0, 0), (0, LANES - N_EXPERTS)))
    wrh = wr.astype(BF16)
    return dict(
        g1=norm1_g[l][None, :],
        w1=w_in[l][:, :W1_COLS].astype(BF16),
        wgates=w_in[l][:, W1_COLS:].astype(BF16),
        bgates=b_gates[l][None, :],
        qg=jnp.tile(qn_a[l], 256 // HEAD_DIM_A)[None, :],
        kg=jnp.tile(kn_a[l], 256 // HEAD_DIM_A)[None, :],
        mg=jnp.tile(qn_m[l], 256 // MEM_HEAD_DIM)[None, :],
        conv_w=conv_w[l], conv_b=conv_b[l][None, :], cn_g=cn_g[l][None, :], cn_b=cn_b[l][None, :],
        wa=w_proj_a[l].astype(BF16), wb=w_proj_b[l].astype(BF16), wm=w_proj_m[l].astype(BF16),
        wo=w_out[l].astype(BF16), g2=norm2_g[l][None, :],
        wrh=wrh, wrl=(wr - wrh.astype(F32)).astype(BF16),
        br=jnp.pad(b_router[l], (0, LANES - N_EXPERTS))[None, :],
        w_gate=w_gate[l], b_gate=b_gate[l], w_up=w_up[l], b_up=b_up[l], w_down=w_down[l], b_down=b_down[l],
        seg64=_segment_mean_matrix(HEAD_DIM_A), seg128=_segment_mean_matrix(MEM_HEAD_DIM),
    )


def _token_mixer_tail(x2d, os_, lses, ob, om, w, tm_merge, moe_rows, tm_route, tm_rows):
    n = x2d.shape[0]
    x1, h2, idx, gw = _merge(x2d, os_, lses, ob, om, w["g1"], w["wgates"], w["bgates"], w["wa"], w["wb"], w["wm"],
                             w["wo"], w["g2"], w["wrh"], w["wrl"], w["br"], tm_merge)
    n_blocks = -(-(n * TOP_K) // moe_rows) + N_EXPERTS
    dest, bexp, seg = _route(idx, moe_rows, n_blocks, tm_route)
    xs = _dispatch(h2, dest, seg, n_blocks * moe_rows, tm_rows)
    yb = _moe(xs, bexp, seg, w["w_gate"], w["b_gate"], w["w_up"], w["b_up"], w["w_down"], w["b_down"], moe_rows)
    return _combine(x1, gw, dest, yb, min(tm_rows, 128))


def _prompt_layer(x, mem, w, mem_norm_g, w_mem_kv, kn_m):
    b, s, _ = x.shape
    n = b * s
    x2d = x.reshape(n, D_MODEL)
    tabs = _rope_tables(jnp.arange(s, dtype=I32))
    q, k, v, u, qm = _in_proj(x2d, w["g1"], w["w1"], w["qg"], w["kg"], w["mg"], tabs, w["seg64"], w["seg128"], 512, F32)
    q3, k3, v3 = (t.reshape(b, s, WIDTH_A) for t in (q, k, v))
    os_, lses, caches = [], [], []
    for g, (win, dil) in enumerate(SWA_GROUPS):
        o, lse = _swa_prompt(q3, k3, v3, g, dil)
        os_.append(o)
        lses.append(lse)
        cs = slice(g * GROUP_W, (g + 1) * GROUP_W)
        caches += [k3[:, s - win:, cs].reshape(b, win, 4, HEAD_DIM_A), v3[:, s - win:, cs].reshape(b, win, 4, HEAD_DIM_A)]
    u3 = u.reshape(b, s, CONV_CH)
    ob = _conv_branch(u3, u3, w["conv_w"], w["conv_b"], w["cn_g"], w["cn_b"], 256, True)
    mk, mv = _mem_kv(mem, mem_norm_g[None, :], w_mem_kv.astype(BF16), kn_m[None, :])
    om = _mem_attn(qm.reshape(b, s, MEM_WIDTH), mk, mv, 512)
    y = _token_mixer_tail(x2d, os_, lses, ob, om, w, 512, 512, 1024, 256)
    state = caches + [u3[:, s - (CONV_WIDTH - 1):], mk.reshape(b, MEM_LEN, MEM_HEADS, MEM_HEAD_DIM),
                      mv.reshape(b, MEM_LEN, MEM_HEADS, MEM_HEAD_DIM)]
    return y.reshape(b, s, D_MODEL), state


def _sample_layer(x, mem_k, mem_v, bufs, conv_state, w):
    b, t, _ = x.shape
    n = b * t
    x2d = x.reshape(n, D_MODEL)
    tabs = _rope_tables(jnp.tile(PAST_LEN + jnp.arange(t, dtype=I32), b))
    q, k, v, u, qm = _in_proj(x2d, w["g1"], w["w1"], w["qg"], w["kg"], w["mg"], tabs, w["seg64"], w["seg128"], 256, BF16)
    q3, k3, v3 = (a.reshape(b, t, WIDTH_A) for a in (q, k, v))
    os_, lses, caches = [], [], []
    for g, (win, dil) in enumerate(SWA_GROUPS):
        to_cm = lambda a: jnp.transpose(a, (0, 2, 3, 1)).reshape(b, GROUP_W, win)
        from_cm = lambda a: jnp.transpose(a.reshape(b, 4, HEAD_DIM_A, win), (0, 3, 1, 2))
        o, lse, kto, vto = _swa_sample(q3, k3, v3, to_cm(bufs[2 * g]), to_cm(bufs[2 * g + 1]), g, win, dil)
        os_.append(o)
        lses.append(lse)
        caches += [from_cm(kto), from_cm(vto)]
    u3 = u.reshape(b, t, CONV_CH)
    halo = jnp.pad(conv_state, ((0, 0), (CONV_HALO - (CONV_WIDTH - 1), 0), (0, 0)))
    ob = _conv_branch(u3, halo, w["conv_w"], w["conv_b"], w["cn_g"], w["cn_b"], t, False)
    om = _mem_attn_rows(qm.reshape(b, t, MEM_WIDTH), mem_k.reshape(b, MEM_LEN * MEM_HEADS, MEM_HEAD_DIM),
                        mem_v.reshape(b, MEM_LEN * MEM_HEADS, MEM_HEAD_DIM))
    y = _token_mixer_tail(x2d, os_, lses, ob, om, w, 256, 128, 512, 128)
    new_conv = jnp.concatenate([conv_state, u3], axis=1)[:, t:]
    return y.reshape(b, t, D_MODEL), caches + [new_conv]


def kernel(x_prompt, x_sample, mem_prompt, cache_swa_k_w128, cache_swa_v_w128, cache_swa_k_w512, cache_swa_v_w512, cache_swa_k_w2048, cache_swa_v_w2048, state_conv, cache_mem_k, cache_mem_v, norm1_g, w_in, b_gates, qn_a, kn_a, conv_w, conv_b, cn_g, cn_b, mem_norm_g, w_mem_kv, qn_m, kn_m, w_proj_a, w_proj_b, w_proj_m, w_out, norm2_g, w_router, b_router, w_gate, b_gate, w_up, b_up, w_down, b_down):
    depth = norm1_g.shape[0]
    y_p, y_s = x_prompt, x_sample
    st_p, st_s = [], []
    for l in range(depth):
        w = _layer_weights(l, norm1_g, w_in, b_gates, qn_a, kn_a, conv_w, conv_b, cn_g, cn_b, qn_m,
                           w_proj_a, w_proj_b, w_proj_m, w_out, norm2_g, w_router, b_router,
                           w_gate, b_gate, w_up, b_up, w_down, b_down)
        y_p, state_p = _prompt_layer(y_p, mem_prompt, w, mem_norm_g[l], w_mem_kv[l], kn_m[l])
        bufs = (cache_swa_k_w128[l], cache_swa_v_w128[l], cache_swa_k_w512[l], cache_swa_v_w512[l],
                cache_swa_k_w2048[l], cache_swa_v_w2048[l])
        y_s, state_s = _sample_layer(y_s, cache_mem_k[l], cache_mem_v[l], bufs, state_conv[l], w)
        st_p.append(state_p)
        st_s.append(state_s)
    outs_p = [jnp.stack(a) for a in zip(*st_p)]
    outs_s = [jnp.stack(a) for a in zip(*st_s)]
    return (y_p, y_s, *outs_p, *outs_s)
```

```python
import functools

import jax
import jax.numpy as jnp
from jax import lax
from jax.experimental import pallas as pl
from jax.experimental.pallas import tpu as pltpu

F32 = jnp.float32
BF16 = jnp.bfloat16
I32 = jnp.int32

D_MODEL = 1024
HEAD_DIM_A = 64
GROUP_W = 256
N_GROUPS = 3
WIDTH_A = N_GROUPS * GROUP_W
SWA_GROUPS = ((128, 1), (512, 4), (2048, 16))
SWA_BLOCK = 128
ROPE_DIM = 16
ROPE_THETA = 500000.0
CONV_CH = 512
CONV_WIDTH = 31
CONV_HALO = 32
MEM_LEN = 256
MEM_HEADS = 4
MEM_HEAD_DIM = 128
MEM_WIDTH = 512
N_EXPERTS = 32
TOP_K = 4
D_FF = 1024
SWIGLU_LIMIT = 7.0
SWIGLU_ALPHA = 1.702
EPS = 1e-6
PAST_LEN = 8192
LANES = 128
ROW_TILE = D_MODEL // LANES
W1_COLS = 3 * WIDTH_A + 2 * CONV_CH + MEM_WIDTH
NEG_INF = float("-inf")
MIB = 2 ** 20


def _params(semantics, vmem_mib):
    return pltpu.CompilerParams(dimension_semantics=semantics, vmem_limit_bytes=vmem_mib * MIB)


def _rms(x, gain):
    return x * lax.rsqrt(jnp.mean(x * x, axis=-1, keepdims=True) + EPS) * gain


def _dot(a, b):
    return jnp.dot(a, b, preferred_element_type=F32)


def _dot_nt(a, b):
    return lax.dot_general(a, b, (((1,), (1,)), ((), ())), preferred_element_type=F32)


def _in_proj_kernel(x_ref, g1_ref, w_ref, qg_ref, kg_ref, mg_ref, ra_ref, rp_ref, rm_ref, s64_ref, s128_ref,
                    q_ref, k_ref, v_ref, u_ref, qm_ref):
    hb = _rms(x_ref[...], g1_ref[...]).astype(BF16)
    ra, rp, rm = ra_ref[...], rp_ref[...], rm_ref[...]

    def proj(c0):
        return _dot(hb, w_ref[:, c0:c0 + 256])

    def head_norm(p, seg_ref, gain):
        ms = _dot((p * p).astype(BF16), seg_ref[...])
        return p * lax.rsqrt(ms + EPS) * gain

    def rope(p):
        outs = []
        for j in range(2):
            pj = p[:, j * LANES:(j + 1) * LANES]
            outs.append(pj * ra + pltpu.roll(pj, 8, 1) * rp + pltpu.roll(pj, LANES - 8, 1) * rm)
        return jnp.concatenate(outs, axis=1)

    for c in range(3):
        cs = slice(c * 256, (c + 1) * 256)
        q_ref[:, cs] = rope(head_norm(proj(c * 256), s64_ref, qg_ref[...])).astype(q_ref.dtype)
        k_ref[:, cs] = rope(head_norm(proj(WIDTH_A + c * 256), s64_ref, kg_ref[...]))
        v_ref[:, cs] = proj(2 * WIDTH_A + c * 256)
    for c in range(2):
        cs = slice(c * 256, (c + 1) * 256)
        a = proj(3 * WIDTH_A + c * 256)
        gate = proj(3 * WIDTH_A + CONV_CH + c * 256)
        u_ref[:, cs] = a * jax.nn.sigmoid(gate)
        qm_ref[:, cs] = head_norm(proj(3 * WIDTH_A + 2 * CONV_CH + c * 256), s128_ref, mg_ref[...]).astype(BF16)


def _in_proj(x, g1, w1, qg, kg, mg, tabs, seg64, seg128, tm, q_dtype):
    n = x.shape[0]
    t_rows = tabs[0].shape[0]
    assert n % tm == 0 and t_rows % tm == 0
    t_blocks = t_rows // tm
    row = lambda i: (i, 0)
    const = lambda i: (0, 0)
    tab = lambda i: (i % t_blocks, 0)
    return pl.pallas_call(
        _in_proj_kernel,
        grid=(n // tm,),
        in_specs=[
            pl.BlockSpec((tm, D_MODEL), row),
            pl.BlockSpec((1, D_MODEL), const),
            pl.BlockSpec((D_MODEL, W1_COLS), const),
            pl.BlockSpec((1, 256), const), pl.BlockSpec((1, 256), const), pl.BlockSpec((1, 256), const),
            pl.BlockSpec((tm, LANES), tab), pl.BlockSpec((tm, LANES), tab), pl.BlockSpec((tm, LANES), tab),
            pl.BlockSpec((256, 256), const), pl.BlockSpec((256, 256), const),
        ],
        out_specs=[
            pl.BlockSpec((tm, WIDTH_A), row), pl.BlockSpec((tm, WIDTH_A), row), pl.BlockSpec((tm, WIDTH_A), row),
            pl.BlockSpec((tm, CONV_CH), row), pl.BlockSpec((tm, MEM_WIDTH), row),
        ],
        out_shape=[
            jax.ShapeDtypeStruct((n, WIDTH_A), q_dtype), jax.ShapeDtypeStruct((n, WIDTH_A), F32),
            jax.ShapeDtypeStruct((n, WIDTH_A), F32), jax.ShapeDtypeStruct((n, CONV_CH), F32),
            jax.ShapeDtypeStruct((n, MEM_WIDTH), BF16),
        ],
        compiler_params=_params(("arbitrary",), 48),
        name="in_proj",
    )(x, g1, w1, qg, kg, mg, *tabs, seg64, seg128)


def _swa_prompt_kernel(q_ref, k_ref, v_ref, o_ref, lse_ref, q_s, k_s, v_s, o_s, l_s, *, dil, nsub, unroll):
    n = pl.program_id(1)
    span = SWA_BLOCK * dil
    cur = n % 2
    prv = 1 - cur
    for s in range(2):
        ls = slice(s * LANES, (s + 1) * LANES)
        q_s[s] = q_ref[:, ls]
        k_s[cur, s] = k_ref[:, ls]
        v_s[cur, s] = v_ref[:, ls]

    @pl.when(n == 0)
    def _():
        k_s[prv] = jnp.zeros(k_s.shape[1:], F32)
        v_s[prv] = jnp.zeros(v_s.shape[1:], F32)

    shape = (SWA_BLOCK, 2 * SWA_BLOCK)
    col = lax.broadcasted_iota(I32, shape, 1)
    off = lax.broadcasted_iota(I32, shape, 0) + SWA_BLOCK - col
    band = (off >= 0) & (off <= SWA_BLOCK)
    band_first = band & ((n > 0) | (col >= SWA_BLOCK))
    head = col // HEAD_DIM_A

    hms = [head == h for h in range(4)]

    def both(ref, lead, rows):
        return jnp.concatenate([ref[(*lead, s, rows, slice(None))] for s in range(2)], axis=1)

    def blocks(items):
        rows_of, masks, qs, kks, vvs = [], [], [], [], []
        for j, r in items:
            rows = pl.ds(j * span + r, SWA_BLOCK, stride=dil)
            before = (prv, pl.ds((nsub - 1) * span + r, SWA_BLOCK, stride=dil)) if j == 0 else \
                     (cur, pl.ds((j - 1) * span + r, SWA_BLOCK, stride=dil))
            rows_of.append(rows)
            masks.append(band_first if j == 0 else band)
            qs.append(both(q_s, (), rows).astype(BF16))
            kks.append(jnp.concatenate([both(k_s, before[:1], before[1]), both(k_s, (cur,), rows)],
                                       axis=0).astype(BF16))
            vvs.append(jnp.concatenate([both(v_s, before[:1], before[1]), both(v_s, (cur,), rows)],
                                       axis=0).astype(BF16))
        pairs = [(b, h) for b in range(len(items)) for h in range(4)]
        ss = [jnp.where(masks[b], _dot_nt(jnp.where(hms[h], qs[b], jnp.zeros_like(qs[b])), kks[b])
                        * (HEAD_DIM_A ** -0.5), NEG_INF) for b, h in pairs]
        ms = [jnp.max(s, axis=1, keepdims=True) for s in ss]
        ps = [jnp.exp(s - m) for s, m in zip(ss, ms)]
        sums = [jnp.sum(p, axis=1, keepdims=True) for p in ps]
        ohs = [_dot(p.astype(BF16), vvs[b]) for p, (b, h) in zip(ps, pairs)]
        for b in range(len(items)):
            o_acc = jnp.zeros(shape, F32)
            lse_acc = jnp.zeros(shape, F32)
            for h in range(4):
                i = 4 * b + h
                o_acc = jnp.where(hms[h], ohs[i] / sums[i], o_acc)
                lse_acc = jnp.where(hms[h], ms[i] + jnp.log(sums[i]), lse_acc)
            for s in range(2):
                ls = slice(s * LANES, (s + 1) * LANES)
                o_s[s, rows_of[b], :] = o_acc[:, ls]
                l_s[s, rows_of[b], :] = lse_acc[:, ls]

    if nsub == 1:
        def trip(i, carry):
            blocks([(0, 2 * i), (0, 2 * i + 1)])
            return carry

        lax.fori_loop(0, dil // 2, trip, 0)
    else:
        def trip(r, carry):
            for j0 in range(0, nsub, 2):
                blocks([(j0, r), (j0 + 1, r)])
            return carry

        lax.fori_loop(0, dil, trip, 0)
    for s in range(2):
        ls = slice(s * LANES, (s + 1) * LANES)
        o_ref[:, ls] = o_s[s]
        lse_ref[:, ls] = l_s[s]


def _swa_prompt(q, k, v, g, dil):
    b, s, _ = q.shape
    nsub = max(1, 4 // dil)
    unroll = 2 if nsub == 1 else 1
    t = nsub * dil * SWA_BLOCK
    assert s % t == 0
    inp = pl.BlockSpec((None, t, GROUP_W), lambda bi, n: (bi, n, g))
    out = pl.BlockSpec((None, t, GROUP_W), lambda bi, n: (bi, n, 0))
    slab = lambda *lead: pltpu.VMEM((*lead, 2, t, LANES), F32)
    o, lse = pl.pallas_call(
        functools.partial(_swa_prompt_kernel, dil=dil, nsub=nsub, unroll=unroll),
        grid=(b, s // t),
        in_specs=[inp, inp, inp],
        out_specs=[out, out],
        out_shape=[jax.ShapeDtypeStruct((b, s, GROUP_W), F32)] * 2,
        scratch_shapes=[slab(), slab(2), slab(2), slab(), slab()],
        compiler_params=_params(("arbitrary", "arbitrary"), 48),
        name=f"swa_prompt_d{dil}",
    )(q, k, v)
    return o.reshape(b * s, GROUP_W), lse.reshape(b * s, GROUP_W)


def _swa_sample_kernel(*refs, win, dil):
    def one(i, carry):
        _swa_sample_one(*(r.at[i] for r in refs), win=win, dil=dil)
        return carry

    lax.fori_loop(0, refs[0].shape[0], one, 0)


def _swa_sample_one(q_ref, kn_ref, vn_ref, kt_ref, vt_ref, o_ref, lse_ref, kto_ref, vto_ref, *, win, dil):
    t_new = q_ref.shape[0]
    n_tiles = win // LANES
    pad = jnp.zeros((LANES - t_new, GROUP_W), F32)
    kn = jnp.concatenate([kn_ref[...], pad], axis=0)
    vn = jnp.concatenate([vn_ref[...], pad], axis=0)

    def channel_major(a):
        return jnp.concatenate([a[:, :LANES].T, a[:, LANES:].T], axis=0)

    knt, vnt = channel_major(kn), channel_major(vn)
    lane = lax.broadcasted_iota(I32, (GROUP_W, LANES), 1)
    keep = lane < LANES - t_new
    for src_ref, new_t, dst_ref in ((kt_ref, knt, kto_ref), (vt_ref, vnt, vto_ref)):
        nxt = pltpu.roll(src_ref[:, 0:LANES], LANES - t_new, 1)
        for j in range(n_tiles):
            this = nxt
            following = new_t if j + 1 == n_tiles else src_ref[:, (j + 1) * LANES:(j + 2) * LANES]
            nxt = pltpu.roll(following, LANES - t_new, 1)
            dst_ref[:, j * LANES:(j + 1) * LANES] = jnp.where(keep, this, nxt)

    qf = q_ref[...].astype(F32)
    ch_head = lax.broadcasted_iota(I32, (t_new, GROUP_W), 1) // HEAD_DIM_A
    qm = jnp.concatenate([jnp.where(ch_head == h, qf, 0.0) for h in range(4)], axis=0).astype(BF16)
    nq = 4 * t_new
    scale = HEAD_DIM_A ** -0.5
    s_c = _dot(qm, kt_ref[...].astype(BF16)) * scale
    s_n = _dot(qm, knt.astype(BF16)) * scale
    t_c = lax.broadcasted_iota(I32, (nq, win), 0) & (t_new - 1)
    d_c = win + t_c - lax.broadcasted_iota(I32, (nq, win), 1)
    s_c = jnp.where((d_c <= win) & ((d_c & (dil - 1)) == 0), s_c, NEG_INF)
    t_n = lax.broadcasted_iota(I32, (nq, LANES), 0) & (t_new - 1)
    d_n = t_n - lax.broadcasted_iota(I32, (nq, LANES), 1)
    s_n = jnp.where((d_n >= 0) & ((d_n & (dil - 1)) == 0), s_n, NEG_INF)
    m = jnp.maximum(jnp.max(s_c, axis=1, keepdims=True), jnp.max(s_n, axis=1, keepdims=True))
    p_c = jnp.exp(s_c - m)
    p_n = jnp.exp(s_n - m)
    l = jnp.sum(p_c, axis=1, keepdims=True) + jnp.sum(p_n, axis=1, keepdims=True)
    o_all = (_dot_nt(p_c.astype(BF16), vt_ref[...].astype(BF16)) + _dot(p_n.astype(BF16), vn.astype(BF16))) / l
    lse_all = m + jnp.log(l)
    o_acc = jnp.zeros((t_new, GROUP_W), F32)
    lse_acc = jnp.zeros((t_new, GROUP_W), F32)
    for h in range(4):
        hm = ch_head == h
        o_acc = jnp.where(hm, o_all[h * t_new:(h + 1) * t_new, :], o_acc)
        lse_acc = jnp.where(hm, lse_all[h * t_new:(h + 1) * t_new, :], lse_acc)
    o_ref[...] = o_acc
    lse_ref[...] = lse_acc


def _swa_sample(q, k_new, v_new, kt_buf, vt_buf, g, win, dil):
    b, t, _ = q.shape
    assert t == 8 and win % LANES == 0 and dil & (dil - 1) == 0
    bb = max(1, 2048 // win)
    assert b % bb == 0
    new = pl.BlockSpec((bb, t, GROUP_W), lambda bi: (bi, 0, g))
    buf = pl.BlockSpec((bb, GROUP_W, win), lambda bi: (bi, 0, 0))
    out = pl.BlockSpec((bb, t, GROUP_W), lambda bi: (bi, 0, 0))
    o, lse, kto, vto = pl.pallas_call(
        functools.partial(_swa_sample_kernel, win=win, dil=dil),
        grid=(b // bb,),
        in_specs=[new, new, new, buf, buf],
        out_specs=[out, out, buf, buf],
        out_shape=[jax.ShapeDtypeStruct((b, t, GROUP_W), F32)] * 2
        + [jax.ShapeDtypeStruct((b, GROUP_W, win), F32)] * 2,
        compiler_params=_params(("arbitrary",), 48),
        name=f"swa_sample_w{win}",
    )(q, k_new, v_new, kt_buf, vt_buf)
    return o.reshape(b * t, GROUP_W), lse.reshape(b * t, GROUP_W), kto, vto


def _conv_kernel(halo_ref, cur_ref, w_ref, b_ref, g_ref, beta_ref, o_ref, ctx_ref, *, zero_first_halo, chunk):
    tm = cur_ref.shape[0]
    halo = halo_ref[...]
    if zero_first_halo:
        halo = jnp.where(pl.program_id(1) == 0, 0.0, halo)
    ctx_ref[pl.ds(0, CONV_HALO), :] = halo
    ctx_ref[pl.ds(CONV_HALO, tm), :] = cur_ref[...]
    first = CONV_HALO - (CONV_WIDTH - 1)
    for c0 in range(0, tm, chunk):
        acc = jnp.zeros((chunk, CONV_CH), F32) + b_ref[...]
        for phase in range(8):
            taps = range(phase, CONV_WIDTH, 8)
            start, shift = divmod(first + phase, 8)
            need = chunk + 8 * (len(taps) - 1)
            if shift == 0:
                window = ctx_ref[pl.ds(c0 + 8 * start, need), :]
            else:
                rows = ctx_ref[pl.ds(c0 + 8 * start, need + 8), :]
                window = pltpu.roll(rows, need + 8 - shift, 0)[0:need, :]
            for a, w in enumerate(taps):
                acc = acc + window[8 * a:8 * a + chunk, :] * w_ref[pl.ds(w, 1), :]
        mu = jnp.mean(acc, axis=-1, keepdims=True)
        xc = acc - mu
        var = jnp.mean(xc * xc, axis=-1, keepdims=True)
        y = xc * lax.rsqrt(var + EPS) * g_ref[...] + beta_ref[...]
        o_ref[pl.ds(c0, chunk), :] = (y * jax.nn.sigmoid(y)).astype(BF16)


def _conv_branch(u, halo, conv_w, conv_b, cn_g, cn_b, tm, halo_from_u):
    b, t, _ = u.shape
    assert t % tm == 0
    ratio = tm // CONV_HALO if halo_from_u else 0
    halo_map = (lambda bi, i: (bi, jnp.maximum(i * ratio - 1, 0), 0)) if halo_from_u else (lambda bi, i: (bi, 0, 0))
    const = lambda bi, i: (0, 0)
    return pl.pallas_call(
        functools.partial(_conv_kernel, zero_first_halo=halo_from_u, chunk=min(tm, 64)),
        grid=(b, t // tm),
        in_specs=[
            pl.BlockSpec((None, CONV_HALO, CONV_CH), halo_map),
            pl.BlockSpec((None, tm, CONV_CH), lambda bi, i: (bi, i, 0)),
            pl.BlockSpec((CONV_WIDTH, CONV_CH), const),
            pl.BlockSpec((1, CONV_CH), const), pl.BlockSpec((1, CONV_CH), const), pl.BlockSpec((1, CONV_CH), const),
        ],
        out_specs=pl.BlockSpec((None, tm, CONV_CH), lambda bi, i: (bi, i, 0)),
        out_shape=jax.ShapeDtypeStruct((b, t, CONV_CH), BF16),
        scratch_shapes=[pltpu.VMEM((CONV_HALO + tm, CONV_CH), F32)],
        compiler_params=_params(("arbitrary", "arbitrary"), 32),
        name="conv_module",
    )(halo, u, conv_w, conv_b, cn_g, cn_b).reshape(b * t, CONV_CH)


def _mem_kv_kernel(mem_ref, g_ref, w_ref, kg_ref, k_ref, v_ref):
    hb = _rms(mem_ref[...], g_ref[...]).astype(BF16)
    for h in range(MEM_HEADS):
        cs = slice(h * MEM_HEAD_DIM, (h + 1) * MEM_HEAD_DIM)
        k_ref[:, cs] = _rms(_dot(hb, w_ref[:, cs]), kg_ref[...])
        v_ref[:, cs] = _dot(hb, w_ref[:, MEM_WIDTH + h * MEM_HEAD_DIM:MEM_WIDTH + (h + 1) * MEM_HEAD_DIM])


def _mem_kv(mem, g, w, kg):
    b = mem.shape[0]
    const = lambda bi: (0, 0)
    blk = pl.BlockSpec((None, MEM_LEN, MEM_WIDTH), lambda bi: (bi, 0, 0))
    return pl.pallas_call(
        _mem_kv_kernel,
        grid=(b,),
        in_specs=[pl.BlockSpec((None, MEM_LEN, D_MODEL), lambda bi: (bi, 0, 0)), pl.BlockSpec((1, D_MODEL), const),
                  pl.BlockSpec((D_MODEL, 2 * MEM_WIDTH), const), pl.BlockSpec((1, MEM_HEAD_DIM), const)],
        out_specs=[blk, blk],
        out_shape=[jax.ShapeDtypeStruct((b, MEM_LEN, MEM_WIDTH), F32)] * 2,
        compiler_params=_params(("arbitrary",), 32),
        name="mem_kv",
    )(mem, g, w, kg)


def _mem_attn_kernel(q_ref, k_ref, v_ref, o_ref):
    for h in range(MEM_HEADS):
        cs = slice(h * MEM_HEAD_DIM, (h + 1) * MEM_HEAD_DIM)
        s = _dot_nt(q_ref[:, cs], k_ref[:, cs].astype(BF16)) * (MEM_HEAD_DIM ** -0.5)
        m = jnp.max(s, axis=1, keepdims=True)
        p = jnp.exp(s - m)
        l = jnp.sum(p, axis=1, keepdims=True)
        o_ref[:, cs] = (_dot(p.astype(BF16), v_ref[:, cs].astype(BF16)) / l).astype(BF16)


def _mem_attn(q, k, v, tm):
    b, t, _ = q.shape
    assert t % tm == 0
    kv = pl.BlockSpec((None, MEM_LEN, MEM_WIDTH), lambda bi, i: (bi, 0, 0))
    qo = pl.BlockSpec((None, tm, MEM_WIDTH), lambda bi, i: (bi, i, 0))
    return pl.pallas_call(
        _mem_attn_kernel,
        grid=(b, t // tm),
        in_specs=[qo, kv, kv],
        out_specs=qo,
        out_shape=jax.ShapeDtypeStruct((b, t, MEM_WIDTH), BF16),
        compiler_params=_params(("arbitrary", "arbitrary"), 32),
        name="mem_attn",
    )(q, k, v).reshape(b * t, MEM_WIDTH)


def _mem_attn_rows_kernel(*refs):
    def one(i, carry):
        _mem_attn_rows_one(*(r.at[i] for r in refs))
        return carry

    lax.fori_loop(0, refs[0].shape[0], one, 0)


def _mem_attn_rows_one(q_ref, k_ref, v_ref, o_ref):
    t = q_ref.shape[0]
    qf = q_ref[...].astype(F32)
    qs = jnp.concatenate([qf[:, h * MEM_HEAD_DIM:(h + 1) * MEM_HEAD_DIM] for h in range(MEM_HEADS)], axis=0)
    s = _dot_nt(qs.astype(BF16), k_ref[...].astype(BF16)) * (MEM_HEAD_DIM ** -0.5)
    shape = s.shape
    same_head = (lax.broadcasted_iota(I32, shape, 0) // t) == (lax.broadcasted_iota(I32, shape, 1) & (MEM_HEADS - 1))
    s = jnp.where(same_head, s, NEG_INF)
    m = jnp.max(s, axis=1, keepdims=True)
    p = jnp.exp(s - m)
    l = jnp.sum(p, axis=1, keepdims=True)
    o = _dot(p.astype(BF16), v_ref[...].astype(BF16)) / l
    o_ref[...] = jnp.concatenate([o[h * t:(h + 1) * t, :] for h in range(MEM_HEADS)], axis=1).astype(BF16)


def _mem_attn_rows(q, k_rows, v_rows):
    b, t, _ = q.shape
    bb = 4
    assert b % bb == 0
    kv = pl.BlockSpec((bb, MEM_LEN * MEM_HEADS, MEM_HEAD_DIM), lambda bi: (bi, 0, 0))
    qo = pl.BlockSpec((bb, t, MEM_WIDTH), lambda bi: (bi, 0, 0))
    return pl.pallas_call(
        _mem_attn_rows_kernel,
        grid=(b // bb,),
        in_specs=[qo, kv, kv],
        out_specs=qo,
        out_shape=jax.ShapeDtypeStruct((b, t, MEM_WIDTH), BF16),
        compiler_params=_params(("arbitrary",), 32),
        name="mem_attn_rows",
    )(q, k_rows, v_rows).reshape(b * t, MEM_WIDTH)


def _merge_kernel(x_ref, o1_ref, o2_ref, o3_ref, l1_ref, l2_ref, l3_ref, ob_ref, om_ref,
                  g1_ref, wg_ref, bg_ref, wa_ref, wb_ref, wm_ref, wo_ref, g2_ref, wrh_ref, wrl_ref, br_ref,
                  x1_ref, h2_ref, idx_ref, gw_ref, mix_ref):
    tm = x_ref.shape[0]
    x = x_ref[...]
    hb = _rms(x, g1_ref[...]).astype(BF16)
    la, lb, lc = l1_ref[...], l2_ref[...], l3_ref[...]
    m = jnp.maximum(la, jnp.maximum(lb, lc))
    ea, eb, ec = jnp.exp(la - m), jnp.exp(lb - m), jnp.exp(lc - m)
    oa = ((ea * o1_ref[...] + eb * o2_ref[...] + ec * o3_ref[...]) / (ea + eb + ec)).astype(BF16)
    ob, om = ob_ref[...], om_ref[...]
    for j in range(4):
        cs = slice(j * 256, (j + 1) * 256)

        def gate(branch):
            c0 = branch * D_MODEL + j * 256
            return jax.nn.sigmoid(_dot(hb, wg_ref[:, c0:c0 + 256]) + bg_ref[:, c0:c0 + 256])

        mixed = gate(0) * _dot(oa, wa_ref[:, cs]) + gate(1) * _dot(ob, wb_ref[:, cs]) + gate(2) * _dot(om, wm_ref[:, cs])
        mix_ref[:, cs] = mixed.astype(BF16)
    x1 = x + _dot(mix_ref[...], wo_ref[...])
    x1_ref[...] = x1
    h2 = _rms(x1, g2_ref[...])
    for j in range(ROW_TILE):
        h2_ref[pl.ds(j, tm, stride=ROW_TILE), :] = h2[:, j * LANES:(j + 1) * LANES]
    hi = h2.astype(BF16)
    lo = (h2 - hi.astype(F32)).astype(BF16)
    logits = _dot(hi, wrh_ref[...]) + _dot(lo, wrh_ref[...]) + _dot(hi, wrl_ref[...]) + br_ref[...]
    lane = lax.broadcasted_iota(I32, (tm, LANES), 1)
    logits = jnp.where(lane < N_EXPERTS, logits, NEG_INF)
    vals, idxs = [], []
    for _ in range(TOP_K):
        mk = jnp.max(logits, axis=1, keepdims=True)
        ik = jnp.min(jnp.where(logits == mk, lane, LANES), axis=1, keepdims=True)
        vals.append(mk)
        idxs.append(ik)
        logits = jnp.where(lane == ik, NEG_INF, logits)
    es = [jnp.exp(v - vals[0]) for v in vals]
    den = es[0] + es[1] + es[2] + es[3]
    l4 = lax.broadcasted_iota(I32, (tm, TOP_K), 1)
    idx_out = jnp.zeros((tm, TOP_K), I32)
    gw_out = jnp.zeros((tm, TOP_K), F32)
    for k in range(TOP_K):
        idx_out = jnp.where(l4 == k, idxs[k], idx_out)
        gw_out = jnp.where(l4 == k, es[k] / den, gw_out)
    idx_ref[...] = idx_out
    gw_ref[...] = gw_out


def _merge(x, os_, lses, ob, om, g1, wg, bg, wa, wb, wm, wo, g2, wrh, wrl, br, tm):
    n = x.shape[0]
    assert n % tm == 0
    row = lambda i: (i, 0)
    const = lambda i: (0, 0)
    rows = lambda w: pl.BlockSpec((tm, w), row)
    full = lambda a: pl.BlockSpec(a.shape, const)
    return pl.pallas_call(
        _merge_kernel,
        grid=(n // tm,),
        in_specs=[rows(D_MODEL)] + [rows(GROUP_W)] * 6 + [rows(CONV_CH), rows(MEM_WIDTH)]
        + [full(a) for a in (g1, wg, bg, wa, wb, wm, wo, g2, wrh, wrl, br)],
        out_specs=[rows(D_MODEL), pl.BlockSpec((tm * ROW_TILE, LANES), row), rows(TOP_K), rows(TOP_K)],
        out_shape=[jax.ShapeDtypeStruct((n, D_MODEL), F32), jax.ShapeDtypeStruct((n * ROW_TILE, LANES), F32),
                   jax.ShapeDtypeStruct((n, TOP_K), I32), jax.ShapeDtypeStruct((n, TOP_K), F32)],
        scratch_shapes=[pltpu.VMEM((tm, D_MODEL), BF16)],
        compiler_params=_params(("arbitrary",), 56),
        name="merge_router",
    )(x, *os_, *lses, ob, om, g1, wg, bg, wa, wb, wm, wo, g2, wrh, wrl, br)


def _route_kernel(idx_ref, dest_ref, bexp_ref, seg_ref, cnt_ref, carry_ref, *, block_rows):
    phase, i = pl.program_id(0), pl.program_id(1)
    tq = idx_ref.shape[0]
    nb = bexp_ref.shape[0]
    idx = idx_ref[...]
    lane = lax.broadcasted_iota(I32, (tq, LANES), 1)
    member = jnp.zeros((tq, LANES), F32)
    for k in range(TOP_K):
        member = member + (lane == idx[:, k:k + 1]).astype(F32)
    tile_counts = jnp.sum(member, axis=0, keepdims=True)

    @pl.when(phase == 0)
    def _():
        @pl.when(i == 0)
        def _():
            cnt_ref[...] = jnp.zeros_like(cnt_ref)

        cnt_ref[...] += tile_counts
        dest_ref[...] = jnp.zeros_like(dest_ref)
        bexp_ref[...] = jnp.zeros_like(bexp_ref)
        seg_ref[...] = jnp.zeros_like(seg_ref)

    @pl.when(phase == 1)
    def _():
        @pl.when(i == 0)
        def _():
            carry_ref[...] = jnp.zeros_like(carry_ref)

        counts = jnp.broadcast_to(cnt_ref[...], (8, LANES))
        padded = jnp.floor((counts + (block_rows - 1)) * (1.0 / block_rows)) * block_rows
        lane8 = lax.broadcasted_iota(I32, (8, LANES), 1)
        ends = padded
        for s in (1, 2, 4, 8, 16, 32, 64):
            ends = ends + jnp.where(lane8 >= s, pltpu.roll(ends, s, 1), 0.0)
        starts = (ends - padded)[0:1, :]
        r = lax.broadcasted_iota(I32, (tq, tq), 0)
        c = lax.broadcasted_iota(I32, (tq, tq), 1)
        earlier = _dot((c < r).astype(BF16), member.astype(BF16)) + carry_ref[...]
        carry_ref[...] += tile_counts
        base = earlier + starts
        l4 = lax.broadcasted_iota(I32, (tq, TOP_K), 1)
        dest = jnp.zeros((tq, TOP_K), F32)
        for k in range(TOP_K):
            dk = jnp.sum(jnp.where(lane == idx[:, k:k + 1], base, 0.0), axis=1, keepdims=True)
            dest = jnp.where(l4 == k, dk, dest)
        dest_ref[...] = dest.astype(I32)
        first_row = (lax.broadcasted_iota(I32, (nb, LANES), 0) * block_rows).astype(F32)
        lane_nb = lax.broadcasted_iota(I32, (nb, LANES), 1)
        done = jnp.where((ends[0:1, :] <= first_row) & (lane_nb < N_EXPERTS), 1.0, 0.0)
        bexp_ref[...] = jnp.minimum(jnp.sum(done, axis=1, keepdims=True), N_EXPERTS - 1.0).astype(I32)
        row8 = lax.broadcasted_iota(I32, (8, LANES), 0)
        seg = jnp.where(row8 == 0, ends - padded + counts, jnp.where(row8 == 1, ends, 0.0))
        seg_ref[...] = seg.astype(I32)


def _route(idx, block_rows, n_blocks, tq):
    n = idx.shape[0]
    assert n % tq == 0
    nb_pad = -(-n_blocks // 8) * 8
    dest, bexp, seg = pl.pallas_call(
        functools.partial(_route_kernel, block_rows=block_rows),
        grid=(2, n // tq),
        in_specs=[pl.BlockSpec((tq, TOP_K), lambda p, i: (i, 0))],
        out_specs=[pl.BlockSpec((tq, TOP_K), lambda p, i: (i * p, 0)),
                   pl.BlockSpec((nb_pad, 1), lambda p, i: (0, 0)),
                   pl.BlockSpec((8, LANES), lambda p, i: (0, 0))],
        out_shape=[jax.ShapeDtypeStruct((n, TOP_K), I32), jax.ShapeDtypeStruct((nb_pad, 1), I32),
                   jax.ShapeDtypeStruct((8, LANES), I32)],
        scratch_shapes=[pltpu.VMEM((1, LANES), F32), pltpu.VMEM((1, LANES), F32)],
        compiler_params=_params(("arbitrary", "arbitrary"), 40),
        name="route",
    )(idx)
    return dest.reshape(n * TOP_K), bexp.reshape(nb_pad)[:n_blocks], seg


def _tile_rows(index):
    return pl.ds(pl.multiple_of(index * ROW_TILE, ROW_TILE), ROW_TILE)


def _part_specs(parts, block):
    specs, bounds, first = [], [], 0
    for a in parts:
        assert a.shape[0] % block[0] == 0
        count = a.shape[0] // block[0]
        specs.append(pl.BlockSpec(block, lambda i, first=first, count=count: (jnp.clip(i - first, 0, count - 1), 0)))
        bounds.append((first, count))
        first += count
    return specs, tuple(bounds), first


def _dispatch_kernel(seg_ref, dest_ref, *refs, bounds):
    h_refs, (xs_ref, zero_ref, sem, zero_sem) = refs[:len(bounds)], refs[len(bounds):]
    tm = h_refs[0].shape[0] // ROW_TILE
    i = pl.program_id(0)

    @pl.when(pl.program_id(0) == 0)
    def _():
        zero_ref[...] = jnp.zeros_like(zero_ref)

        def zero_copy(slot):
            return pltpu.make_async_copy(zero_ref, xs_ref.at[_tile_rows(slot)], zero_sem)

        def start_expert(e):
            lax.fori_loop(seg_ref[0, e], seg_ref[1, e], lambda s, c: (zero_copy(s).start(), c)[1], 0)

        def wait_expert(e):
            lax.fori_loop(seg_ref[0, e], seg_ref[1, e], lambda s, c: (zero_copy(s).wait(), c)[1], 0)

        def expert(e, carry):
            start_expert(e)
            wait_expert(e - 1)
            return carry

        start_expert(0)
        lax.fori_loop(1, N_EXPERTS, expert, 0)
        wait_expert(N_EXPERTS - 1)

    for h_ref, (first, count) in zip(h_refs, bounds):
        @pl.when((i >= first) & (i < first + count))
        def _(h_ref=h_ref):
            def issue(r, carry):
                for k in range(TOP_K):
                    pltpu.make_async_copy(h_ref.at[_tile_rows(r)], xs_ref.at[_tile_rows(dest_ref[r * TOP_K + k])],
                                          sem).start(priority=k % 2)
                return carry

            lax.fori_loop(0, tm, issue, 0, unroll=4)

    for k in range(TOP_K):
        pltpu.make_async_copy(h_refs[0], xs_ref.at[pl.ds(0, tm * ROW_TILE)], sem).wait()


def _dispatch(h2t_parts, dest, seg, n_slots, tm):
    specs, bounds, steps = _part_specs(h2t_parts, (tm * ROW_TILE, LANES))
    return pl.pallas_call(
        functools.partial(_dispatch_kernel, bounds=bounds),
        grid=(steps,),
        in_specs=[pl.BlockSpec(memory_space=pltpu.SMEM),
                  pl.BlockSpec((tm * TOP_K,), lambda i: (i,), memory_space=pltpu.SMEM)] + specs,
        out_specs=pl.BlockSpec(memory_space=pl.ANY),
        out_shape=jax.ShapeDtypeStruct((n_slots * ROW_TILE, LANES), F32),
        scratch_shapes=[pltpu.VMEM((ROW_TILE, LANES), F32), pltpu.SemaphoreType.DMA, pltpu.SemaphoreType.DMA],
        compiler_params=_params(("arbitrary",), 32),
        name="dispatch",
    )(seg, dest, *h2t_parts)


def _moe_kernel(be_ref, seg_ref, xs_ref, wg_ref, bg_ref, wu_ref, bu_ref, wd_ref, bd_ref, yb_ref, wg_s, wu_s, wd_s):
    i = pl.program_id(0)
    bm = xs_ref.shape[0] // ROW_TILE
    used_rows = seg_ref[LANES + N_EXPERTS - 1]

    @pl.when(i * bm < used_rows)
    def _():
        changed = (i == 0) | (be_ref[i] != be_ref[jnp.maximum(i - 1, 0)])

        @pl.when(changed)
        def _():
            for r0 in range(0, D_MODEL, 128):
                rs = pl.ds(r0, 128)
                wg_s[rs, :] = wg_ref[rs, :].astype(BF16)
                wu_s[rs, :] = wu_ref[rs, :].astype(BF16)
                wd_s[rs, :] = wd_ref[rs, :].astype(BF16)

        xb = jnp.concatenate([xs_ref[pl.ds(j, bm, stride=ROW_TILE), :] for j in range(ROW_TILE)],
                             axis=1).astype(BF16)
        chunks = [slice(c * 256, (c + 1) * 256) for c in range(D_FF // 256)]
        gs = [_dot(xb, wg_s[:, cs]) + bg_ref[:, cs] for cs in chunks]
        us = [_dot(xb, wu_s[:, cs]) + bu_ref[:, cs] for cs in chunks]
        acts = []
        for g, u in zip(gs, us):
            g = jnp.minimum(g, SWIGLU_LIMIT)
            u = jnp.clip(u, -SWIGLU_LIMIT, SWIGLU_LIMIT)
            acts.append((g * jax.nn.sigmoid(SWIGLU_ALPHA * g) * (u + 1.0)).astype(BF16))
        acc = jnp.zeros((bm, D_MODEL), F32) + bd_ref[...]
        for act, cs in zip(acts, chunks):
            acc = acc + _dot(act, wd_s[cs, :])
        for j in range(ROW_TILE):
            yb_ref[pl.ds(j, bm, stride=ROW_TILE), :] = acc[:, j * LANES:(j + 1) * LANES]


def _moe(xs, bexp, seg, wg, bg, wu, bu, wd, bd, bm):
    n_slots = xs.shape[0] // ROW_TILE
    assert n_slots % bm == 0
    wspec = pl.BlockSpec((None, D_MODEL, D_FF), lambda i, be, sg: (be[i], 0, 0))
    bspec = pl.BlockSpec((None, 1, D_FF), lambda i, be, sg: (be[i], 0, 0))
    rows = pl.BlockSpec((bm * ROW_TILE, LANES), lambda i, be, sg: (i, 0))
    grid_spec = pltpu.PrefetchScalarGridSpec(
        num_scalar_prefetch=2,
        grid=(n_slots // bm,),
        in_specs=[rows, wspec, bspec, wspec, bspec, wspec, bspec],
        out_specs=rows,
        scratch_shapes=[pltpu.VMEM((D_MODEL, D_FF), BF16)] * 3,
    )
    return pl.pallas_call(
        _moe_kernel,
        grid_spec=grid_spec,
        out_shape=jax.ShapeDtypeStruct((n_slots * ROW_TILE, LANES), F32),
        compiler_params=_params(("arbitrary",), 56),
        name="moe_experts",
    )(bexp, seg.reshape(-1), xs, wg, bg.reshape(N_EXPERTS, 1, D_FF), wu, bu.reshape(N_EXPERTS, 1, D_FF),
      wd, bd.reshape(N_EXPERTS, 1, D_MODEL))


def _combine_kernel(dest_ref, dest_next_ref, gw_ref, yb_ref, *refs, bounds):
    nparts = len(bounds)
    x1_refs, y_refs, (rows_ref, sems) = refs[:nparts], refs[nparts:2 * nparts], refs[2 * nparts:]
    tm = x1_refs[0].shape[0]
    i = pl.program_id(0)
    slot = i % 2

    def issue(dref, buf):
        def body(r, carry):
            for k in range(TOP_K):
                pltpu.make_async_copy(yb_ref.at[_tile_rows(dref[r * TOP_K + k])], rows_ref.at[buf, k, _tile_rows(r)],
                                      sems.at[buf]).start(priority=k % 2)
            return carry

        lax.fori_loop(0, tm, body, 0, unroll=4)

    @pl.when(i == 0)
    def _():
        issue(dest_ref, slot)

    @pl.when(i + 1 < pl.num_programs(0))
    def _():
        issue(dest_next_ref, 1 - slot)

    for k in range(TOP_K):
        pltpu.make_async_copy(yb_ref.at[pl.ds(0, tm * ROW_TILE)], rows_ref.at[slot, k], sems.at[slot]).wait()
    gw = gw_ref[...]
    for x1_ref, y_ref, (first, count) in zip(x1_refs, y_refs, bounds):
        @pl.when((i >= first) & (i < first + count))
        def _(x1_ref=x1_ref, y_ref=y_ref):
            for j in range(ROW_TILE):
                ls = slice(j * LANES, (j + 1) * LANES)
                y = x1_ref[:, ls]
                for k in range(TOP_K):
                    y = y + gw[:, k:k + 1] * rows_ref[slot, k, pl.ds(j, tm, stride=ROW_TILE), :]
                y_ref[:, ls] = y


def _combine(x1_parts, gw, dest, yb, tm):
    specs, bounds, steps = _part_specs(x1_parts, (tm, D_MODEL))
    return pl.pallas_call(
        functools.partial(_combine_kernel, bounds=bounds),
        grid=(steps,),
        in_specs=[pl.BlockSpec((tm * TOP_K,), lambda i: (i,), memory_space=pltpu.SMEM),
                  pl.BlockSpec((tm * TOP_K,), lambda i: (jnp.minimum(i + 1, steps - 1),), memory_space=pltpu.SMEM),
                  pl.BlockSpec((tm, TOP_K), lambda i: (i, 0)),
                  pl.BlockSpec(memory_space=pl.ANY)] + specs,
        out_specs=specs,
        out_shape=[jax.ShapeDtypeStruct(a.shape, F32) for a in x1_parts],
        scratch_shapes=[pltpu.VMEM((2, TOP_K, tm * ROW_TILE, LANES), F32), pltpu.SemaphoreType.DMA((2,))],
        compiler_params=_params(("arbitrary",), 32),
        name="combine",
    )(dest, dest, gw, yb, *x1_parts)


def _rope_tables(positions):
    half = ROPE_DIM // 2
    inv = jnp.power(jnp.float32(ROPE_THETA), -jnp.arange(half, dtype=F32) / half)
    ang = positions.astype(F32)[:, None] * inv[None, :]
    cos, sin = jnp.cos(ang), jnp.sin(ang)
    t = positions.shape[0]
    z8 = jnp.zeros((t, half), F32)
    rest0 = jnp.zeros((t, HEAD_DIM_A - ROPE_DIM), F32)
    a = jnp.concatenate([cos, cos, jnp.ones((t, HEAD_DIM_A - ROPE_DIM), F32)], axis=1)
    bp = jnp.concatenate([z8, sin, rest0], axis=1)
    bm = jnp.concatenate([-sin, z8, rest0], axis=1)
    return tuple(jnp.tile(m, (1, LANES // HEAD_DIM_A)) for m in (a, bp, bm))


def _segment_mean_matrix(seg):
    i = jnp.arange(256)
    return jnp.where((i[:, None] // seg) == (i[None, :] // seg), 1.0 / seg, 0.0).astype(BF16)


def _layer_weights(l, norm1_g, w_in, b_gates, qn_a, kn_a, conv_w, conv_b, cn_g, cn_b, qn_m,
                   w_proj_a, w_proj_b, w_proj_m, w_out, norm2_g, w_router, b_router,
                   w_gate, b_gate, w_up, b_up, w_down, b_down):
    wr = jnp.pad(w_router[l], ((0, 0), (0, LANES - N_EXPERTS)))
    wrh = wr.astype(BF16)
    return dict(
        g1=norm1_g[l][None, :],
        w1=w_in[l][:, :W1_COLS].astype(BF16),
        wgates=w_in[l][:, W1_COLS:].astype(BF16),
        bgates=b_gates[l][None, :],
        qg=jnp.tile(qn_a[l], 256 // HEAD_DIM_A)[None, :],
        kg=jnp.tile(kn_a[l], 256 // HEAD_DIM_A)[None, :],
        mg=jnp.tile(qn_m[l], 256 // MEM_HEAD_DIM)[None, :],
        conv_w=conv_w[l], conv_b=conv_b[l][None, :], cn_g=cn_g[l][None, :], cn_b=cn_b[l][None, :],
        wa=w_proj_a[l].astype(BF16), wb=w_proj_b[l].astype(BF16), wm=w_proj_m[l].astype(BF16),
        wo=w_out[l].astype(BF16), g2=norm2_g[l][None, :],
        wrh=wrh, wrl=(wr - wrh.astype(F32)).astype(BF16),
        br=jnp.pad(b_router[l], (0, LANES - N_EXPERTS))[None, :],
        w_gate=w_gate[l], b_gate=b_gate[l], w_up=w_up[l], b_up=b_up[l], w_down=w_down[l], b_down=b_down[l],
        seg64=_segment_mean_matrix(HEAD_DIM_A), seg128=_segment_mean_matrix(MEM_HEAD_DIM),
    )


def _merge_group(x2d, os_, lses, ob, om, w, tm):
    return _merge(x2d, os_, lses, ob, om, w["g1"], w["wgates"], w["bgates"], w["wa"], w["wb"], w["wm"],
                  w["wo"], w["g2"], w["wrh"], w["wrl"], w["br"], tm)


MOE_ROWS = 512
ROUTE_TILE = 1024
DISPATCH_TILE = 256
COMBINE_TILE = 128


def _moe_groups(merged, w):
    x1s, h2ts, idxs, gws = zip(*merged)
    idx = jnp.concatenate(idxs, axis=0)
    gw = jnp.concatenate(gws, axis=0)
    n = idx.shape[0]
    n_blocks = -(-(n * TOP_K) // MOE_ROWS) + N_EXPERTS
    dest, bexp, seg = _route(idx, MOE_ROWS, n_blocks, ROUTE_TILE)
    xs = _dispatch(h2ts, dest, seg, n_blocks * MOE_ROWS, DISPATCH_TILE)
    yb = _moe(xs, bexp, seg, w["w_gate"], w["b_gate"], w["w_up"], w["b_up"], w["w_down"], w["b_down"], MOE_ROWS)
    return _combine(x1s, gw, dest, yb, COMBINE_TILE)


def _prompt_layer(x, mem, w, mem_norm_g, w_mem_kv, kn_m):
    b, s, _ = x.shape
    n = b * s
    x2d = x.reshape(n, D_MODEL)
    tabs = _rope_tables(jnp.arange(s, dtype=I32))
    q, k, v, u, qm = _in_proj(x2d, w["g1"], w["w1"], w["qg"], w["kg"], w["mg"], tabs, w["seg64"], w["seg128"], 512, F32)
    q3, k3, v3 = (t.reshape(b, s, WIDTH_A) for t in (q, k, v))
    os_, lses, caches = [], [], []
    for g, (win, dil) in enumerate(SWA_GROUPS):
        o, lse = _swa_prompt(q3, k3, v3, g, dil)
        os_.append(o)
        lses.append(lse)
        cs = slice(g * GROUP_W, (g + 1) * GROUP_W)
        caches += [k3[:, s - win:, cs].reshape(b, win, 4, HEAD_DIM_A), v3[:, s - win:, cs].reshape(b, win, 4, HEAD_DIM_A)]
    u3 = u.reshape(b, s, CONV_CH)
    ob = _conv_branch(u3, u3, w["conv_w"], w["conv_b"], w["cn_g"], w["cn_b"], 256, True)
    mk, mv = _mem_kv(mem, mem_norm_g[None, :], w_mem_kv.astype(BF16), kn_m[None, :])
    om = _mem_attn(qm.reshape(b, s, MEM_WIDTH), mk, mv, 512)
    merged = _merge_group(x2d, os_, lses, ob, om, w, 512)
    state = caches + [u3[:, s - (CONV_WIDTH - 1):], mk.reshape(b, MEM_LEN, MEM_HEADS, MEM_HEAD_DIM),
                      mv.reshape(b, MEM_LEN, MEM_HEADS, MEM_HEAD_DIM)]
    return merged, state


def _sample_layer(x, mem_k, mem_v, bufs, conv_state, w):
    b, t, _ = x.shape
    n = b * t
    x2d = x.reshape(n, D_MODEL)
    tabs = _rope_tables(jnp.tile(PAST_LEN + jnp.arange(t, dtype=I32), b))
    q, k, v, u, qm = _in_proj(x2d, w["g1"], w["w1"], w["qg"], w["kg"], w["mg"], tabs, w["seg64"], w["seg128"], 256, BF16)
    q3, k3, v3 = (a.reshape(b, t, WIDTH_A) for a in (q, k, v))
    os_, lses, caches = [], [], []
    for g, (win, dil) in enumerate(SWA_GROUPS):
        to_cm = lambda a: jnp.transpose(a, (0, 2, 3, 1)).reshape(b, GROUP_W, win)
        from_cm = lambda a: jnp.transpose(a.reshape(b, 4, HEAD_DIM_A, win), (0, 3, 1, 2))
        o, lse, kto, vto = _swa_sample(q3, k3, v3, to_cm(bufs[2 * g]), to_cm(bufs[2 * g + 1]), g, win, dil)
        os_.append(o)
        lses.append(lse)
        caches += [from_cm(kto), from_cm(vto)]
    u3 = u.reshape(b, t, CONV_CH)
    halo = jnp.pad(conv_state, ((0, 0), (CONV_HALO - (CONV_WIDTH - 1), 0), (0, 0)))
    ob = _conv_branch(u3, halo, w["conv_w"], w["conv_b"], w["cn_g"], w["cn_b"], t, False)
    om = _mem_attn_rows(qm.reshape(b, t, MEM_WIDTH), mem_k.reshape(b, MEM_LEN * MEM_HEADS, MEM_HEAD_DIM),
                        mem_v.reshape(b, MEM_LEN * MEM_HEADS, MEM_HEAD_DIM))
    merged = _merge_group(x2d, os_, lses, ob, om, w, 256)
    new_conv = jnp.concatenate([conv_state, u3], axis=1)[:, t:]
    return merged, caches + [new_conv]


def kernel(x_prompt, x_sample, mem_prompt, cache_swa_k_w128, cache_swa_v_w128, cache_swa_k_w512, cache_swa_v_w512, cache_swa_k_w2048, cache_swa_v_w2048, state_conv, cache_mem_k, cache_mem_v, norm1_g, w_in, b_gates, qn_a, kn_a, conv_w, conv_b, cn_g, cn_b, mem_norm_g, w_mem_kv, qn_m, kn_m, w_proj_a, w_proj_b, w_proj_m, w_out, norm2_g, w_router, b_router, w_gate, b_gate, w_up, b_up, w_down, b_down):
    depth = norm1_g.shape[0]
    y_p, y_s = x_prompt, x_sample
    st_p, st_s = [], []
    for l in range(depth):
        w = _layer_weights(l, norm1_g, w_in, b_gates, qn_a, kn_a, conv_w, conv_b, cn_g, cn_b, qn_m,
                           w_proj_a, w_proj_b, w_proj_m, w_out, norm2_g, w_router, b_router,
                           w_gate, b_gate, w_up, b_up, w_down, b_down)
        merged_p, state_p = _prompt_layer(y_p, mem_prompt, w, mem_norm_g[l], w_mem_kv[l], kn_m[l])
        bufs = (cache_swa_k_w128[l], cache_swa_v_w128[l], cache_swa_k_w512[l], cache_swa_v_w512[l],
                cache_swa_k_w2048[l], cache_swa_v_w2048[l])
        merged_s, state_s = _sample_layer(y_s, cache_mem_k[l], cache_mem_v[l], bufs, state_conv[l], w)
        y_p2d, y_s2d = _moe_groups((merged_p, merged_s), w)
        y_p, y_s = y_p2d.reshape(y_p.shape), y_s2d.reshape(y_s.shape)
        st_p.append(state_p)
        st_s.append(state_s)
    outs_p = [jnp.stack(a) for a in zip(*st_p)]
    outs_s = [jnp.stack(a) for a in zip(*st_s)]
    return (y_p, y_s, *outs_p, *outs_s)
```

```python
import functools

import jax
import jax.numpy as jnp
from jax import lax
from jax.experimental import pallas as pl
from jax.experimental.pallas import tpu as pltpu

F32 = jnp.float32
BF16 = jnp.bfloat16
I32 = jnp.int32

D_MODEL = 1024
HEAD_DIM_A = 64
GROUP_W = 256
N_GROUPS = 3
WIDTH_A = N_GROUPS * GROUP_W
SWA_GROUPS = ((128, 1), (512, 4), (2048, 16))
SWA_BLOCK = 128
ROPE_DIM = 16
ROPE_THETA = 500000.0
CONV_CH = 512
CONV_WIDTH = 31
CONV_HALO = 32
MEM_LEN = 256
MEM_HEADS = 4
MEM_HEAD_DIM = 128
MEM_WIDTH = 512
N_EXPERTS = 32
TOP_K = 4
D_FF = 1024
SWIGLU_LIMIT = 7.0
SWIGLU_ALPHA = 1.702
EPS = 1e-6
PAST_LEN = 8192
LANES = 128
ROW_TILE = D_MODEL // LANES
W1_COLS = 3 * WIDTH_A + 2 * CONV_CH + MEM_WIDTH
NEG_INF = float("-inf")
MIB = 2 ** 20


def _params(semantics, vmem_mib):
    return pltpu.CompilerParams(dimension_semantics=semantics, vmem_limit_bytes=vmem_mib * MIB)


def _rms(x, gain):
    return x * lax.rsqrt(jnp.mean(x * x, axis=-1, keepdims=True) + EPS) * gain


def _dot(a, b):
    return jnp.dot(a, b, preferred_element_type=F32)


def _dot_nt(a, b):
    return lax.dot_general(a, b, (((1,), (1,)), ((), ())), preferred_element_type=F32)


def _in_proj_kernel(x_ref, g1_ref, w_ref, qg_ref, kg_ref, mg_ref, ra_ref, rp_ref, rm_ref, s64_ref, s128_ref,
                    q_ref, k_ref, v_ref, u_ref, qm_ref):
    hb = _rms(x_ref[...], g1_ref[...]).astype(BF16)
    ra, rp, rm = ra_ref[...], rp_ref[...], rm_ref[...]

    def proj(c0):
        return _dot(hb, w_ref[:, c0:c0 + 256])

    def head_norm(p, seg_ref, gain):
        ms = _dot((p * p).astype(BF16), seg_ref[...])
        return p * lax.rsqrt(ms + EPS) * gain

    def rope(p):
        outs = []
        for j in range(2):
            pj = p[:, j * LANES:(j + 1) * LANES]
            outs.append(pj * ra + pltpu.roll(pj, 8, 1) * rp + pltpu.roll(pj, LANES - 8, 1) * rm)
        return jnp.concatenate(outs, axis=1)

    for c in range(3):
        cs = slice(c * 256, (c + 1) * 256)
        q_ref[:, cs] = rope(head_norm(proj(c * 256), s64_ref, qg_ref[...])).astype(q_ref.dtype)
        k_ref[:, cs] = rope(head_norm(proj(WIDTH_A + c * 256), s64_ref, kg_ref[...]))
        v_ref[:, cs] = proj(2 * WIDTH_A + c * 256)
    for c in range(2):
        cs = slice(c * 256, (c + 1) * 256)
        a = proj(3 * WIDTH_A + c * 256)
        gate = proj(3 * WIDTH_A + CONV_CH + c * 256)
        u_ref[:, cs] = a * jax.nn.sigmoid(gate)
        qm_ref[:, cs] = head_norm(proj(3 * WIDTH_A + 2 * CONV_CH + c * 256), s128_ref, mg_ref[...]).astype(BF16)


def _in_proj(x, g1, w1, qg, kg, mg, tabs, seg64, seg128, tm, q_dtype):
    n = x.shape[0]
    t_rows = tabs[0].shape[0]
    assert n % tm == 0 and t_rows % tm == 0
    t_blocks = t_rows // tm
    row = lambda i: (i, 0)
    const = lambda i: (0, 0)
    tab = lambda i: (i % t_blocks, 0)
    return pl.pallas_call(
        _in_proj_kernel,
        grid=(n // tm,),
        in_specs=[
            pl.BlockSpec((tm, D_MODEL), row),
            pl.BlockSpec((1, D_MODEL), const),
            pl.BlockSpec((D_MODEL, W1_COLS), const),
            pl.BlockSpec((1, 256), const), pl.BlockSpec((1, 256), const), pl.BlockSpec((1, 256), const),
            pl.BlockSpec((tm, LANES), tab), pl.BlockSpec((tm, LANES), tab), pl.BlockSpec((tm, LANES), tab),
            pl.BlockSpec((256, 256), const), pl.BlockSpec((256, 256), const),
        ],
        out_specs=[
            pl.BlockSpec((tm, WIDTH_A), row), pl.BlockSpec((tm, WIDTH_A), row), pl.BlockSpec((tm, WIDTH_A), row),
            pl.BlockSpec((tm, CONV_CH), row), pl.BlockSpec((tm, MEM_WIDTH), row),
        ],
        out_shape=[
            jax.ShapeDtypeStruct((n, WIDTH_A), q_dtype), jax.ShapeDtypeStruct((n, WIDTH_A), F32),
            jax.ShapeDtypeStruct((n, WIDTH_A), F32), jax.ShapeDtypeStruct((n, CONV_CH), F32),
            jax.ShapeDtypeStruct((n, MEM_WIDTH), BF16),
        ],
        compiler_params=_params(("arbitrary",), 48),
        name="in_proj",
    )(x, g1, w1, qg, kg, mg, *tabs, seg64, seg128)


def _swa_prompt_kernel(q_ref, k_ref, v_ref, o_ref, lse_ref, q_s, k_s, v_s, o_s, l_s, *, dil, nsub):
    n = pl.program_id(1)
    span = SWA_BLOCK * dil
    cur = n % 2
    prv = 1 - cur
    for s in range(2):
        ls = slice(s * LANES, (s + 1) * LANES)
        q_s[s] = q_ref[:, ls]
        k_s[cur, s] = k_ref[:, ls]
        v_s[cur, s] = v_ref[:, ls]

    @pl.when(n == 0)
    def _():
        k_s[prv] = jnp.zeros(k_s.shape[1:], F32)
        v_s[prv] = jnp.zeros(v_s.shape[1:], F32)

    shape = (SWA_BLOCK, 2 * SWA_BLOCK)
    col = lax.broadcasted_iota(I32, shape, 1)
    off = lax.broadcasted_iota(I32, shape, 0) + SWA_BLOCK - col
    band = (off >= 0) & (off <= SWA_BLOCK)
    band_first = band & ((n > 0) | (col >= SWA_BLOCK))
    head = col // HEAD_DIM_A

    hms = [head == h for h in range(4)]

    def both(ref, lead, rows):
        return jnp.concatenate([ref[(*lead, s, rows, slice(None))] for s in range(2)], axis=1)

    def blocks(items):
        rows_of, masks, qs, kks, vvs = [], [], [], [], []
        for j, r in items:
            rows = pl.ds(j * span + r, SWA_BLOCK, stride=dil)
            before = (prv, pl.ds((nsub - 1) * span + r, SWA_BLOCK, stride=dil)) if j == 0 else \
                     (cur, pl.ds((j - 1) * span + r, SWA_BLOCK, stride=dil))
            rows_of.append(rows)
            masks.append(band_first if j == 0 else band)
            qs.append(both(q_s, (), rows).astype(BF16))
            kks.append(jnp.concatenate([both(k_s, before[:1], before[1]), both(k_s, (cur,), rows)],
                                       axis=0).astype(BF16))
            vvs.append(jnp.concatenate([both(v_s, before[:1], before[1]), both(v_s, (cur,), rows)],
                                       axis=0).astype(BF16))
        pairs = [(b, h) for b in range(len(items)) for h in range(4)]
        ss = [jnp.where(masks[b], _dot_nt(jnp.where(hms[h], qs[b], jnp.zeros_like(qs[b])), kks[b])
                        * (HEAD_DIM_A ** -0.5), NEG_INF) for b, h in pairs]
        ms = [jnp.max(s, axis=1, keepdims=True) for s in ss]
        ps = [jnp.exp(s - m) for s, m in zip(ss, ms)]
        sums = [jnp.sum(p, axis=1, keepdims=True) for p in ps]
        ohs = [_dot(p.astype(BF16), vvs[b]) for p, (b, h) in zip(ps, pairs)]
        for b in range(len(items)):
            o_acc = jnp.zeros(shape, F32)
            lse_acc = jnp.zeros(shape, F32)
            for h in range(4):
                i = 4 * b + h
                o_acc = jnp.where(hms[h], ohs[i] / sums[i], o_acc)
                lse_acc = jnp.where(hms[h], ms[i] + jnp.log(sums[i]), lse_acc)
            for s in range(2):
                ls = slice(s * LANES, (s + 1) * LANES)
                o_s[s, rows_of[b], :] = o_acc[:, ls]
                l_s[s, rows_of[b], :] = lse_acc[:, ls]

    if nsub == 1:
        def trip(i, carry):
            blocks([(0, 2 * i), (0, 2 * i + 1)])
            return carry

        lax.fori_loop(0, dil // 2, trip, 0)
    else:
        def trip(r, carry):
            for j0 in range(0, nsub, 2):
                blocks([(j0, r), (j0 + 1, r)])
            return carry

        lax.fori_loop(0, dil, trip, 0)
    for s in range(2):
        ls = slice(s * LANES, (s + 1) * LANES)
        o_ref[:, ls] = o_s[s]
        lse_ref[:, ls] = l_s[s]


def _swa_prompt(q, k, v, g, dil):
    b, s, _ = q.shape
    nsub = max(1, 4 // dil)
    t = nsub * dil * SWA_BLOCK
    assert s % t == 0
    inp = pl.BlockSpec((None, t, GROUP_W), lambda bi, n: (bi, n, g))
    out = pl.BlockSpec((None, t, GROUP_W), lambda bi, n: (bi, n, 0))
    slab = lambda *lead: pltpu.VMEM((*lead, 2, t, LANES), F32)
    o, lse = pl.pallas_call(
        functools.partial(_swa_prompt_kernel, dil=dil, nsub=nsub),
        grid=(b, s // t),
        in_specs=[inp, inp, inp],
        out_specs=[out, out],
        out_shape=[jax.ShapeDtypeStruct((b, s, GROUP_W), F32)] * 2,
        scratch_shapes=[slab(), slab(2), slab(2), slab(), slab()],
        compiler_params=_params(("arbitrary", "arbitrary"), 48),
        name=f"swa_prompt_d{dil}",
    )(q, k, v)
    return o.reshape(b * s, GROUP_W), lse.reshape(b * s, GROUP_W)


def _swa_sample_kernel(*refs, win, dil):
    def one(i, carry):
        _swa_sample_one(*(r.at[i] for r in refs), win=win, dil=dil)
        return carry

    lax.fori_loop(0, refs[0].shape[0], one, 0)


def _swa_sample_one(q_ref, kn_ref, vn_ref, kt_ref, vt_ref, o_ref, lse_ref, kto_ref, vto_ref, *, win, dil):
    t_new = q_ref.shape[0]
    n_tiles = win // LANES
    pad = jnp.zeros((LANES - t_new, GROUP_W), F32)
    kn = jnp.concatenate([kn_ref[...], pad], axis=0)
    vn = jnp.concatenate([vn_ref[...], pad], axis=0)

    def channel_major(a):
        return jnp.concatenate([a[:, :LANES].T, a[:, LANES:].T], axis=0)

    knt, vnt = channel_major(kn), channel_major(vn)
    lane = lax.broadcasted_iota(I32, (GROUP_W, LANES), 1)
    keep = lane < LANES - t_new
    for src_ref, new_t, dst_ref in ((kt_ref, knt, kto_ref), (vt_ref, vnt, vto_ref)):
        nxt = pltpu.roll(src_ref[:, 0:LANES], LANES - t_new, 1)
        for j in range(n_tiles):
            this = nxt
            following = new_t if j + 1 == n_tiles else src_ref[:, (j + 1) * LANES:(j + 2) * LANES]
            nxt = pltpu.roll(following, LANES - t_new, 1)
            dst_ref[:, j * LANES:(j + 1) * LANES] = jnp.where(keep, this, nxt)

    qf = q_ref[...].astype(F32)
    ch_head = lax.broadcasted_iota(I32, (t_new, GROUP_W), 1) // HEAD_DIM_A
    qm = jnp.concatenate([jnp.where(ch_head == h, qf, 0.0) for h in range(4)], axis=0).astype(BF16)
    nq = 4 * t_new
    scale = HEAD_DIM_A ** -0.5
    s_c = _dot(qm, kt_ref[...].astype(BF16)) * scale
    s_n = _dot(qm, knt.astype(BF16)) * scale
    t_c = lax.broadcasted_iota(I32, (nq, win), 0) & (t_new - 1)
    d_c = win + t_c - lax.broadcasted_iota(I32, (nq, win), 1)
    s_c = jnp.where((d_c <= win) & ((d_c & (dil - 1)) == 0), s_c, NEG_INF)
    t_n = lax.broadcasted_iota(I32, (nq, LANES), 0) & (t_new - 1)
    d_n = t_n - lax.broadcasted_iota(I32, (nq, LANES), 1)
    s_n = jnp.where((d_n >= 0) & ((d_n & (dil - 1)) == 0), s_n, NEG_INF)
    m = jnp.maximum(jnp.max(s_c, axis=1, keepdims=True), jnp.max(s_n, axis=1, keepdims=True))
    p_c = jnp.exp(s_c - m)
    p_n = jnp.exp(s_n - m)
    l = jnp.sum(p_c, axis=1, keepdims=True) + jnp.sum(p_n, axis=1, keepdims=True)
    o_all = (_dot_nt(p_c.astype(BF16), vt_ref[...].astype(BF16)) + _dot(p_n.astype(BF16), vn.astype(BF16))) / l
    lse_all = m + jnp.log(l)
    o_acc = jnp.zeros((t_new, GROUP_W), F32)
    lse_acc = jnp.zeros((t_new, GROUP_W), F32)
    for h in range(4):
        hm = ch_head == h
        o_acc = jnp.where(hm, o_all[h * t_new:(h + 1) * t_new, :], o_acc)
        lse_acc = jnp.where(hm, lse_all[h * t_new:(h + 1) * t_new, :], lse_acc)
    o_ref[...] = o_acc
    lse_ref[...] = lse_acc


def _swa_sample(q, k_new, v_new, kt_buf, vt_buf, g, win, dil):
    b, t, _ = q.shape
    assert t == 8 and win % LANES == 0 and dil & (dil - 1) == 0
    bb = max(1, 2048 // win)
    assert b % bb == 0
    new = pl.BlockSpec((bb, t, GROUP_W), lambda bi: (bi, 0, g))
    buf = pl.BlockSpec((bb, GROUP_W, win), lambda bi: (bi, 0, 0))
    out = pl.BlockSpec((bb, t, GROUP_W), lambda bi: (bi, 0, 0))
    o, lse, kto, vto = pl.pallas_call(
        functools.partial(_swa_sample_kernel, win=win, dil=dil),
        grid=(b // bb,),
        in_specs=[new, new, new, buf, buf],
        out_specs=[out, out, buf, buf],
        out_shape=[jax.ShapeDtypeStruct((b, t, GROUP_W), F32)] * 2
        + [jax.ShapeDtypeStruct((b, GROUP_W, win), F32)] * 2,
        compiler_params=_params(("arbitrary",), 48),
        name=f"swa_sample_w{win}",
    )(q, k_new, v_new, kt_buf, vt_buf)
    return o.reshape(b * t, GROUP_W), lse.reshape(b * t, GROUP_W), kto, vto


def _conv_kernel(halo_ref, cur_ref, w_ref, b_ref, g_ref, beta_ref, o_ref, ctx_ref, *, zero_first_halo, chunk):
    halo = halo_ref[...]
    if zero_first_halo:
        halo = jnp.where(pl.program_id(1) == 0, 0.0, halo)
    _conv_module(halo, cur_ref, w_ref, b_ref, g_ref, beta_ref, o_ref, ctx_ref, chunk)


def _conv_module(halo, cur_ref, w_ref, b_ref, g_ref, beta_ref, o_ref, ctx_ref, chunk):
    for rows in _conv_module_parts(halo, cur_ref, w_ref, b_ref, g_ref, beta_ref, o_ref, ctx_ref, chunk):
        rows()


def _conv_module_parts(halo, cur_ref, w_ref, b_ref, g_ref, beta_ref, o_ref, ctx_ref, chunk):
    tm = cur_ref.shape[0]
    ctx_ref[pl.ds(0, CONV_HALO), :] = halo
    ctx_ref[pl.ds(CONV_HALO, tm), :] = cur_ref[...]
    first = CONV_HALO - (CONV_WIDTH - 1)

    def rows_from(c0):
        acc = jnp.zeros((chunk, CONV_CH), F32) + b_ref[...]
        for phase in range(8):
            taps = range(phase, CONV_WIDTH, 8)
            start, shift = divmod(first + phase, 8)
            need = chunk + 8 * (len(taps) - 1)
            if shift == 0:
                window = ctx_ref[pl.ds(c0 + 8 * start, need), :]
            else:
                rows = ctx_ref[pl.ds(c0 + 8 * start, need + 8), :]
                window = pltpu.roll(rows, need + 8 - shift, 0)[0:need, :]
            for a, w in enumerate(taps):
                acc = acc + window[8 * a:8 * a + chunk, :] * w_ref[pl.ds(w, 1), :]
        mu = jnp.mean(acc, axis=-1, keepdims=True)
        xc = acc - mu
        var = jnp.mean(xc * xc, axis=-1, keepdims=True)
        y = xc * lax.rsqrt(var + EPS) * g_ref[...] + beta_ref[...]
        o_ref[pl.ds(c0, chunk), :] = (y * jax.nn.sigmoid(y)).astype(BF16)

    return [functools.partial(rows_from, c0) for c0 in range(0, tm, chunk)]


def _conv_branch(u, halo, conv_w, conv_b, cn_g, cn_b, tm, halo_from_u):
    b, t, _ = u.shape
    assert t % tm == 0
    ratio = tm // CONV_HALO if halo_from_u else 0
    halo_map = (lambda bi, i: (bi, jnp.maximum(i * ratio - 1, 0), 0)) if halo_from_u else (lambda bi, i: (bi, 0, 0))
    const = lambda bi, i: (0, 0)
    return pl.pallas_call(
        functools.partial(_conv_kernel, zero_first_halo=halo_from_u, chunk=min(tm, 64)),
        grid=(b, t // tm),
        in_specs=[
            pl.BlockSpec((None, CONV_HALO, CONV_CH), halo_map),
            pl.BlockSpec((None, tm, CONV_CH), lambda bi, i: (bi, i, 0)),
            pl.BlockSpec((CONV_WIDTH, CONV_CH), const),
            pl.BlockSpec((1, CONV_CH), const), pl.BlockSpec((1, CONV_CH), const), pl.BlockSpec((1, CONV_CH), const),
        ],
        out_specs=pl.BlockSpec((None, tm, CONV_CH), lambda bi, i: (bi, i, 0)),
        out_shape=jax.ShapeDtypeStruct((b, t, CONV_CH), BF16),
        scratch_shapes=[pltpu.VMEM((CONV_HALO + tm, CONV_CH), F32)],
        compiler_params=_params(("arbitrary", "arbitrary"), 32),
        name="conv_module",
    )(halo, u, conv_w, conv_b, cn_g, cn_b).reshape(b * t, CONV_CH)


def _mem_kv_kernel(mem_ref, g_ref, w_ref, kg_ref, k_ref, v_ref):
    hb = _rms(mem_ref[...], g_ref[...]).astype(BF16)
    for h in range(MEM_HEADS):
        cs = slice(h * MEM_HEAD_DIM, (h + 1) * MEM_HEAD_DIM)
        k_ref[:, cs] = _rms(_dot(hb, w_ref[:, cs]), kg_ref[...])
        v_ref[:, cs] = _dot(hb, w_ref[:, MEM_WIDTH + h * MEM_HEAD_DIM:MEM_WIDTH + (h + 1) * MEM_HEAD_DIM])


def _mem_kv(mem, g, w, kg):
    b = mem.shape[0]
    const = lambda bi: (0, 0)
    blk = pl.BlockSpec((None, MEM_LEN, MEM_WIDTH), lambda bi: (bi, 0, 0))
    return pl.pallas_call(
        _mem_kv_kernel,
        grid=(b,),
        in_specs=[pl.BlockSpec((None, MEM_LEN, D_MODEL), lambda bi: (bi, 0, 0)), pl.BlockSpec((1, D_MODEL), const),
                  pl.BlockSpec((D_MODEL, 2 * MEM_WIDTH), const), pl.BlockSpec((1, MEM_HEAD_DIM), const)],
        out_specs=[blk, blk],
        out_shape=[jax.ShapeDtypeStruct((b, MEM_LEN, MEM_WIDTH), F32)] * 2,
        compiler_params=_params(("arbitrary",), 32),
        name="mem_kv",
    )(mem, g, w, kg)


def _mem_attn_kernel(q_ref, k_ref, v_ref, o_ref):
    for h in range(MEM_HEADS):
        cs = slice(h * MEM_HEAD_DIM, (h + 1) * MEM_HEAD_DIM)
        s = _dot_nt(q_ref[:, cs], k_ref[:, cs].astype(BF16)) * (MEM_HEAD_DIM ** -0.5)
        m = jnp.max(s, axis=1, keepdims=True)
        p = jnp.exp(s - m)
        l = jnp.sum(p, axis=1, keepdims=True)
        o_ref[:, cs] = (_dot(p.astype(BF16), v_ref[:, cs].astype(BF16)) / l).astype(BF16)


def _mem_attn(q, k, v, tm):
    b, t, _ = q.shape
    assert t % tm == 0
    kv = pl.BlockSpec((None, MEM_LEN, MEM_WIDTH), lambda bi, i: (bi, 0, 0))
    qo = pl.BlockSpec((None, tm, MEM_WIDTH), lambda bi, i: (bi, i, 0))
    return pl.pallas_call(
        _mem_attn_kernel,
        grid=(b, t // tm),
        in_specs=[qo, kv, kv],
        out_specs=qo,
        out_shape=jax.ShapeDtypeStruct((b, t, MEM_WIDTH), BF16),
        compiler_params=_params(("arbitrary", "arbitrary"), 32),
        name="mem_attn",
    )(q, k, v).reshape(b * t, MEM_WIDTH)


def _mem_attn_rows_kernel(*refs):
    def one(i, carry):
        _mem_attn_rows_one(*(r.at[i] for r in refs))
        return carry

    lax.fori_loop(0, refs[0].shape[0], one, 0)


def _mem_attn_rows_one(q_ref, k_ref, v_ref, o_ref):
    t = q_ref.shape[0]
    qf = q_ref[...].astype(F32)
    qs = jnp.concatenate([qf[:, h * MEM_HEAD_DIM:(h + 1) * MEM_HEAD_DIM] for h in range(MEM_HEADS)], axis=0)
    s = _dot_nt(qs.astype(BF16), k_ref[...].astype(BF16)) * (MEM_HEAD_DIM ** -0.5)
    shape = s.shape
    same_head = (lax.broadcasted_iota(I32, shape, 0) // t) == (lax.broadcasted_iota(I32, shape, 1) & (MEM_HEADS - 1))
    s = jnp.where(same_head, s, NEG_INF)
    m = jnp.max(s, axis=1, keepdims=True)
    p = jnp.exp(s - m)
    l = jnp.sum(p, axis=1, keepdims=True)
    o = _dot(p.astype(BF16), v_ref[...].astype(BF16)) / l
    o_ref[...] = jnp.concatenate([o[h * t:(h + 1) * t, :] for h in range(MEM_HEADS)], axis=1).astype(BF16)


def _mem_attn_rows(q, k_rows, v_rows):
    b, t, _ = q.shape
    bb = 4
    assert b % bb == 0
    kv = pl.BlockSpec((bb, MEM_LEN * MEM_HEADS, MEM_HEAD_DIM), lambda bi: (bi, 0, 0))
    qo = pl.BlockSpec((bb, t, MEM_WIDTH), lambda bi: (bi, 0, 0))
    return pl.pallas_call(
        _mem_attn_rows_kernel,
        grid=(b // bb,),
        in_specs=[qo, kv, kv],
        out_specs=qo,
        out_shape=jax.ShapeDtypeStruct((b, t, MEM_WIDTH), BF16),
        compiler_params=_params(("arbitrary",), 32),
        name="mem_attn_rows",
    )(q, k_rows, v_rows).reshape(b * t, MEM_WIDTH)


def _merge_kernel(x_ref, o1_ref, o2_ref, o3_ref, l1_ref, l2_ref, l3_ref, ob_ref, om_ref, *rest):
    _merge_body((), x_ref, o1_ref, o2_ref, o3_ref, l1_ref, l2_ref, l3_ref, ob_ref, om_ref, *rest)


def _merge_conv_kernel(x_ref, o1_ref, o2_ref, o3_ref, l1_ref, l2_ref, l3_ref, halo_ref, u_ref, om_ref,
                       g1_ref, wg_ref, bg_ref, wa_ref, wb_ref, wm_ref, wo_ref, g2_ref, wrh_ref, wrl_ref, br_ref,
                       cw_ref, cb_ref, cg_ref, cbeta_ref, x1_ref, h2_ref, idx_ref, gw_ref,
                       mix_ref, ctx_ref, ob_ref, *, tiles_per_seq):
    halo = jnp.where(pl.program_id(0) % tiles_per_seq == 0, 0.0, halo_ref[...])
    conv_parts = _conv_module_parts(halo, u_ref, cw_ref, cb_ref, cg_ref, cbeta_ref, ob_ref, ctx_ref, 64)
    _merge_body(conv_parts, x_ref, o1_ref, o2_ref, o3_ref, l1_ref, l2_ref, l3_ref, ob_ref, om_ref,
                g1_ref, wg_ref, bg_ref, wa_ref, wb_ref, wm_ref, wo_ref, g2_ref, wrh_ref, wrl_ref, br_ref,
                x1_ref, h2_ref, idx_ref, gw_ref, mix_ref)


def _merge_body(conv_parts, x_ref, o1_ref, o2_ref, o3_ref, l1_ref, l2_ref, l3_ref, ob_ref, om_ref,
                g1_ref, wg_ref, bg_ref, wa_ref, wb_ref, wm_ref, wo_ref, g2_ref, wrh_ref, wrl_ref, br_ref,
                x1_ref, h2_ref, idx_ref, gw_ref, mix_ref):
    tm = x_ref.shape[0]
    x = x_ref[...]
    hb = _rms(x, g1_ref[...]).astype(BF16)
    la, lb, lc = l1_ref[...], l2_ref[...], l3_ref[...]
    m = jnp.maximum(la, jnp.maximum(lb, lc))
    ea, eb, ec = jnp.exp(la - m), jnp.exp(lb - m), jnp.exp(lc - m)
    oa = ((ea * o1_ref[...] + eb * o2_ref[...] + ec * o3_ref[...]) / (ea + eb + ec)).astype(BF16)
    om = om_ref[...]
    chunks = [slice(j * 256, (j + 1) * 256) for j in range(4)]

    def gate(branch, j):
        c0 = branch * D_MODEL + j * 256
        return jax.nn.sigmoid(_dot(hb, wg_ref[:, c0:c0 + 256]) + bg_ref[:, c0:c0 + 256])

    per_j = -(-len(conv_parts) // 4)
    partial, gate_b = [], []
    for j, cs in enumerate(chunks):
        for part in conv_parts[j * per_j:(j + 1) * per_j]:
            part()
        partial.append(gate(0, j) * _dot(oa, wa_ref[:, cs]) + gate(2, j) * _dot(om, wm_ref[:, cs]))
        gate_b.append(gate(1, j))
    ob = ob_ref[...]
    for j, cs in enumerate(chunks):
        mix_ref[:, cs] = (partial[j] + gate_b[j] * _dot(ob, wb_ref[:, cs])).astype(BF16)
    x1 = x + _dot(mix_ref[...], wo_ref[...])
    x1_ref[...] = x1
    h2 = _rms(x1, g2_ref[...])
    for j in range(ROW_TILE):
        h2_ref[pl.ds(j, tm, stride=ROW_TILE), :] = h2[:, j * LANES:(j + 1) * LANES]
    hi = h2.astype(BF16)
    lo = (h2 - hi.astype(F32)).astype(BF16)
    logits = _dot(hi, wrh_ref[...]) + _dot(lo, wrh_ref[...]) + _dot(hi, wrl_ref[...]) + br_ref[...]
    lane = lax.broadcasted_iota(I32, (tm, LANES), 1)
    logits = jnp.where(lane < N_EXPERTS, logits, NEG_INF)
    vals, idxs = [], []
    for _ in range(TOP_K):
        mk = jnp.max(logits, axis=1, keepdims=True)
        ik = jnp.min(jnp.where(logits == mk, lane, LANES), axis=1, keepdims=True)
        vals.append(mk)
        idxs.append(ik)
        logits = jnp.where(lane == ik, NEG_INF, logits)
    es = [jnp.exp(v - vals[0]) for v in vals]
    den = es[0] + es[1] + es[2] + es[3]
    l4 = lax.broadcasted_iota(I32, (tm, TOP_K), 1)
    idx_out = jnp.zeros((tm, TOP_K), I32)
    gw_out = jnp.zeros((tm, TOP_K), F32)
    for k in range(TOP_K):
        idx_out = jnp.where(l4 == k, idxs[k], idx_out)
        gw_out = jnp.where(l4 == k, es[k] / den, gw_out)
    idx_ref[...] = idx_out
    gw_ref[...] = gw_out


def _merge(x, os_, lses, ob, om, g1, wg, bg, wa, wb, wm, wo, g2, wrh, wrl, br, tm, conv=None):
    n = x.shape[0]
    assert n % tm == 0
    row = lambda i: (i, 0)
    const = lambda i: (0, 0)
    rows = lambda w: pl.BlockSpec((tm, w), row)
    full = lambda a: pl.BlockSpec(a.shape, const)
    weights = (g1, wg, bg, wa, wb, wm, wo, g2, wrh, wrl, br)
    scratch = [pltpu.VMEM((tm, D_MODEL), BF16)]
    if conv is None:
        body, b_specs, b_args, extra = _merge_kernel, [rows(CONV_CH)], (ob,), ()
    else:
        *conv_params, seq_len = conv
        assert seq_len % tm == 0 and tm % CONV_HALO == 0
        ratio = tm // CONV_HALO
        body = functools.partial(_merge_conv_kernel, tiles_per_seq=seq_len // tm)
        b_specs = [pl.BlockSpec((CONV_HALO, CONV_CH), lambda i: (jnp.maximum(i * ratio - 1, 0), 0)), rows(CONV_CH)]
        b_args, extra = (ob, ob), tuple(conv_params)
        scratch += [pltpu.VMEM((CONV_HALO + tm, CONV_CH), F32), pltpu.VMEM((tm, CONV_CH), BF16)]
    return pl.pallas_call(
        body,
        grid=(n // tm,),
        in_specs=[rows(D_MODEL)] + [rows(GROUP_W)] * 6 + b_specs + [rows(MEM_WIDTH)]
        + [full(a) for a in weights + extra],
        out_specs=[rows(D_MODEL), pl.BlockSpec((tm * ROW_TILE, LANES), row), rows(TOP_K), rows(TOP_K)],
        out_shape=[jax.ShapeDtypeStruct((n, D_MODEL), F32), jax.ShapeDtypeStruct((n * ROW_TILE, LANES), F32),
                   jax.ShapeDtypeStruct((n, TOP_K), I32), jax.ShapeDtypeStruct((n, TOP_K), F32)],
        scratch_shapes=scratch,
        compiler_params=_params(("arbitrary",), 56),
        name="merge_router",
    )(x, *os_, *lses, *b_args, om, *weights, *extra)


def _route_kernel(idx_ref, dest_ref, bexp_ref, seg_ref, cnt_ref, carry_ref, *, block_rows):
    phase, i = pl.program_id(0), pl.program_id(1)
    tq = idx_ref.shape[0]
    nb = bexp_ref.shape[0]
    idx = idx_ref[...]
    lane = lax.broadcasted_iota(I32, (tq, LANES), 1)
    member = jnp.zeros((tq, LANES), F32)
    for k in range(TOP_K):
        member = member + (lane == idx[:, k:k + 1]).astype(F32)
    tile_counts = jnp.sum(member, axis=0, keepdims=True)

    @pl.when(phase == 0)
    def _():
        @pl.when(i == 0)
        def _():
            cnt_ref[...] = jnp.zeros_like(cnt_ref)

        cnt_ref[...] += tile_counts
        dest_ref[...] = jnp.zeros_like(dest_ref)
        bexp_ref[...] = jnp.zeros_like(bexp_ref)
        seg_ref[...] = jnp.zeros_like(seg_ref)

    @pl.when(phase == 1)
    def _():
        @pl.when(i == 0)
        def _():
            carry_ref[...] = jnp.zeros_like(carry_ref)

        counts = jnp.broadcast_to(cnt_ref[...], (8, LANES))
        padded = jnp.floor((counts + (block_rows - 1)) * (1.0 / block_rows)) * block_rows
        lane8 = lax.broadcasted_iota(I32, (8, LANES), 1)
        ends = padded
        for s in (1, 2, 4, 8, 16, 32, 64):
            ends = ends + jnp.where(lane8 >= s, pltpu.roll(ends, s, 1), 0.0)
        starts = (ends - padded)[0:1, :]
        r = lax.broadcasted_iota(I32, (tq, tq), 0)
        c = lax.broadcasted_iota(I32, (tq, tq), 1)
        earlier = _dot((c < r).astype(BF16), member.astype(BF16)) + carry_ref[...]
        carry_ref[...] += tile_counts
        base = earlier + starts
        l4 = lax.broadcasted_iota(I32, (tq, TOP_K), 1)
        dest = jnp.zeros((tq, TOP_K), F32)
        for k in range(TOP_K):
            dk = jnp.sum(jnp.where(lane == idx[:, k:k + 1], base, 0.0), axis=1, keepdims=True)
            dest = jnp.where(l4 == k, dk, dest)
        dest_ref[...] = dest.astype(I32)
        first_row = (lax.broadcasted_iota(I32, (nb, LANES), 0) * block_rows).astype(F32)
        lane_nb = lax.broadcasted_iota(I32, (nb, LANES), 1)
        done = jnp.where((ends[0:1, :] <= first_row) & (lane_nb < N_EXPERTS), 1.0, 0.0)
        bexp_ref[...] = jnp.minimum(jnp.sum(done, axis=1, keepdims=True), N_EXPERTS - 1.0).astype(I32)
        row8 = lax.broadcasted_iota(I32, (8, LANES), 0)
        seg = jnp.where(row8 == 0, ends - padded + counts, jnp.where(row8 == 1, ends, 0.0))
        seg_ref[...] = seg.astype(I32)


def _route(idx, block_rows, n_blocks, tq):
    n = idx.shape[0]
    assert n % tq == 0
    nb_pad = -(-n_blocks // 8) * 8
    dest, bexp, seg = pl.pallas_call(
        functools.partial(_route_kernel, block_rows=block_rows),
        grid=(2, n // tq),
        in_specs=[pl.BlockSpec((tq, TOP_K), lambda p, i: (i, 0))],
        out_specs=[pl.BlockSpec((tq, TOP_K), lambda p, i: (i * p, 0)),
                   pl.BlockSpec((nb_pad, 1), lambda p, i: (0, 0)),
                   pl.BlockSpec((8, LANES), lambda p, i: (0, 0))],
        out_shape=[jax.ShapeDtypeStruct((n, TOP_K), I32), jax.ShapeDtypeStruct((nb_pad, 1), I32),
                   jax.ShapeDtypeStruct((8, LANES), I32)],
        scratch_shapes=[pltpu.VMEM((1, LANES), F32), pltpu.VMEM((1, LANES), F32)],
        compiler_params=_params(("arbitrary", "arbitrary"), 40),
        name="route",
    )(idx)
    return dest.reshape(n * TOP_K), bexp.reshape(nb_pad)[:n_blocks], seg


def _tile_rows(index):
    return pl.ds(pl.multiple_of(index * ROW_TILE, ROW_TILE), ROW_TILE)


def _part_specs(parts, block):
    specs, bounds, first = [], [], 0
    for a in parts:
        assert a.shape[0] % block[0] == 0
        count = a.shape[0] // block[0]
        specs.append(pl.BlockSpec(block, lambda i, first=first, count=count: (jnp.clip(i - first, 0, count - 1), 0)))
        bounds.append((first, count))
        first += count
    return specs, tuple(bounds), first


def _dispatch_kernel(seg_ref, dest_ref, *refs, bounds):
    h_refs, (xs_ref, zero_ref, sem, zero_sem) = refs[:len(bounds)], refs[len(bounds):]
    tm = h_refs[0].shape[0] // ROW_TILE
    i = pl.program_id(0)

    @pl.when(pl.program_id(0) == 0)
    def _():
        zero_ref[...] = jnp.zeros_like(zero_ref)

        def zero_copy(slot):
            return pltpu.make_async_copy(zero_ref, xs_ref.at[_tile_rows(slot)], zero_sem)

        def start_expert(e):
            lax.fori_loop(seg_ref[0, e], seg_ref[1, e], lambda s, c: (zero_copy(s).start(), c)[1], 0)

        def wait_expert(e):
            lax.fori_loop(seg_ref[0, e], seg_ref[1, e], lambda s, c: (zero_copy(s).wait(), c)[1], 0)

        def expert(e, carry):
            start_expert(e)
            wait_expert(e - 1)
            return carry

        start_expert(0)
        lax.fori_loop(1, N_EXPERTS, expert, 0)
        wait_expert(N_EXPERTS - 1)

    for h_ref, (first, count) in zip(h_refs, bounds):
        @pl.when((i >= first) & (i < first + count))
        def _(h_ref=h_ref):
            def issue(r, carry):
                for k in range(TOP_K):
                    pltpu.make_async_copy(h_ref.at[_tile_rows(r)], xs_ref.at[_tile_rows(dest_ref[r * TOP_K + k])],
                                          sem).start(priority=k % 2)
                return carry

            lax.fori_loop(0, tm, issue, 0, unroll=4)

    for k in range(TOP_K):
        pltpu.make_async_copy(h_refs[0], xs_ref.at[pl.ds(0, tm * ROW_TILE)], sem).wait()


def _dispatch(h2t_parts, dest, seg, n_slots, tm):
    specs, bounds, steps = _part_specs(h2t_parts, (tm * ROW_TILE, LANES))
    return pl.pallas_call(
        functools.partial(_dispatch_kernel, bounds=bounds),
        grid=(steps,),
        in_specs=[pl.BlockSpec(memory_space=pltpu.SMEM),
                  pl.BlockSpec((tm * TOP_K,), lambda i: (i,), memory_space=pltpu.SMEM)] + specs,
        out_specs=pl.BlockSpec(memory_space=pl.ANY),
        out_shape=jax.ShapeDtypeStruct((n_slots * ROW_TILE, LANES), F32),
        scratch_shapes=[pltpu.VMEM((ROW_TILE, LANES), F32), pltpu.SemaphoreType.DMA, pltpu.SemaphoreType.DMA],
        compiler_params=_params(("arbitrary",), 32),
        name="dispatch",
    )(seg, dest, *h2t_parts)


def _moe_kernel(be_ref, seg_ref, xs_ref, wg_ref, bg_ref, wu_ref, bu_ref, wd_ref, bd_ref, yb_ref, wg_s, wu_s, wd_s):
    i = pl.program_id(0)
    bm = xs_ref.shape[0] // ROW_TILE
    used_rows = seg_ref[LANES + N_EXPERTS - 1]

    @pl.when(i * bm < used_rows)
    def _():
        changed = (i == 0) | (be_ref[i] != be_ref[jnp.maximum(i - 1, 0)])

        @pl.when(changed)
        def _():
            for r0 in range(0, D_MODEL, 128):
                rs = pl.ds(r0, 128)
                wg_s[rs, :] = wg_ref[rs, :].astype(BF16)
                wu_s[rs, :] = wu_ref[rs, :].astype(BF16)
                wd_s[rs, :] = wd_ref[rs, :].astype(BF16)

        xb = jnp.concatenate([xs_ref[pl.ds(j, bm, stride=ROW_TILE), :] for j in range(ROW_TILE)],
                             axis=1).astype(BF16)
        chunks = [slice(c * 256, (c + 1) * 256) for c in range(D_FF // 256)]
        gs = [_dot(xb, wg_s[:, cs]) + bg_ref[:, cs] for cs in chunks]
        us = [_dot(xb, wu_s[:, cs]) + bu_ref[:, cs] for cs in chunks]
        acts = []
        for g, u in zip(gs, us):
            g = jnp.minimum(g, SWIGLU_LIMIT)
            u = jnp.clip(u, -SWIGLU_LIMIT, SWIGLU_LIMIT)
            acts.append((g * jax.nn.sigmoid(SWIGLU_ALPHA * g) * (u + 1.0)).astype(BF16))
        acc = jnp.zeros((bm, D_MODEL), F32) + bd_ref[...]
        for act, cs in zip(acts, chunks):
            acc = acc + _dot(act, wd_s[cs, :])
        for j in range(ROW_TILE):
            yb_ref[pl.ds(j, bm, stride=ROW_TILE), :] = acc[:, j * LANES:(j + 1) * LANES]


def _moe(xs, bexp, seg, wg, bg, wu, bu, wd, bd, bm):
    n_slots = xs.shape[0] // ROW_TILE
    assert n_slots % bm == 0
    wspec = pl.BlockSpec((None, D_MODEL, D_FF), lambda i, be, sg: (be[i], 0, 0))
    bspec = pl.BlockSpec((None, 1, D_FF), lambda i, be, sg: (be[i], 0, 0))
    rows = pl.BlockSpec((bm * ROW_TILE, LANES), lambda i, be, sg: (i, 0))
    grid_spec = pltpu.PrefetchScalarGridSpec(
        num_scalar_prefetch=2,
        grid=(n_slots // bm,),
        in_specs=[rows, wspec, bspec, wspec, bspec, wspec, bspec],
        out_specs=rows,
        scratch_shapes=[pltpu.VMEM((D_MODEL, D_FF), BF16)] * 3,
    )
    return pl.pallas_call(
        _moe_kernel,
        grid_spec=grid_spec,
        out_shape=jax.ShapeDtypeStruct((n_slots * ROW_TILE, LANES), F32),
        compiler_params=_params(("arbitrary",), 56),
        name="moe_experts",
    )(bexp, seg.reshape(-1), xs, wg, bg.reshape(N_EXPERTS, 1, D_FF), wu, bu.reshape(N_EXPERTS, 1, D_FF),
      wd, bd.reshape(N_EXPERTS, 1, D_MODEL))


def _combine_kernel(dest_ref, dest_next_ref, gw_ref, yb_ref, *refs, bounds):
    nparts = len(bounds)
    x1_refs, y_refs, (rows_ref, sems) = refs[:nparts], refs[nparts:2 * nparts], refs[2 * nparts:]
    tm = x1_refs[0].shape[0]
    i = pl.program_id(0)
    slot = i % 2

    def issue(dref, buf):
        def body(r, carry):
            for k in range(TOP_K):
                pltpu.make_async_copy(yb_ref.at[_tile_rows(dref[r * TOP_K + k])], rows_ref.at[buf, k, _tile_rows(r)],
                                      sems.at[buf]).start(priority=k % 2)
            return carry

        lax.fori_loop(0, tm, body, 0, unroll=4)

    @pl.when(i == 0)
    def _():
        issue(dest_ref, slot)

    @pl.when(i + 1 < pl.num_programs(0))
    def _():
        issue(dest_next_ref, 1 - slot)

    for k in range(TOP_K):
        pltpu.make_async_copy(yb_ref.at[pl.ds(0, tm * ROW_TILE)], rows_ref.at[slot, k], sems.at[slot]).wait()
    gw = gw_ref[...]
    for x1_ref, y_ref, (first, count) in zip(x1_refs, y_refs, bounds):
        @pl.when((i >= first) & (i < first + count))
        def _(x1_ref=x1_ref, y_ref=y_ref):
            for j in range(ROW_TILE):
                ls = slice(j * LANES, (j + 1) * LANES)
                y = x1_ref[:, ls]
                for k in range(TOP_K):
                    y = y + gw[:, k:k + 1] * rows_ref[slot, k, pl.ds(j, tm, stride=ROW_TILE), :]
                y_ref[:, ls] = y


def _combine(x1_parts, gw, dest, yb, tm):
    specs, bounds, steps = _part_specs(x1_parts, (tm, D_MODEL))
    return pl.pallas_call(
        functools.partial(_combine_kernel, bounds=bounds),
        grid=(steps,),
        in_specs=[pl.BlockSpec((tm * TOP_K,), lambda i: (i,), memory_space=pltpu.SMEM),
                  pl.BlockSpec((tm * TOP_K,), lambda i: (jnp.minimum(i + 1, steps - 1),), memory_space=pltpu.SMEM),
                  pl.BlockSpec((tm, TOP_K), lambda i: (i, 0)),
                  pl.BlockSpec(memory_space=pl.ANY)] + specs,
        out_specs=specs,
        out_shape=[jax.ShapeDtypeStruct(a.shape, F32) for a in x1_parts],
        scratch_shapes=[pltpu.VMEM((2, TOP_K, tm * ROW_TILE, LANES), F32), pltpu.SemaphoreType.DMA((2,))],
        compiler_params=_params(("arbitrary",), 32),
        name="combine",
    )(dest, dest, gw, yb, *x1_parts)


def _rope_tables(positions):
    half = ROPE_DIM // 2
    inv = jnp.power(jnp.float32(ROPE_THETA), -jnp.arange(half, dtype=F32) / half)
    ang = positions.astype(F32)[:, None] * inv[None, :]
    cos, sin = jnp.cos(ang), jnp.sin(ang)
    t = positions.shape[0]
    z8 = jnp.zeros((t, half), F32)
    rest0 = jnp.zeros((t, HEAD_DIM_A - ROPE_DIM), F32)
    a = jnp.concatenate([cos, cos, jnp.ones((t, HEAD_DIM_A - ROPE_DIM), F32)], axis=1)
    bp = jnp.concatenate([z8, sin, rest0], axis=1)
    bm = jnp.concatenate([-sin, z8, rest0], axis=1)
    return tuple(jnp.tile(m, (1, LANES // HEAD_DIM_A)) for m in (a, bp, bm))


def _segment_mean_matrix(seg):
    i = jnp.arange(256)
    return jnp.where((i[:, None] // seg) == (i[None, :] // seg), 1.0 / seg, 0.0).astype(BF16)


def _layer_weights(l, norm1_g, w_in, b_gates, qn_a, kn_a, conv_w, conv_b, cn_g, cn_b, qn_m,
                   w_proj_a, w_proj_b, w_proj_m, w_out, norm2_g, w_router, b_router,
                   w_gate, b_gate, w_up, b_up, w_down, b_down):
    wr = jnp.pad(w_router[l], ((0, 0), (0, LANES - N_EXPERTS)))
    wrh = wr.astype(BF16)
    return dict(
        g1=norm1_g[l][None, :],
        w1=w_in[l][:, :W1_COLS].astype(BF16),
        wgates=w_in[l][:, W1_COLS:].astype(BF16),
        bgates=b_gates[l][None, :],
        qg=jnp.tile(qn_a[l], 256 // HEAD_DIM_A)[None, :],
        kg=jnp.tile(kn_a[l], 256 // HEAD_DIM_A)[None, :],
        mg=jnp.tile(qn_m[l], 256 // MEM_HEAD_DIM)[None, :],
        conv_w=conv_w[l], conv_b=conv_b[l][None, :], cn_g=cn_g[l][None, :], cn_b=cn_b[l][None, :],
        wa=w_proj_a[l].astype(BF16), wb=w_proj_b[l].astype(BF16), wm=w_proj_m[l].astype(BF16),
        wo=w_out[l].astype(BF16), g2=norm2_g[l][None, :],
        wrh=wrh, wrl=(wr - wrh.astype(F32)).astype(BF16),
        br=jnp.pad(b_router[l], (0, LANES - N_EXPERTS))[None, :],
        w_gate=w_gate[l], b_gate=b_gate[l], w_up=w_up[l], b_up=b_up[l], w_down=w_down[l], b_down=b_down[l],
        seg64=_segment_mean_matrix(HEAD_DIM_A), seg128=_segment_mean_matrix(MEM_HEAD_DIM),
    )


def _merge_group(x2d, os_, lses, ob, om, w, tm, conv_seq_len=None):
    conv = None if conv_seq_len is None else (w["conv_w"], w["conv_b"], w["cn_g"], w["cn_b"], conv_seq_len)
    return _merge(x2d, os_, lses, ob, om, w["g1"], w["wgates"], w["bgates"], w["wa"], w["wb"], w["wm"],
                  w["wo"], w["g2"], w["wrh"], w["wrl"], w["br"], tm, conv)


MOE_ROWS = 512
ROUTE_TILE = 1024
DISPATCH_TILE = 256
COMBINE_TILE = 128


def _moe_groups(merged, w):
    x1s, h2ts, idxs, gws = zip(*merged)
    idx = jnp.concatenate(idxs, axis=0)
    gw = jnp.concatenate(gws, axis=0)
    n = idx.shape[0]
    n_blocks = -(-(n * TOP_K) // MOE_ROWS) + N_EXPERTS
    dest, bexp, seg = _route(idx, MOE_ROWS, n_blocks, ROUTE_TILE)
    xs = _dispatch(h2ts, dest, seg, n_blocks * MOE_ROWS, DISPATCH_TILE)
    yb = _moe(xs, bexp, seg, w["w_gate"], w["b_gate"], w["w_up"], w["b_up"], w["w_down"], w["b_down"], MOE_ROWS)
    return _combine(x1s, gw, dest, yb, COMBINE_TILE)


def _prompt_layer(x, mem, w, mem_norm_g, w_mem_kv, kn_m):
    b, s, _ = x.shape
    n = b * s
    x2d = x.reshape(n, D_MODEL)
    tabs = _rope_tables(jnp.arange(s, dtype=I32))
    q, k, v, u, qm = _in_proj(x2d, w["g1"], w["w1"], w["qg"], w["kg"], w["mg"], tabs, w["seg64"], w["seg128"], 512, F32)
    q3, k3, v3 = (t.reshape(b, s, WIDTH_A) for t in (q, k, v))
    os_, lses, caches = [], [], []
    for g, (win, dil) in enumerate(SWA_GROUPS):
        o, lse = _swa_prompt(q3, k3, v3, g, dil)
        os_.append(o)
        lses.append(lse)
        cs = slice(g * GROUP_W, (g + 1) * GROUP_W)
        caches += [k3[:, s - win:, cs].reshape(b, win, 4, HEAD_DIM_A), v3[:, s - win:, cs].reshape(b, win, 4, HEAD_DIM_A)]
    u3 = u.reshape(b, s, CONV_CH)
    mk, mv = _mem_kv(mem, mem_norm_g[None, :], w_mem_kv.astype(BF16), kn_m[None, :])
    om = _mem_attn(qm.reshape(b, s, MEM_WIDTH), mk, mv, 512)
    merged = _merge_group(x2d, os_, lses, u, om, w, 512, conv_seq_len=s)
    state = caches + [u3[:, s - (CONV_WIDTH - 1):], mk.reshape(b, MEM_LEN, MEM_HEADS, MEM_HEAD_DIM),
                      mv.reshape(b, MEM_LEN, MEM_HEADS, MEM_HEAD_DIM)]
    return merged, state


def _sample_layer(x, mem_k, mem_v, bufs, conv_state, w):
    b, t, _ = x.shape
    n = b * t
    x2d = x.reshape(n, D_MODEL)
    tabs = _rope_tables(jnp.tile(PAST_LEN + jnp.arange(t, dtype=I32), b))
    q, k, v, u, qm = _in_proj(x2d, w["g1"], w["w1"], w["qg"], w["kg"], w["mg"], tabs, w["seg64"], w["seg128"], 256, BF16)
    q3, k3, v3 = (a.reshape(b, t, WIDTH_A) for a in (q, k, v))
    os_, lses, caches = [], [], []
    for g, (win, dil) in enumerate(SWA_GROUPS):
        to_cm = lambda a: jnp.transpose(a, (0, 2, 3, 1)).reshape(b, GROUP_W, win)
        from_cm = lambda a: jnp.transpose(a.reshape(b, 4, HEAD_DIM_A, win), (0, 3, 1, 2))
        o, lse, kto, vto = _swa_sample(q3, k3, v3, to_cm(bufs[2 * g]), to_cm(bufs[2 * g + 1]), g, win, dil)
        os_.append(o)
        lses.append(lse)
        caches += [from_cm(kto), from_cm(vto)]
    u3 = u.reshape(b, t, CONV_CH)
    halo = jnp.pad(conv_state, ((0, 0), (CONV_HALO - (CONV_WIDTH - 1), 0), (0, 0)))
    ob = _conv_branch(u3, halo, w["conv_w"], w["conv_b"], w["cn_g"], w["cn_b"], t, False)
    om = _mem_attn_rows(qm.reshape(b, t, MEM_WIDTH), mem_k.reshape(b, MEM_LEN * MEM_HEADS, MEM_HEAD_DIM),
                        mem_v.reshape(b, MEM_LEN * MEM_HEADS, MEM_HEAD_DIM))
    merged = _merge_group(x2d, os_, lses, ob, om, w, 256)
    new_conv = jnp.concatenate([conv_state, u3], axis=1)[:, t:]
    return merged, caches + [new_conv]


def kernel(x_prompt, x_sample, mem_prompt, cache_swa_k_w128, cache_swa_v_w128, cache_swa_k_w512, cache_swa_v_w512, cache_swa_k_w2048, cache_swa_v_w2048, state_conv, cache_mem_k, cache_mem_v, norm1_g, w_in, b_gates, qn_a, kn_a, conv_w, conv_b, cn_g, cn_b, mem_norm_g, w_mem_kv, qn_m, kn_m, w_proj_a, w_proj_b, w_proj_m, w_out, norm2_g, w_router, b_router, w_gate, b_gate, w_up, b_up, w_down, b_down):
    depth = norm1_g.shape[0]
    y_p, y_s = x_prompt, x_sample
    st_p, st_s = [], []
    for l in range(depth):
        w = _layer_weights(l, norm1_g, w_in, b_gates, qn_a, kn_a, conv_w, conv_b, cn_g, cn_b, qn_m,
                           w_proj_a, w_proj_b, w_proj_m, w_out, norm2_g, w_router, b_router,
                           w_gate, b_gate, w_up, b_up, w_down, b_down)
        merged_p, state_p = _prompt_layer(y_p, mem_prompt, w, mem_norm_g[l], w_mem_kv[l], kn_m[l])
        bufs = (cache_swa_k_w128[l], cache_swa_v_w128[l], cache_swa_k_w512[l], cache_swa_v_w512[l],
                cache_swa_k_w2048[l], cache_swa_v_w2048[l])
        merged_s, state_s = _sample_layer(y_s, cache_mem_k[l], cache_mem_v[l], bufs, state_conv[l], w)
        y_p2d, y_s2d = _moe_groups((merged_p, merged_s), w)
        y_p, y_s = y_p2d.reshape(y_p.shape), y_s2d.reshape(y_s.shape)
        st_p.append(state_p)
        st_s.append(state_s)
    outs_p = [jnp.stack(a) for a in zip(*st_p)]
    outs_s = [jnp.stack(a) for a in zip(*st_s)]
    return (y_p, y_s, *outs_p, *outs_s)
```

```python
import functools

import jax
import jax.numpy as jnp
from jax import lax
from jax.experimental import pallas as pl
from jax.experimental.pallas import tpu as pltpu

F32 = jnp.float32
BF16 = jnp.bfloat16
I32 = jnp.int32

D_MODEL = 1024
HEAD_DIM_A = 64
GROUP_W = 256
N_GROUPS = 3
WIDTH_A = N_GROUPS * GROUP_W
SWA_GROUPS = ((128, 1), (512, 4), (2048, 16))
SWA_BLOCK = 128
ROPE_DIM = 16
ROPE_THETA = 500000.0
CONV_CH = 512
CONV_WIDTH = 31
CONV_HALO = 32
MEM_LEN = 256
MEM_HEADS = 4
MEM_HEAD_DIM = 128
MEM_WIDTH = 512
N_EXPERTS = 32
TOP_K = 4
D_FF = 1024
SWIGLU_LIMIT = 7.0
SWIGLU_ALPHA = 1.702
EPS = 1e-6
PAST_LEN = 8192
LANES = 128
ROW_TILE = D_MODEL // LANES
W1_COLS = 3 * WIDTH_A + 2 * CONV_CH + MEM_WIDTH
NEG_INF = float("-inf")
MIB = 2 ** 20


def _params(semantics, vmem_mib):
    return pltpu.CompilerParams(dimension_semantics=semantics, vmem_limit_bytes=vmem_mib * MIB)


def _rms(x, gain):
    return x * lax.rsqrt(jnp.mean(x * x, axis=-1, keepdims=True) + EPS) * gain


def _dot(a, b):
    return jnp.dot(a, b, preferred_element_type=F32)


def _dot_nt(a, b):
    return lax.dot_general(a, b, (((1,), (1,)), ((), ())), preferred_element_type=F32)


def _in_proj_kernel(x_ref, g1_ref, w_ref, qg_ref, kg_ref, mg_ref, ra_ref, rp_ref, rm_ref, s64_ref, s128_ref,
                    q_ref, k_ref, v_ref, u_ref, qm_ref):
    hb = _rms(x_ref[...], g1_ref[...]).astype(BF16)
    ra, rp, rm = ra_ref[...], rp_ref[...], rm_ref[...]

    def proj(c0):
        return _dot(hb, w_ref[:, c0:c0 + 256])

    def head_norm(p, seg_ref, gain):
        ms = _dot((p * p).astype(BF16), seg_ref[...])
        return p * lax.rsqrt(ms + EPS) * gain

    def rope(p):
        outs = []
        for j in range(2):
            pj = p[:, j * LANES:(j + 1) * LANES]
            outs.append(pj * ra + pltpu.roll(pj, 8, 1) * rp + pltpu.roll(pj, LANES - 8, 1) * rm)
        return jnp.concatenate(outs, axis=1)

    for c in range(3):
        cs = slice(c * 256, (c + 1) * 256)
        q_ref[:, cs] = rope(head_norm(proj(c * 256), s64_ref, qg_ref[...])).astype(q_ref.dtype)
        k_ref[:, cs] = rope(head_norm(proj(WIDTH_A + c * 256), s64_ref, kg_ref[...]))
        v_ref[:, cs] = proj(2 * WIDTH_A + c * 256)
    for c in range(2):
        cs = slice(c * 256, (c + 1) * 256)
        a = proj(3 * WIDTH_A + c * 256)
        gate = proj(3 * WIDTH_A + CONV_CH + c * 256)
        u_ref[:, cs] = a * jax.nn.sigmoid(gate)
        qm_ref[:, cs] = head_norm(proj(3 * WIDTH_A + 2 * CONV_CH + c * 256), s128_ref, mg_ref[...]).astype(BF16)


def _in_proj(x, g1, w1, qg, kg, mg, tabs, seg64, seg128, tm, q_dtype):
    n = x.shape[0]
    t_rows = tabs[0].shape[0]
    assert n % tm == 0 and t_rows % tm == 0
    t_blocks = t_rows // tm
    row = lambda i: (i, 0)
    const = lambda i: (0, 0)
    tab = lambda i: (i % t_blocks, 0)
    return pl.pallas_call(
        _in_proj_kernel,
        grid=(n // tm,),
        in_specs=[
            pl.BlockSpec((tm, D_MODEL), row),
            pl.BlockSpec((1, D_MODEL), const),
            pl.BlockSpec((D_MODEL, W1_COLS), const),
            pl.BlockSpec((1, 256), const), pl.BlockSpec((1, 256), const), pl.BlockSpec((1, 256), const),
            pl.BlockSpec((tm, LANES), tab), pl.BlockSpec((tm, LANES), tab), pl.BlockSpec((tm, LANES), tab),
            pl.BlockSpec((256, 256), const), pl.BlockSpec((256, 256), const),
        ],
        out_specs=[
            pl.BlockSpec((tm, WIDTH_A), row), pl.BlockSpec((tm, WIDTH_A), row), pl.BlockSpec((tm, WIDTH_A), row),
            pl.BlockSpec((tm, CONV_CH), row), pl.BlockSpec((tm, MEM_WIDTH), row),
        ],
        out_shape=[
            jax.ShapeDtypeStruct((n, WIDTH_A), q_dtype), jax.ShapeDtypeStruct((n, WIDTH_A), F32),
            jax.ShapeDtypeStruct((n, WIDTH_A), F32), jax.ShapeDtypeStruct((n, CONV_CH), F32),
            jax.ShapeDtypeStruct((n, MEM_WIDTH), BF16),
        ],
        compiler_params=_params(("arbitrary",), 48),
        name="in_proj",
    )(x, g1, w1, qg, kg, mg, *tabs, seg64, seg128)


def _swa_prompt_kernel(q_ref, k_ref, v_ref, o_ref, lse_ref, q_s, k_s, v_s, o_s, l_s, *, dil, nsub):
    n = pl.program_id(1)
    span = SWA_BLOCK * dil
    cur = n % 2
    prv = 1 - cur
    for s in range(2):
        ls = slice(s * LANES, (s + 1) * LANES)
        q_s[s] = q_ref[:, ls]
        k_s[cur, s] = k_ref[:, ls]
        v_s[cur, s] = v_ref[:, ls]

    @pl.when(n == 0)
    def _():
        k_s[prv] = jnp.zeros(k_s.shape[1:], F32)
        v_s[prv] = jnp.zeros(v_s.shape[1:], F32)

    shape = (SWA_BLOCK, 2 * SWA_BLOCK)
    col = lax.broadcasted_iota(I32, shape, 1)
    off = lax.broadcasted_iota(I32, shape, 0) + SWA_BLOCK - col
    band = (off >= 0) & (off <= SWA_BLOCK)
    band_first = band & ((n > 0) | (col >= SWA_BLOCK))
    head = col // HEAD_DIM_A

    hms = [head == h for h in range(4)]

    def both(ref, lead, rows):
        return jnp.concatenate([ref[(*lead, s, rows, slice(None))] for s in range(2)], axis=1)

    def blocks(items):
        rows_of, masks, qs, kks, vvs = [], [], [], [], []
        for j, r in items:
            rows = pl.ds(j * span + r, SWA_BLOCK, stride=dil)
            before = (prv, pl.ds((nsub - 1) * span + r, SWA_BLOCK, stride=dil)) if j == 0 else \
                     (cur, pl.ds((j - 1) * span + r, SWA_BLOCK, stride=dil))
            rows_of.append(rows)
            masks.append(band_first if j == 0 else band)
            qs.append(both(q_s, (), rows).astype(BF16))
            kks.append(jnp.concatenate([both(k_s, before[:1], before[1]), both(k_s, (cur,), rows)],
                                       axis=0).astype(BF16))
            vvs.append(jnp.concatenate([both(v_s, before[:1], before[1]), both(v_s, (cur,), rows)],
                                       axis=0).astype(BF16))
        pairs = [(b, h) for b in range(len(items)) for h in range(4)]
        ss = [jnp.where(masks[b], _dot_nt(jnp.where(hms[h], qs[b], jnp.zeros_like(qs[b])), kks[b])
                        * (HEAD_DIM_A ** -0.5), NEG_INF) for b, h in pairs]
        ms = [jnp.max(s, axis=1, keepdims=True) for s in ss]
        ps = [jnp.exp(s - m) for s, m in zip(ss, ms)]
        sums = [jnp.sum(p, axis=1, keepdims=True) for p in ps]
        ohs = [_dot(p.astype(BF16), vvs[b]) for p, (b, h) in zip(ps, pairs)]
        for b in range(len(items)):
            o_acc = jnp.zeros(shape, F32)
            lse_acc = jnp.zeros(shape, F32)
            for h in range(4):
                i = 4 * b + h
                o_acc = jnp.where(hms[h], ohs[i] / sums[i], o_acc)
                lse_acc = jnp.where(hms[h], ms[i] + jnp.log(sums[i]), lse_acc)
            for s in range(2):
                ls = slice(s * LANES, (s + 1) * LANES)
                o_s[s, rows_of[b], :] = o_acc[:, ls]
                l_s[s, rows_of[b], :] = lse_acc[:, ls]

    if nsub == 1:
        def trip(i, carry):
            blocks([(0, 2 * i), (0, 2 * i + 1)])
            return carry

        lax.fori_loop(0, dil // 2, trip, 0)
    else:
        def trip(r, carry):
            for j0 in range(0, nsub, 2):
                blocks([(j0, r), (j0 + 1, r)])
            return carry

        lax.fori_loop(0, dil, trip, 0)
    for s in range(2):
        ls = slice(s * LANES, (s + 1) * LANES)
        o_ref[:, ls] = o_s[s]
        lse_ref[:, ls] = l_s[s]


def _swa_prompt(q, k, v, g, dil):
    b, s, _ = q.shape
    nsub = max(1, 4 // dil)
    t = nsub * dil * SWA_BLOCK
    assert s % t == 0
    inp = pl.BlockSpec((None, t, GROUP_W), lambda bi, n: (bi, n, g))
    out = pl.BlockSpec((None, t, GROUP_W), lambda bi, n: (bi, n, 0))
    slab = lambda *lead: pltpu.VMEM((*lead, 2, t, LANES), F32)
    o, lse = pl.pallas_call(
        functools.partial(_swa_prompt_kernel, dil=dil, nsub=nsub),
        grid=(b, s // t),
        in_specs=[inp, inp, inp],
        out_specs=[out, out],
        out_shape=[jax.ShapeDtypeStruct((b, s, GROUP_W), F32)] * 2,
        scratch_shapes=[slab(), slab(2), slab(2), slab(), slab()],
        compiler_params=_params(("arbitrary", "arbitrary"), 48),
        name=f"swa_prompt_d{dil}",
    )(q, k, v)
    return o.reshape(b * s, GROUP_W), lse.reshape(b * s, GROUP_W)


def _swa_sample_kernel(*refs, win, dil):
    def one(i, carry):
        _swa_sample_one(*(r.at[i] for r in refs), win=win, dil=dil)
        return carry

    lax.fori_loop(0, refs[0].shape[0], one, 0)


def _swa_sample_one(q_ref, kn_ref, vn_ref, kt_ref, vt_ref, o_ref, lse_ref, kto_ref, vto_ref, *, win, dil):
    t_new = q_ref.shape[0]
    n_tiles = win // LANES
    pad = jnp.zeros((LANES - t_new, GROUP_W), F32)
    kn = jnp.concatenate([kn_ref[...], pad], axis=0)
    vn = jnp.concatenate([vn_ref[...], pad], axis=0)

    def channel_major(a):
        return jnp.concatenate([a[:, :LANES].T, a[:, LANES:].T], axis=0)

    knt, vnt = channel_major(kn), channel_major(vn)
    lane = lax.broadcasted_iota(I32, (GROUP_W, LANES), 1)
    keep = lane < LANES - t_new
    for src_ref, new_t, dst_ref in ((kt_ref, knt, kto_ref), (vt_ref, vnt, vto_ref)):
        nxt = pltpu.roll(src_ref[:, 0:LANES], LANES - t_new, 1)
        for j in range(n_tiles):
            this = nxt
            following = new_t if j + 1 == n_tiles else src_ref[:, (j + 1) * LANES:(j + 2) * LANES]
            nxt = pltpu.roll(following, LANES - t_new, 1)
            dst_ref[:, j * LANES:(j + 1) * LANES] = jnp.where(keep, this, nxt)

    qf = q_ref[...].astype(F32)
    ch_head = lax.broadcasted_iota(I32, (t_new, GROUP_W), 1) // HEAD_DIM_A
    qm = jnp.concatenate([jnp.where(ch_head == h, qf, 0.0) for h in range(4)], axis=0).astype(BF16)
    nq = 4 * t_new
    scale = HEAD_DIM_A ** -0.5
    s_c = _dot(qm, kt_ref[...].astype(BF16)) * scale
    s_n = _dot(qm, knt.astype(BF16)) * scale
    t_c = lax.broadcasted_iota(I32, (nq, win), 0) & (t_new - 1)
    d_c = win + t_c - lax.broadcasted_iota(I32, (nq, win), 1)
    s_c = jnp.where((d_c <= win) & ((d_c & (dil - 1)) == 0), s_c, NEG_INF)
    t_n = lax.broadcasted_iota(I32, (nq, LANES), 0) & (t_new - 1)
    d_n = t_n - lax.broadcasted_iota(I32, (nq, LANES), 1)
    s_n = jnp.where((d_n >= 0) & ((d_n & (dil - 1)) == 0), s_n, NEG_INF)
    m = jnp.maximum(jnp.max(s_c, axis=1, keepdims=True), jnp.max(s_n, axis=1, keepdims=True))
    p_c = jnp.exp(s_c - m)
    p_n = jnp.exp(s_n - m)
    l = jnp.sum(p_c, axis=1, keepdims=True) + jnp.sum(p_n, axis=1, keepdims=True)
    o_all = (_dot_nt(p_c.astype(BF16), vt_ref[...].astype(BF16)) + _dot(p_n.astype(BF16), vn.astype(BF16))) / l
    lse_all = m + jnp.log(l)
    o_acc = jnp.zeros((t_new, GROUP_W), F32)
    lse_acc = jnp.zeros((t_new, GROUP_W), F32)
    for h in range(4):
        hm = ch_head == h
        o_acc = jnp.where(hm, o_all[h * t_new:(h + 1) * t_new, :], o_acc)
        lse_acc = jnp.where(hm, lse_all[h * t_new:(h + 1) * t_new, :], lse_acc)
    o_ref[...] = o_acc
    lse_ref[...] = lse_acc


def _swa_sample(q, k_new, v_new, kt_buf, vt_buf, g, win, dil):
    b, t, _ = q.shape
    assert t == 8 and win % LANES == 0 and dil & (dil - 1) == 0
    bb = max(1, 2048 // win)
    assert b % bb == 0
    new = pl.BlockSpec((bb, t, GROUP_W), lambda bi: (bi, 0, g))
    buf = pl.BlockSpec((bb, GROUP_W, win), lambda bi: (bi, 0, 0))
    out = pl.BlockSpec((bb, t, GROUP_W), lambda bi: (bi, 0, 0))
    o, lse, kto, vto = pl.pallas_call(
        functools.partial(_swa_sample_kernel, win=win, dil=dil),
        grid=(b // bb,),
        in_specs=[new, new, new, buf, buf],
        out_specs=[out, out, buf, buf],
        out_shape=[jax.ShapeDtypeStruct((b, t, GROUP_W), F32)] * 2
        + [jax.ShapeDtypeStruct((b, GROUP_W, win), F32)] * 2,
        compiler_params=_params(("arbitrary",), 48),
        name=f"swa_sample_w{win}",
    )(q, k_new, v_new, kt_buf, vt_buf)
    return o.reshape(b * t, GROUP_W), lse.reshape(b * t, GROUP_W), kto, vto


def _conv_kernel(halo_ref, cur_ref, w_ref, b_ref, g_ref, beta_ref, o_ref, ctx_ref, *, zero_first_halo, chunk):
    halo = halo_ref[...]
    if zero_first_halo:
        halo = jnp.where(pl.program_id(1) == 0, 0.0, halo)
    _conv_module(halo, cur_ref, w_ref, b_ref, g_ref, beta_ref, o_ref, ctx_ref, chunk)


def _conv_module(halo, cur_ref, w_ref, b_ref, g_ref, beta_ref, o_ref, ctx_ref, chunk):
    for rows in _conv_module_parts(halo, cur_ref, w_ref, b_ref, g_ref, beta_ref, o_ref, ctx_ref, chunk):
        rows()


def _conv_module_parts(halo, cur_ref, w_ref, b_ref, g_ref, beta_ref, o_ref, ctx_ref, chunk):
    tm = cur_ref.shape[0]
    ctx_ref[pl.ds(0, CONV_HALO), :] = halo
    ctx_ref[pl.ds(CONV_HALO, tm), :] = cur_ref[...]
    first = CONV_HALO - (CONV_WIDTH - 1)

    def rows_from(c0):
        acc = jnp.zeros((chunk, CONV_CH), F32) + b_ref[...]
        for phase in range(8):
            taps = range(phase, CONV_WIDTH, 8)
            start, shift = divmod(first + phase, 8)
            need = chunk + 8 * (len(taps) - 1)
            if shift == 0:
                window = ctx_ref[pl.ds(c0 + 8 * start, need), :]
            else:
                rows = ctx_ref[pl.ds(c0 + 8 * start, need + 8), :]
                window = pltpu.roll(rows, need + 8 - shift, 0)[0:need, :]
            for a, w in enumerate(taps):
                acc = acc + window[8 * a:8 * a + chunk, :] * w_ref[pl.ds(w, 1), :]
        mu = jnp.mean(acc, axis=-1, keepdims=True)
        xc = acc - mu
        var = jnp.mean(xc * xc, axis=-1, keepdims=True)
        y = xc * lax.rsqrt(var + EPS) * g_ref[...] + beta_ref[...]
        o_ref[pl.ds(c0, chunk), :] = (y * jax.nn.sigmoid(y)).astype(BF16)

    return [functools.partial(rows_from, c0) for c0 in range(0, tm, chunk)]


def _conv_branch(u, halo, conv_w, conv_b, cn_g, cn_b, tm, halo_from_u):
    b, t, _ = u.shape
    assert t % tm == 0
    ratio = tm // CONV_HALO if halo_from_u else 0
    halo_map = (lambda bi, i: (bi, jnp.maximum(i * ratio - 1, 0), 0)) if halo_from_u else (lambda bi, i: (bi, 0, 0))
    const = lambda bi, i: (0, 0)
    return pl.pallas_call(
        functools.partial(_conv_kernel, zero_first_halo=halo_from_u, chunk=min(tm, 64)),
        grid=(b, t // tm),
        in_specs=[
            pl.BlockSpec((None, CONV_HALO, CONV_CH), halo_map),
            pl.BlockSpec((None, tm, CONV_CH), lambda bi, i: (bi, i, 0)),
            pl.BlockSpec((CONV_WIDTH, CONV_CH), const),
            pl.BlockSpec((1, CONV_CH), const), pl.BlockSpec((1, CONV_CH), const), pl.BlockSpec((1, CONV_CH), const),
        ],
        out_specs=pl.BlockSpec((None, tm, CONV_CH), lambda bi, i: (bi, i, 0)),
        out_shape=jax.ShapeDtypeStruct((b, t, CONV_CH), BF16),
        scratch_shapes=[pltpu.VMEM((CONV_HALO + tm, CONV_CH), F32)],
        compiler_params=_params(("arbitrary", "arbitrary"), 32),
        name="conv_module",
    )(halo, u, conv_w, conv_b, cn_g, cn_b).reshape(b * t, CONV_CH)


def _mem_kv_kernel(mem_ref, g_ref, w_ref, kg_ref, k_ref, v_ref):
    hb = _rms(mem_ref[...], g_ref[...]).astype(BF16)
    for h in range(MEM_HEADS):
        cs = slice(h * MEM_HEAD_DIM, (h + 1) * MEM_HEAD_DIM)
        k_ref[:, cs] = _rms(_dot(hb, w_ref[:, cs]), kg_ref[...])
        v_ref[:, cs] = _dot(hb, w_ref[:, MEM_WIDTH + h * MEM_HEAD_DIM:MEM_WIDTH + (h + 1) * MEM_HEAD_DIM])


def _mem_kv(mem, g, w, kg):
    b = mem.shape[0]
    const = lambda bi: (0, 0)
    blk = pl.BlockSpec((None, MEM_LEN, MEM_WIDTH), lambda bi: (bi, 0, 0))
    return pl.pallas_call(
        _mem_kv_kernel,
        grid=(b,),
        in_specs=[pl.BlockSpec((None, MEM_LEN, D_MODEL), lambda bi: (bi, 0, 0)), pl.BlockSpec((1, D_MODEL), const),
                  pl.BlockSpec((D_MODEL, 2 * MEM_WIDTH), const), pl.BlockSpec((1, MEM_HEAD_DIM), const)],
        out_specs=[blk, blk],
        out_shape=[jax.ShapeDtypeStruct((b, MEM_LEN, MEM_WIDTH), F32)] * 2,
        compiler_params=_params(("arbitrary",), 32),
        name="mem_kv",
    )(mem, g, w, kg)


def _mem_attn_kernel(q_ref, k_ref, v_ref, o_ref):
    for h in range(MEM_HEADS):
        cs = slice(h * MEM_HEAD_DIM, (h + 1) * MEM_HEAD_DIM)
        s = _dot_nt(q_ref[:, cs], k_ref[:, cs].astype(BF16)) * (MEM_HEAD_DIM ** -0.5)
        m = jnp.max(s, axis=1, keepdims=True)
        p = jnp.exp(s - m)
        l = jnp.sum(p, axis=1, keepdims=True)
        o_ref[:, cs] = (_dot(p.astype(BF16), v_ref[:, cs].astype(BF16)) / l).astype(BF16)


def _mem_attn(q, k, v, tm):
    b, t, _ = q.shape
    assert t % tm == 0
    kv = pl.BlockSpec((None, MEM_LEN, MEM_WIDTH), lambda bi, i: (bi, 0, 0))
    qo = pl.BlockSpec((None, tm, MEM_WIDTH), lambda bi, i: (bi, i, 0))
    return pl.pallas_call(
        _mem_attn_kernel,
        grid=(b, t // tm),
        in_specs=[qo, kv, kv],
        out_specs=qo,
        out_shape=jax.ShapeDtypeStruct((b, t, MEM_WIDTH), BF16),
        compiler_params=_params(("arbitrary", "arbitrary"), 32),
        name="mem_attn",
    )(q, k, v).reshape(b * t, MEM_WIDTH)


def _mem_attn_rows_kernel(*refs):
    def one(i, carry):
        _mem_attn_rows_one(*(r.at[i] for r in refs))
        return carry

    lax.fori_loop(0, refs[0].shape[0], one, 0)


def _mem_attn_rows_one(q_ref, k_ref, v_ref, o_ref):
    t = q_ref.shape[0]
    qf = q_ref[...].astype(F32)
    qs = jnp.concatenate([qf[:, h * MEM_HEAD_DIM:(h + 1) * MEM_HEAD_DIM] for h in range(MEM_HEADS)], axis=0)
    s = _dot_nt(qs.astype(BF16), k_ref[...].astype(BF16)) * (MEM_HEAD_DIM ** -0.5)
    shape = s.shape
    same_head = (lax.broadcasted_iota(I32, shape, 0) // t) == (lax.broadcasted_iota(I32, shape, 1) & (MEM_HEADS - 1))
    s = jnp.where(same_head, s, NEG_INF)
    m = jnp.max(s, axis=1, keepdims=True)
    p = jnp.exp(s - m)
    l = jnp.sum(p, axis=1, keepdims=True)
    o = _dot(p.astype(BF16), v_ref[...].astype(BF16)) / l
    o_ref[...] = jnp.concatenate([o[h * t:(h + 1) * t, :] for h in range(MEM_HEADS)], axis=1).astype(BF16)


def _mem_attn_rows(q, k_rows, v_rows):
    b, t, _ = q.shape
    bb = 4
    assert b % bb == 0
    kv = pl.BlockSpec((bb, MEM_LEN * MEM_HEADS, MEM_HEAD_DIM), lambda bi: (bi, 0, 0))
    qo = pl.BlockSpec((bb, t, MEM_WIDTH), lambda bi: (bi, 0, 0))
    return pl.pallas_call(
        _mem_attn_rows_kernel,
        grid=(b // bb,),
        in_specs=[qo, kv, kv],
        out_specs=qo,
        out_shape=jax.ShapeDtypeStruct((b, t, MEM_WIDTH), BF16),
        compiler_params=_params(("arbitrary",), 32),
        name="mem_attn_rows",
    )(q, k_rows, v_rows).reshape(b * t, MEM_WIDTH)


def _merge_kernel(x_ref, o1_ref, o2_ref, o3_ref, l1_ref, l2_ref, l3_ref, ob_ref, om_ref, *rest):
    _merge_body((), x_ref, o1_ref, o2_ref, o3_ref, l1_ref, l2_ref, l3_ref, ob_ref, om_ref, *rest)


def _merge_conv_kernel(x_ref, o1_ref, o2_ref, o3_ref, l1_ref, l2_ref, l3_ref, halo_ref, u_ref, om_ref,
                       g1_ref, wg_ref, bg_ref, wa_ref, wb_ref, wm_ref, wo_ref, g2_ref, wrh_ref, wrc_ref, br_ref,
                       cw_ref, cb_ref, cg_ref, cbeta_ref, x1_ref, h2_ref, idx_ref, gw_ref,
                       mix_ref, ctx_ref, ob_ref, *, tiles_per_seq):
    halo = jnp.where(pl.program_id(0) % tiles_per_seq == 0, 0.0, halo_ref[...])
    conv_parts = _conv_module_parts(halo, u_ref, cw_ref, cb_ref, cg_ref, cbeta_ref, ob_ref, ctx_ref, 64)
    _merge_body(conv_parts, x_ref, o1_ref, o2_ref, o3_ref, l1_ref, l2_ref, l3_ref, ob_ref, om_ref,
                g1_ref, wg_ref, bg_ref, wa_ref, wb_ref, wm_ref, wo_ref, g2_ref, wrh_ref, wrc_ref, br_ref,
                x1_ref, h2_ref, idx_ref, gw_ref, mix_ref)


def _merge_body(conv_parts, x_ref, o1_ref, o2_ref, o3_ref, l1_ref, l2_ref, l3_ref, ob_ref, om_ref,
                g1_ref, wg_ref, bg_ref, wa_ref, wb_ref, wm_ref, wo_ref, g2_ref, wrh_ref, wrc_ref, br_ref,
                x1_ref, h2_ref, idx_ref, gw_ref, mix_ref):
    tm = x_ref.shape[0]
    x = x_ref[...]
    hb = _rms(x, g1_ref[...]).astype(BF16)
    la, lb, lc = l1_ref[...], l2_ref[...], l3_ref[...]
    m = jnp.maximum(la, jnp.maximum(lb, lc))
    ea, eb, ec = jnp.exp(la - m), jnp.exp(lb - m), jnp.exp(lc - m)
    oa = ((ea * o1_ref[...] + eb * o2_ref[...] + ec * o3_ref[...]) / (ea + eb + ec)).astype(BF16)
    om = om_ref[...]
    chunks = [slice(j * 256, (j + 1) * 256) for j in range(4)]

    def gate(branch, j):
        c0 = branch * D_MODEL + j * 256
        return jax.nn.sigmoid(_dot(hb, wg_ref[:, c0:c0 + 256]) + bg_ref[:, c0:c0 + 256])

    per_j = -(-len(conv_parts) // 4)
    partial, gate_b = [], []
    for j, cs in enumerate(chunks):
        for part in conv_parts[j * per_j:(j + 1) * per_j]:
            part()
        partial.append(gate(0, j) * _dot(oa, wa_ref[:, cs]) + gate(2, j) * _dot(om, wm_ref[:, cs]))
        gate_b.append(gate(1, j))
    ob = ob_ref[...]
    for j, cs in enumerate(chunks):
        mix_ref[:, cs] = (partial[j] + gate_b[j] * _dot(ob, wb_ref[:, cs])).astype(BF16)
    x1 = x + _dot(mix_ref[...], wo_ref[...])
    x1_ref[...] = x1
    h2 = _rms(x1, g2_ref[...])
    for j in range(ROW_TILE):
        h2_ref[pl.ds(j, tm, stride=ROW_TILE), :] = h2[:, j * LANES:(j + 1) * LANES]
    hi = h2.astype(BF16)
    lo = (h2 - hi.astype(F32)).astype(BF16)
    both = _dot(hi, wrc_ref[...])
    logits = both[:, :LANES] + both[:, LANES:] + _dot(lo, wrh_ref[...]) + br_ref[...]
    lane = lax.broadcasted_iota(I32, (tm, LANES), 1)
    logits = jnp.where(lane < N_EXPERTS, logits, NEG_INF)
    vals, idxs = [], []
    for _ in range(TOP_K):
        mk = jnp.max(logits, axis=1, keepdims=True)
        ik = jnp.min(jnp.where(logits == mk, lane, LANES), axis=1, keepdims=True)
        vals.append(mk)
        idxs.append(ik)
        logits = jnp.where(lane == ik, NEG_INF, logits)
    es = [jnp.exp(v - vals[0]) for v in vals]
    den = es[0] + es[1] + es[2] + es[3]
    l4 = lax.broadcasted_iota(I32, (tm, TOP_K), 1)
    idx_out = jnp.zeros((tm, TOP_K), I32)
    gw_out = jnp.zeros((tm, TOP_K), F32)
    for k in range(TOP_K):
        idx_out = jnp.where(l4 == k, idxs[k], idx_out)
        gw_out = jnp.where(l4 == k, es[k] / den, gw_out)
    idx_ref[...] = idx_out
    gw_ref[...] = gw_out


def _merge(x, os_, lses, ob, om, g1, wg, bg, wa, wb, wm, wo, g2, wrh, wrc, br, tm, conv=None):
    n = x.shape[0]
    assert n % tm == 0
    row = lambda i: (i, 0)
    const = lambda i: (0, 0)
    rows = lambda w: pl.BlockSpec((tm, w), row)
    full = lambda a: pl.BlockSpec(a.shape, const)
    weights = (g1, wg, bg, wa, wb, wm, wo, g2, wrh, wrc, br)
    scratch = [pltpu.VMEM((tm, D_MODEL), BF16)]
    if conv is None:
        body, b_specs, b_args, extra = _merge_kernel, [rows(CONV_CH)], (ob,), ()
    else:
        *conv_params, seq_len = conv
        assert seq_len % tm == 0 and tm % CONV_HALO == 0
        ratio = tm // CONV_HALO
        body = functools.partial(_merge_conv_kernel, tiles_per_seq=seq_len // tm)
        b_specs = [pl.BlockSpec((CONV_HALO, CONV_CH), lambda i: (jnp.maximum(i * ratio - 1, 0), 0)), rows(CONV_CH)]
        b_args, extra = (ob, ob), tuple(conv_params)
        scratch += [pltpu.VMEM((CONV_HALO + tm, CONV_CH), F32), pltpu.VMEM((tm, CONV_CH), BF16)]
    return pl.pallas_call(
        body,
        grid=(n // tm,),
        in_specs=[rows(D_MODEL)] + [rows(GROUP_W)] * 6 + b_specs + [rows(MEM_WIDTH)]
        + [full(a) for a in weights + extra],
        out_specs=[rows(D_MODEL), pl.BlockSpec((tm * ROW_TILE, LANES), row), rows(TOP_K), rows(TOP_K)],
        out_shape=[jax.ShapeDtypeStruct((n, D_MODEL), F32), jax.ShapeDtypeStruct((n * ROW_TILE, LANES), F32),
                   jax.ShapeDtypeStruct((n, TOP_K), I32), jax.ShapeDtypeStruct((n, TOP_K), F32)],
        scratch_shapes=scratch,
        compiler_params=_params(("arbitrary",), 56),
        name="merge_router",
    )(x, *os_, *lses, *b_args, om, *weights, *extra)


def _route_kernel(idx_ref, dest_ref, bexp_ref, seg_ref, cnt_ref, carry_ref, *, block_rows):
    phase, i = pl.program_id(0), pl.program_id(1)
    tq = idx_ref.shape[0]
    nb = bexp_ref.shape[0]
    idx = idx_ref[...]
    lane = lax.broadcasted_iota(I32, (tq, LANES), 1)
    member = jnp.zeros((tq, LANES), F32)
    for k in range(TOP_K):
        member = member + (lane == idx[:, k:k + 1]).astype(F32)
    tile_counts = jnp.sum(member, axis=0, keepdims=True)

    @pl.when(phase == 0)
    def _():
        @pl.when(i == 0)
        def _():
            cnt_ref[...] = jnp.zeros_like(cnt_ref)

        cnt_ref[...] += tile_counts
        dest_ref[...] = jnp.zeros_like(dest_ref)
        bexp_ref[...] = jnp.zeros_like(bexp_ref)
        seg_ref[...] = jnp.zeros_like(seg_ref)

    @pl.when(phase == 1)
    def _():
        @pl.when(i == 0)
        def _():
            carry_ref[...] = jnp.zeros_like(carry_ref)

        counts = jnp.broadcast_to(cnt_ref[...], (8, LANES))
        padded = jnp.floor((counts + (block_rows - 1)) * (1.0 / block_rows)) * block_rows
        lane8 = lax.broadcasted_iota(I32, (8, LANES), 1)
        ends = padded
        for s in (1, 2, 4, 8, 16, 32, 64):
            ends = ends + jnp.where(lane8 >= s, pltpu.roll(ends, s, 1), 0.0)
        starts = (ends - padded)[0:1, :]
        r = lax.broadcasted_iota(I32, (tq, tq), 0)
        c = lax.broadcasted_iota(I32, (tq, tq), 1)
        earlier = _dot((c < r).astype(BF16), member.astype(BF16)) + carry_ref[...]
        carry_ref[...] += tile_counts
        base = earlier + starts
        l4 = lax.broadcasted_iota(I32, (tq, TOP_K), 1)
        dest = jnp.zeros((tq, TOP_K), F32)
        for k in range(TOP_K):
            dk = jnp.sum(jnp.where(lane == idx[:, k:k + 1], base, 0.0), axis=1, keepdims=True)
            dest = jnp.where(l4 == k, dk, dest)
        dest_ref[...] = dest.astype(I32)
        first_row = (lax.broadcasted_iota(I32, (nb, LANES), 0) * block_rows).astype(F32)
        lane_nb = lax.broadcasted_iota(I32, (nb, LANES), 1)
        done = jnp.where((ends[0:1, :] <= first_row) & (lane_nb < N_EXPERTS), 1.0, 0.0)
        bexp_ref[...] = jnp.minimum(jnp.sum(done, axis=1, keepdims=True), N_EXPERTS - 1.0).astype(I32)
        row8 = lax.broadcasted_iota(I32, (8, LANES), 0)
        seg = jnp.where(row8 == 0, ends - padded + counts, jnp.where(row8 == 1, ends, 0.0))
        seg_ref[...] = seg.astype(I32)


def _route(idx, block_rows, n_blocks, tq):
    n = idx.shape[0]
    assert n % tq == 0
    nb_pad = -(-n_blocks // 8) * 8
    dest, bexp, seg = pl.pallas_call(
        functools.partial(_route_kernel, block_rows=block_rows),
        grid=(2, n // tq),
        in_specs=[pl.BlockSpec((tq, TOP_K), lambda p, i: (i, 0))],
        out_specs=[pl.BlockSpec((tq, TOP_K), lambda p, i: (i * p, 0)),
                   pl.BlockSpec((nb_pad, 1), lambda p, i: (0, 0)),
                   pl.BlockSpec((8, LANES), lambda p, i: (0, 0))],
        out_shape=[jax.ShapeDtypeStruct((n, TOP_K), I32), jax.ShapeDtypeStruct((nb_pad, 1), I32),
                   jax.ShapeDtypeStruct((8, LANES), I32)],
        scratch_shapes=[pltpu.VMEM((1, LANES), F32), pltpu.VMEM((1, LANES), F32)],
        compiler_params=_params(("arbitrary", "arbitrary"), 40),
        name="route",
    )(idx)
    return dest.reshape(n * TOP_K), bexp.reshape(nb_pad)[:n_blocks], seg


def _tile_rows(index):
    return pl.ds(pl.multiple_of(index * ROW_TILE, ROW_TILE), ROW_TILE)


def _part_specs(parts, block):
    specs, bounds, first = [], [], 0
    for a in parts:
        assert a.shape[0] % block[0] == 0
        count = a.shape[0] // block[0]
        specs.append(pl.BlockSpec(block, lambda i, first=first, count=count: (jnp.clip(i - first, 0, count - 1), 0)))
        bounds.append((first, count))
        first += count
    return specs, tuple(bounds), first


def _dispatch_kernel(seg_ref, dest_ref, *refs, bounds):
    h_refs, (xs_ref, zero_ref, sem, zero_sem) = refs[:len(bounds)], refs[len(bounds):]
    tm = h_refs[0].shape[0] // ROW_TILE
    i = pl.program_id(0)

    @pl.when(pl.program_id(0) == 0)
    def _():
        zero_ref[...] = jnp.zeros_like(zero_ref)

        def pad_copies(e):
            slot, length = seg_ref[0, e], seg_ref[1, e] - seg_ref[0, e]
            pieces = []
            size = zero_ref.shape[0] // ROW_TILE
            while size >= 1:
                present = (length & size) != 0
                rows = pl.ds(pl.multiple_of(slot * ROW_TILE, ROW_TILE), size * ROW_TILE)
                pieces.append((present, pltpu.make_async_copy(zero_ref.at[pl.ds(0, size * ROW_TILE)],
                                                              xs_ref.at[rows], zero_sem)))
                slot = slot + jnp.where(present, size, 0)
                size //= 2
            return pieces

        def start_expert(e):
            for present, piece in pad_copies(e):
                pl.when(present)(piece.start)

        def wait_expert(e):
            for present, piece in pad_copies(e):
                pl.when(present)(piece.wait)

        def expert(e, carry):
            start_expert(e)
            wait_expert(e - 1)
            return carry

        start_expert(0)
        lax.fori_loop(1, N_EXPERTS, expert, 0)
        wait_expert(N_EXPERTS - 1)

    for h_ref, (first, count) in zip(h_refs, bounds):
        @pl.when((i >= first) & (i < first + count))
        def _(h_ref=h_ref):
            def issue(r, carry):
                for k in range(TOP_K):
                    pltpu.make_async_copy(h_ref.at[_tile_rows(r)], xs_ref.at[_tile_rows(dest_ref[r * TOP_K + k])],
                                          sem).start(priority=k % 2)
                return carry

            lax.fori_loop(0, tm, issue, 0, unroll=4)

    for k in range(TOP_K):
        pltpu.make_async_copy(h_refs[0], xs_ref.at[pl.ds(0, tm * ROW_TILE)], sem).wait()


def _dispatch(h2t_parts, dest, seg, n_slots, tm, block_rows):
    specs, bounds, steps = _part_specs(h2t_parts, (tm * ROW_TILE, LANES))
    assert block_rows & (block_rows - 1) == 0
    return pl.pallas_call(
        functools.partial(_dispatch_kernel, bounds=bounds),
        grid=(steps,),
        in_specs=[pl.BlockSpec(memory_space=pltpu.SMEM),
                  pl.BlockSpec((tm * TOP_K,), lambda i: (i,), memory_space=pltpu.SMEM)] + specs,
        out_specs=pl.BlockSpec(memory_space=pl.ANY),
        out_shape=jax.ShapeDtypeStruct((n_slots * ROW_TILE, LANES), F32),
        scratch_shapes=[pltpu.VMEM((block_rows // 2 * ROW_TILE, LANES), F32), pltpu.SemaphoreType.DMA,
                        pltpu.SemaphoreType.DMA],
        compiler_params=_params(("arbitrary",), 32),
        name="dispatch",
    )(seg, dest, *h2t_parts)


def _moe_kernel(be_ref, seg_ref, xs_ref, wg_ref, bg_ref, wu_ref, bu_ref, wd_ref, bd_ref, yb_ref, wg_s, wu_s, wd_s):
    i = pl.program_id(0)
    bm = xs_ref.shape[0] // ROW_TILE
    used_rows = seg_ref[LANES + N_EXPERTS - 1]

    @pl.when(i * bm < used_rows)
    def _():
        changed = (i == 0) | (be_ref[i] != be_ref[jnp.maximum(i - 1, 0)])

        @pl.when(changed)
        def _():
            for r0 in range(0, D_MODEL, 128):
                rs = pl.ds(r0, 128)
                wg_s[rs, :] = wg_ref[rs, :].astype(BF16)
                wu_s[rs, :] = wu_ref[rs, :].astype(BF16)
                wd_s[rs, :] = wd_ref[rs, :].astype(BF16)

        xb = jnp.concatenate([xs_ref[pl.ds(j, bm, stride=ROW_TILE), :] for j in range(ROW_TILE)],
                             axis=1).astype(BF16)
        chunks = [slice(c * 256, (c + 1) * 256) for c in range(D_FF // 256)]
        gs = [_dot(xb, wg_s[:, cs]) + bg_ref[:, cs] for cs in chunks]
        us = [_dot(xb, wu_s[:, cs]) + bu_ref[:, cs] for cs in chunks]
        acts = []
        for g, u in zip(gs, us):
            g = jnp.minimum(g, SWIGLU_LIMIT)
            u = jnp.clip(u, -SWIGLU_LIMIT, SWIGLU_LIMIT)
            acts.append((g * jax.nn.sigmoid(SWIGLU_ALPHA * g) * (u + 1.0)).astype(BF16))
        acc = jnp.zeros((bm, D_MODEL), F32) + bd_ref[...]
        for act, cs in zip(acts, chunks):
            acc = acc + _dot(act, wd_s[cs, :])
        for j in range(ROW_TILE):
            yb_ref[pl.ds(j, bm, stride=ROW_TILE), :] = acc[:, j * LANES:(j + 1) * LANES]


def _moe(xs, bexp, seg, wg, bg, wu, bu, wd, bd, bm):
    n_slots = xs.shape[0] // ROW_TILE
    assert n_slots % bm == 0
    wspec = pl.BlockSpec((None, D_MODEL, D_FF), lambda i, be, sg: (be[i], 0, 0))
    bspec = pl.BlockSpec((None, 1, D_FF), lambda i, be, sg: (be[i], 0, 0))
    rows = pl.BlockSpec((bm * ROW_TILE, LANES), lambda i, be, sg: (i, 0))
    grid_spec = pltpu.PrefetchScalarGridSpec(
        num_scalar_prefetch=2,
        grid=(n_slots // bm,),
        in_specs=[rows, wspec, bspec, wspec, bspec, wspec, bspec],
        out_specs=rows,
        scratch_shapes=[pltpu.VMEM((D_MODEL, D_FF), BF16)] * 3,
    )
    return pl.pallas_call(
        _moe_kernel,
        grid_spec=grid_spec,
        out_shape=jax.ShapeDtypeStruct((n_slots * ROW_TILE, LANES), F32),
        compiler_params=_params(("arbitrary",), 56),
        name="moe_experts",
    )(bexp, seg.reshape(-1), xs, wg, bg.reshape(N_EXPERTS, 1, D_FF), wu, bu.reshape(N_EXPERTS, 1, D_FF),
      wd, bd.reshape(N_EXPERTS, 1, D_MODEL))


def _combine_kernel(dest_ref, dest_next_ref, gw_ref, yb_ref, *refs, bounds):
    nparts = len(bounds)
    x1_refs, y_refs, (rows_ref, sems) = refs[:nparts], refs[nparts:2 * nparts], refs[2 * nparts:]
    tm = x1_refs[0].shape[0]
    i = pl.program_id(0)
    slot = i % 2

    def issue(dref, buf):
        def body(r, carry):
            for k in range(TOP_K):
                pltpu.make_async_copy(yb_ref.at[_tile_rows(dref[r * TOP_K + k])], rows_ref.at[buf, k, _tile_rows(r)],
                                      sems.at[buf]).start(priority=k % 2)
            return carry

        lax.fori_loop(0, tm, body, 0, unroll=4)

    @pl.when(i == 0)
    def _():
        issue(dest_ref, slot)

    @pl.when(i + 1 < pl.num_programs(0))
    def _():
        issue(dest_next_ref, 1 - slot)

    for k in range(TOP_K):
        pltpu.make_async_copy(yb_ref.at[pl.ds(0, tm * ROW_TILE)], rows_ref.at[slot, k], sems.at[slot]).wait()
    gw = gw_ref[...]
    for x1_ref, y_ref, (first, count) in zip(x1_refs, y_refs, bounds):
        @pl.when((i >= first) & (i < first + count))
        def _(x1_ref=x1_ref, y_ref=y_ref):
            for j in range(ROW_TILE):
                ls = slice(j * LANES, (j + 1) * LANES)
                y = x1_ref[:, ls]
                for k in range(TOP_K):
                    y = y + gw[:, k:k + 1] * rows_ref[slot, k, pl.ds(j, tm, stride=ROW_TILE), :]
                y_ref[:, ls] = y


def _combine(x1_parts, gw, dest, yb, tm):
    specs, bounds, steps = _part_specs(x1_parts, (tm, D_MODEL))
    return pl.pallas_call(
        functools.partial(_combine_kernel, bounds=bounds),
        grid=(steps,),
        in_specs=[pl.BlockSpec((tm * TOP_K,), lambda i: (i,), memory_space=pltpu.SMEM),
                  pl.BlockSpec((tm * TOP_K,), lambda i: (jnp.minimum(i + 1, steps - 1),), memory_space=pltpu.SMEM),
                  pl.BlockSpec((tm, TOP_K), lambda i: (i, 0)),
                  pl.BlockSpec(memory_space=pl.ANY)] + specs,
        out_specs=specs,
        out_shape=[jax.ShapeDtypeStruct(a.shape, F32) for a in x1_parts],
        scratch_shapes=[pltpu.VMEM((2, TOP_K, tm * ROW_TILE, LANES), F32), pltpu.SemaphoreType.DMA((2,))],
        compiler_params=_params(("arbitrary",), 32),
        name="combine",
    )(dest, dest, gw, yb, *x1_parts)


def _rope_tables(positions):
    half = ROPE_DIM // 2
    inv = jnp.power(jnp.float32(ROPE_THETA), -jnp.arange(half, dtype=F32) / half)
    ang = positions.astype(F32)[:, None] * inv[None, :]
    cos, sin = jnp.cos(ang), jnp.sin(ang)
    t = positions.shape[0]
    z8 = jnp.zeros((t, half), F32)
    rest0 = jnp.zeros((t, HEAD_DIM_A - ROPE_DIM), F32)
    a = jnp.concatenate([cos, cos, jnp.ones((t, HEAD_DIM_A - ROPE_DIM), F32)], axis=1)
    bp = jnp.concatenate([z8, sin, rest0], axis=1)
    bm = jnp.concatenate([-sin, z8, rest0], axis=1)
    return tuple(jnp.tile(m, (1, LANES // HEAD_DIM_A)) for m in (a, bp, bm))


def _segment_mean_matrix(seg):
    i = jnp.arange(256)
    return jnp.where((i[:, None] // seg) == (i[None, :] // seg), 1.0 / seg, 0.0).astype(BF16)


def _layer_weights(l, norm1_g, w_in, b_gates, qn_a, kn_a, conv_w, conv_b, cn_g, cn_b, qn_m,
                   w_proj_a, w_proj_b, w_proj_m, w_out, norm2_g, w_router, b_router,
                   w_gate, b_gate, w_up, b_up, w_down, b_down):
    wr = jnp.pad(w_router[l], ((0, 0), (0, LANES - N_EXPERTS)))
    wrh = wr.astype(BF16)
    return dict(
        g1=norm1_g[l][None, :],
        w1=w_in[l][:, :W1_COLS].astype(BF16),
        wgates=w_in[l][:, W1_COLS:].astype(BF16),
        bgates=b_gates[l][None, :],
        qg=jnp.tile(qn_a[l], 256 // HEAD_DIM_A)[None, :],
        kg=jnp.tile(kn_a[l], 256 // HEAD_DIM_A)[None, :],
        mg=jnp.tile(qn_m[l], 256 // MEM_HEAD_DIM)[None, :],
        conv_w=conv_w[l], conv_b=conv_b[l][None, :], cn_g=cn_g[l][None, :], cn_b=cn_b[l][None, :],
        wa=w_proj_a[l].astype(BF16), wb=w_proj_b[l].astype(BF16), wm=w_proj_m[l].astype(BF16),
        wo=w_out[l].astype(BF16), g2=norm2_g[l][None, :],
        wrh=wrh, wrc=jnp.concatenate([wrh, (wr - wrh.astype(F32)).astype(BF16)], axis=1),
        br=jnp.pad(b_router[l], (0, LANES - N_EXPERTS))[None, :],
        w_gate=w_gate[l], b_gate=b_gate[l], w_up=w_up[l], b_up=b_up[l], w_down=w_down[l], b_down=b_down[l],
        seg64=_segment_mean_matrix(HEAD_DIM_A), seg128=_segment_mean_matrix(MEM_HEAD_DIM),
    )


def _merge_group(x2d, os_, lses, ob, om, w, tm, conv_seq_len=None):
    conv = None if conv_seq_len is None else (w["conv_w"], w["conv_b"], w["cn_g"], w["cn_b"], conv_seq_len)
    return _merge(x2d, os_, lses, ob, om, w["g1"], w["wgates"], w["bgates"], w["wa"], w["wb"], w["wm"],
                  w["wo"], w["g2"], w["wrh"], w["wrc"], w["br"], tm, conv)


MOE_ROWS = 512
ROUTE_TILE = 1024
DISPATCH_TILE = 256
COMBINE_TILE = 128


def _moe_groups(merged, w):
    x1s, h2ts, idxs, gws = zip(*merged)
    idx = jnp.concatenate(idxs, axis=0)
    gw = jnp.concatenate(gws, axis=0)
    n = idx.shape[0]
    n_blocks = -(-(n * TOP_K) // MOE_ROWS) + N_EXPERTS
    dest, bexp, seg = _route(idx, MOE_ROWS, n_blocks, ROUTE_TILE)
    xs = _dispatch(h2ts, dest, seg, n_blocks * MOE_ROWS, DISPATCH_TILE, MOE_ROWS)
    yb = _moe(xs, bexp, seg, w["w_gate"], w["b_gate"], w["w_up"], w["b_up"], w["w_down"], w["b_down"], MOE_ROWS)
    return _combine(x1s, gw, dest, yb, COMBINE_TILE)


def _prompt_layer(x, mem, w, mem_norm_g, w_mem_kv, kn_m):
    b, s, _ = x.shape
    n = b * s
    x2d = x.reshape(n, D_MODEL)
    tabs = _rope_tables(jnp.arange(s, dtype=I32))
    q, k, v, u, qm = _in_proj(x2d, w["g1"], w["w1"], w["qg"], w["kg"], w["mg"], tabs, w["seg64"], w["seg128"], 512, F32)
    q3, k3, v3 = (t.reshape(b, s, WIDTH_A) for t in (q, k, v))
    os_, lses, caches = [], [], []
    for g, (win, dil) in enumerate(SWA_GROUPS):
        o, lse = _swa_prompt(q3, k3, v3, g, dil)
        os_.append(o)
        lses.append(lse)
        cs = slice(g * GROUP_W, (g + 1) * GROUP_W)
        caches += [k3[:, s - win:, cs].reshape(b, win, 4, HEAD_DIM_A), v3[:, s - win:, cs].reshape(b, win, 4, HEAD_DIM_A)]
    u3 = u.reshape(b, s, CONV_CH)
    mk, mv = _mem_kv(mem, mem_norm_g[None, :], w_mem_kv.astype(BF16), kn_m[None, :])
    om = _mem_attn(qm.reshape(b, s, MEM_WIDTH), mk, mv, 512)
    merged = _merge_group(x2d, os_, lses, u, om, w, 512, conv_seq_len=s)
    state = caches + [u3[:, s - (CONV_WIDTH - 1):], mk.reshape(b, MEM_LEN, MEM_HEADS, MEM_HEAD_DIM),
                      mv.reshape(b, MEM_LEN, MEM_HEADS, MEM_HEAD_DIM)]
    return merged, state


def _sample_layer(x, mem_k, mem_v, bufs, conv_state, w):
    b, t, _ = x.shape
    n = b * t
    x2d = x.reshape(n, D_MODEL)
    tabs = _rope_tables(jnp.tile(PAST_LEN + jnp.arange(t, dtype=I32), b))
    q, k, v, u, qm = _in_proj(x2d, w["g1"], w["w1"], w["qg"], w["kg"], w["mg"], tabs, w["seg64"], w["seg128"], 256, BF16)
    q3, k3, v3 = (a.reshape(b, t, WIDTH_A) for a in (q, k, v))
    os_, lses, caches = [], [], []
    for g, (win, dil) in enumerate(SWA_GROUPS):
        to_cm = lambda a: jnp.transpose(a, (0, 2, 3, 1)).reshape(b, GROUP_W, win)
        from_cm = lambda a: jnp.transpose(a.reshape(b, 4, HEAD_DIM_A, win), (0, 3, 1, 2))
        o, lse, kto, vto = _swa_sample(q3, k3, v3, to_cm(bufs[2 * g]), to_cm(bufs[2 * g + 1]), g, win, dil)
        os_.append(o)
        lses.append(lse)
        caches += [from_cm(kto), from_cm(vto)]
    u3 = u.reshape(b, t, CONV_CH)
    halo = jnp.pad(conv_state, ((0, 0), (CONV_HALO - (CONV_WIDTH - 1), 0), (0, 0)))
    ob = _conv_branch(u3, halo, w["conv_w"], w["conv_b"], w["cn_g"], w["cn_b"], t, False)
    om = _mem_attn_rows(qm.reshape(b, t, MEM_WIDTH), mem_k.reshape(b, MEM_LEN * MEM_HEADS, MEM_HEAD_DIM),
                        mem_v.reshape(b, MEM_LEN * MEM_HEADS, MEM_HEAD_DIM))
    merged = _merge_group(x2d, os_, lses, ob, om, w, 256)
    new_conv = jnp.concatenate([conv_state, u3], axis=1)[:, t:]
    return merged, caches + [new_conv]


def kernel(x_prompt, x_sample, mem_prompt, cache_swa_k_w128, cache_swa_v_w128, cache_swa_k_w512, cache_swa_v_w512, cache_swa_k_w2048, cache_swa_v_w2048, state_conv, cache_mem_k, cache_mem_v, norm1_g, w_in, b_gates, qn_a, kn_a, conv_w, conv_b, cn_g, cn_b, mem_norm_g, w_mem_kv, qn_m, kn_m, w_proj_a, w_proj_b, w_proj_m, w_out, norm2_g, w_router, b_router, w_gate, b_gate, w_up, b_up, w_down, b_down):
    depth = norm1_g.shape[0]
    y_p, y_s = x_prompt, x_sample
    st_p, st_s = [], []
    for l in range(depth):
        w = _layer_weights(l, norm1_g, w_in, b_gates, qn_a, kn_a, conv_w, conv_b, cn_g, cn_b, qn_m,
                           w_proj_a, w_proj_b, w_proj_m, w_out, norm2_g, w_router, b_router,
                           w_gate, b_gate, w_up, b_up, w_down, b_down)
        merged_p, state_p = _prompt_layer(y_p, mem_prompt, w, mem_norm_g[l], w_mem_kv[l], kn_m[l])
        bufs = (cache_swa_k_w128[l], cache_swa_v_w128[l], cache_swa_k_w512[l], cache_swa_v_w512[l],
                cache_swa_k_w2048[l], cache_swa_v_w2048[l])
        merged_s, state_s = _sample_layer(y_s, cache_mem_k[l], cache_mem_v[l], bufs, state_conv[l], w)
        y_p2d, y_s2d = _moe_groups((merged_p, merged_s), w)
        y_p, y_s = y_p2d.reshape(y_p.shape), y_s2d.reshape(y_s.shape)
        st_p.append(state_p)
        st_s.append(state_s)
    outs_p = [jnp.stack(a) for a in zip(*st_p)]
    outs_s = [jnp.stack(a) for a in zip(*st_s)]
    return (y_p, y_s, *outs_p, *outs_s)
```

```python
import functools

import jax
import jax.numpy as jnp
from jax import lax
from jax.experimental import pallas as pl
from jax.experimental.pallas import tpu as pltpu

F32 = jnp.float32
BF16 = jnp.bfloat16
I32 = jnp.int32

D_MODEL = 1024
HEAD_DIM_A = 64
GROUP_W = 256
N_GROUPS = 3
WIDTH_A = N_GROUPS * GROUP_W
SWA_GROUPS = ((128, 1), (512, 4), (2048, 16))
SWA_BLOCK = 128
ROPE_DIM = 16
ROPE_THETA = 500000.0
CONV_CH = 512
CONV_WIDTH = 31
CONV_HALO = 32
MEM_LEN = 256
MEM_HEADS = 4
MEM_HEAD_DIM = 128
MEM_WIDTH = 512
N_EXPERTS = 32
TOP_K = 4
D_FF = 1024
SWIGLU_LIMIT = 7.0
SWIGLU_ALPHA = 1.702
EPS = 1e-6
PAST_LEN = 8192
LANES = 128
ROW_TILE = D_MODEL // LANES
W1_COLS = 3 * WIDTH_A + 2 * CONV_CH + MEM_WIDTH
NEG_INF = float("-inf")
MIB = 2 ** 20


def _params(semantics, vmem_mib):
    return pltpu.CompilerParams(dimension_semantics=semantics, vmem_limit_bytes=vmem_mib * MIB)


def _rms(x, gain):
    return x * lax.rsqrt(jnp.mean(x * x, axis=-1, keepdims=True) + EPS) * gain


def _dot(a, b):
    return jnp.dot(a, b, preferred_element_type=F32)


def _dot_nt(a, b):
    return lax.dot_general(a, b, (((1,), (1,)), ((), ())), preferred_element_type=F32)


def _in_proj_kernel(x_ref, g1_ref, w_ref, qg_ref, kg_ref, mg_ref, ra_ref, rp_ref, rm_ref, s64_ref, s128_ref,
                    q_ref, k_ref, v_ref, u_ref, qm_ref):
    hb = _rms(x_ref[...], g1_ref[...]).astype(BF16)
    ra, rp, rm = ra_ref[...], rp_ref[...], rm_ref[...]

    def proj(c0):
        return _dot(hb, w_ref[:, c0:c0 + 256])

    def head_norm(p, seg_ref, gain):
        ms = _dot((p * p).astype(BF16), seg_ref[...])
        return p * lax.rsqrt(ms + EPS) * gain

    def rope(p):
        outs = []
        for j in range(2):
            pj = p[:, j * LANES:(j + 1) * LANES]
            outs.append(pj * ra + pltpu.roll(pj, 8, 1) * rp + pltpu.roll(pj, LANES - 8, 1) * rm)
        return jnp.concatenate(outs, axis=1)

    for c in range(3):
        cs = slice(c * 256, (c + 1) * 256)
        q_ref[:, cs] = rope(head_norm(proj(c * 256), s64_ref, qg_ref[...])).astype(q_ref.dtype)
        k_ref[:, cs] = rope(head_norm(proj(WIDTH_A + c * 256), s64_ref, kg_ref[...]))
        v_ref[:, cs] = proj(2 * WIDTH_A + c * 256)
    for c in range(2):
        cs = slice(c * 256, (c + 1) * 256)
        a = proj(3 * WIDTH_A + c * 256)
        gate = proj(3 * WIDTH_A + CONV_CH + c * 256)
        u_ref[:, cs] = a * jax.nn.sigmoid(gate)
        qm_ref[:, cs] = head_norm(proj(3 * WIDTH_A + 2 * CONV_CH + c * 256), s128_ref, mg_ref[...]).astype(BF16)


def _in_proj(x, g1, w1, qg, kg, mg, tabs, seg64, seg128, tm, q_dtype):
    n = x.shape[0]
    t_rows = tabs[0].shape[0]
    assert n % tm == 0 and t_rows % tm == 0
    t_blocks = t_rows // tm
    row = lambda i: (i, 0)
    const = lambda i: (0, 0)
    tab = lambda i: (i % t_blocks, 0)
    return pl.pallas_call(
        _in_proj_kernel,
        grid=(n // tm,),
        in_specs=[
            pl.BlockSpec((tm, D_MODEL), row),
            pl.BlockSpec((1, D_MODEL), const),
            pl.BlockSpec((D_MODEL, W1_COLS), const),
            pl.BlockSpec((1, 256), const), pl.BlockSpec((1, 256), const), pl.BlockSpec((1, 256), const),
            pl.BlockSpec((tm, LANES), tab), pl.BlockSpec((tm, LANES), tab), pl.BlockSpec((tm, LANES), tab),
            pl.BlockSpec((256, 256), const), pl.BlockSpec((256, 256), const),
        ],
        out_specs=[
            pl.BlockSpec((tm, WIDTH_A), row), pl.BlockSpec((tm, WIDTH_A), row), pl.BlockSpec((tm, WIDTH_A), row),
            pl.BlockSpec((tm, CONV_CH), row), pl.BlockSpec((tm, MEM_WIDTH), row),
        ],
        out_shape=[
            jax.ShapeDtypeStruct((n, WIDTH_A), q_dtype), jax.ShapeDtypeStruct((n, WIDTH_A), F32),
            jax.ShapeDtypeStruct((n, WIDTH_A), F32), jax.ShapeDtypeStruct((n, CONV_CH), F32),
            jax.ShapeDtypeStruct((n, MEM_WIDTH), BF16),
        ],
        compiler_params=_params(("arbitrary",), 48),
        name="in_proj",
    )(x, g1, w1, qg, kg, mg, *tabs, seg64, seg128)


def _swa_prompt_kernel(q_ref, k_ref, v_ref, o_ref, lse_ref, q_s, k_s, v_s, o_s, l_s, *, dil, nsub):
    n = pl.program_id(1)
    span = SWA_BLOCK * dil
    cur = n % 2
    prv = 1 - cur
    for s in range(2):
        ls = slice(s * LANES, (s + 1) * LANES)
        q_s[s] = q_ref[:, ls]
        k_s[cur, s] = k_ref[:, ls]
        v_s[cur, s] = v_ref[:, ls]

    @pl.when(n == 0)
    def _():
        k_s[prv] = jnp.zeros(k_s.shape[1:], F32)
        v_s[prv] = jnp.zeros(v_s.shape[1:], F32)

    shape = (SWA_BLOCK, 2 * SWA_BLOCK)
    col = lax.broadcasted_iota(I32, shape, 1)
    off = lax.broadcasted_iota(I32, shape, 0) + SWA_BLOCK - col
    band = (off >= 0) & (off <= SWA_BLOCK)
    band_first = band & ((n > 0) | (col >= SWA_BLOCK))
    head = col // HEAD_DIM_A

    hms = [head == h for h in range(4)]

    def both(ref, lead, rows):
        return jnp.concatenate([ref[(*lead, s, rows, slice(None))] for s in range(2)], axis=1)

    def blocks(items):
        rows_of, masks, qs, kks, vvs = [], [], [], [], []
        for j, r in items:
            rows = pl.ds(j * span + r, SWA_BLOCK, stride=dil)
            before = (prv, pl.ds((nsub - 1) * span + r, SWA_BLOCK, stride=dil)) if j == 0 else \
                     (cur, pl.ds((j - 1) * span + r, SWA_BLOCK, stride=dil))
            rows_of.append(rows)
            masks.append(band_first if j == 0 else band)
            qs.append(both(q_s, (), rows).astype(BF16))
            kks.append(jnp.concatenate([both(k_s, before[:1], before[1]), both(k_s, (cur,), rows)],
                                       axis=0).astype(BF16))
            vvs.append(jnp.concatenate([both(v_s, before[:1], before[1]), both(v_s, (cur,), rows)],
                                       axis=0).astype(BF16))
        pairs = [(b, h) for b in range(len(items)) for h in range(4)]
        ss = [jnp.where(masks[b], _dot_nt(jnp.where(hms[h], qs[b], jnp.zeros_like(qs[b])), kks[b])
                        * (HEAD_DIM_A ** -0.5), NEG_INF) for b, h in pairs]
        ms = [jnp.max(s, axis=1, keepdims=True) for s in ss]
        ps = [jnp.exp(s - m) for s, m in zip(ss, ms)]
        sums = [jnp.sum(p, axis=1, keepdims=True) for p in ps]
        ohs = [_dot(p.astype(BF16), vvs[b]) for p, (b, h) in zip(ps, pairs)]
        for b in range(len(items)):
            o_acc = jnp.zeros(shape, F32)
            lse_acc = jnp.zeros(shape, F32)
            for h in range(4):
                i = 4 * b + h
                o_acc = jnp.where(hms[h], ohs[i] / sums[i], o_acc)
                lse_acc = jnp.where(hms[h], ms[i] + jnp.log(sums[i]), lse_acc)
            for s in range(2):
                ls = slice(s * LANES, (s + 1) * LANES)
                o_s[s, rows_of[b], :] = o_acc[:, ls]
                l_s[s, rows_of[b], :] = lse_acc[:, ls]

    if nsub == 1:
        def trip(i, carry):
            blocks([(0, 2 * i), (0, 2 * i + 1)])
            return carry

        lax.fori_loop(0, dil // 2, trip, 0)
    else:
        def trip(r, carry):
            for j0 in range(0, nsub, 2):
                blocks([(j0, r), (j0 + 1, r)])
            return carry

        lax.fori_loop(0, dil, trip, 0)
    for s in range(2):
        ls = slice(s * LANES, (s + 1) * LANES)
        o_ref[:, ls] = o_s[s]
        lse_ref[:, ls] = l_s[s]


def _swa_prompt(q, k, v, g, dil):
    b, s, _ = q.shape
    nsub = max(1, 4 // dil)
    t = nsub * dil * SWA_BLOCK
    assert s % t == 0
    inp = pl.BlockSpec((None, t, GROUP_W), lambda bi, n: (bi, n, g))
    out = pl.BlockSpec((None, t, GROUP_W), lambda bi, n: (bi, n, 0))
    slab = lambda *lead: pltpu.VMEM((*lead, 2, t, LANES), F32)
    o, lse = pl.pallas_call(
        functools.partial(_swa_prompt_kernel, dil=dil, nsub=nsub),
        grid=(b, s // t),
        in_specs=[inp, inp, inp],
        out_specs=[out, out],
        out_shape=[jax.ShapeDtypeStruct((b, s, GROUP_W), F32)] * 2,
        scratch_shapes=[slab(), slab(2), slab(2), slab(), slab()],
        compiler_params=_params(("arbitrary", "arbitrary"), 48),
        name=f"swa_prompt_d{dil}",
    )(q, k, v)
    return o.reshape(b * s, GROUP_W), lse.reshape(b * s, GROUP_W)


def _swa_sample_kernel(*refs, win, dil):
    def one(i, carry):
        _swa_sample_one(*(r.at[i] for r in refs), win=win, dil=dil)
        return carry

    lax.fori_loop(0, refs[0].shape[0], one, 0)


def _swa_sample_one(q_ref, kn_ref, vn_ref, kt_ref, vt_ref, o_ref, lse_ref, kto_ref, vto_ref, *, win, dil):
    t_new = q_ref.shape[0]
    n_tiles = win // LANES
    pad = jnp.zeros((LANES - t_new, GROUP_W), F32)
    kn = jnp.concatenate([kn_ref[...], pad], axis=0)
    vn = jnp.concatenate([vn_ref[...], pad], axis=0)

    def channel_major(a):
        return jnp.concatenate([a[:, :LANES].T, a[:, LANES:].T], axis=0)

    knt, vnt = channel_major(kn), channel_major(vn)
    lane = lax.broadcasted_iota(I32, (GROUP_W, LANES), 1)
    keep = lane < LANES - t_new
    for src_ref, new_t, dst_ref in ((kt_ref, knt, kto_ref), (vt_ref, vnt, vto_ref)):
        nxt = pltpu.roll(src_ref[:, 0:LANES], LANES - t_new, 1)
        for j in range(n_tiles):
            this = nxt
            following = new_t if j + 1 == n_tiles else src_ref[:, (j + 1) * LANES:(j + 2) * LANES]
            nxt = pltpu.roll(following, LANES - t_new, 1)
            dst_ref[:, j * LANES:(j + 1) * LANES] = jnp.where(keep, this, nxt)

    qf = q_ref[...].astype(F32)
    ch_head = lax.broadcasted_iota(I32, (t_new, GROUP_W), 1) // HEAD_DIM_A
    qm = jnp.concatenate([jnp.where(ch_head == h, qf, 0.0) for h in range(4)], axis=0).astype(BF16)
    nq = 4 * t_new
    scale = HEAD_DIM_A ** -0.5
    s_c = _dot(qm, kt_ref[...].astype(BF16)) * scale
    s_n = _dot(qm, knt.astype(BF16)) * scale
    t_c = lax.broadcasted_iota(I32, (nq, win), 0) & (t_new - 1)
    d_c = win + t_c - lax.broadcasted_iota(I32, (nq, win), 1)
    s_c = jnp.where((d_c <= win) & ((d_c & (dil - 1)) == 0), s_c, NEG_INF)
    t_n = lax.broadcasted_iota(I32, (nq, LANES), 0) & (t_new - 1)
    d_n = t_n - lax.broadcasted_iota(I32, (nq, LANES), 1)
    s_n = jnp.where((d_n >= 0) & ((d_n & (dil - 1)) == 0), s_n, NEG_INF)
    m = jnp.maximum(jnp.max(s_c, axis=1, keepdims=True), jnp.max(s_n, axis=1, keepdims=True))
    p_c = jnp.exp(s_c - m)
    p_n = jnp.exp(s_n - m)
    l = jnp.sum(p_c, axis=1, keepdims=True) + jnp.sum(p_n, axis=1, keepdims=True)
    o_all = (_dot_nt(p_c.astype(BF16), vt_ref[...].astype(BF16)) + _dot(p_n.astype(BF16), vn.astype(BF16))) / l
    lse_all = m + jnp.log(l)
    o_acc = jnp.zeros((t_new, GROUP_W), F32)
    lse_acc = jnp.zeros((t_new, GROUP_W), F32)
    for h in range(4):
        hm = ch_head == h
        o_acc = jnp.where(hm, o_all[h * t_new:(h + 1) * t_new, :], o_acc)
        lse_acc = jnp.where(hm, lse_all[h * t_new:(h + 1) * t_new, :], lse_acc)
    o_ref[...] = o_acc
    lse_ref[...] = lse_acc


def _swa_sample(q, k_new, v_new, kt_buf, vt_buf, g, win, dil):
    b, t, _ = q.shape
    assert t == 8 and win % LANES == 0 and dil & (dil - 1) == 0
    bb = max(1, 2048 // win)
    assert b % bb == 0
    new = pl.BlockSpec((bb, t, GROUP_W), lambda bi: (bi, 0, g))
    buf = pl.BlockSpec((bb, GROUP_W, win), lambda bi: (bi, 0, 0))
    out = pl.BlockSpec((bb, t, GROUP_W), lambda bi: (bi, 0, 0))
    o, lse, kto, vto = pl.pallas_call(
        functools.partial(_swa_sample_kernel, win=win, dil=dil),
        grid=(b // bb,),
        in_specs=[new, new, new, buf, buf],
        out_specs=[out, out, buf, buf],
        out_shape=[jax.ShapeDtypeStruct((b, t, GROUP_W), F32)] * 2
        + [jax.ShapeDtypeStruct((b, GROUP_W, win), F32)] * 2,
        compiler_params=_params(("arbitrary",), 48),
        name=f"swa_sample_w{win}",
    )(q, k_new, v_new, kt_buf, vt_buf)
    return o.reshape(b * t, GROUP_W), lse.reshape(b * t, GROUP_W), kto, vto


def _conv_kernel(halo_ref, cur_ref, w_ref, b_ref, g_ref, beta_ref, o_ref, ctx_ref, *, zero_first_halo, chunk):
    halo = halo_ref[...]
    if zero_first_halo:
        halo = jnp.where(pl.program_id(1) == 0, 0.0, halo)
    _conv_module(halo, cur_ref, w_ref, b_ref, g_ref, beta_ref, o_ref, ctx_ref, chunk)


def _conv_module(halo, cur_ref, w_ref, b_ref, g_ref, beta_ref, o_ref, ctx_ref, chunk):
    for rows in _conv_module_parts(halo, cur_ref, w_ref, b_ref, g_ref, beta_ref, o_ref, ctx_ref, chunk):
        rows()


def _conv_module_parts(halo, cur_ref, w_ref, b_ref, g_ref, beta_ref, o_ref, ctx_ref, chunk):
    tm = cur_ref.shape[0]
    ctx_ref[pl.ds(0, CONV_HALO), :] = halo
    ctx_ref[pl.ds(CONV_HALO, tm), :] = cur_ref[...]
    first = CONV_HALO - (CONV_WIDTH - 1)

    def rows_from(c0):
        acc = jnp.zeros((chunk, CONV_CH), F32) + b_ref[...]
        for phase in range(8):
            taps = range(phase, CONV_WIDTH, 8)
            start, shift = divmod(first + phase, 8)
            need = chunk + 8 * (len(taps) - 1)
            if shift == 0:
                window = ctx_ref[pl.ds(c0 + 8 * start, need), :]
            else:
                rows = ctx_ref[pl.ds(c0 + 8 * start, need + 8), :]
                window = pltpu.roll(rows, need + 8 - shift, 0)[0:need, :]
            for a, w in enumerate(taps):
                acc = acc + window[8 * a:8 * a + chunk, :] * w_ref[pl.ds(w, 1), :]
        mu = jnp.mean(acc, axis=-1, keepdims=True)
        xc = acc - mu
        var = jnp.mean(xc * xc, axis=-1, keepdims=True)
        y = xc * lax.rsqrt(var + EPS) * g_ref[...] + beta_ref[...]
        o_ref[pl.ds(c0, chunk), :] = (y * jax.nn.sigmoid(y)).astype(BF16)

    return [functools.partial(rows_from, c0) for c0 in range(0, tm, chunk)]


def _conv_branch(u, halo, conv_w, conv_b, cn_g, cn_b, tm, halo_from_u):
    b, t, _ = u.shape
    assert t % tm == 0
    ratio = tm // CONV_HALO if halo_from_u else 0
    halo_map = (lambda bi, i: (bi, jnp.maximum(i * ratio - 1, 0), 0)) if halo_from_u else (lambda bi, i: (bi, 0, 0))
    const = lambda bi, i: (0, 0)
    return pl.pallas_call(
        functools.partial(_conv_kernel, zero_first_halo=halo_from_u, chunk=min(tm, 64)),
        grid=(b, t // tm),
        in_specs=[
            pl.BlockSpec((None, CONV_HALO, CONV_CH), halo_map),
            pl.BlockSpec((None, tm, CONV_CH), lambda bi, i: (bi, i, 0)),
            pl.BlockSpec((CONV_WIDTH, CONV_CH), const),
            pl.BlockSpec((1, CONV_CH), const), pl.BlockSpec((1, CONV_CH), const), pl.BlockSpec((1, CONV_CH), const),
        ],
        out_specs=pl.BlockSpec((None, tm, CONV_CH), lambda bi, i: (bi, i, 0)),
        out_shape=jax.ShapeDtypeStruct((b, t, CONV_CH), BF16),
        scratch_shapes=[pltpu.VMEM((CONV_HALO + tm, CONV_CH), F32)],
        compiler_params=_params(("arbitrary", "arbitrary"), 32),
        name="conv_module",
    )(halo, u, conv_w, conv_b, cn_g, cn_b).reshape(b * t, CONV_CH)


def _mem_kv_kernel(mem_ref, g_ref, w_ref, kg_ref, k_ref, v_ref):
    hb = _rms(mem_ref[...], g_ref[...]).astype(BF16)
    for h in range(MEM_HEADS):
        cs = slice(h * MEM_HEAD_DIM, (h + 1) * MEM_HEAD_DIM)
        k_ref[:, cs] = _rms(_dot(hb, w_ref[:, cs]), kg_ref[...])
        v_ref[:, cs] = _dot(hb, w_ref[:, MEM_WIDTH + h * MEM_HEAD_DIM:MEM_WIDTH + (h + 1) * MEM_HEAD_DIM])


def _mem_kv(mem, g, w, kg):
    b = mem.shape[0]
    const = lambda bi: (0, 0)
    blk = pl.BlockSpec((None, MEM_LEN, MEM_WIDTH), lambda bi: (bi, 0, 0))
    return pl.pallas_call(
        _mem_kv_kernel,
        grid=(b,),
        in_specs=[pl.BlockSpec((None, MEM_LEN, D_MODEL), lambda bi: (bi, 0, 0)), pl.BlockSpec((1, D_MODEL), const),
                  pl.BlockSpec((D_MODEL, 2 * MEM_WIDTH), const), pl.BlockSpec((1, MEM_HEAD_DIM), const)],
        out_specs=[blk, blk],
        out_shape=[jax.ShapeDtypeStruct((b, MEM_LEN, MEM_WIDTH), F32)] * 2,
        compiler_params=_params(("arbitrary",), 32),
        name="mem_kv",
    )(mem, g, w, kg)


def _mem_attn_kernel(q_ref, k_ref, v_ref, o_ref):
    for h in range(MEM_HEADS):
        cs = slice(h * MEM_HEAD_DIM, (h + 1) * MEM_HEAD_DIM)
        s = _dot_nt(q_ref[:, cs], k_ref[:, cs].astype(BF16)) * (MEM_HEAD_DIM ** -0.5)
        m = jnp.max(s, axis=1, keepdims=True)
        p = jnp.exp(s - m)
        l = jnp.sum(p, axis=1, keepdims=True)
        o_ref[:, cs] = (_dot(p.astype(BF16), v_ref[:, cs].astype(BF16)) / l).astype(BF16)


def _mem_attn(q, k, v, tm):
    b, t, _ = q.shape
    assert t % tm == 0
    kv = pl.BlockSpec((None, MEM_LEN, MEM_WIDTH), lambda bi, i: (bi, 0, 0))
    qo = pl.BlockSpec((None, tm, MEM_WIDTH), lambda bi, i: (bi, i, 0))
    return pl.pallas_call(
        _mem_attn_kernel,
        grid=(b, t // tm),
        in_specs=[qo, kv, kv],
        out_specs=qo,
        out_shape=jax.ShapeDtypeStruct((b, t, MEM_WIDTH), BF16),
        compiler_params=_params(("arbitrary", "arbitrary"), 32),
        name="mem_attn",
    )(q, k, v).reshape(b * t, MEM_WIDTH)


def _mem_attn_rows_kernel(*refs):
    def one(i, carry):
        _mem_attn_rows_one(*(r.at[i] for r in refs))
        return carry

    lax.fori_loop(0, refs[0].shape[0], one, 0)


def _mem_attn_rows_one(q_ref, k_ref, v_ref, o_ref):
    t = q_ref.shape[0]
    qf = q_ref[...].astype(F32)
    qs = jnp.concatenate([qf[:, h * MEM_HEAD_DIM:(h + 1) * MEM_HEAD_DIM] for h in range(MEM_HEADS)], axis=0)
    s = _dot_nt(qs.astype(BF16), k_ref[...].astype(BF16)) * (MEM_HEAD_DIM ** -0.5)
    shape = s.shape
    same_head = (lax.broadcasted_iota(I32, shape, 0) // t) == (lax.broadcasted_iota(I32, shape, 1) & (MEM_HEADS - 1))
    s = jnp.where(same_head, s, NEG_INF)
    m = jnp.max(s, axis=1, keepdims=True)
    p = jnp.exp(s - m)
    l = jnp.sum(p, axis=1, keepdims=True)
    o = _dot(p.astype(BF16), v_ref[...].astype(BF16)) / l
    o_ref[...] = jnp.concatenate([o[h * t:(h + 1) * t, :] for h in range(MEM_HEADS)], axis=1).astype(BF16)


def _mem_attn_rows(q, k_rows, v_rows):
    b, t, _ = q.shape
    bb = 4
    assert b % bb == 0
    kv = pl.BlockSpec((bb, MEM_LEN * MEM_HEADS, MEM_HEAD_DIM), lambda bi: (bi, 0, 0))
    qo = pl.BlockSpec((bb, t, MEM_WIDTH), lambda bi: (bi, 0, 0))
    return pl.pallas_call(
        _mem_attn_rows_kernel,
        grid=(b // bb,),
        in_specs=[qo, kv, kv],
        out_specs=qo,
        out_shape=jax.ShapeDtypeStruct((b, t, MEM_WIDTH), BF16),
        compiler_params=_params(("arbitrary",), 32),
        name="mem_attn_rows",
    )(q, k_rows, v_rows).reshape(b * t, MEM_WIDTH)


def _merge_kernel(x_ref, o1_ref, o2_ref, o3_ref, l1_ref, l2_ref, l3_ref, ob_ref, om_ref, *rest):
    _merge_body((), x_ref, o1_ref, o2_ref, o3_ref, l1_ref, l2_ref, l3_ref, ob_ref, om_ref, *rest)


def _merge_conv_kernel(x_ref, o1_ref, o2_ref, o3_ref, l1_ref, l2_ref, l3_ref, halo_ref, u_ref, om_ref,
                       g1_ref, wg_ref, bg_ref, wa_ref, wb_ref, wm_ref, wo_ref, g2_ref, wrh_ref, wrc_ref, br_ref,
                       cw_ref, cb_ref, cg_ref, cbeta_ref, x1_ref, h2_ref, idx_ref, gw_ref,
                       mix_ref, ctx_ref, ob_ref, *, tiles_per_seq):
    halo = jnp.where(pl.program_id(0) % tiles_per_seq == 0, 0.0, halo_ref[...])
    conv_parts = _conv_module_parts(halo, u_ref, cw_ref, cb_ref, cg_ref, cbeta_ref, ob_ref, ctx_ref, 64)
    _merge_body(conv_parts, x_ref, o1_ref, o2_ref, o3_ref, l1_ref, l2_ref, l3_ref, ob_ref, om_ref,
                g1_ref, wg_ref, bg_ref, wa_ref, wb_ref, wm_ref, wo_ref, g2_ref, wrh_ref, wrc_ref, br_ref,
                x1_ref, h2_ref, idx_ref, gw_ref, mix_ref)


def _merge_body(conv_parts, x_ref, o1_ref, o2_ref, o3_ref, l1_ref, l2_ref, l3_ref, ob_ref, om_ref,
                g1_ref, wg_ref, bg_ref, wa_ref, wb_ref, wm_ref, wo_ref, g2_ref, wrh_ref, wrc_ref, br_ref,
                x1_ref, h2_ref, idx_ref, gw_ref, mix_ref):
    tm = x_ref.shape[0]
    x = x_ref[...]
    hb = _rms(x, g1_ref[...]).astype(BF16)
    la, lb, lc = l1_ref[...], l2_ref[...], l3_ref[...]
    m = jnp.maximum(la, jnp.maximum(lb, lc))
    ea, eb, ec = jnp.exp(la - m), jnp.exp(lb - m), jnp.exp(lc - m)
    oa = ((ea * o1_ref[...] + eb * o2_ref[...] + ec * o3_ref[...]) / (ea + eb + ec)).astype(BF16)
    om = om_ref[...]
    chunks = [slice(j * 256, (j + 1) * 256) for j in range(4)]

    def gate(branch, j):
        c0 = branch * D_MODEL + j * 256
        return jax.nn.sigmoid(_dot(hb, wg_ref[:, c0:c0 + 256]) + bg_ref[:, c0:c0 + 256])

    per_j = -(-len(conv_parts) // 4)
    partial, gate_b = [], []
    for j, cs in enumerate(chunks):
        for part in conv_parts[j * per_j:(j + 1) * per_j]:
            part()
        partial.append(gate(0, j) * _dot(oa, wa_ref[:, cs]) + gate(2, j) * _dot(om, wm_ref[:, cs]))
        gate_b.append(gate(1, j))
    ob = ob_ref[...]
    for j, cs in enumerate(chunks):
        mix_ref[:, cs] = (partial[j] + gate_b[j] * _dot(ob, wb_ref[:, cs])).astype(BF16)
    x1 = x + _dot(mix_ref[...], wo_ref[...])
    x1_ref[...] = x1
    h2 = _rms(x1, g2_ref[...])
    for j in range(ROW_TILE):
        h2_ref[pl.ds(j, tm, stride=ROW_TILE), :] = h2[:, j * LANES:(j + 1) * LANES]
    hi = h2.astype(BF16)
    lo = (h2 - hi.astype(F32)).astype(BF16)
    both = _dot(hi, wrc_ref[...])
    logits = both[:, :LANES] + both[:, LANES:] + _dot(lo, wrh_ref[...]) + br_ref[...]
    lane = lax.broadcasted_iota(I32, (tm, LANES), 1)
    logits = jnp.where(lane < N_EXPERTS, logits, NEG_INF)
    vals, idxs = [], []
    for _ in range(TOP_K):
        mk = jnp.max(logits, axis=1, keepdims=True)
        ik = jnp.min(jnp.where(logits == mk, lane, LANES), axis=1, keepdims=True)
        vals.append(mk)
        idxs.append(ik)
        logits = jnp.where(lane == ik, NEG_INF, logits)
    es = [jnp.exp(v - vals[0]) for v in vals]
    den = es[0] + es[1] + es[2] + es[3]
    l4 = lax.broadcasted_iota(I32, (tm, TOP_K), 1)
    idx_out = jnp.zeros((tm, TOP_K), I32)
    gw_out = jnp.zeros((tm, TOP_K), F32)
    for k in range(TOP_K):
        idx_out = jnp.where(l4 == k, idxs[k], idx_out)
        gw_out = jnp.where(l4 == k, es[k] / den, gw_out)
    idx_ref[...] = idx_out
    gw_ref[...] = gw_out


def _merge(x, os_, lses, ob, om, g1, wg, bg, wa, wb, wm, wo, g2, wrh, wrc, br, tm, conv=None):
    n = x.shape[0]
    assert n % tm == 0
    row = lambda i: (i, 0)
    const = lambda i: (0, 0)
    rows = lambda w: pl.BlockSpec((tm, w), row)
    full = lambda a: pl.BlockSpec(a.shape, const)
    weights = (g1, wg, bg, wa, wb, wm, wo, g2, wrh, wrc, br)
    scratch = [pltpu.VMEM((tm, D_MODEL), BF16)]
    if conv is None:
        body, b_specs, b_args, extra = _merge_kernel, [rows(CONV_CH)], (ob,), ()
    else:
        *conv_params, seq_len = conv
        assert seq_len % tm == 0 and tm % CONV_HALO == 0
        ratio = tm // CONV_HALO
        body = functools.partial(_merge_conv_kernel, tiles_per_seq=seq_len // tm)
        b_specs = [pl.BlockSpec((CONV_HALO, CONV_CH), lambda i: (jnp.maximum(i * ratio - 1, 0), 0)), rows(CONV_CH)]
        b_args, extra = (ob, ob), tuple(conv_params)
        scratch += [pltpu.VMEM((CONV_HALO + tm, CONV_CH), F32), pltpu.VMEM((tm, CONV_CH), BF16)]
    return pl.pallas_call(
        body,
        grid=(n // tm,),
        in_specs=[rows(D_MODEL)] + [rows(GROUP_W)] * 6 + b_specs + [rows(MEM_WIDTH)]
        + [full(a) for a in weights + extra],
        out_specs=[rows(D_MODEL), pl.BlockSpec((tm * ROW_TILE, LANES), row), rows(TOP_K), rows(TOP_K)],
        out_shape=[jax.ShapeDtypeStruct((n, D_MODEL), F32), jax.ShapeDtypeStruct((n * ROW_TILE, LANES), F32),
                   jax.ShapeDtypeStruct((n, TOP_K), I32), jax.ShapeDtypeStruct((n, TOP_K), F32)],
        scratch_shapes=scratch,
        compiler_params=_params(("arbitrary",), 56),
        name="merge_router",
    )(x, *os_, *lses, *b_args, om, *weights, *extra)


def _route_kernel(idx_ref, dest_ref, bexp_ref, seg_ref, cnt_ref, carry_ref, *, block_rows):
    phase, i = pl.program_id(0), pl.program_id(1)
    tq = idx_ref.shape[0]
    nb = bexp_ref.shape[0]
    idx = idx_ref[...]
    lane = lax.broadcasted_iota(I32, (tq, LANES), 1)
    member = jnp.zeros((tq, LANES), F32)
    for k in range(TOP_K):
        member = member + (lane == idx[:, k:k + 1]).astype(F32)
    tile_counts = jnp.sum(member, axis=0, keepdims=True)

    @pl.when(phase == 0)
    def _():
        @pl.when(i == 0)
        def _():
            cnt_ref[...] = jnp.zeros_like(cnt_ref)

        cnt_ref[...] += tile_counts
        dest_ref[...] = jnp.zeros_like(dest_ref)
        bexp_ref[...] = jnp.zeros_like(bexp_ref)
        seg_ref[...] = jnp.zeros_like(seg_ref)

    @pl.when(phase == 1)
    def _():
        @pl.when(i == 0)
        def _():
            carry_ref[...] = jnp.zeros_like(carry_ref)

        counts = jnp.broadcast_to(cnt_ref[...], (8, LANES))
        padded = jnp.floor((counts + (block_rows - 1)) * (1.0 / block_rows)) * block_rows
        lane8 = lax.broadcasted_iota(I32, (8, LANES), 1)
        ends = padded
        for s in (1, 2, 4, 8, 16, 32, 64):
            ends = ends + jnp.where(lane8 >= s, pltpu.roll(ends, s, 1), 0.0)
        starts = (ends - padded)[0:1, :]
        r = lax.broadcasted_iota(I32, (tq, tq), 0)
        c = lax.broadcasted_iota(I32, (tq, tq), 1)
        earlier = _dot((c < r).astype(BF16), member.astype(BF16)) + carry_ref[...]
        carry_ref[...] += tile_counts
        base = earlier + starts
        l4 = lax.broadcasted_iota(I32, (tq, TOP_K), 1)
        dest = jnp.zeros((tq, TOP_K), F32)
        for k in range(TOP_K):
            dk = jnp.sum(jnp.where(lane == idx[:, k:k + 1], base, 0.0), axis=1, keepdims=True)
            dest = jnp.where(l4 == k, dk, dest)
        dest_ref[...] = dest.astype(I32)
        first_row = (lax.broadcasted_iota(I32, (nb, LANES), 0) * block_rows).astype(F32)
        lane_nb = lax.broadcasted_iota(I32, (nb, LANES), 1)
        done = jnp.where((ends[0:1, :] <= first_row) & (lane_nb < N_EXPERTS), 1.0, 0.0)
        bexp_ref[...] = jnp.minimum(jnp.sum(done, axis=1, keepdims=True), N_EXPERTS - 1.0).astype(I32)
        row8 = lax.broadcasted_iota(I32, (8, LANES), 0)
        seg = jnp.where(row8 == 0, ends - padded + counts, jnp.where(row8 == 1, ends, 0.0))
        seg_ref[...] = seg.astype(I32)


def _route(idx, block_rows, n_blocks, tq):
    n = idx.shape[0]
    assert n % tq == 0
    nb_pad = -(-n_blocks // 8) * 8
    dest, bexp, seg = pl.pallas_call(
        functools.partial(_route_kernel, block_rows=block_rows),
        grid=(2, n // tq),
        in_specs=[pl.BlockSpec((tq, TOP_K), lambda p, i: (i, 0))],
        out_specs=[pl.BlockSpec((tq, TOP_K), lambda p, i: (i * p, 0)),
                   pl.BlockSpec((nb_pad, 1), lambda p, i: (0, 0)),
                   pl.BlockSpec((8, LANES), lambda p, i: (0, 0))],
        out_shape=[jax.ShapeDtypeStruct((n, TOP_K), I32), jax.ShapeDtypeStruct((nb_pad, 1), I32),
                   jax.ShapeDtypeStruct((8, LANES), I32)],
        scratch_shapes=[pltpu.VMEM((1, LANES), F32), pltpu.VMEM((1, LANES), F32)],
        compiler_params=_params(("arbitrary", "arbitrary"), 40),
        name="route",
    )(idx)
    return dest.reshape(n * TOP_K), bexp.reshape(nb_pad)[:n_blocks], seg


def _tile_rows(index):
    if isinstance(index, int):
        return pl.ds(index * ROW_TILE, ROW_TILE)
    return pl.ds(pl.multiple_of(index * ROW_TILE, ROW_TILE), ROW_TILE)


def _part_specs(parts, block):
    specs, bounds, first = [], [], 0
    for a in parts:
        assert a.shape[0] % block[0] == 0
        count = a.shape[0] // block[0]
        specs.append(pl.BlockSpec(block, lambda i, first=first, count=count: (jnp.clip(i - first, 0, count - 1), 0)))
        bounds.append((first, count))
        first += count
    return specs, tuple(bounds), first


def _dispatch_kernel(seg_ref, dest_ref, *refs, bounds):
    h_refs, (xs_ref, zero_ref, sem, zero_sem) = refs[:len(bounds)], refs[len(bounds):]
    tm = h_refs[0].shape[0] // ROW_TILE
    i = pl.program_id(0)

    @pl.when(pl.program_id(0) == 0)
    def _():
        zero_ref[...] = jnp.zeros_like(zero_ref)

        def pad_copies(e):
            slot, length = seg_ref[0, e], seg_ref[1, e] - seg_ref[0, e]
            pieces = []
            size = zero_ref.shape[0] // ROW_TILE
            while size >= 1:
                present = (length & size) != 0
                rows = pl.ds(pl.multiple_of(slot * ROW_TILE, ROW_TILE), size * ROW_TILE)
                pieces.append((present, pltpu.make_async_copy(zero_ref.at[pl.ds(0, size * ROW_TILE)],
                                                              xs_ref.at[rows], zero_sem)))
                slot = slot + jnp.where(present, size, 0)
                size //= 2
            return pieces

        def start_expert(e):
            for present, piece in pad_copies(e):
                pl.when(present)(piece.start)

        def wait_expert(e):
            for present, piece in pad_copies(e):
                pl.when(present)(piece.wait)

        def expert(e, carry):
            start_expert(e)
            wait_expert(e - 1)
            return carry

        start_expert(0)
        lax.fori_loop(1, N_EXPERTS, expert, 0)
        wait_expert(N_EXPERTS - 1)

    for h_ref, (first, count) in zip(h_refs, bounds):
        @pl.when((i >= first) & (i < first + count))
        def _(h_ref=h_ref):
            def issue(r, carry):
                for k in range(TOP_K):
                    pltpu.make_async_copy(h_ref.at[_tile_rows(r)], xs_ref.at[_tile_rows(dest_ref[r * TOP_K + k])],
                                          sem).start(priority=k % 2)
                return carry

            for r in range(tm):
                issue(r, 0)

    for k in range(TOP_K):
        pltpu.make_async_copy(h_refs[0], xs_ref.at[pl.ds(0, tm * ROW_TILE)], sem).wait()


def _dispatch(h2t_parts, dest, seg, n_slots, tm, block_rows):
    specs, bounds, steps = _part_specs(h2t_parts, (tm * ROW_TILE, LANES))
    assert block_rows & (block_rows - 1) == 0
    return pl.pallas_call(
        functools.partial(_dispatch_kernel, bounds=bounds),
        grid=(steps,),
        in_specs=[pl.BlockSpec(memory_space=pltpu.SMEM),
                  pl.BlockSpec((tm * TOP_K,), lambda i: (i,), memory_space=pltpu.SMEM)] + specs,
        out_specs=pl.BlockSpec(memory_space=pl.ANY),
        out_shape=jax.ShapeDtypeStruct((n_slots * ROW_TILE, LANES), F32),
        scratch_shapes=[pltpu.VMEM((block_rows // 2 * ROW_TILE, LANES), F32), pltpu.SemaphoreType.DMA,
                        pltpu.SemaphoreType.DMA],
        compiler_params=_params(("arbitrary",), 32),
        name="dispatch",
    )(seg, dest, *h2t_parts)


def _moe_kernel(be_ref, seg_ref, xs_ref, wg_ref, bg_ref, wu_ref, bu_ref, wd_ref, bd_ref, yb_ref, wg_s, wu_s, wd_s):
    i = pl.program_id(0)
    bm = xs_ref.shape[0] // ROW_TILE
    used_rows = seg_ref[LANES + N_EXPERTS - 1]

    @pl.when(i * bm < used_rows)
    def _():
        changed = (i == 0) | (be_ref[i] != be_ref[jnp.maximum(i - 1, 0)])

        @pl.when(changed)
        def _():
            for r0 in range(0, D_MODEL, 128):
                rs = pl.ds(r0, 128)
                wg_s[rs, :] = wg_ref[rs, :].astype(BF16)
                wu_s[rs, :] = wu_ref[rs, :].astype(BF16)
                wd_s[rs, :] = wd_ref[rs, :].astype(BF16)

        xb = jnp.concatenate([xs_ref[pl.ds(j, bm, stride=ROW_TILE), :] for j in range(ROW_TILE)],
                             axis=1).astype(BF16)
        chunks = [slice(c * 256, (c + 1) * 256) for c in range(D_FF // 256)]
        gs = [_dot(xb, wg_s[:, cs]) + bg_ref[:, cs] for cs in chunks]
        us = [_dot(xb, wu_s[:, cs]) + bu_ref[:, cs] for cs in chunks]
        acts = []
        for g, u in zip(gs, us):
            g = jnp.minimum(g, SWIGLU_LIMIT)
            u = jnp.clip(u, -SWIGLU_LIMIT, SWIGLU_LIMIT)
            acts.append((g * jax.nn.sigmoid(SWIGLU_ALPHA * g) * (u + 1.0)).astype(BF16))
        acc = jnp.zeros((bm, D_MODEL), F32) + bd_ref[...]
        for act, cs in zip(acts, chunks):
            acc = acc + _dot(act, wd_s[cs, :])
        for j in range(ROW_TILE):
            yb_ref[pl.ds(j, bm, stride=ROW_TILE), :] = acc[:, j * LANES:(j + 1) * LANES]


def _moe(xs, bexp, seg, wg, bg, wu, bu, wd, bd, bm):
    n_slots = xs.shape[0] // ROW_TILE
    assert n_slots % bm == 0
    wspec = pl.BlockSpec((None, D_MODEL, D_FF), lambda i, be, sg: (be[i], 0, 0))
    bspec = pl.BlockSpec((None, 1, D_FF), lambda i, be, sg: (be[i], 0, 0))
    rows = pl.BlockSpec((bm * ROW_TILE, LANES), lambda i, be, sg: (i, 0))
    grid_spec = pltpu.PrefetchScalarGridSpec(
        num_scalar_prefetch=2,
        grid=(n_slots // bm,),
        in_specs=[rows, wspec, bspec, wspec, bspec, wspec, bspec],
        out_specs=rows,
        scratch_shapes=[pltpu.VMEM((D_MODEL, D_FF), BF16)] * 3,
    )
    return pl.pallas_call(
        _moe_kernel,
        grid_spec=grid_spec,
        out_shape=jax.ShapeDtypeStruct((n_slots * ROW_TILE, LANES), F32),
        compiler_params=_params(("arbitrary",), 56),
        name="moe_experts",
    )(bexp, seg.reshape(-1), xs, wg, bg.reshape(N_EXPERTS, 1, D_FF), wu, bu.reshape(N_EXPERTS, 1, D_FF),
      wd, bd.reshape(N_EXPERTS, 1, D_MODEL))


def _combine_kernel(dest_ref, dest_next_ref, gw_ref, yb_ref, *refs, bounds):
    nparts = len(bounds)
    x1_refs, y_refs, (rows_ref, sems) = refs[:nparts], refs[nparts:2 * nparts], refs[2 * nparts:]
    tm = x1_refs[0].shape[0]
    i = pl.program_id(0)
    slot = i % 2

    def issue(dref, buf, unrolled):
        def body(r, carry):
            for k in range(TOP_K):
                pltpu.make_async_copy(yb_ref.at[_tile_rows(dref[r * TOP_K + k])], rows_ref.at[buf, k, _tile_rows(r)],
                                      sems.at[buf]).start(priority=k % 2)
            return carry

        if unrolled:
            for r in range(tm):
                body(r, 0)
        else:
            lax.fori_loop(0, tm, body, 0, unroll=4)

    @pl.when(i == 0)
    def _():
        issue(dest_ref, slot, False)

    @pl.when(i + 1 < pl.num_programs(0))
    def _():
        issue(dest_next_ref, 1 - slot, True)

    for k in range(TOP_K):
        pltpu.make_async_copy(yb_ref.at[pl.ds(0, tm * ROW_TILE)], rows_ref.at[slot, k], sems.at[slot]).wait()
    gw = gw_ref[...]
    for x1_ref, y_ref, (first, count) in zip(x1_refs, y_refs, bounds):
        @pl.when((i >= first) & (i < first + count))
        def _(x1_ref=x1_ref, y_ref=y_ref):
            for j in range(ROW_TILE):
                ls = slice(j * LANES, (j + 1) * LANES)
                y = x1_ref[:, ls]
                for k in range(TOP_K):
                    y = y + gw[:, k:k + 1] * rows_ref[slot, k, pl.ds(j, tm, stride=ROW_TILE), :]
                y_ref[:, ls] = y


def _combine(x1_parts, gw, dest, yb, tm):
    specs, bounds, steps = _part_specs(x1_parts, (tm, D_MODEL))
    return pl.pallas_call(
        functools.partial(_combine_kernel, bounds=bounds),
        grid=(steps,),
        in_specs=[pl.BlockSpec((tm * TOP_K,), lambda i: (i,), memory_space=pltpu.SMEM),
                  pl.BlockSpec((tm * TOP_K,), lambda i: (jnp.minimum(i + 1, steps - 1),), memory_space=pltpu.SMEM),
                  pl.BlockSpec((tm, TOP_K), lambda i: (i, 0)),
                  pl.BlockSpec(memory_space=pl.ANY)] + specs,
        out_specs=specs,
        out_shape=[jax.ShapeDtypeStruct(a.shape, F32) for a in x1_parts],
        scratch_shapes=[pltpu.VMEM((2, TOP_K, tm * ROW_TILE, LANES), F32), pltpu.SemaphoreType.DMA((2,))],
        compiler_params=_params(("arbitrary",), 32),
        name="combine",
    )(dest, dest, gw, yb, *x1_parts)


def _rope_tables(positions):
    half = ROPE_DIM // 2
    inv = jnp.power(jnp.float32(ROPE_THETA), -jnp.arange(half, dtype=F32) / half)
    ang = positions.astype(F32)[:, None] * inv[None, :]
    cos, sin = jnp.cos(ang), jnp.sin(ang)
    t = positions.shape[0]
    z8 = jnp.zeros((t, half), F32)
    rest0 = jnp.zeros((t, HEAD_DIM_A - ROPE_DIM), F32)
    a = jnp.concatenate([cos, cos, jnp.ones((t, HEAD_DIM_A - ROPE_DIM), F32)], axis=1)
    bp = jnp.concatenate([z8, sin, rest0], axis=1)
    bm = jnp.concatenate([-sin, z8, rest0], axis=1)
    return tuple(jnp.tile(m, (1, LANES // HEAD_DIM_A)) for m in (a, bp, bm))


def _segment_mean_matrix(seg):
    i = jnp.arange(256)
    return jnp.where((i[:, None] // seg) == (i[None, :] // seg), 1.0 / seg, 0.0).astype(BF16)


def _layer_weights(l, norm1_g, w_in, b_gates, qn_a, kn_a, conv_w, conv_b, cn_g, cn_b, qn_m,
                   w_proj_a, w_proj_b, w_proj_m, w_out, norm2_g, w_router, b_router,
                   w_gate, b_gate, w_up, b_up, w_down, b_down):
    wr = jnp.pad(w_router[l], ((0, 0), (0, LANES - N_EXPERTS)))
    wrh = wr.astype(BF16)
    return dict(
        g1=norm1_g[l][None, :],
        w1=w_in[l][:, :W1_COLS].astype(BF16),
        wgates=w_in[l][:, W1_COLS:].astype(BF16),
        bgates=b_gates[l][None, :],
        qg=jnp.tile(qn_a[l], 256 // HEAD_DIM_A)[None, :],
        kg=jnp.tile(kn_a[l], 256 // HEAD_DIM_A)[None, :],
        mg=jnp.tile(qn_m[l], 256 // MEM_HEAD_DIM)[None, :],
        conv_w=conv_w[l], conv_b=conv_b[l][None, :], cn_g=cn_g[l][None, :], cn_b=cn_b[l][None, :],
        wa=w_proj_a[l].astype(BF16), wb=w_proj_b[l].astype(BF16), wm=w_proj_m[l].astype(BF16),
        wo=w_out[l].astype(BF16), g2=norm2_g[l][None, :],
        wrh=wrh, wrc=jnp.concatenate([wrh, (wr - wrh.astype(F32)).astype(BF16)], axis=1),
        br=jnp.pad(b_router[l], (0, LANES - N_EXPERTS))[None, :],
        w_gate=w_gate[l], b_gate=b_gate[l], w_up=w_up[l], b_up=b_up[l], w_down=w_down[l], b_down=b_down[l],
        seg64=_segment_mean_matrix(HEAD_DIM_A), seg128=_segment_mean_matrix(MEM_HEAD_DIM),
    )


def _merge_group(x2d, os_, lses, ob, om, w, tm, conv_seq_len=None):
    conv = None if conv_seq_len is None else (w["conv_w"], w["conv_b"], w["cn_g"], w["cn_b"], conv_seq_len)
    return _merge(x2d, os_, lses, ob, om, w["g1"], w["wgates"], w["bgates"], w["wa"], w["wb"], w["wm"],
                  w["wo"], w["g2"], w["wrh"], w["wrc"], w["br"], tm, conv)


MOE_ROWS = 512
ROUTE_TILE = 1024
DISPATCH_TILE = 256
COMBINE_TILE = 128


def _moe_groups(merged, w):
    x1s, h2ts, idxs, gws = zip(*merged)
    idx = jnp.concatenate(idxs, axis=0)
    gw = jnp.concatenate(gws, axis=0)
    n = idx.shape[0]
    n_blocks = -(-(n * TOP_K) // MOE_ROWS) + N_EXPERTS
    dest, bexp, seg = _route(idx, MOE_ROWS, n_blocks, ROUTE_TILE)
    xs = _dispatch(h2ts, dest, seg, n_blocks * MOE_ROWS, DISPATCH_TILE, MOE_ROWS)
    yb = _moe(xs, bexp, seg, w["w_gate"], w["b_gate"], w["w_up"], w["b_up"], w["w_down"], w["b_down"], MOE_ROWS)
    return _combine(x1s, gw, dest, yb, COMBINE_TILE)


def _prompt_layer(x, mem, w, mem_norm_g, w_mem_kv, kn_m):
    b, s, _ = x.shape
    n = b * s
    x2d = x.reshape(n, D_MODEL)
    tabs = _rope_tables(jnp.arange(s, dtype=I32))
    q, k, v, u, qm = _in_proj(x2d, w["g1"], w["w1"], w["qg"], w["kg"], w["mg"], tabs, w["seg64"], w["seg128"], 512, F32)
    q3, k3, v3 = (t.reshape(b, s, WIDTH_A) for t in (q, k, v))
    os_, lses, caches = [], [], []
    for g, (win, dil) in enumerate(SWA_GROUPS):
        o, lse = _swa_prompt(q3, k3, v3, g, dil)
        os_.append(o)
        lses.append(lse)
        cs = slice(g * GROUP_W, (g + 1) * GROUP_W)
        caches += [k3[:, s - win:, cs].reshape(b, win, 4, HEAD_DIM_A), v3[:, s - win:, cs].reshape(b, win, 4, HEAD_DIM_A)]
    u3 = u.reshape(b, s, CONV_CH)
    mk, mv = _mem_kv(mem, mem_norm_g[None, :], w_mem_kv.astype(BF16), kn_m[None, :])
    om = _mem_attn(qm.reshape(b, s, MEM_WIDTH), mk, mv, 512)
    merged = _merge_group(x2d, os_, lses, u, om, w, 512, conv_seq_len=s)
    state = caches + [u3[:, s - (CONV_WIDTH - 1):], mk.reshape(b, MEM_LEN, MEM_HEADS, MEM_HEAD_DIM),
                      mv.reshape(b, MEM_LEN, MEM_HEADS, MEM_HEAD_DIM)]
    return merged, state


def _sample_layer(x, mem_k, mem_v, bufs, conv_state, w):
    b, t, _ = x.shape
    n = b * t
    x2d = x.reshape(n, D_MODEL)
    tabs = _rope_tables(jnp.tile(PAST_LEN + jnp.arange(t, dtype=I32), b))
    q, k, v, u, qm = _in_proj(x2d, w["g1"], w["w1"], w["qg"], w["kg"], w["mg"], tabs, w["seg64"], w["seg128"], 256, BF16)
    q3, k3, v3 = (a.reshape(b, t, WIDTH_A) for a in (q, k, v))
    os_, lses, caches = [], [], []
    for g, (win, dil) in enumerate(SWA_GROUPS):
        to_cm = lambda a: jnp.transpose(a, (0, 2, 3, 1)).reshape(b, GROUP_W, win)
        from_cm = lambda a: jnp.transpose(a.reshape(b, 4, HEAD_DIM_A, win), (0, 3, 1, 2))
        o, lse, kto, vto = _swa_sample(q3, k3, v3, to_cm(bufs[2 * g]), to_cm(bufs[2 * g + 1]), g, win, dil)
        os_.append(o)
        lses.append(lse)
        caches += [from_cm(kto), from_cm(vto)]
    u3 = u.reshape(b, t, CONV_CH)
    halo = jnp.pad(conv_state, ((0, 0), (CONV_HALO - (CONV_WIDTH - 1), 0), (0, 0)))
    ob = _conv_branch(u3, halo, w["conv_w"], w["conv_b"], w["cn_g"], w["cn_b"], t, False)
    om = _mem_attn_rows(qm.reshape(b, t, MEM_WIDTH), mem_k.reshape(b, MEM_LEN * MEM_HEADS, MEM_HEAD_DIM),
                        mem_v.reshape(b, MEM_LEN * MEM_HEADS, MEM_HEAD_DIM))
    merged = _merge_group(x2d, os_, lses, ob, om, w, 256)
    new_conv = jnp.concatenate([conv_state, u3], axis=1)[:, t:]
    return merged, caches + [new_conv]


def kernel(x_prompt, x_sample, mem_prompt, cache_swa_k_w128, cache_swa_v_w128, cache_swa_k_w512, cache_swa_v_w512, cache_swa_k_w2048, cache_swa_v_w2048, state_conv, cache_mem_k, cache_mem_v, norm1_g, w_in, b_gates, qn_a, kn_a, conv_w, conv_b, cn_g, cn_b, mem_norm_g, w_mem_kv, qn_m, kn_m, w_proj_a, w_proj_b, w_proj_m, w_out, norm2_g, w_router, b_router, w_gate, b_gate, w_up, b_up, w_down, b_down):
    depth = norm1_g.shape[0]
    y_p, y_s = x_prompt, x_sample
    st_p, st_s = [], []
    for l in range(depth):
        w = _layer_weights(l, norm1_g, w_in, b_gates, qn_a, kn_a, conv_w, conv_b, cn_g, cn_b, qn_m,
                           w_proj_a, w_proj_b, w_proj_m, w_out, norm2_g, w_router, b_router,
                           w_gate, b_gate, w_up, b_up, w_down, b_down)
        merged_p, state_p = _prompt_layer(y_p, mem_prompt, w, mem_norm_g[l], w_mem_kv[l], kn_m[l])
        bufs = (cache_swa_k_w128[l], cache_swa_v_w128[l], cache_swa_k_w512[l], cache_swa_v_w512[l],
                cache_swa_k_w2048[l], cache_swa_v_w2048[l])
        merged_s, state_s = _sample_layer(y_s, cache_mem_k[l], cache_mem_v[l], bufs, state_conv[l], w)
        y_p2d, y_s2d = _moe_groups((merged_p, merged_s), w)
        y_p, y_s = y_p2d.reshape(y_p.shape), y_s2d.reshape(y_s.shape)
        st_p.append(state_p)
        st_s.append(state_s)
    outs_p = [jnp.stack(a) for a in zip(*st_p)]
    outs_s = [jnp.stack(a) for a in zip(*st_s)]
    return (y_p, y_s, *outs_p, *outs_s)
```

```python
import functools

import jax
import jax.numpy as jnp
from jax import lax
from jax.experimental import pallas as pl
from jax.experimental.pallas import tpu as pltpu

F32 = jnp.float32
BF16 = jnp.bfloat16
I32 = jnp.int32

D_MODEL = 1024
HEAD_DIM_A = 64
GROUP_W = 256
N_GROUPS = 3
WIDTH_A = N_GROUPS * GROUP_W
SWA_GROUPS = ((128, 1), (512, 4), (2048, 16))
SWA_BLOCK = 128
ROPE_DIM = 16
ROPE_THETA = 500000.0
CONV_CH = 512
CONV_WIDTH = 31
CONV_HALO = 32
MEM_LEN = 256
MEM_HEADS = 4
MEM_HEAD_DIM = 128
MEM_WIDTH = 512
N_EXPERTS = 32
TOP_K = 4
D_FF = 1024
SWIGLU_LIMIT = 7.0
SWIGLU_ALPHA = 1.702
EPS = 1e-6
PAST_LEN = 8192
LANES = 128
ROW_TILE = D_MODEL // LANES
W1_COLS = 3 * WIDTH_A + 2 * CONV_CH + MEM_WIDTH
NEG_INF = float("-inf")
MIB = 2 ** 20


def _params(semantics, vmem_mib):
    return pltpu.CompilerParams(dimension_semantics=semantics, vmem_limit_bytes=vmem_mib * MIB)


def _rms(x, gain):
    return x * lax.rsqrt(jnp.mean(x * x, axis=-1, keepdims=True) + EPS) * gain


def _dot(a, b):
    return jnp.dot(a, b, preferred_element_type=F32)


def _dot_nt(a, b):
    return lax.dot_general(a, b, (((1,), (1,)), ((), ())), preferred_element_type=F32)


def _in_proj_kernel(x_ref, g1_ref, w_ref, qg_ref, kg_ref, mg_ref, ra_ref, rp_ref, rm_ref, s64_ref, s128_ref,
                    q_ref, k_ref, v_ref, u_ref, qm_ref):
    hb = _rms(x_ref[...], g1_ref[...]).astype(BF16)
    ra, rp, rm = ra_ref[...], rp_ref[...], rm_ref[...]

    def proj(c0):
        return _dot(hb, w_ref[:, c0:c0 + 256])

    def head_norm(p, seg_ref, gain):
        ms = _dot((p * p).astype(BF16), seg_ref[...])
        return p * lax.rsqrt(ms + EPS) * gain

    def rope(p):
        outs = []
        for j in range(2):
            pj = p[:, j * LANES:(j + 1) * LANES]
            outs.append(pj * ra + pltpu.roll(pj, 8, 1) * rp + pltpu.roll(pj, LANES - 8, 1) * rm)
        return jnp.concatenate(outs, axis=1)

    for c in range(3):
        cs = slice(c * 256, (c + 1) * 256)
        q_ref[:, cs] = rope(head_norm(proj(c * 256), s64_ref, qg_ref[...])).astype(q_ref.dtype)
        k_ref[:, cs] = rope(head_norm(proj(WIDTH_A + c * 256), s64_ref, kg_ref[...]))
        v_ref[:, cs] = proj(2 * WIDTH_A + c * 256)
    for c in range(2):
        cs = slice(c * 256, (c + 1) * 256)
        a = proj(3 * WIDTH_A + c * 256)
        gate = proj(3 * WIDTH_A + CONV_CH + c * 256)
        u_ref[:, cs] = a * jax.nn.sigmoid(gate)
        qm_ref[:, cs] = head_norm(proj(3 * WIDTH_A + 2 * CONV_CH + c * 256), s128_ref, mg_ref[...]).astype(BF16)


def _in_proj(x, g1, w1, qg, kg, mg, tabs, seg64, seg128, tm, q_dtype):
    n = x.shape[0]
    t_rows = tabs[0].shape[0]
    assert n % tm == 0 and t_rows % tm == 0
    t_blocks = t_rows // tm
    row = lambda i: (i, 0)
    const = lambda i: (0, 0)
    tab = lambda i: (i % t_blocks, 0)
    return pl.pallas_call(
        _in_proj_kernel,
        grid=(n // tm,),
        in_specs=[
            pl.BlockSpec((tm, D_MODEL), row),
            pl.BlockSpec((1, D_MODEL), const),
            pl.BlockSpec((D_MODEL, W1_COLS), const),
            pl.BlockSpec((1, 256), const), pl.BlockSpec((1, 256), const), pl.BlockSpec((1, 256), const),
            pl.BlockSpec((tm, LANES), tab), pl.BlockSpec((tm, LANES), tab), pl.BlockSpec((tm, LANES), tab),
            pl.BlockSpec((256, 256), const), pl.BlockSpec((256, 256), const),
        ],
        out_specs=[
            pl.BlockSpec((tm, WIDTH_A), row), pl.BlockSpec((tm, WIDTH_A), row), pl.BlockSpec((tm, WIDTH_A), row),
            pl.BlockSpec((tm, CONV_CH), row), pl.BlockSpec((tm, MEM_WIDTH), row),
        ],
        out_shape=[
            jax.ShapeDtypeStruct((n, WIDTH_A), q_dtype), jax.ShapeDtypeStruct((n, WIDTH_A), F32),
            jax.ShapeDtypeStruct((n, WIDTH_A), F32), jax.ShapeDtypeStruct((n, CONV_CH), F32),
            jax.ShapeDtypeStruct((n, MEM_WIDTH), BF16),
        ],
        compiler_params=_params(("arbitrary",), 48),
        name="in_proj",
    )(x, g1, w1, qg, kg, mg, *tabs, seg64, seg128)


def _swa_prompt_kernel(q_ref, k_ref, v_ref, o_ref, lse_ref, q_s, k_s, v_s, o_s, l_s, *, dil, nsub):
    n = pl.program_id(1)
    span = SWA_BLOCK * dil
    cur = n % 2
    prv = 1 - cur
    for s in range(2):
        ls = slice(s * LANES, (s + 1) * LANES)
        q_s[s] = q_ref[:, ls]
        k_s[cur, s] = k_ref[:, ls]
        v_s[cur, s] = v_ref[:, ls]

    @pl.when(n == 0)
    def _():
        k_s[prv] = jnp.zeros(k_s.shape[1:], F32)
        v_s[prv] = jnp.zeros(v_s.shape[1:], F32)

    shape = (SWA_BLOCK, 2 * SWA_BLOCK)
    col = lax.broadcasted_iota(I32, shape, 1)
    off = lax.broadcasted_iota(I32, shape, 0) + SWA_BLOCK - col
    band = (off >= 0) & (off <= SWA_BLOCK)
    band_first = band & ((n > 0) | (col >= SWA_BLOCK))
    head = col // HEAD_DIM_A

    hms = [head == h for h in range(4)]

    def both(ref, lead, rows):
        return jnp.concatenate([ref[(*lead, s, rows, slice(None))] for s in range(2)], axis=1)

    def blocks(items):
        rows_of, masks, qs, kks, vvs = [], [], [], [], []
        for j, r in items:
            rows = pl.ds(j * span + r, SWA_BLOCK, stride=dil)
            before = (prv, pl.ds((nsub - 1) * span + r, SWA_BLOCK, stride=dil)) if j == 0 else \
                     (cur, pl.ds((j - 1) * span + r, SWA_BLOCK, stride=dil))
            rows_of.append(rows)
            masks.append(band_first if j == 0 else band)
            qs.append(both(q_s, (), rows).astype(BF16))
            kks.append(jnp.concatenate([both(k_s, before[:1], before[1]), both(k_s, (cur,), rows)],
                                       axis=0).astype(BF16))
            vvs.append(jnp.concatenate([both(v_s, before[:1], before[1]), both(v_s, (cur,), rows)],
                                       axis=0).astype(BF16))
        pairs = [(b, h) for b in range(len(items)) for h in range(4)]
        ss = [jnp.where(masks[b], _dot_nt(jnp.where(hms[h], qs[b], jnp.zeros_like(qs[b])), kks[b])
                        * (HEAD_DIM_A ** -0.5), NEG_INF) for b, h in pairs]
        ms = [jnp.max(s, axis=1, keepdims=True) for s in ss]
        ps = [jnp.exp(s - m) for s, m in zip(ss, ms)]
        sums = [jnp.sum(p, axis=1, keepdims=True) for p in ps]
        ohs = [_dot(p.astype(BF16), vvs[b]) for p, (b, h) in zip(ps, pairs)]
        for b in range(len(items)):
            o_acc = jnp.zeros(shape, F32)
            lse_acc = jnp.zeros(shape, F32)
            for h in range(4):
                i = 4 * b + h
                o_acc = jnp.where(hms[h], ohs[i] / sums[i], o_acc)
                lse_acc = jnp.where(hms[h], ms[i] + jnp.log(sums[i]), lse_acc)
            for s in range(2):
                ls = slice(s * LANES, (s + 1) * LANES)
                o_s[s, rows_of[b], :] = o_acc[:, ls]
                l_s[s, rows_of[b], :] = lse_acc[:, ls]

    if nsub == 1:
        def trip(i, carry):
            blocks([(0, 2 * i), (0, 2 * i + 1)])
            return carry

        lax.fori_loop(0, dil // 2, trip, 0)
    else:
        def trip(r, carry):
            for j0 in range(0, nsub, 2):
                blocks([(j0, r), (j0 + 1, r)])
            return carry

        lax.fori_loop(0, dil, trip, 0)
    for s in range(2):
        ls = slice(s * LANES, (s + 1) * LANES)
        o_ref[:, ls] = o_s[s]
        lse_ref[:, ls] = l_s[s]


def _swa_prompt(q, k, v, g, dil):
    b, s, _ = q.shape
    nsub = max(1, 4 // dil)
    t = nsub * dil * SWA_BLOCK
    assert s % t == 0
    inp = pl.BlockSpec((None, t, GROUP_W), lambda bi, n: (bi, n, g))
    out = pl.BlockSpec((None, t, GROUP_W), lambda bi, n: (bi, n, 0))
    slab = lambda *lead: pltpu.VMEM((*lead, 2, t, LANES), F32)
    o, lse = pl.pallas_call(
        functools.partial(_swa_prompt_kernel, dil=dil, nsub=nsub),
        grid=(b, s // t),
        in_specs=[inp, inp, inp],
        out_specs=[out, out],
        out_shape=[jax.ShapeDtypeStruct((b, s, GROUP_W), F32)] * 2,
        scratch_shapes=[slab(), slab(2), slab(2), slab(), slab()],
        compiler_params=_params(("arbitrary", "arbitrary"), 48),
        name=f"swa_prompt_d{dil}",
    )(q, k, v)
    return o.reshape(b * s, GROUP_W), lse.reshape(b * s, GROUP_W)


def _swa_sample_kernel(*refs, win, dil):
    def one(i, carry):
        _swa_sample_one(*(r.at[i] for r in refs), win=win, dil=dil)
        return carry

    lax.fori_loop(0, refs[0].shape[0], one, 0)


def _swa_sample_one(q_ref, kn_ref, vn_ref, kt_ref, vt_ref, o_ref, lse_ref, kto_ref, vto_ref, *, win, dil):
    t_new = q_ref.shape[0]
    n_tiles = win // LANES
    pad = jnp.zeros((LANES - t_new, GROUP_W), F32)
    kn = jnp.concatenate([kn_ref[...], pad], axis=0)
    vn = jnp.concatenate([vn_ref[...], pad], axis=0)

    def channel_major(a):
        return jnp.concatenate([a[:, :LANES].T, a[:, LANES:].T], axis=0)

    knt, vnt = channel_major(kn), channel_major(vn)
    lane = lax.broadcasted_iota(I32, (GROUP_W, LANES), 1)
    keep = lane < LANES - t_new
    for src_ref, new_t, dst_ref in ((kt_ref, knt, kto_ref), (vt_ref, vnt, vto_ref)):
        nxt = pltpu.roll(src_ref[:, 0:LANES], LANES - t_new, 1)
        for j in range(n_tiles):
            this = nxt
            following = new_t if j + 1 == n_tiles else src_ref[:, (j + 1) * LANES:(j + 2) * LANES]
            nxt = pltpu.roll(following, LANES - t_new, 1)
            dst_ref[:, j * LANES:(j + 1) * LANES] = jnp.where(keep, this, nxt)

    qf = q_ref[...].astype(F32)
    ch_head = lax.broadcasted_iota(I32, (t_new, GROUP_W), 1) // HEAD_DIM_A
    qm = jnp.concatenate([jnp.where(ch_head == h, qf, 0.0) for h in range(4)], axis=0).astype(BF16)
    nq = 4 * t_new
    scale = HEAD_DIM_A ** -0.5
    s_c = _dot(qm, kt_ref[...].astype(BF16)) * scale
    s_n = _dot(qm, knt.astype(BF16)) * scale
    t_c = lax.broadcasted_iota(I32, (nq, win), 0) & (t_new - 1)
    d_c = win + t_c - lax.broadcasted_iota(I32, (nq, win), 1)
    s_c = jnp.where((d_c <= win) & ((d_c & (dil - 1)) == 0), s_c, NEG_INF)
    t_n = lax.broadcasted_iota(I32, (nq, LANES), 0) & (t_new - 1)
    d_n = t_n - lax.broadcasted_iota(I32, (nq, LANES), 1)
    s_n = jnp.where((d_n >= 0) & ((d_n & (dil - 1)) == 0), s_n, NEG_INF)
    m = jnp.maximum(jnp.max(s_c, axis=1, keepdims=True), jnp.max(s_n, axis=1, keepdims=True))
    p_c = jnp.exp(s_c - m)
    p_n = jnp.exp(s_n - m)
    l = jnp.sum(p_c, axis=1, keepdims=True) + jnp.sum(p_n, axis=1, keepdims=True)
    o_all = (_dot_nt(p_c.astype(BF16), vt_ref[...].astype(BF16)) + _dot(p_n.astype(BF16), vn.astype(BF16))) / l
    lse_all = m + jnp.log(l)
    o_acc = jnp.zeros((t_new, GROUP_W), F32)
    lse_acc = jnp.zeros((t_new, GROUP_W), F32)
    for h in range(4):
        hm = ch_head == h
        o_acc = jnp.where(hm, o_all[h * t_new:(h + 1) * t_new, :], o_acc)
        lse_acc = jnp.where(hm, lse_all[h * t_new:(h + 1) * t_new, :], lse_acc)
    o_ref[...] = o_acc
    lse_ref[...] = lse_acc


def _swa_sample(q, k_new, v_new, kt_buf, vt_buf, g, win, dil):
    b, t, _ = q.shape
    assert t == 8 and win % LANES == 0 and dil & (dil - 1) == 0
    bb = max(1, 4096 // win)
    assert b % bb == 0
    new = pl.BlockSpec((bb, t, GROUP_W), lambda bi: (bi, 0, g))
    buf = pl.BlockSpec((bb, GROUP_W, win), lambda bi: (bi, 0, 0))
    out = pl.BlockSpec((bb, t, GROUP_W), lambda bi: (bi, 0, 0))
    o, lse, kto, vto = pl.pallas_call(
        functools.partial(_swa_sample_kernel, win=win, dil=dil),
        grid=(b // bb,),
        in_specs=[new, new, new, buf, buf],
        out_specs=[out, out, buf, buf],
        out_shape=[jax.ShapeDtypeStruct((b, t, GROUP_W), F32)] * 2
        + [jax.ShapeDtypeStruct((b, GROUP_W, win), F32)] * 2,
        compiler_params=_params(("arbitrary",), 48),
        name=f"swa_sample_w{win}",
    )(q, k_new, v_new, kt_buf, vt_buf)
    return o.reshape(b * t, GROUP_W), lse.reshape(b * t, GROUP_W), kto, vto


def _conv_kernel(halo_ref, cur_ref, w_ref, b_ref, g_ref, beta_ref, o_ref, ctx_ref, *, zero_first_halo, chunk):
    halo = halo_ref[...]
    if zero_first_halo:
        halo = jnp.where(pl.program_id(1) == 0, 0.0, halo)
    _conv_module(halo, cur_ref, w_ref, b_ref, g_ref, beta_ref, o_ref, ctx_ref, chunk)


def _conv_module(halo, cur_ref, w_ref, b_ref, g_ref, beta_ref, o_ref, ctx_ref, chunk):
    for rows in _conv_module_parts(halo, cur_ref, w_ref, b_ref, g_ref, beta_ref, o_ref, ctx_ref, chunk):
        rows()


def _conv_module_parts(halo, cur_ref, w_ref, b_ref, g_ref, beta_ref, o_ref, ctx_ref, chunk):
    tm = cur_ref.shape[0]
    ctx_ref[pl.ds(0, CONV_HALO), :] = halo
    ctx_ref[pl.ds(CONV_HALO, tm), :] = cur_ref[...]
    first = CONV_HALO - (CONV_WIDTH - 1)

    def rows_from(c0):
        acc = jnp.zeros((chunk, CONV_CH), F32) + b_ref[...]
        for phase in range(8):
            taps = range(phase, CONV_WIDTH, 8)
            start, shift = divmod(first + phase, 8)
            need = chunk + 8 * (len(taps) - 1)
            if shift == 0:
                window = ctx_ref[pl.ds(c0 + 8 * start, need), :]
            else:
                rows = ctx_ref[pl.ds(c0 + 8 * start, need + 8), :]
                window = pltpu.roll(rows, need + 8 - shift, 0)[0:need, :]
            for a, w in enumerate(taps):
                acc = acc + window[8 * a:8 * a + chunk, :] * w_ref[pl.ds(w, 1), :]
        mu = jnp.mean(acc, axis=-1, keepdims=True)
        xc = acc - mu
        var = jnp.mean(xc * xc, axis=-1, keepdims=True)
        y = xc * lax.rsqrt(var + EPS) * g_ref[...] + beta_ref[...]
        o_ref[pl.ds(c0, chunk), :] = (y * jax.nn.sigmoid(y)).astype(BF16)

    return [functools.partial(rows_from, c0) for c0 in range(0, tm, chunk)]


def _conv_branch(u, halo, conv_w, conv_b, cn_g, cn_b, tm, halo_from_u):
    b, t, _ = u.shape
    assert t % tm == 0
    ratio = tm // CONV_HALO if halo_from_u else 0
    halo_map = (lambda bi, i: (bi, jnp.maximum(i * ratio - 1, 0), 0)) if halo_from_u else (lambda bi, i: (bi, 0, 0))
    const = lambda bi, i: (0, 0)
    return pl.pallas_call(
        functools.partial(_conv_kernel, zero_first_halo=halo_from_u, chunk=min(tm, 64)),
        grid=(b, t // tm),
        in_specs=[
            pl.BlockSpec((None, CONV_HALO, CONV_CH), halo_map),
            pl.BlockSpec((None, tm, CONV_CH), lambda bi, i: (bi, i, 0)),
            pl.BlockSpec((CONV_WIDTH, CONV_CH), const),
            pl.BlockSpec((1, CONV_CH), const), pl.BlockSpec((1, CONV_CH), const), pl.BlockSpec((1, CONV_CH), const),
        ],
        out_specs=pl.BlockSpec((None, tm, CONV_CH), lambda bi, i: (bi, i, 0)),
        out_shape=jax.ShapeDtypeStruct((b, t, CONV_CH), BF16),
        scratch_shapes=[pltpu.VMEM((CONV_HALO + tm, CONV_CH), F32)],
        compiler_params=_params(("arbitrary", "arbitrary"), 32),
        name="conv_module",
    )(halo, u, conv_w, conv_b, cn_g, cn_b).reshape(b * t, CONV_CH)


def _mem_kv_kernel(mem_ref, g_ref, w_ref, kg_ref, k_ref, v_ref):
    hb = _rms(mem_ref[...], g_ref[...]).astype(BF16)
    for h in range(MEM_HEADS):
        cs = slice(h * MEM_HEAD_DIM, (h + 1) * MEM_HEAD_DIM)
        k_ref[:, cs] = _rms(_dot(hb, w_ref[:, cs]), kg_ref[...])
        v_ref[:, cs] = _dot(hb, w_ref[:, MEM_WIDTH + h * MEM_HEAD_DIM:MEM_WIDTH + (h + 1) * MEM_HEAD_DIM])


def _mem_kv(mem, g, w, kg):
    b = mem.shape[0]
    const = lambda bi: (0, 0)
    blk = pl.BlockSpec((None, MEM_LEN, MEM_WIDTH), lambda bi: (bi, 0, 0))
    return pl.pallas_call(
        _mem_kv_kernel,
        grid=(b,),
        in_specs=[pl.BlockSpec((None, MEM_LEN, D_MODEL), lambda bi: (bi, 0, 0)), pl.BlockSpec((1, D_MODEL), const),
                  pl.BlockSpec((D_MODEL, 2 * MEM_WIDTH), const), pl.BlockSpec((1, MEM_HEAD_DIM), const)],
        out_specs=[blk, blk],
        out_shape=[jax.ShapeDtypeStruct((b, MEM_LEN, MEM_WIDTH), F32)] * 2,
        compiler_params=_params(("arbitrary",), 32),
        name="mem_kv",
    )(mem, g, w, kg)


def _mem_attn_kernel(q_ref, k_ref, v_ref, o_ref):
    for h in range(MEM_HEADS):
        cs = slice(h * MEM_HEAD_DIM, (h + 1) * MEM_HEAD_DIM)
        s = _dot_nt(q_ref[:, cs], k_ref[:, cs].astype(BF16)) * (MEM_HEAD_DIM ** -0.5)
        m = jnp.max(s, axis=1, keepdims=True)
        p = jnp.exp(s - m)
        l = jnp.sum(p, axis=1, keepdims=True)
        o_ref[:, cs] = (_dot(p.astype(BF16), v_ref[:, cs].astype(BF16)) / l).astype(BF16)


def _mem_attn(q, k, v, tm):
    b, t, _ = q.shape
    assert t % tm == 0
    kv = pl.BlockSpec((None, MEM_LEN, MEM_WIDTH), lambda bi, i: (bi, 0, 0))
    qo = pl.BlockSpec((None, tm, MEM_WIDTH), lambda bi, i: (bi, i, 0))
    return pl.pallas_call(
        _mem_attn_kernel,
        grid=(b, t // tm),
        in_specs=[qo, kv, kv],
        out_specs=qo,
        out_shape=jax.ShapeDtypeStruct((b, t, MEM_WIDTH), BF16),
        compiler_params=_params(("arbitrary", "arbitrary"), 32),
        name="mem_attn",
    )(q, k, v).reshape(b * t, MEM_WIDTH)


def _mem_attn_rows_kernel(*refs):
    def one(i, carry):
        _mem_attn_rows_one(*(r.at[i] for r in refs))
        return carry

    lax.fori_loop(0, refs[0].shape[0], one, 0)


def _mem_attn_rows_one(q_ref, k_ref, v_ref, o_ref):
    t = q_ref.shape[0]
    qf = q_ref[...].astype(F32)
    qs = jnp.concatenate([qf[:, h * MEM_HEAD_DIM:(h + 1) * MEM_HEAD_DIM] for h in range(MEM_HEADS)], axis=0)
    s = _dot_nt(qs.astype(BF16), k_ref[...].astype(BF16)) * (MEM_HEAD_DIM ** -0.5)
    shape = s.shape
    same_head = (lax.broadcasted_iota(I32, shape, 0) // t) == (lax.broadcasted_iota(I32, shape, 1) & (MEM_HEADS - 1))
    s = jnp.where(same_head, s, NEG_INF)
    m = jnp.max(s, axis=1, keepdims=True)
    p = jnp.exp(s - m)
    l = jnp.sum(p, axis=1, keepdims=True)
    o = _dot(p.astype(BF16), v_ref[...].astype(BF16)) / l
    o_ref[...] = jnp.concatenate([o[h * t:(h + 1) * t, :] for h in range(MEM_HEADS)], axis=1).astype(BF16)


def _mem_attn_rows(q, k_rows, v_rows):
    b, t, _ = q.shape
    bb = 4
    assert b % bb == 0
    kv = pl.BlockSpec((bb, MEM_LEN * MEM_HEADS, MEM_HEAD_DIM), lambda bi: (bi, 0, 0))
    qo = pl.BlockSpec((bb, t, MEM_WIDTH), lambda bi: (bi, 0, 0))
    return pl.pallas_call(
        _mem_attn_rows_kernel,
        grid=(b // bb,),
        in_specs=[qo, kv, kv],
        out_specs=qo,
        out_shape=jax.ShapeDtypeStruct((b, t, MEM_WIDTH), BF16),
        compiler_params=_params(("arbitrary",), 32),
        name="mem_attn_rows",
    )(q, k_rows, v_rows).reshape(b * t, MEM_WIDTH)


def _merge_kernel(x_ref, o1_ref, o2_ref, o3_ref, l1_ref, l2_ref, l3_ref, ob_ref, om_ref, *rest):
    _merge_body((), x_ref, o1_ref, o2_ref, o3_ref, l1_ref, l2_ref, l3_ref, ob_ref, om_ref, *rest)


def _merge_conv_kernel(x_ref, o1_ref, o2_ref, o3_ref, l1_ref, l2_ref, l3_ref, halo_ref, u_ref, om_ref,
                       g1_ref, wg_ref, bg_ref, wa_ref, wb_ref, wm_ref, wo_ref, g2_ref, wrh_ref, wrc_ref, br_ref,
                       cw_ref, cb_ref, cg_ref, cbeta_ref, x1_ref, h2_ref, idx_ref, gw_ref,
                       mix_ref, ctx_ref, ob_ref, *, tiles_per_seq):
    halo = jnp.where(pl.program_id(0) % tiles_per_seq == 0, 0.0, halo_ref[...])
    conv_parts = _conv_module_parts(halo, u_ref, cw_ref, cb_ref, cg_ref, cbeta_ref, ob_ref, ctx_ref, 64)
    _merge_body(conv_parts, x_ref, o1_ref, o2_ref, o3_ref, l1_ref, l2_ref, l3_ref, ob_ref, om_ref,
                g1_ref, wg_ref, bg_ref, wa_ref, wb_ref, wm_ref, wo_ref, g2_ref, wrh_ref, wrc_ref, br_ref,
                x1_ref, h2_ref, idx_ref, gw_ref, mix_ref)


def _merge_body(conv_parts, x_ref, o1_ref, o2_ref, o3_ref, l1_ref, l2_ref, l3_ref, ob_ref, om_ref,
                g1_ref, wg_ref, bg_ref, wa_ref, wb_ref, wm_ref, wo_ref, g2_ref, wrh_ref, wrc_ref, br_ref,
                x1_ref, h2_ref, idx_ref, gw_ref, mix_ref):
    tm = x_ref.shape[0]
    x = x_ref[...]
    hb = _rms(x, g1_ref[...]).astype(BF16)
    la, lb, lc = l1_ref[...], l2_ref[...], l3_ref[...]
    m = jnp.maximum(la, jnp.maximum(lb, lc))
    ea, eb, ec = jnp.exp(la - m), jnp.exp(lb - m), jnp.exp(lc - m)
    oa = ((ea * o1_ref[...] + eb * o2_ref[...] + ec * o3_ref[...]) / (ea + eb + ec)).astype(BF16)
    om = om_ref[...]
    chunks = [slice(j * 256, (j + 1) * 256) for j in range(4)]

    def gate(branch, j):
        c0 = branch * D_MODEL + j * 256
        return jax.nn.sigmoid(_dot(hb, wg_ref[:, c0:c0 + 256]) + bg_ref[:, c0:c0 + 256])

    per_j = -(-len(conv_parts) // 4)
    partial, gate_b = [], []
    for j, cs in enumerate(chunks):
        for part in conv_parts[j * per_j:(j + 1) * per_j]:
            part()
        partial.append(gate(0, j) * _dot(oa, wa_ref[:, cs]) + gate(2, j) * _dot(om, wm_ref[:, cs]))
        gate_b.append(gate(1, j))
    ob = ob_ref[...]
    for j, cs in enumerate(chunks):
        mix_ref[:, cs] = (partial[j] + gate_b[j] * _dot(ob, wb_ref[:, cs])).astype(BF16)
    x1 = x + _dot(mix_ref[...], wo_ref[...])
    x1_ref[...] = x1
    h2 = _rms(x1, g2_ref[...])
    for j in range(ROW_TILE):
        h2_ref[pl.ds(j, tm, stride=ROW_TILE), :] = h2[:, j * LANES:(j + 1) * LANES]
    hi = h2.astype(BF16)
    lo = (h2 - hi.astype(F32)).astype(BF16)
    both = _dot(hi, wrc_ref[...])
    logits = both[:, :LANES] + both[:, LANES:] + _dot(lo, wrh_ref[...]) + br_ref[...]
    lane = lax.broadcasted_iota(I32, (tm, LANES), 1)
    logits = jnp.where(lane < N_EXPERTS, logits, NEG_INF)
    vals, idxs = [], []
    for _ in range(TOP_K):
        mk = jnp.max(logits, axis=1, keepdims=True)
        ik = jnp.min(jnp.where(logits == mk, lane, LANES), axis=1, keepdims=True)
        vals.append(mk)
        idxs.append(ik)
        logits = jnp.where(lane == ik, NEG_INF, logits)
    es = [jnp.exp(v - vals[0]) for v in vals]
    den = es[0] + es[1] + es[2] + es[3]
    l4 = lax.broadcasted_iota(I32, (tm, TOP_K), 1)
    idx_out = jnp.zeros((tm, TOP_K), I32)
    gw_out = jnp.zeros((tm, TOP_K), F32)
    for k in range(TOP_K):
        idx_out = jnp.where(l4 == k, idxs[k], idx_out)
        gw_out = jnp.where(l4 == k, es[k] / den, gw_out)
    idx_ref[...] = idx_out
    gw_ref[...] = gw_out


def _merge(x, os_, lses, ob, om, g1, wg, bg, wa, wb, wm, wo, g2, wrh, wrc, br, tm, conv=None):
    n = x.shape[0]
    assert n % tm == 0
    row = lambda i: (i, 0)
    const = lambda i: (0, 0)
    rows = lambda w: pl.BlockSpec((tm, w), row)
    full = lambda a: pl.BlockSpec(a.shape, const)
    weights = (g1, wg, bg, wa, wb, wm, wo, g2, wrh, wrc, br)
    scratch = [pltpu.VMEM((tm, D_MODEL), BF16)]
    if conv is None:
        body, b_specs, b_args, extra = _merge_kernel, [rows(CONV_CH)], (ob,), ()
    else:
        *conv_params, seq_len = conv
        assert seq_len % tm == 0 and tm % CONV_HALO == 0
        ratio = tm // CONV_HALO
        body = functools.partial(_merge_conv_kernel, tiles_per_seq=seq_len // tm)
        b_specs = [pl.BlockSpec((CONV_HALO, CONV_CH), lambda i: (jnp.maximum(i * ratio - 1, 0), 0)), rows(CONV_CH)]
        b_args, extra = (ob, ob), tuple(conv_params)
        scratch += [pltpu.VMEM((CONV_HALO + tm, CONV_CH), F32), pltpu.VMEM((tm, CONV_CH), BF16)]
    return pl.pallas_call(
        body,
        grid=(n // tm,),
        in_specs=[rows(D_MODEL)] + [rows(GROUP_W)] * 6 + b_specs + [rows(MEM_WIDTH)]
        + [full(a) for a in weights + extra],
        out_specs=[rows(D_MODEL), pl.BlockSpec((tm * ROW_TILE, LANES), row), rows(TOP_K), rows(TOP_K)],
        out_shape=[jax.ShapeDtypeStruct((n, D_MODEL), F32), jax.ShapeDtypeStruct((n * ROW_TILE, LANES), F32),
                   jax.ShapeDtypeStruct((n, TOP_K), I32), jax.ShapeDtypeStruct((n, TOP_K), F32)],
        scratch_shapes=scratch,
        compiler_params=_params(("arbitrary",), 56),
        name="merge_router",
    )(x, *os_, *lses, *b_args, om, *weights, *extra)


def _route_kernel(idx_ref, dest_ref, bexp_ref, seg_ref, cnt_ref, carry_ref, *, block_rows):
    phase, i = pl.program_id(0), pl.program_id(1)
    tq = idx_ref.shape[0]
    nb = bexp_ref.shape[0]
    idx = idx_ref[...]
    lane = lax.broadcasted_iota(I32, (tq, LANES), 1)
    member = jnp.zeros((tq, LANES), F32)
    for k in range(TOP_K):
        member = member + (lane == idx[:, k:k + 1]).astype(F32)
    tile_counts = jnp.sum(member, axis=0, keepdims=True)

    @pl.when(phase == 0)
    def _():
        @pl.when(i == 0)
        def _():
            cnt_ref[...] = jnp.zeros_like(cnt_ref)

        cnt_ref[...] += tile_counts
        dest_ref[...] = jnp.zeros_like(dest_ref)
        bexp_ref[...] = jnp.zeros_like(bexp_ref)
        seg_ref[...] = jnp.zeros_like(seg_ref)

    @pl.when(phase == 1)
    def _():
        @pl.when(i == 0)
        def _():
            carry_ref[...] = jnp.zeros_like(carry_ref)

        counts = jnp.broadcast_to(cnt_ref[...], (8, LANES))
        padded = jnp.floor((counts + (block_rows - 1)) * (1.0 / block_rows)) * block_rows
        lane8 = lax.broadcasted_iota(I32, (8, LANES), 1)
        ends = padded
        for s in (1, 2, 4, 8, 16, 32, 64):
            ends = ends + jnp.where(lane8 >= s, pltpu.roll(ends, s, 1), 0.0)
        starts = (ends - padded)[0:1, :]
        r = lax.broadcasted_iota(I32, (tq, tq), 0)
        c = lax.broadcasted_iota(I32, (tq, tq), 1)
        earlier = _dot((c < r).astype(BF16), member.astype(BF16)) + carry_ref[...]
        carry_ref[...] += tile_counts
        base = earlier + starts
        l4 = lax.broadcasted_iota(I32, (tq, TOP_K), 1)
        dest = jnp.zeros((tq, TOP_K), F32)
        for k in range(TOP_K):
            dk = jnp.sum(jnp.where(lane == idx[:, k:k + 1], base, 0.0), axis=1, keepdims=True)
            dest = jnp.where(l4 == k, dk, dest)
        dest_ref[...] = dest.astype(I32)
        first_row = (lax.broadcasted_iota(I32, (nb, LANES), 0) * block_rows).astype(F32)
        lane_nb = lax.broadcasted_iota(I32, (nb, LANES), 1)
        done = jnp.where((ends[0:1, :] <= first_row) & (lane_nb < N_EXPERTS), 1.0, 0.0)
        bexp_ref[...] = jnp.minimum(jnp.sum(done, axis=1, keepdims=True), N_EXPERTS - 1.0).astype(I32)
        row8 = lax.broadcasted_iota(I32, (8, LANES), 0)
        seg = jnp.where(row8 == 0, ends - padded + counts, jnp.where(row8 == 1, ends, 0.0))
        seg_ref[...] = seg.astype(I32)


def _route(idx, block_rows, n_blocks, tq):
    n = idx.shape[0]
    assert n % tq == 0
    nb_pad = -(-n_blocks // 8) * 8
    dest, bexp, seg = pl.pallas_call(
        functools.partial(_route_kernel, block_rows=block_rows),
        grid=(2, n // tq),
        in_specs=[pl.BlockSpec((tq, TOP_K), lambda p, i: (i, 0))],
        out_specs=[pl.BlockSpec((tq, TOP_K), lambda p, i: (i * p, 0)),
                   pl.BlockSpec((nb_pad, 1), lambda p, i: (0, 0)),
                   pl.BlockSpec((8, LANES), lambda p, i: (0, 0))],
        out_shape=[jax.ShapeDtypeStruct((n, TOP_K), I32), jax.ShapeDtypeStruct((nb_pad, 1), I32),
                   jax.ShapeDtypeStruct((8, LANES), I32)],
        scratch_shapes=[pltpu.VMEM((1, LANES), F32), pltpu.VMEM((1, LANES), F32)],
        compiler_params=_params(("arbitrary", "arbitrary"), 40),
        name="route",
    )(idx)
    return dest.reshape(n * TOP_K), bexp.reshape(nb_pad)[:n_blocks], seg


def _tile_rows(index):
    if isinstance(index, int):
        return pl.ds(index * ROW_TILE, ROW_TILE)
    return pl.ds(pl.multiple_of(index * ROW_TILE, ROW_TILE), ROW_TILE)


def _part_specs(parts, block):
    specs, bounds, first = [], [], 0
    for a in parts:
        assert a.shape[0] % block[0] == 0
        count = a.shape[0] // block[0]
        specs.append(pl.BlockSpec(block, lambda i, first=first, count=count: (jnp.clip(i - first, 0, count - 1), 0)))
        bounds.append((first, count))
        first += count
    return specs, tuple(bounds), first


def _dispatch_kernel(seg_ref, dest_ref, *refs, bounds):
    h_refs, (xs_ref, zero_ref, sem, zero_sem) = refs[:len(bounds)], refs[len(bounds):]
    tm = h_refs[0].shape[0] // ROW_TILE
    i = pl.program_id(0)

    @pl.when(pl.program_id(0) == 0)
    def _():
        zero_ref[...] = jnp.zeros_like(zero_ref)

        def pad_copies(e):
            slot, length = seg_ref[0, e], seg_ref[1, e] - seg_ref[0, e]
            pieces = []
            size = zero_ref.shape[0] // ROW_TILE
            while size >= 1:
                present = (length & size) != 0
                rows = pl.ds(pl.multiple_of(slot * ROW_TILE, ROW_TILE), size * ROW_TILE)
                pieces.append((present, pltpu.make_async_copy(zero_ref.at[pl.ds(0, size * ROW_TILE)],
                                                              xs_ref.at[rows], zero_sem)))
                slot = slot + jnp.where(present, size, 0)
                size //= 2
            return pieces

        def start_expert(e):
            for present, piece in pad_copies(e):
                pl.when(present)(piece.start)

        def wait_expert(e):
            for present, piece in pad_copies(e):
                pl.when(present)(piece.wait)

        def expert(e, carry):
            start_expert(e)
            wait_expert(e - 1)
            return carry

        start_expert(0)
        lax.fori_loop(1, N_EXPERTS, expert, 0)
        wait_expert(N_EXPERTS - 1)

    for h_ref, (first, count) in zip(h_refs, bounds):
        @pl.when((i >= first) & (i < first + count))
        def _(h_ref=h_ref):
            def issue(r, carry):
                for k in range(TOP_K):
                    pltpu.make_async_copy(h_ref.at[_tile_rows(r)], xs_ref.at[_tile_rows(dest_ref[r * TOP_K + k])],
                                          sem).start(priority=k % 2)
                return carry

            for r in range(tm):
                issue(r, 0)

    for k in range(TOP_K):
        pltpu.make_async_copy(h_refs[0], xs_ref.at[pl.ds(0, tm * ROW_TILE)], sem).wait()


def _dispatch(h2t_parts, dest, seg, n_slots, tm, block_rows):
    specs, bounds, steps = _part_specs(h2t_parts, (tm * ROW_TILE, LANES))
    assert block_rows & (block_rows - 1) == 0
    return pl.pallas_call(
        functools.partial(_dispatch_kernel, bounds=bounds),
        grid=(steps,),
        in_specs=[pl.BlockSpec(memory_space=pltpu.SMEM),
                  pl.BlockSpec((tm * TOP_K,), lambda i: (i,), memory_space=pltpu.SMEM)] + specs,
        out_specs=pl.BlockSpec(memory_space=pl.ANY),
        out_shape=jax.ShapeDtypeStruct((n_slots * ROW_TILE, LANES), F32),
        scratch_shapes=[pltpu.VMEM((block_rows // 2 * ROW_TILE, LANES), F32), pltpu.SemaphoreType.DMA,
                        pltpu.SemaphoreType.DMA],
        compiler_params=_params(("arbitrary",), 32),
        name="dispatch",
    )(seg, dest, *h2t_parts)


def _moe_kernel(be_ref, seg_ref, xs_ref, wg_ref, bg_ref, wu_ref, bu_ref, wd_ref, bd_ref, yb_ref, wg_s, wu_s, wd_s):
    i = pl.program_id(0)
    bm = xs_ref.shape[0] // ROW_TILE
    used_rows = seg_ref[LANES + N_EXPERTS - 1]

    @pl.when(i * bm < used_rows)
    def _():
        changed = (i == 0) | (be_ref[i] != be_ref[jnp.maximum(i - 1, 0)])

        @pl.when(changed)
        def _():
            for r0 in range(0, D_MODEL, 128):
                rs = pl.ds(r0, 128)
                wg_s[rs, :] = wg_ref[rs, :].astype(BF16)
                wu_s[rs, :] = wu_ref[rs, :].astype(BF16)
                wd_s[rs, :] = wd_ref[rs, :].astype(BF16)

        xb = jnp.concatenate([xs_ref[pl.ds(j, bm, stride=ROW_TILE), :] for j in range(ROW_TILE)],
                             axis=1).astype(BF16)
        chunks = [slice(c * 256, (c + 1) * 256) for c in range(D_FF // 256)]
        acts = []
        for cs in chunks:
            g = jnp.minimum(_dot(xb, wg_s[:, cs]) + bg_ref[:, cs], SWIGLU_LIMIT)
            u = jnp.clip(_dot(xb, wu_s[:, cs]) + bu_ref[:, cs], -SWIGLU_LIMIT, SWIGLU_LIMIT)
            acts.append((g * jax.nn.sigmoid(SWIGLU_ALPHA * g) * (u + 1.0)).astype(BF16))
        acc = jnp.zeros((bm, D_MODEL), F32) + bd_ref[...]
        for act, cs in zip(acts, chunks):
            acc = acc + _dot(act, wd_s[cs, :])
        for j in range(ROW_TILE):
            yb_ref[pl.ds(j, bm, stride=ROW_TILE), :] = acc[:, j * LANES:(j + 1) * LANES]


def _moe(xs, bexp, seg, wg, bg, wu, bu, wd, bd, bm):
    n_slots = xs.shape[0] // ROW_TILE
    assert n_slots % bm == 0
    wspec = pl.BlockSpec((None, D_MODEL, D_FF), lambda i, be, sg: (be[i], 0, 0))
    bspec = pl.BlockSpec((None, 1, D_FF), lambda i, be, sg: (be[i], 0, 0))
    rows = pl.BlockSpec((bm * ROW_TILE, LANES), lambda i, be, sg: (i, 0))
    grid_spec = pltpu.PrefetchScalarGridSpec(
        num_scalar_prefetch=2,
        grid=(n_slots // bm,),
        in_specs=[rows, wspec, bspec, wspec, bspec, wspec, bspec],
        out_specs=rows,
        scratch_shapes=[pltpu.VMEM((D_MODEL, D_FF), BF16)] * 3,
    )
    return pl.pallas_call(
        _moe_kernel,
        grid_spec=grid_spec,
        out_shape=jax.ShapeDtypeStruct((n_slots * ROW_TILE, LANES), F32),
        compiler_params=_params(("arbitrary",), 56),
        name="moe_experts",
    )(bexp, seg.reshape(-1), xs, wg, bg.reshape(N_EXPERTS, 1, D_FF), wu, bu.reshape(N_EXPERTS, 1, D_FF),
      wd, bd.reshape(N_EXPERTS, 1, D_MODEL))


def _combine_kernel(dest_ref, dest_next_ref, gw_ref, yb_ref, *refs, bounds):
    nparts = len(bounds)
    x1_refs, y_refs, (rows_ref, sems) = refs[:nparts], refs[nparts:2 * nparts], refs[2 * nparts:]
    tm = x1_refs[0].shape[0]
    i = pl.program_id(0)
    slot = i % 2

    def issue(dref, buf, unrolled):
        def body(r, carry):
            for k in range(TOP_K):
                pltpu.make_async_copy(yb_ref.at[_tile_rows(dref[r * TOP_K + k])], rows_ref.at[buf, k, _tile_rows(r)],
                                      sems.at[buf]).start(priority=k % 2)
            return carry

        if unrolled:
            for r in range(tm):
                body(r, 0)
        else:
            lax.fori_loop(0, tm, body, 0, unroll=4)

    @pl.when(i == 0)
    def _():
        issue(dest_ref, slot, False)

    @pl.when(i + 1 < pl.num_programs(0))
    def _():
        issue(dest_next_ref, 1 - slot, True)

    for k in range(TOP_K):
        pltpu.make_async_copy(yb_ref.at[pl.ds(0, tm * ROW_TILE)], rows_ref.at[slot, k], sems.at[slot]).wait()
    gw = gw_ref[...]
    for x1_ref, y_ref, (first, count) in zip(x1_refs, y_refs, bounds):
        @pl.when((i >= first) & (i < first + count))
        def _(x1_ref=x1_ref, y_ref=y_ref):
            for j in range(ROW_TILE):
                ls = slice(j * LANES, (j + 1) * LANES)
                y = x1_ref[:, ls]
                for k in range(TOP_K):
                    y = y + gw[:, k:k + 1] * rows_ref[slot, k, pl.ds(j, tm, stride=ROW_TILE), :]
                y_ref[:, ls] = y


def _combine(x1_parts, gw, dest, yb, tm):
    specs, bounds, steps = _part_specs(x1_parts, (tm, D_MODEL))
    return pl.pallas_call(
        functools.partial(_combine_kernel, bounds=bounds),
        grid=(steps,),
        in_specs=[pl.BlockSpec((tm * TOP_K,), lambda i: (i,), memory_space=pltpu.SMEM),
                  pl.BlockSpec((tm * TOP_K,), lambda i: (jnp.minimum(i + 1, steps - 1),), memory_space=pltpu.SMEM),
                  pl.BlockSpec((tm, TOP_K), lambda i: (i, 0)),
                  pl.BlockSpec(memory_space=pl.ANY)] + specs,
        out_specs=specs,
        out_shape=[jax.ShapeDtypeStruct(a.shape, F32) for a in x1_parts],
        scratch_shapes=[pltpu.VMEM((2, TOP_K, tm * ROW_TILE, LANES), F32), pltpu.SemaphoreType.DMA((2,))],
        compiler_params=_params(("arbitrary",), 32),
        name="combine",
    )(dest, dest, gw, yb, *x1_parts)


def _rope_tables(positions):
    half = ROPE_DIM // 2
    inv = jnp.power(jnp.float32(ROPE_THETA), -jnp.arange(half, dtype=F32) / half)
    ang = positions.astype(F32)[:, None] * inv[None, :]
    cos, sin = jnp.cos(ang), jnp.sin(ang)
    t = positions.shape[0]
    z8 = jnp.zeros((t, half), F32)
    rest0 = jnp.zeros((t, HEAD_DIM_A - ROPE_DIM), F32)
    a = jnp.concatenate([cos, cos, jnp.ones((t, HEAD_DIM_A - ROPE_DIM), F32)], axis=1)
    bp = jnp.concatenate([z8, sin, rest0], axis=1)
    bm = jnp.concatenate([-sin, z8, rest0], axis=1)
    return tuple(jnp.tile(m, (1, LANES // HEAD_DIM_A)) for m in (a, bp, bm))


def _segment_mean_matrix(seg):
    i = jnp.arange(256)
    return jnp.where((i[:, None] // seg) == (i[None, :] // seg), 1.0 / seg, 0.0).astype(BF16)


def _layer_weights(l, norm1_g, w_in, b_gates, qn_a, kn_a, conv_w, conv_b, cn_g, cn_b, qn_m,
                   w_proj_a, w_proj_b, w_proj_m, w_out, norm2_g, w_router, b_router,
                   w_gate, b_gate, w_up, b_up, w_down, b_down):
    wr = jnp.pad(w_router[l], ((0, 0), (0, LANES - N_EXPERTS)))
    wrh = wr.astype(BF16)
    return dict(
        g1=norm1_g[l][None, :],
        w1=w_in[l][:, :W1_COLS].astype(BF16),
        wgates=w_in[l][:, W1_COLS:].astype(BF16),
        bgates=b_gates[l][None, :],
        qg=jnp.tile(qn_a[l], 256 // HEAD_DIM_A)[None, :],
        kg=jnp.tile(kn_a[l], 256 // HEAD_DIM_A)[None, :],
        mg=jnp.tile(qn_m[l], 256 // MEM_HEAD_DIM)[None, :],
        conv_w=conv_w[l], conv_b=conv_b[l][None, :], cn_g=cn_g[l][None, :], cn_b=cn_b[l][None, :],
        wa=w_proj_a[l].astype(BF16), wb=w_proj_b[l].astype(BF16), wm=w_proj_m[l].astype(BF16),
        wo=w_out[l].astype(BF16), g2=norm2_g[l][None, :],
        wrh=wrh, wrc=jnp.concatenate([wrh, (wr - wrh.astype(F32)).astype(BF16)], axis=1),
        br=jnp.pad(b_router[l], (0, LANES - N_EXPERTS))[None, :],
        w_gate=w_gate[l], b_gate=b_gate[l], w_up=w_up[l], b_up=b_up[l], w_down=w_down[l], b_down=b_down[l],
        seg64=_segment_mean_matrix(HEAD_DIM_A), seg128=_segment_mean_matrix(MEM_HEAD_DIM),
    )


def _merge_group(x2d, os_, lses, ob, om, w, tm, conv_seq_len=None):
    conv = None if conv_seq_len is None else (w["conv_w"], w["conv_b"], w["cn_g"], w["cn_b"], conv_seq_len)
    return _merge(x2d, os_, lses, ob, om, w["g1"], w["wgates"], w["bgates"], w["wa"], w["wb"], w["wm"],
                  w["wo"], w["g2"], w["wrh"], w["wrc"], w["br"], tm, conv)


MOE_ROWS = 512
ROUTE_TILE = 1024
DISPATCH_TILE = 256
COMBINE_TILE = 128


def _moe_groups(merged, w):
    x1s, h2ts, idxs, gws = zip(*merged)
    idx = jnp.concatenate(idxs, axis=0)
    gw = jnp.concatenate(gws, axis=0)
    n = idx.shape[0]
    n_blocks = -(-(n * TOP_K) // MOE_ROWS) + N_EXPERTS
    dest, bexp, seg = _route(idx, MOE_ROWS, n_blocks, ROUTE_TILE)
    xs = _dispatch(h2ts, dest, seg, n_blocks * MOE_ROWS, DISPATCH_TILE, MOE_ROWS)
    yb = _moe(xs, bexp, seg, w["w_gate"], w["b_gate"], w["w_up"], w["b_up"], w["w_down"], w["b_down"], MOE_ROWS)
    return _combine(x1s, gw, dest, yb, COMBINE_TILE)


def _prompt_layer(x, mem, w, mem_norm_g, w_mem_kv, kn_m):
    b, s, _ = x.shape
    n = b * s
    x2d = x.reshape(n, D_MODEL)
    tabs = _rope_tables(jnp.arange(s, dtype=I32))
    q, k, v, u, qm = _in_proj(x2d, w["g1"], w["w1"], w["qg"], w["kg"], w["mg"], tabs, w["seg64"], w["seg128"], 512, F32)
    q3, k3, v3 = (t.reshape(b, s, WIDTH_A) for t in (q, k, v))
    os_, lses, caches = [], [], []
    for g, (win, dil) in enumerate(SWA_GROUPS):
        o, lse = _swa_prompt(q3, k3, v3, g, dil)
        os_.append(o)
        lses.append(lse)
        cs = slice(g * GROUP_W, (g + 1) * GROUP_W)
        caches += [k3[:, s - win:, cs].reshape(b, win, 4, HEAD_DIM_A), v3[:, s - win:, cs].reshape(b, win, 4, HEAD_DIM_A)]
    u3 = u.reshape(b, s, CONV_CH)
    mk, mv = _mem_kv(mem, mem_norm_g[None, :], w_mem_kv.astype(BF16), kn_m[None, :])
    om = _mem_attn(qm.reshape(b, s, MEM_WIDTH), mk, mv, 512)
    merged = _merge_group(x2d, os_, lses, u, om, w, 512, conv_seq_len=s)
    state = caches + [u3[:, s - (CONV_WIDTH - 1):], mk.reshape(b, MEM_LEN, MEM_HEADS, MEM_HEAD_DIM),
                      mv.reshape(b, MEM_LEN, MEM_HEADS, MEM_HEAD_DIM)]
    return merged, state


def _sample_layer(x, mem_k, mem_v, bufs, conv_state, w):
    b, t, _ = x.shape
    n = b * t
    x2d = x.reshape(n, D_MODEL)
    tabs = _rope_tables(jnp.tile(PAST_LEN + jnp.arange(t, dtype=I32), b))
    q, k, v, u, qm = _in_proj(x2d, w["g1"], w["w1"], w["qg"], w["kg"], w["mg"], tabs, w["seg64"], w["seg128"], 256, BF16)
    q3, k3, v3 = (a.reshape(b, t, WIDTH_A) for a in (q, k, v))
    os_, lses, caches = [], [], []
    for g, (win, dil) in enumerate(SWA_GROUPS):
        to_cm = lambda a: jnp.transpose(a, (0, 2, 3, 1)).reshape(b, GROUP_W, win)
        from_cm = lambda a: jnp.transpose(a.reshape(b, 4, HEAD_DIM_A, win), (0, 3, 1, 2))
        o, lse, kto, vto = _swa_sample(q3, k3, v3, to_cm(bufs[2 * g]), to_cm(bufs[2 * g + 1]), g, win, dil)
        os_.append(o)
        lses.append(lse)
        caches += [from_cm(kto), from_cm(vto)]
    u3 = u.reshape(b, t, CONV_CH)
    halo = jnp.pad(conv_state, ((0, 0), (CONV_HALO - (CONV_WIDTH - 1), 0), (0, 0)))
    ob = _conv_branch(u3, halo, w["conv_w"], w["conv_b"], w["cn_g"], w["cn_b"], t, False)
    om = _mem_attn_rows(qm.reshape(b, t, MEM_WIDTH), mem_k.reshape(b, MEM_LEN * MEM_HEADS, MEM_HEAD_DIM),
                        mem_v.reshape(b, MEM_LEN * MEM_HEADS, MEM_HEAD_DIM))
    merged = _merge_group(x2d, os_, lses, ob, om, w, 256)
    new_conv = jnp.concatenate([conv_state, u3], axis=1)[:, t:]
    return merged, caches + [new_conv]


def kernel(x_prompt, x_sample, mem_prompt, cache_swa_k_w128, cache_swa_v_w128, cache_swa_k_w512, cache_swa_v_w512, cache_swa_k_w2048, cache_swa_v_w2048, state_conv, cache_mem_k, cache_mem_v, norm1_g, w_in, b_gates, qn_a, kn_a, conv_w, conv_b, cn_g, cn_b, mem_norm_g, w_mem_kv, qn_m, kn_m, w_proj_a, w_proj_b, w_proj_m, w_out, norm2_g, w_router, b_router, w_gate, b_gate, w_up, b_up, w_down, b_down):
    depth = norm1_g.shape[0]
    y_p, y_s = x_prompt, x_sample
    st_p, st_s = [], []
    for l in range(depth):
        w = _layer_weights(l, norm1_g, w_in, b_gates, qn_a, kn_a, conv_w, conv_b, cn_g, cn_b, qn_m,
                           w_proj_a, w_proj_b, w_proj_m, w_out, norm2_g, w_router, b_router,
                           w_gate, b_gate, w_up, b_up, w_down, b_down)
        merged_p, state_p = _prompt_layer(y_p, mem_prompt, w, mem_norm_g[l], w_mem_kv[l], kn_m[l])
        bufs = (cache_swa_k_w128[l], cache_swa_v_w128[l], cache_swa_k_w512[l], cache_swa_v_w512[l],
                cache_swa_k_w2048[l], cache_swa_v_w2048[l])
        merged_s, state_s = _sample_layer(y_s, cache_mem_k[l], cache_mem_v[l], bufs, state_conv[l], w)
        y_p2d, y_s2d = _moe_groups((merged_p, merged_s), w)
        y_p, y_s = y_p2d.reshape(y_p.shape), y_s2d.reshape(y_s.shape)
        st_p.append(state_p)
        st_s.append(state_s)
    outs_p = [jnp.stack(a) for a in zip(*st_p)]
    outs_s = [jnp.stack(a) for a in zip(*st_s)]
    return (y_p, y_s, *outs_p, *outs_s)
```

```python
import functools

import jax
import jax.numpy as jnp
from jax import lax
from jax.experimental import pallas as pl
from jax.experimental.pallas import tpu as pltpu

F32 = jnp.float32
BF16 = jnp.bfloat16
I32 = jnp.int32

D_MODEL = 1024
HEAD_DIM_A = 64
GROUP_W = 256
N_GROUPS = 3
WIDTH_A = N_GROUPS * GROUP_W
SWA_GROUPS = ((128, 1), (512, 4), (2048, 16))
SWA_BLOCK = 128
ROPE_DIM = 16
ROPE_THETA = 500000.0
CONV_CH = 512
CONV_WIDTH = 31
CONV_HALO = 32
MEM_LEN = 256
MEM_HEADS = 4
MEM_HEAD_DIM = 128
MEM_WIDTH = 512
N_EXPERTS = 32
TOP_K = 4
D_FF = 1024
SWIGLU_LIMIT = 7.0
SWIGLU_ALPHA = 1.702
EPS = 1e-6
PAST_LEN = 8192
LANES = 128
ROW_TILE = D_MODEL // LANES
W1_COLS = 3 * WIDTH_A + 2 * CONV_CH + MEM_WIDTH
NEG_INF = float("-inf")
MIB = 2 ** 20


def _params(semantics, vmem_mib):
    return pltpu.CompilerParams(dimension_semantics=semantics, vmem_limit_bytes=vmem_mib * MIB)


def _rms(x, gain):
    return x * lax.rsqrt(jnp.mean(x * x, axis=-1, keepdims=True) + EPS) * gain


def _dot(a, b):
    return jnp.dot(a, b, preferred_element_type=F32)


def _dot_nt(a, b):
    return lax.dot_general(a, b, (((1,), (1,)), ((), ())), preferred_element_type=F32)


def _in_proj_kernel(x_ref, g1_ref, w_ref, qg_ref, kg_ref, mg_ref, ra_ref, rp_ref, rm_ref, s64_ref, s128_ref,
                    q_ref, k_ref, v_ref, u_ref, qm_ref):
    hb = _rms(x_ref[...], g1_ref[...]).astype(BF16)
    ra, rp, rm = ra_ref[...], rp_ref[...], rm_ref[...]

    def proj(c0):
        return _dot(hb, w_ref[:, c0:c0 + 256])

    def head_norm(p, seg_ref, gain):
        ms = _dot((p * p).astype(BF16), seg_ref[...])
        return p * lax.rsqrt(ms + EPS) * gain

    def rope(p):
        outs = []
        for j in range(2):
            pj = p[:, j * LANES:(j + 1) * LANES]
            outs.append(pj * ra + pltpu.roll(pj, 8, 1) * rp + pltpu.roll(pj, LANES - 8, 1) * rm)
        return jnp.concatenate(outs, axis=1)

    for c in range(3):
        cs = slice(c * 256, (c + 1) * 256)
        q_ref[:, cs] = rope(head_norm(proj(c * 256), s64_ref, qg_ref[...])).astype(q_ref.dtype)
        k_ref[:, cs] = rope(head_norm(proj(WIDTH_A + c * 256), s64_ref, kg_ref[...]))
        v_ref[:, cs] = proj(2 * WIDTH_A + c * 256)
    for c in range(2):
        cs = slice(c * 256, (c + 1) * 256)
        a = proj(3 * WIDTH_A + c * 256)
        gate = proj(3 * WIDTH_A + CONV_CH + c * 256)
        u_ref[:, cs] = a * jax.nn.sigmoid(gate)
        qm_ref[:, cs] = head_norm(proj(3 * WIDTH_A + 2 * CONV_CH + c * 256), s128_ref, mg_ref[...]).astype(BF16)


def _in_proj(x, g1, w1, qg, kg, mg, tabs, seg64, seg128, tm, q_dtype):
    n = x.shape[0]
    t_rows = tabs[0].shape[0]
    assert n % tm == 0 and t_rows % tm == 0
    t_blocks = t_rows // tm
    row = lambda i: (i, 0)
    const = lambda i: (0, 0)
    tab = lambda i: (i % t_blocks, 0)
    return pl.pallas_call(
        _in_proj_kernel,
        grid=(n // tm,),
        in_specs=[
            pl.BlockSpec((tm, D_MODEL), row),
            pl.BlockSpec((1, D_MODEL), const),
            pl.BlockSpec((D_MODEL, W1_COLS), const),
            pl.BlockSpec((1, 256), const), pl.BlockSpec((1, 256), const), pl.BlockSpec((1, 256), const),
            pl.BlockSpec((tm, LANES), tab), pl.BlockSpec((tm, LANES), tab), pl.BlockSpec((tm, LANES), tab),
            pl.BlockSpec((256, 256), const), pl.BlockSpec((256, 256), const),
        ],
        out_specs=[
            pl.BlockSpec((tm, WIDTH_A), row), pl.BlockSpec((tm, WIDTH_A), row), pl.BlockSpec((tm, WIDTH_A), row),
            pl.BlockSpec((tm, CONV_CH), row), pl.BlockSpec((tm, MEM_WIDTH), row),
        ],
        out_shape=[
            jax.ShapeDtypeStruct((n, WIDTH_A), q_dtype), jax.ShapeDtypeStruct((n, WIDTH_A), F32),
            jax.ShapeDtypeStruct((n, WIDTH_A), F32), jax.ShapeDtypeStruct((n, CONV_CH), F32),
            jax.ShapeDtypeStruct((n, MEM_WIDTH), BF16),
        ],
        compiler_params=_params(("arbitrary",), 48),
        name="in_proj",
    )(x, g1, w1, qg, kg, mg, *tabs, seg64, seg128)


def _swa_prompt_kernel(q_ref, k_ref, v_ref, o_ref, lse_ref, q_s, k_s, v_s, o_s, l_s, *, dil, nsub):
    n = pl.program_id(1)
    span = SWA_BLOCK * dil
    cur = n % 2
    prv = 1 - cur
    for s in range(2):
        ls = slice(s * LANES, (s + 1) * LANES)
        q_s[s] = q_ref[:, ls]
        k_s[cur, s] = k_ref[:, ls]
        v_s[cur, s] = v_ref[:, ls]

    @pl.when(n == 0)
    def _():
        k_s[prv] = jnp.zeros(k_s.shape[1:], F32)
        v_s[prv] = jnp.zeros(v_s.shape[1:], F32)

    shape = (SWA_BLOCK, 2 * SWA_BLOCK)
    col = lax.broadcasted_iota(I32, shape, 1)
    off = lax.broadcasted_iota(I32, shape, 0) + SWA_BLOCK - col
    band = (off >= 0) & (off <= SWA_BLOCK)
    band_first = band & ((n > 0) | (col >= SWA_BLOCK))
    head = col // HEAD_DIM_A

    hms = [head == h for h in range(4)]

    def both(ref, lead, rows):
        return jnp.concatenate([ref[(*lead, s, rows, slice(None))] for s in range(2)], axis=1)

    def blocks(items):
        rows_of, masks, qs, kks, vvs = [], [], [], [], []
        for j, r in items:
            rows = pl.ds(j * span + r, SWA_BLOCK, stride=dil)
            before = (prv, pl.ds((nsub - 1) * span + r, SWA_BLOCK, stride=dil)) if j == 0 else \
                     (cur, pl.ds((j - 1) * span + r, SWA_BLOCK, stride=dil))
            rows_of.append(rows)
            masks.append(band_first if j == 0 else band)
            qs.append(both(q_s, (), rows).astype(BF16))
            kks.append(jnp.concatenate([both(k_s, before[:1], before[1]), both(k_s, (cur,), rows)],
                                       axis=0).astype(BF16))
            vvs.append(jnp.concatenate([both(v_s, before[:1], before[1]), both(v_s, (cur,), rows)],
                                       axis=0).astype(BF16))
        pairs = [(b, h) for b in range(len(items)) for h in range(4)]
        ss = [jnp.where(masks[b], _dot_nt(jnp.where(hms[h], qs[b], jnp.zeros_like(qs[b])), kks[b])
                        * (HEAD_DIM_A ** -0.5), NEG_INF) for b, h in pairs]
        ms = [jnp.max(s, axis=1, keepdims=True) for s in ss]
        ps = [jnp.exp(s - m) for s, m in zip(ss, ms)]
        sums = [jnp.sum(p, axis=1, keepdims=True) for p in ps]
        ohs = [_dot(p.astype(BF16), vvs[b]) for p, (b, h) in zip(ps, pairs)]
        for b in range(len(items)):
            o_acc = jnp.zeros(shape, F32)
            lse_acc = jnp.zeros(shape, F32)
            for h in range(4):
                i = 4 * b + h
                o_acc = jnp.where(hms[h], ohs[i] / sums[i], o_acc)
                lse_acc = jnp.where(hms[h], ms[i] + jnp.log(sums[i]), lse_acc)
            for s in range(2):
                ls = slice(s * LANES, (s + 1) * LANES)
                o_s[s, rows_of[b], :] = o_acc[:, ls]
                l_s[s, rows_of[b], :] = lse_acc[:, ls]

    if nsub == 1:
        def trip(i, carry):
            blocks([(0, 2 * i), (0, 2 * i + 1)])
            return carry

        lax.fori_loop(0, dil // 2, trip, 0)
    else:
        def trip(r, carry):
            for j0 in range(0, nsub, 2):
                blocks([(j0, r), (j0 + 1, r)])
            return carry

        lax.fori_loop(0, dil, trip, 0)
    for s in range(2):
        ls = slice(s * LANES, (s + 1) * LANES)
        o_ref[:, ls] = o_s[s]
        lse_ref[:, ls] = l_s[s]


def _swa_prompt(q, k, v, g, dil):
    b, s, _ = q.shape
    nsub = max(1, 4 // dil)
    t = nsub * dil * SWA_BLOCK
    assert s % t == 0
    inp = pl.BlockSpec((None, t, GROUP_W), lambda bi, n: (bi, n, g))
    out = pl.BlockSpec((None, t, GROUP_W), lambda bi, n: (bi, n, 0))
    slab = lambda *lead: pltpu.VMEM((*lead, 2, t, LANES), F32)
    o, lse = pl.pallas_call(
        functools.partial(_swa_prompt_kernel, dil=dil, nsub=nsub),
        grid=(b, s // t),
        in_specs=[inp, inp, inp],
        out_specs=[out, out],
        out_shape=[jax.ShapeDtypeStruct((b, s, GROUP_W), F32)] * 2,
        scratch_shapes=[slab(), slab(2), slab(2), slab(), slab()],
        compiler_params=_params(("arbitrary", "arbitrary"), 48),
        name=f"swa_prompt_d{dil}",
    )(q, k, v)
    return o.reshape(b * s, GROUP_W), lse.reshape(b * s, GROUP_W)


def _swa_sample_kernel(*refs, win, dil):
    def one(i, carry):
        _swa_sample_one(*(r.at[i] for r in refs), win=win, dil=dil)
        return carry

    lax.fori_loop(0, refs[0].shape[0], one, 0)


def _swa_sample_one(q_ref, kn_ref, vn_ref, kt_ref, vt_ref, o_ref, lse_ref, kto_ref, vto_ref, *, win, dil):
    t_new = q_ref.shape[0]
    n_tiles = win // LANES
    pad = jnp.zeros((LANES - t_new, GROUP_W), F32)
    kn = jnp.concatenate([kn_ref[...], pad], axis=0)
    vn = jnp.concatenate([vn_ref[...], pad], axis=0)

    def channel_major(a):
        return jnp.concatenate([a[:, :LANES].T, a[:, LANES:].T], axis=0)

    knt, vnt = channel_major(kn), channel_major(vn)
    lane = lax.broadcasted_iota(I32, (GROUP_W, LANES), 1)
    keep = lane < LANES - t_new
    for src_ref, new_t, dst_ref in ((kt_ref, knt, kto_ref), (vt_ref, vnt, vto_ref)):
        nxt = pltpu.roll(src_ref[:, 0:LANES], LANES - t_new, 1)
        for j in range(n_tiles):
            this = nxt
            following = new_t if j + 1 == n_tiles else src_ref[:, (j + 1) * LANES:(j + 2) * LANES]
            nxt = pltpu.roll(following, LANES - t_new, 1)
            dst_ref[:, j * LANES:(j + 1) * LANES] = jnp.where(keep, this, nxt)

    qf = q_ref[...].astype(F32)
    ch_head = lax.broadcasted_iota(I32, (t_new, GROUP_W), 1) // HEAD_DIM_A
    qm = jnp.concatenate([jnp.where(ch_head == h, qf, 0.0) for h in range(4)], axis=0).astype(BF16)
    nq = 4 * t_new
    scale = HEAD_DIM_A ** -0.5
    s_c = _dot(qm, kt_ref[...].astype(BF16)) * scale
    s_n = _dot(qm, knt.astype(BF16)) * scale
    t_c = lax.broadcasted_iota(I32, (nq, win), 0) & (t_new - 1)
    d_c = win + t_c - lax.broadcasted_iota(I32, (nq, win), 1)
    s_c = jnp.where((d_c <= win) & ((d_c & (dil - 1)) == 0), s_c, NEG_INF)
    t_n = lax.broadcasted_iota(I32, (nq, LANES), 0) & (t_new - 1)
    d_n = t_n - lax.broadcasted_iota(I32, (nq, LANES), 1)
    s_n = jnp.where((d_n >= 0) & ((d_n & (dil - 1)) == 0), s_n, NEG_INF)
    m = jnp.maximum(jnp.max(s_c, axis=1, keepdims=True), jnp.max(s_n, axis=1, keepdims=True))
    p_c = jnp.exp(s_c - m)
    p_n = jnp.exp(s_n - m)
    l = jnp.sum(p_c, axis=1, keepdims=True) + jnp.sum(p_n, axis=1, keepdims=True)
    o_all = (_dot_nt(p_c.astype(BF16), vt_ref[...].astype(BF16)) + _dot(p_n.astype(BF16), vn.astype(BF16))) / l
    lse_all = m + jnp.log(l)
    o_acc = jnp.zeros((t_new, GROUP_W), F32)
    lse_acc = jnp.zeros((t_new, GROUP_W), F32)
    for h in range(4):
        hm = ch_head == h
        o_acc = jnp.where(hm, o_all[h * t_new:(h + 1) * t_new, :], o_acc)
        lse_acc = jnp.where(hm, lse_all[h * t_new:(h + 1) * t_new, :], lse_acc)
    o_ref[...] = o_acc
    lse_ref[...] = lse_acc


def _swa_sample(q, k_new, v_new, kt_buf, vt_buf, g, win, dil):
    b, t, _ = q.shape
    assert t == 8 and win % LANES == 0 and dil & (dil - 1) == 0
    bb = max(1, 4096 // win)
    assert b % bb == 0
    new = pl.BlockSpec((bb, t, GROUP_W), lambda bi: (bi, 0, g))
    buf = pl.BlockSpec((bb, GROUP_W, win), lambda bi: (bi, 0, 0))
    out = pl.BlockSpec((bb, t, GROUP_W), lambda bi: (bi, 0, 0))
    o, lse, kto, vto = pl.pallas_call(
        functools.partial(_swa_sample_kernel, win=win, dil=dil),
        grid=(b // bb,),
        in_specs=[new, new, new, buf, buf],
        out_specs=[out, out, buf, buf],
        out_shape=[jax.ShapeDtypeStruct((b, t, GROUP_W), F32)] * 2
        + [jax.ShapeDtypeStruct((b, GROUP_W, win), F32)] * 2,
        compiler_params=_params(("arbitrary",), 48),
        name=f"swa_sample_w{win}",
    )(q, k_new, v_new, kt_buf, vt_buf)
    return o.reshape(b * t, GROUP_W), lse.reshape(b * t, GROUP_W), kto, vto


def _conv_kernel(halo_ref, cur_ref, w_ref, b_ref, g_ref, beta_ref, o_ref, ctx_ref, *, zero_first_halo, chunk):
    halo = halo_ref[...]
    if zero_first_halo:
        halo = jnp.where(pl.program_id(1) == 0, 0.0, halo)
    _conv_module(halo, cur_ref, w_ref, b_ref, g_ref, beta_ref, o_ref, ctx_ref, chunk)


def _conv_module(halo, cur_ref, w_ref, b_ref, g_ref, beta_ref, o_ref, ctx_ref, chunk):
    for rows in _conv_module_parts(halo, cur_ref, w_ref, b_ref, g_ref, beta_ref, o_ref, ctx_ref, chunk):
        rows()


def _conv_module_parts(halo, cur_ref, w_ref, b_ref, g_ref, beta_ref, o_ref, ctx_ref, chunk):
    tm = cur_ref.shape[0]
    ctx_ref[pl.ds(0, CONV_HALO), :] = halo
    ctx_ref[pl.ds(CONV_HALO, tm), :] = cur_ref[...]
    first = CONV_HALO - (CONV_WIDTH - 1)

    def rows_from(c0):
        acc = jnp.zeros((chunk, CONV_CH), F32) + b_ref[...]
        for phase in range(8):
            taps = range(phase, CONV_WIDTH, 8)
            start, shift = divmod(first + phase, 8)
            need = chunk + 8 * (len(taps) - 1)
            if shift == 0:
                window = ctx_ref[pl.ds(c0 + 8 * start, need), :]
            else:
                rows = ctx_ref[pl.ds(c0 + 8 * start, need + 8), :]
                window = pltpu.roll(rows, need + 8 - shift, 0)[0:need, :]
            for a, w in enumerate(taps):
                acc = acc + window[8 * a:8 * a + chunk, :] * w_ref[pl.ds(w, 1), :]
        mu = jnp.mean(acc, axis=-1, keepdims=True)
        xc = acc - mu
        var = jnp.mean(xc * xc, axis=-1, keepdims=True)
        y = xc * lax.rsqrt(var + EPS) * g_ref[...] + beta_ref[...]
        o_ref[pl.ds(c0, chunk), :] = (y * jax.nn.sigmoid(y)).astype(BF16)

    return [functools.partial(rows_from, c0) for c0 in range(0, tm, chunk)]


def _conv_branch(u, halo, conv_w, conv_b, cn_g, cn_b, tm, halo_from_u):
    b, t, _ = u.shape
    assert t % tm == 0
    ratio = tm // CONV_HALO if halo_from_u else 0
    halo_map = (lambda bi, i: (bi, jnp.maximum(i * ratio - 1, 0), 0)) if halo_from_u else (lambda bi, i: (bi, 0, 0))
    const = lambda bi, i: (0, 0)
    return pl.pallas_call(
        functools.partial(_conv_kernel, zero_first_halo=halo_from_u, chunk=min(tm, 64)),
        grid=(b, t // tm),
        in_specs=[
            pl.BlockSpec((None, CONV_HALO, CONV_CH), halo_map),
            pl.BlockSpec((None, tm, CONV_CH), lambda bi, i: (bi, i, 0)),
            pl.BlockSpec((CONV_WIDTH, CONV_CH), const),
            pl.BlockSpec((1, CONV_CH), const), pl.BlockSpec((1, CONV_CH), const), pl.BlockSpec((1, CONV_CH), const),
        ],
        out_specs=pl.BlockSpec((None, tm, CONV_CH), lambda bi, i: (bi, i, 0)),
        out_shape=jax.ShapeDtypeStruct((b, t, CONV_CH), BF16),
        scratch_shapes=[pltpu.VMEM((CONV_HALO + tm, CONV_CH), F32)],
        compiler_params=_params(("arbitrary", "arbitrary"), 32),
        name="conv_module",
    )(halo, u, conv_w, conv_b, cn_g, cn_b).reshape(b * t, CONV_CH)


def _mem_kv_kernel(mem_ref, g_ref, w_ref, kg_ref, k_ref, v_ref):
    hb = _rms(mem_ref[...], g_ref[...]).astype(BF16)
    for h in range(MEM_HEADS):
        cs = slice(h * MEM_HEAD_DIM, (h + 1) * MEM_HEAD_DIM)
        k_ref[:, cs] = _rms(_dot(hb, w_ref[:, cs]), kg_ref[...])
        v_ref[:, cs] = _dot(hb, w_ref[:, MEM_WIDTH + h * MEM_HEAD_DIM:MEM_WIDTH + (h + 1) * MEM_HEAD_DIM])


def _mem_kv(mem, g, w, kg):
    b = mem.shape[0]
    const = lambda bi: (0, 0)
    blk = pl.BlockSpec((None, MEM_LEN, MEM_WIDTH), lambda bi: (bi, 0, 0))
    return pl.pallas_call(
        _mem_kv_kernel,
        grid=(b,),
        in_specs=[pl.BlockSpec((None, MEM_LEN, D_MODEL), lambda bi: (bi, 0, 0)), pl.BlockSpec((1, D_MODEL), const),
                  pl.BlockSpec((D_MODEL, 2 * MEM_WIDTH), const), pl.BlockSpec((1, MEM_HEAD_DIM), const)],
        out_specs=[blk, blk],
        out_shape=[jax.ShapeDtypeStruct((b, MEM_LEN, MEM_WIDTH), F32)] * 2,
        compiler_params=_params(("arbitrary",), 32),
        name="mem_kv",
    )(mem, g, w, kg)


def _mem_attn_kernel(q_ref, k_ref, v_ref, o_ref):
    for h in range(MEM_HEADS):
        cs = slice(h * MEM_HEAD_DIM, (h + 1) * MEM_HEAD_DIM)
        s = _dot_nt(q_ref[:, cs], k_ref[:, cs].astype(BF16)) * (MEM_HEAD_DIM ** -0.5)
        m = jnp.max(s, axis=1, keepdims=True)
        p = jnp.exp(s - m)
        l = jnp.sum(p, axis=1, keepdims=True)
        o_ref[:, cs] = (_dot(p.astype(BF16), v_ref[:, cs].astype(BF16)) / l).astype(BF16)


def _mem_attn(q, k, v, tm):
    b, t, _ = q.shape
    assert t % tm == 0
    kv = pl.BlockSpec((None, MEM_LEN, MEM_WIDTH), lambda bi, i: (bi, 0, 0))
    qo = pl.BlockSpec((None, tm, MEM_WIDTH), lambda bi, i: (bi, i, 0))
    return pl.pallas_call(
        _mem_attn_kernel,
        grid=(b, t // tm),
        in_specs=[qo, kv, kv],
        out_specs=qo,
        out_shape=jax.ShapeDtypeStruct((b, t, MEM_WIDTH), BF16),
        compiler_params=_params(("arbitrary", "arbitrary"), 32),
        name="mem_attn",
    )(q, k, v).reshape(b * t, MEM_WIDTH)


def _mem_attn_rows_kernel(*refs):
    def one(i, carry):
        _mem_attn_rows_one(*(r.at[i] for r in refs))
        return carry

    lax.fori_loop(0, refs[0].shape[0], one, 0)


def _mem_attn_rows_one(q_ref, k_ref, v_ref, o_ref):
    t = q_ref.shape[0]
    qf = q_ref[...].astype(F32)
    qs = jnp.concatenate([qf[:, h * MEM_HEAD_DIM:(h + 1) * MEM_HEAD_DIM] for h in range(MEM_HEADS)], axis=0)
    s = _dot_nt(qs.astype(BF16), k_ref[...].astype(BF16)) * (MEM_HEAD_DIM ** -0.5)
    shape = s.shape
    same_head = (lax.broadcasted_iota(I32, shape, 0) // t) == (lax.broadcasted_iota(I32, shape, 1) & (MEM_HEADS - 1))
    s = jnp.where(same_head, s, NEG_INF)
    m = jnp.max(s, axis=1, keepdims=True)
    p = jnp.exp(s - m)
    l = jnp.sum(p, axis=1, keepdims=True)
    o = _dot(p.astype(BF16), v_ref[...].astype(BF16)) / l
    o_ref[...] = jnp.concatenate([o[h * t:(h + 1) * t, :] for h in range(MEM_HEADS)], axis=1).astype(BF16)


def _mem_attn_rows(q, k_rows, v_rows):
    b, t, _ = q.shape
    bb = 4
    assert b % bb == 0
    kv = pl.BlockSpec((bb, MEM_LEN * MEM_HEADS, MEM_HEAD_DIM), lambda bi: (bi, 0, 0))
    qo = pl.BlockSpec((bb, t, MEM_WIDTH), lambda bi: (bi, 0, 0))
    return pl.pallas_call(
        _mem_attn_rows_kernel,
        grid=(b // bb,),
        in_specs=[qo, kv, kv],
        out_specs=qo,
        out_shape=jax.ShapeDtypeStruct((b, t, MEM_WIDTH), BF16),
        compiler_params=_params(("arbitrary",), 32),
        name="mem_attn_rows",
    )(q, k_rows, v_rows).reshape(b * t, MEM_WIDTH)


def _merge_kernel(x_ref, o1_ref, o2_ref, o3_ref, l1_ref, l2_ref, l3_ref, ob_ref, om_ref, *rest):
    _merge_body((), x_ref, o1_ref, o2_ref, o3_ref, l1_ref, l2_ref, l3_ref, ob_ref, om_ref, *rest)


def _merge_conv_kernel(x_ref, o1_ref, o2_ref, o3_ref, l1_ref, l2_ref, l3_ref, halo_ref, u_ref, om_ref,
                       g1_ref, wg_ref, bg_ref, wa_ref, wb_ref, wm_ref, wo_ref, g2_ref, wrh_ref, wrc_ref, br_ref,
                       cw_ref, cb_ref, cg_ref, cbeta_ref, x1_ref, h2_ref, idx_ref, gw_ref,
                       mix_ref, ctx_ref, ob_ref, *, tiles_per_seq):
    halo = jnp.where(pl.program_id(0) % tiles_per_seq == 0, 0.0, halo_ref[...])
    conv_parts = _conv_module_parts(halo, u_ref, cw_ref, cb_ref, cg_ref, cbeta_ref, ob_ref, ctx_ref, 64)
    _merge_body(conv_parts, x_ref, o1_ref, o2_ref, o3_ref, l1_ref, l2_ref, l3_ref, ob_ref, om_ref,
                g1_ref, wg_ref, bg_ref, wa_ref, wb_ref, wm_ref, wo_ref, g2_ref, wrh_ref, wrc_ref, br_ref,
                x1_ref, h2_ref, idx_ref, gw_ref, mix_ref)


def _merge_body(conv_parts, x_ref, o1_ref, o2_ref, o3_ref, l1_ref, l2_ref, l3_ref, ob_ref, om_ref,
                g1_ref, wg_ref, bg_ref, wa_ref, wb_ref, wm_ref, wo_ref, g2_ref, wrh_ref, wrc_ref, br_ref,
                x1_ref, h2_ref, idx_ref, gw_ref, mix_ref):
    tm = x_ref.shape[0]
    x = x_ref[...]
    hb = _rms(x, g1_ref[...]).astype(BF16)
    la, lb, lc = l1_ref[...], l2_ref[...], l3_ref[...]
    m = jnp.maximum(la, jnp.maximum(lb, lc))
    ea, eb, ec = jnp.exp(la - m), jnp.exp(lb - m), jnp.exp(lc - m)
    oa = ((ea * o1_ref[...] + eb * o2_ref[...] + ec * o3_ref[...]) / (ea + eb + ec)).astype(BF16)
    om = om_ref[...]
    chunks = [slice(j * 256, (j + 1) * 256) for j in range(4)]

    def gate(branch, j):
        c0 = branch * D_MODEL + j * 256
        return jax.nn.sigmoid(_dot(hb, wg_ref[:, c0:c0 + 256]) + bg_ref[:, c0:c0 + 256])

    per_j = -(-len(conv_parts) // 4)
    partial, gate_b = [], []
    for j, cs in enumerate(chunks):
        for part in conv_parts[j * per_j:(j + 1) * per_j]:
            part()
        partial.append(gate(0, j) * _dot(oa, wa_ref[:, cs]) + gate(2, j) * _dot(om, wm_ref[:, cs]))
        gate_b.append(gate(1, j))
    ob = ob_ref[...]
    for j, cs in enumerate(chunks):
        mix_ref[:, cs] = (partial[j] + gate_b[j] * _dot(ob, wb_ref[:, cs])).astype(BF16)
    x1 = x + _dot(mix_ref[...], wo_ref[...])
    x1_ref[...] = x1
    h2 = _rms(x1, g2_ref[...])
    for j in range(ROW_TILE):
        h2_ref[pl.ds(j, tm, stride=ROW_TILE), :] = h2[:, j * LANES:(j + 1) * LANES]
    hi = h2.astype(BF16)
    lo = (h2 - hi.astype(F32)).astype(BF16)
    both = _dot(hi, wrc_ref[...])
    logits = both[:, :LANES] + both[:, LANES:] + _dot(lo, wrh_ref[...]) + br_ref[...]
    lane = lax.broadcasted_iota(I32, (tm, LANES), 1)
    logits = jnp.where(lane < N_EXPERTS, logits, NEG_INF)
    vals, idxs = [], []
    for _ in range(TOP_K):
        mk = jnp.max(logits, axis=1, keepdims=True)
        ik = jnp.min(jnp.where(logits == mk, lane, LANES), axis=1, keepdims=True)
        vals.append(mk)
        idxs.append(ik)
        logits = jnp.where(lane == ik, NEG_INF, logits)
    es = [jnp.exp(v - vals[0]) for v in vals]
    den = es[0] + es[1] + es[2] + es[3]
    l4 = lax.broadcasted_iota(I32, (tm, TOP_K), 1)
    idx_out = jnp.zeros((tm, TOP_K), I32)
    gw_out = jnp.zeros((tm, TOP_K), F32)
    for k in range(TOP_K):
        idx_out = jnp.where(l4 == k, idxs[k], idx_out)
        gw_out = jnp.where(l4 == k, es[k] / den, gw_out)
    idx_ref[...] = idx_out
    gw_ref[...] = gw_out


def _merge(x, os_, lses, ob, om, g1, wg, bg, wa, wb, wm, wo, g2, wrh, wrc, br, tm, conv=None):
    n = x.shape[0]
    assert n % tm == 0
    row = lambda i: (i, 0)
    const = lambda i: (0, 0)
    rows = lambda w: pl.BlockSpec((tm, w), row)
    full = lambda a: pl.BlockSpec(a.shape, const)
    weights = (g1, wg, bg, wa, wb, wm, wo, g2, wrh, wrc, br)
    scratch = [pltpu.VMEM((tm, D_MODEL), BF16)]
    if conv is None:
        body, b_specs, b_args, extra = _merge_kernel, [rows(CONV_CH)], (ob,), ()
    else:
        *conv_params, seq_len = conv
        assert seq_len % tm == 0 and tm % CONV_HALO == 0
        ratio = tm // CONV_HALO
        body = functools.partial(_merge_conv_kernel, tiles_per_seq=seq_len // tm)
        b_specs = [pl.BlockSpec((CONV_HALO, CONV_CH), lambda i: (jnp.maximum(i * ratio - 1, 0), 0)), rows(CONV_CH)]
        b_args, extra = (ob, ob), tuple(conv_params)
        scratch += [pltpu.VMEM((CONV_HALO + tm, CONV_CH), F32), pltpu.VMEM((tm, CONV_CH), BF16)]
    return pl.pallas_call(
        body,
        grid=(n // tm,),
        in_specs=[rows(D_MODEL)] + [rows(GROUP_W)] * 6 + b_specs + [rows(MEM_WIDTH)]
        + [full(a) for a in weights + extra],
        out_specs=[rows(D_MODEL), pl.BlockSpec((tm * ROW_TILE, LANES), row), rows(TOP_K), rows(TOP_K)],
        out_shape=[jax.ShapeDtypeStruct((n, D_MODEL), F32), jax.ShapeDtypeStruct((n * ROW_TILE, LANES), F32),
                   jax.ShapeDtypeStruct((n, TOP_K), I32), jax.ShapeDtypeStruct((n, TOP_K), F32)],
        scratch_shapes=scratch,
        compiler_params=_params(("arbitrary",), 56),
        name="merge_router",
    )(x, *os_, *lses, *b_args, om, *weights, *extra)


def _active_part(refs, bounds, i):
    value = refs[0][...]
    for ref, (first, _) in zip(refs[1:], bounds[1:]):
        value = jnp.where(i >= first, ref[...], value)
    return value


def _route_kernel(*refs, block_rows, bounds):
    idx_refs, (dest_ref, bexp_ref, seg_ref, cnt_ref, carry_ref) = refs[:len(bounds)], refs[len(bounds):]
    phase, i = pl.program_id(0), pl.program_id(1)
    tq = idx_refs[0].shape[0]
    nb = bexp_ref.shape[0]
    idx = _active_part(idx_refs, bounds, i)
    lane = lax.broadcasted_iota(I32, (tq, LANES), 1)
    member = jnp.zeros((tq, LANES), F32)
    for k in range(TOP_K):
        member = member + (lane == idx[:, k:k + 1]).astype(F32)
    tile_counts = jnp.sum(member, axis=0, keepdims=True)

    @pl.when(phase == 0)
    def _():
        @pl.when(i == 0)
        def _():
            cnt_ref[...] = jnp.zeros_like(cnt_ref)

        cnt_ref[...] += tile_counts
        dest_ref[...] = jnp.zeros_like(dest_ref)
        bexp_ref[...] = jnp.zeros_like(bexp_ref)
        seg_ref[...] = jnp.zeros_like(seg_ref)

    @pl.when(phase == 1)
    def _():
        @pl.when(i == 0)
        def _():
            carry_ref[...] = jnp.zeros_like(carry_ref)

        counts = jnp.broadcast_to(cnt_ref[...], (8, LANES))
        padded = jnp.floor((counts + (block_rows - 1)) * (1.0 / block_rows)) * block_rows
        lane8 = lax.broadcasted_iota(I32, (8, LANES), 1)
        ends = padded
        for s in (1, 2, 4, 8, 16, 32, 64):
            ends = ends + jnp.where(lane8 >= s, pltpu.roll(ends, s, 1), 0.0)
        starts = (ends - padded)[0:1, :]
        r = lax.broadcasted_iota(I32, (tq, tq), 0)
        c = lax.broadcasted_iota(I32, (tq, tq), 1)
        earlier = _dot((c < r).astype(BF16), member.astype(BF16)) + carry_ref[...]
        carry_ref[...] += tile_counts
        base = earlier + starts
        l4 = lax.broadcasted_iota(I32, (tq, TOP_K), 1)
        dest = jnp.zeros((tq, TOP_K), F32)
        for k in range(TOP_K):
            dk = jnp.sum(jnp.where(lane == idx[:, k:k + 1], base, 0.0), axis=1, keepdims=True)
            dest = jnp.where(l4 == k, dk, dest)
        dest_ref[...] = dest.astype(I32)
        first_row = (lax.broadcasted_iota(I32, (nb, LANES), 0) * block_rows).astype(F32)
        lane_nb = lax.broadcasted_iota(I32, (nb, LANES), 1)
        done = jnp.where((ends[0:1, :] <= first_row) & (lane_nb < N_EXPERTS), 1.0, 0.0)
        bexp_ref[...] = jnp.minimum(jnp.sum(done, axis=1, keepdims=True), N_EXPERTS - 1.0).astype(I32)
        row8 = lax.broadcasted_iota(I32, (8, LANES), 0)
        seg = jnp.where(row8 == 0, ends - padded + counts, jnp.where(row8 == 1, ends, 0.0))
        seg_ref[...] = seg.astype(I32)


def _route(idx_parts, block_rows, n_blocks, tq):
    specs, bounds, steps = _part_specs(idx_parts, (tq, TOP_K))
    n = steps * tq
    nb_pad = -(-n_blocks // 8) * 8
    dest, bexp, seg = pl.pallas_call(
        functools.partial(_route_kernel, block_rows=block_rows, bounds=bounds),
        grid=(2, steps),
        in_specs=specs,
        out_specs=[pl.BlockSpec((tq, TOP_K), lambda p, i: (i * p, 0)),
                   pl.BlockSpec((nb_pad, 1), lambda p, i: (0, 0)),
                   pl.BlockSpec((8, LANES), lambda p, i: (0, 0))],
        out_shape=[jax.ShapeDtypeStruct((n, TOP_K), I32), jax.ShapeDtypeStruct((nb_pad, 1), I32),
                   jax.ShapeDtypeStruct((8, LANES), I32)],
        scratch_shapes=[pltpu.VMEM((1, LANES), F32), pltpu.VMEM((1, LANES), F32)],
        compiler_params=_params(("arbitrary", "arbitrary"), 40),
        name="route",
    )(*idx_parts)
    return dest.reshape(n * TOP_K), bexp.reshape(nb_pad)[:n_blocks], seg


def _tile_rows(index):
    if isinstance(index, int):
        return pl.ds(index * ROW_TILE, ROW_TILE)
    return pl.ds(pl.multiple_of(index * ROW_TILE, ROW_TILE), ROW_TILE)


def _part_specs(parts, block):
    specs, bounds, first = [], [], 0
    for a in parts:
        assert a.shape[0] % block[0] == 0
        count = a.shape[0] // block[0]
        specs.append(pl.BlockSpec(block, lambda *g, first=first, count=count: (jnp.clip(g[-1] - first, 0, count - 1), 0)))
        bounds.append((first, count))
        first += count
    return specs, tuple(bounds), first


def _dispatch_kernel(seg_ref, dest_ref, *refs, bounds):
    h_refs, (xs_ref, zero_ref, sem, zero_sem) = refs[:len(bounds)], refs[len(bounds):]
    tm = h_refs[0].shape[0] // ROW_TILE
    i = pl.program_id(0)

    @pl.when(pl.program_id(0) == 0)
    def _():
        zero_ref[...] = jnp.zeros_like(zero_ref)

        def pad_copies(e):
            slot, length = seg_ref[0, e], seg_ref[1, e] - seg_ref[0, e]
            pieces = []
            size = zero_ref.shape[0] // ROW_TILE
            while size >= 1:
                present = (length & size) != 0
                rows = pl.ds(pl.multiple_of(slot * ROW_TILE, ROW_TILE), size * ROW_TILE)
                pieces.append((present, pltpu.make_async_copy(zero_ref.at[pl.ds(0, size * ROW_TILE)],
                                                              xs_ref.at[rows], zero_sem)))
                slot = slot + jnp.where(present, size, 0)
                size //= 2
            return pieces

        def start_expert(e):
            for present, piece in pad_copies(e):
                pl.when(present)(piece.start)

        def wait_expert(e):
            for present, piece in pad_copies(e):
                pl.when(present)(piece.wait)

        def expert(e, carry):
            start_expert(e)
            wait_expert(e - 1)
            return carry

        start_expert(0)
        lax.fori_loop(1, N_EXPERTS, expert, 0)
        wait_expert(N_EXPERTS - 1)

    for h_ref, (first, count) in zip(h_refs, bounds):
        @pl.when((i >= first) & (i < first + count))
        def _(h_ref=h_ref):
            def issue(r, carry):
                for k in range(TOP_K):
                    pltpu.make_async_copy(h_ref.at[_tile_rows(r)], xs_ref.at[_tile_rows(dest_ref[r * TOP_K + k])],
                                          sem).start(priority=k % 2)
                return carry

            for r in range(tm):
                issue(r, 0)

    for k in range(TOP_K):
        pltpu.make_async_copy(h_refs[0], xs_ref.at[pl.ds(0, tm * ROW_TILE)], sem).wait()


def _dispatch(h2t_parts, dest, seg, n_slots, tm, block_rows):
    specs, bounds, steps = _part_specs(h2t_parts, (tm * ROW_TILE, LANES))
    assert block_rows & (block_rows - 1) == 0
    return pl.pallas_call(
        functools.partial(_dispatch_kernel, bounds=bounds),
        grid=(steps,),
        in_specs=[pl.BlockSpec(memory_space=pltpu.SMEM),
                  pl.BlockSpec((tm * TOP_K,), lambda i: (i,), memory_space=pltpu.SMEM)] + specs,
        out_specs=pl.BlockSpec(memory_space=pl.ANY),
        out_shape=jax.ShapeDtypeStruct((n_slots * ROW_TILE, LANES), F32),
        scratch_shapes=[pltpu.VMEM((block_rows // 2 * ROW_TILE, LANES), F32), pltpu.SemaphoreType.DMA,
                        pltpu.SemaphoreType.DMA],
        compiler_params=_params(("arbitrary",), 32),
        name="dispatch",
    )(seg, dest, *h2t_parts)


def _moe_kernel(be_ref, seg_ref, xs_ref, wg_ref, bg_ref, wu_ref, bu_ref, wd_ref, bd_ref, yb_ref, wg_s, wu_s, wd_s):
    i = pl.program_id(0)
    bm = xs_ref.shape[0] // ROW_TILE
    used_rows = seg_ref[LANES + N_EXPERTS - 1]

    @pl.when(i * bm < used_rows)
    def _():
        changed = (i == 0) | (be_ref[i] != be_ref[jnp.maximum(i - 1, 0)])

        @pl.when(changed)
        def _():
            for r0 in range(0, D_MODEL, 128):
                rs = pl.ds(r0, 128)
                wg_s[rs, :] = wg_ref[rs, :].astype(BF16)
                wu_s[rs, :] = wu_ref[rs, :].astype(BF16)
                wd_s[rs, :] = wd_ref[rs, :].astype(BF16)

        xb = jnp.concatenate([xs_ref[pl.ds(j, bm, stride=ROW_TILE), :] for j in range(ROW_TILE)],
                             axis=1).astype(BF16)
        chunks = [slice(c * 256, (c + 1) * 256) for c in range(D_FF // 256)]
        gs = [_dot(xb, wg_s[:, cs]) + bg_ref[:, cs] for cs in chunks]
        us = [_dot(xb, wu_s[:, cs]) + bu_ref[:, cs] for cs in chunks]
        acts = []
        for g, u in zip(gs, us):
            g = jnp.minimum(g, SWIGLU_LIMIT)
            u = jnp.clip(u, -SWIGLU_LIMIT, SWIGLU_LIMIT)
            acts.append((g * jax.nn.sigmoid(SWIGLU_ALPHA * g) * (u + 1.0)).astype(BF16))
        acc = jnp.zeros((bm, D_MODEL), F32) + bd_ref[...]
        for act, cs in zip(acts, chunks):
            acc = acc + _dot(act, wd_s[cs, :])
        for j in range(ROW_TILE):
            yb_ref[pl.ds(j, bm, stride=ROW_TILE), :] = acc[:, j * LANES:(j + 1) * LANES]


def _moe(xs, bexp, seg, wg, bg, wu, bu, wd, bd, bm):
    n_slots = xs.shape[0] // ROW_TILE
    assert n_slots % bm == 0
    wspec = pl.BlockSpec((None, D_MODEL, D_FF), lambda i, be, sg: (be[i], 0, 0))
    bspec = pl.BlockSpec((None, 1, D_FF), lambda i, be, sg: (be[i], 0, 0))
    rows = pl.BlockSpec((bm * ROW_TILE, LANES), lambda i, be, sg: (i, 0))
    grid_spec = pltpu.PrefetchScalarGridSpec(
        num_scalar_prefetch=2,
        grid=(n_slots // bm,),
        in_specs=[rows, wspec, bspec, wspec, bspec, wspec, bspec],
        out_specs=rows,
        scratch_shapes=[pltpu.VMEM((D_MODEL, D_FF), BF16)] * 3,
    )
    return pl.pallas_call(
        _moe_kernel,
        grid_spec=grid_spec,
        out_shape=jax.ShapeDtypeStruct((n_slots * ROW_TILE, LANES), F32),
        compiler_params=_params(("arbitrary",), 56),
        name="moe_experts",
    )(bexp, seg.reshape(-1), xs, wg, bg.reshape(N_EXPERTS, 1, D_FF), wu, bu.reshape(N_EXPERTS, 1, D_FF),
      wd, bd.reshape(N_EXPERTS, 1, D_MODEL))


def _combine_kernel(dest_ref, dest_next_ref, yb_ref, *refs, bounds):
    nparts = len(bounds)
    x1_refs, gw_refs, y_refs, (rows_ref, sems) = (refs[:nparts], refs[nparts:2 * nparts], refs[2 * nparts:3 * nparts],
                                                  refs[3 * nparts:])
    tm = x1_refs[0].shape[0]
    i = pl.program_id(0)
    slot = i % 2

    def issue(dref, buf, unrolled):
        def body(r, carry):
            for k in range(TOP_K):
                pltpu.make_async_copy(yb_ref.at[_tile_rows(dref[r * TOP_K + k])], rows_ref.at[buf, k, _tile_rows(r)],
                                      sems.at[buf]).start(priority=k % 2)
            return carry

        if unrolled:
            for r in range(tm):
                body(r, 0)
        else:
            lax.fori_loop(0, tm, body, 0, unroll=4)

    @pl.when(i == 0)
    def _():
        issue(dest_ref, slot, False)

    @pl.when(i + 1 < pl.num_programs(0))
    def _():
        issue(dest_next_ref, 1 - slot, True)

    for k in range(TOP_K):
        pltpu.make_async_copy(yb_ref.at[pl.ds(0, tm * ROW_TILE)], rows_ref.at[slot, k], sems.at[slot]).wait()
    gw = _active_part(gw_refs, bounds, i)
    for x1_ref, y_ref, (first, count) in zip(x1_refs, y_refs, bounds):
        @pl.when((i >= first) & (i < first + count))
        def _(x1_ref=x1_ref, y_ref=y_ref):
            for j in range(ROW_TILE):
                ls = slice(j * LANES, (j + 1) * LANES)
                y = x1_ref[:, ls]
                for k in range(TOP_K):
                    y = y + gw[:, k:k + 1] * rows_ref[slot, k, pl.ds(j, tm, stride=ROW_TILE), :]
                y_ref[:, ls] = y


def _combine(x1_parts, gw_parts, dest, yb, tm):
    specs, bounds, steps = _part_specs(x1_parts, (tm, D_MODEL))
    gw_specs, gw_bounds, _ = _part_specs(gw_parts, (tm, TOP_K))
    assert gw_bounds == bounds
    return pl.pallas_call(
        functools.partial(_combine_kernel, bounds=bounds),
        grid=(steps,),
        in_specs=[pl.BlockSpec((tm * TOP_K,), lambda i: (i,), memory_space=pltpu.SMEM),
                  pl.BlockSpec((tm * TOP_K,), lambda i: (jnp.minimum(i + 1, steps - 1),), memory_space=pltpu.SMEM),
                  pl.BlockSpec(memory_space=pl.ANY)] + specs + gw_specs,
        out_specs=specs,
        out_shape=[jax.ShapeDtypeStruct(a.shape, F32) for a in x1_parts],
        scratch_shapes=[pltpu.VMEM((2, TOP_K, tm * ROW_TILE, LANES), F32), pltpu.SemaphoreType.DMA((2,))],
        compiler_params=_params(("arbitrary",), 32),
        name="combine",
    )(dest, dest, yb, *x1_parts, *gw_parts)


def _rope_tables(positions):
    half = ROPE_DIM // 2
    inv = jnp.power(jnp.float32(ROPE_THETA), -jnp.arange(half, dtype=F32) / half)
    ang = positions.astype(F32)[:, None] * inv[None, :]
    cos, sin = jnp.cos(ang), jnp.sin(ang)
    t = positions.shape[0]
    z8 = jnp.zeros((t, half), F32)
    rest0 = jnp.zeros((t, HEAD_DIM_A - ROPE_DIM), F32)
    a = jnp.concatenate([cos, cos, jnp.ones((t, HEAD_DIM_A - ROPE_DIM), F32)], axis=1)
    bp = jnp.concatenate([z8, sin, rest0], axis=1)
    bm = jnp.concatenate([-sin, z8, rest0], axis=1)
    return tuple(jnp.tile(m, (1, LANES // HEAD_DIM_A)) for m in (a, bp, bm))


def _segment_mean_matrix(seg):
    i = jnp.arange(256)
    return jnp.where((i[:, None] // seg) == (i[None, :] // seg), 1.0 / seg, 0.0).astype(BF16)


def _layer_weights(l, norm1_g, w_in, b_gates, qn_a, kn_a, conv_w, conv_b, cn_g, cn_b, qn_m,
                   w_proj_a, w_proj_b, w_proj_m, w_out, norm2_g, w_router, b_router,
                   w_gate, b_gate, w_up, b_up, w_down, b_down):
    wr = jnp.pad(w_router[l], ((0, 0), (0, LANES - N_EXPERTS)))
    wrh = wr.astype(BF16)
    return dict(
        g1=norm1_g[l][None, :],
        w1=w_in[l][:, :W1_COLS].astype(BF16),
        wgates=w_in[l][:, W1_COLS:].astype(BF16),
        bgates=b_gates[l][None, :],
        qg=jnp.tile(qn_a[l], 256 // HEAD_DIM_A)[None, :],
        kg=jnp.tile(kn_a[l], 256 // HEAD_DIM_A)[None, :],
        mg=jnp.tile(qn_m[l], 256 // MEM_HEAD_DIM)[None, :],
        conv_w=conv_w[l], conv_b=conv_b[l][None, :], cn_g=cn_g[l][None, :], cn_b=cn_b[l][None, :],
        wa=w_proj_a[l].astype(BF16), wb=w_proj_b[l].astype(BF16), wm=w_proj_m[l].astype(BF16),
        wo=w_out[l].astype(BF16), g2=norm2_g[l][None, :],
        wrh=wrh, wrc=jnp.concatenate([wrh, (wr - wrh.astype(F32)).astype(BF16)], axis=1),
        br=jnp.pad(b_router[l], (0, LANES - N_EXPERTS))[None, :],
        w_gate=w_gate[l], b_gate=b_gate[l], w_up=w_up[l], b_up=b_up[l], w_down=w_down[l], b_down=b_down[l],
        seg64=_segment_mean_matrix(HEAD_DIM_A), seg128=_segment_mean_matrix(MEM_HEAD_DIM),
    )


def _merge_group(x2d, os_, lses, ob, om, w, tm, conv_seq_len=None):
    conv = None if conv_seq_len is None else (w["conv_w"], w["conv_b"], w["cn_g"], w["cn_b"], conv_seq_len)
    return _merge(x2d, os_, lses, ob, om, w["g1"], w["wgates"], w["bgates"], w["wa"], w["wb"], w["wm"],
                  w["wo"], w["g2"], w["wrh"], w["wrc"], w["br"], tm, conv)


MOE_ROWS = 512
ROUTE_TILE = 1024
DISPATCH_TILE = 256
COMBINE_TILE = 128


def _moe_groups(merged, w):
    x1s, h2ts, idxs, gws = zip(*merged)
    n = sum(a.shape[0] for a in idxs)
    n_blocks = -(-(n * TOP_K) // MOE_ROWS) + N_EXPERTS
    dest, bexp, seg = _route(idxs, MOE_ROWS, n_blocks, ROUTE_TILE)
    xs = _dispatch(h2ts, dest, seg, n_blocks * MOE_ROWS, DISPATCH_TILE, MOE_ROWS)
    yb = _moe(xs, bexp, seg, w["w_gate"], w["b_gate"], w["w_up"], w["b_up"], w["w_down"], w["b_down"], MOE_ROWS)
    return _combine(x1s, gws, dest, yb, COMBINE_TILE)


def _prompt_layer(x, mem, w, mem_norm_g, w_mem_kv, kn_m):
    b, s, _ = x.shape
    n = b * s
    x2d = x.reshape(n, D_MODEL)
    tabs = _rope_tables(jnp.arange(s, dtype=I32))
    q, k, v, u, qm = _in_proj(x2d, w["g1"], w["w1"], w["qg"], w["kg"], w["mg"], tabs, w["seg64"], w["seg128"], 512, F32)
    q3, k3, v3 = (t.reshape(b, s, WIDTH_A) for t in (q, k, v))
    os_, lses, caches = [], [], []
    for g, (win, dil) in enumerate(SWA_GROUPS):
        o, lse = _swa_prompt(q3, k3, v3, g, dil)
        os_.append(o)
        lses.append(lse)
        cs = slice(g * GROUP_W, (g + 1) * GROUP_W)
        caches += [k3[:, s - win:, cs].reshape(b, win, 4, HEAD_DIM_A), v3[:, s - win:, cs].reshape(b, win, 4, HEAD_DIM_A)]
    u3 = u.reshape(b, s, CONV_CH)
    mk, mv = _mem_kv(mem, mem_norm_g[None, :], w_mem_kv.astype(BF16), kn_m[None, :])
    om = _mem_attn(qm.reshape(b, s, MEM_WIDTH), mk, mv, 512)
    merged = _merge_group(x2d, os_, lses, u, om, w, 512, conv_seq_len=s)
    state = caches + [u3[:, s - (CONV_WIDTH - 1):], mk.reshape(b, MEM_LEN, MEM_HEADS, MEM_HEAD_DIM),
                      mv.reshape(b, MEM_LEN, MEM_HEADS, MEM_HEAD_DIM)]
    return merged, state


def _sample_layer(x, mem_k, mem_v, bufs, conv_state, w):
    b, t, _ = x.shape
    n = b * t
    x2d = x.reshape(n, D_MODEL)
    tabs = _rope_tables(jnp.tile(PAST_LEN + jnp.arange(t, dtype=I32), b))
    q, k, v, u, qm = _in_proj(x2d, w["g1"], w["w1"], w["qg"], w["kg"], w["mg"], tabs, w["seg64"], w["seg128"], 256, BF16)
    q3, k3, v3 = (a.reshape(b, t, WIDTH_A) for a in (q, k, v))
    os_, lses, caches = [], [], []
    for g, (win, dil) in enumerate(SWA_GROUPS):
        to_cm = lambda a: jnp.transpose(a, (0, 2, 3, 1)).reshape(b, GROUP_W, win)
        from_cm = lambda a: jnp.transpose(a.reshape(b, 4, HEAD_DIM_A, win), (0, 3, 1, 2))
        o, lse, kto, vto = _swa_sample(q3, k3, v3, to_cm(bufs[2 * g]), to_cm(bufs[2 * g + 1]), g, win, dil)
        os_.append(o)
        lses.append(lse)
        caches += [from_cm(kto), from_cm(vto)]
    u3 = u.reshape(b, t, CONV_CH)
    halo = jnp.pad(conv_state, ((0, 0), (CONV_HALO - (CONV_WIDTH - 1), 0), (0, 0)))
    ob = _conv_branch(u3, halo, w["conv_w"], w["conv_b"], w["cn_g"], w["cn_b"], t, False)
    om = _mem_attn_rows(qm.reshape(b, t, MEM_WIDTH), mem_k.reshape(b, MEM_LEN * MEM_HEADS, MEM_HEAD_DIM),
                        mem_v.reshape(b, MEM_LEN * MEM_HEADS, MEM_HEAD_DIM))
    merged = _merge_group(x2d, os_, lses, ob, om, w, 256)
    new_conv = jnp.concatenate([conv_state, u3], axis=1)[:, t:]
    return merged, caches + [new_conv]


def kernel(x_prompt, x_sample, mem_prompt, cache_swa_k_w128, cache_swa_v_w128, cache_swa_k_w512, cache_swa_v_w512, cache_swa_k_w2048, cache_swa_v_w2048, state_conv, cache_mem_k, cache_mem_v, norm1_g, w_in, b_gates, qn_a, kn_a, conv_w, conv_b, cn_g, cn_b, mem_norm_g, w_mem_kv, qn_m, kn_m, w_proj_a, w_proj_b, w_proj_m, w_out, norm2_g, w_router, b_router, w_gate, b_gate, w_up, b_up, w_down, b_down):
    depth = norm1_g.shape[0]
    y_p, y_s = x_prompt, x_sample
    st_p, st_s = [], []
    for l in range(depth):
        w = _layer_weights(l, norm1_g, w_in, b_gates, qn_a, kn_a, conv_w, conv_b, cn_g, cn_b, qn_m,
                           w_proj_a, w_proj_b, w_proj_m, w_out, norm2_g, w_router, b_router,
                           w_gate, b_gate, w_up, b_up, w_down, b_down)
        merged_p, state_p = _prompt_layer(y_p, mem_prompt, w, mem_norm_g[l], w_mem_kv[l], kn_m[l])
        bufs = (cache_swa_k_w128[l], cache_swa_v_w128[l], cache_swa_k_w512[l], cache_swa_v_w512[l],
                cache_swa_k_w2048[l], cache_swa_v_w2048[l])
        merged_s, state_s = _sample_layer(y_s, cache_mem_k[l], cache_mem_v[l], bufs, state_conv[l], w)
        y_p2d, y_s2d = _moe_groups((merged_p, merged_s), w)
        y_p, y_s = y_p2d.reshape(y_p.shape), y_s2d.reshape(y_s.shape)
        st_p.append(state_p)
        st_s.append(state_s)
    outs_p = [jnp.stack(a) for a in zip(*st_p)]
    outs_s = [jnp.stack(a) for a in zip(*st_s)]
    return (y_p, y_s, *outs_p, *outs_s)
```

```python
import functools

import jax
import jax.numpy as jnp
from jax import lax
from jax.experimental import pallas as pl
from jax.experimental.pallas import tpu as pltpu

F32 = jnp.float32
BF16 = jnp.bfloat16
I32 = jnp.int32

D_MODEL = 1024
HEAD_DIM_A = 64
GROUP_W = 256
N_GROUPS = 3
WIDTH_A = N_GROUPS * GROUP_W
SWA_GROUPS = ((128, 1), (512, 4), (2048, 16))
SWA_BLOCK = 128
ROPE_DIM = 16
ROPE_THETA = 500000.0
CONV_CH = 512
CONV_WIDTH = 31
CONV_HALO = 32
MEM_LEN = 256
MEM_HEADS = 4
MEM_HEAD_DIM = 128
MEM_WIDTH = 512
N_EXPERTS = 32
TOP_K = 4
D_FF = 1024
SWIGLU_LIMIT = 7.0
SWIGLU_ALPHA = 1.702
EPS = 1e-6
PAST_LEN = 8192
LANES = 128
ROW_TILE = D_MODEL // LANES
W1_COLS = 3 * WIDTH_A + 2 * CONV_CH + MEM_WIDTH
NEG_INF = float("-inf")
MIB = 2 ** 20


def _params(semantics, vmem_mib):
    return pltpu.CompilerParams(dimension_semantics=semantics, vmem_limit_bytes=vmem_mib * MIB)


def _rms(x, gain):
    return x * lax.rsqrt(jnp.mean(x * x, axis=-1, keepdims=True) + EPS) * gain


def _dot(a, b):
    return jnp.dot(a, b, preferred_element_type=F32)


def _dot_nt(a, b):
    return lax.dot_general(a, b, (((1,), (1,)), ((), ())), preferred_element_type=F32)


def _in_proj_kernel(x_ref, g1_ref, w_ref, qg_ref, kg_ref, mg_ref, ra_ref, rp_ref, rm_ref, s64_ref, s128_ref,
                    q_ref, k_ref, v_ref, u_ref, qm_ref):
    hb = _rms(x_ref[...], g1_ref[...]).astype(BF16)
    ra, rp, rm = ra_ref[...], rp_ref[...], rm_ref[...]

    def proj(c0):
        return _dot(hb, w_ref[:, c0:c0 + 256])

    def head_norm(p, seg_ref, gain):
        ms = _dot((p * p).astype(BF16), seg_ref[...])
        return p * lax.rsqrt(ms + EPS) * gain

    def rope(p):
        outs = []
        for j in range(2):
            pj = p[:, j * LANES:(j + 1) * LANES]
            outs.append(pj * ra + pltpu.roll(pj, 8, 1) * rp + pltpu.roll(pj, LANES - 8, 1) * rm)
        return jnp.concatenate(outs, axis=1)

    for c in range(3):
        cs = slice(c * 256, (c + 1) * 256)
        q_ref[:, cs] = rope(head_norm(proj(c * 256), s64_ref, qg_ref[...])).astype(q_ref.dtype)
        k_ref[:, cs] = rope(head_norm(proj(WIDTH_A + c * 256), s64_ref, kg_ref[...]))
        v_ref[:, cs] = proj(2 * WIDTH_A + c * 256)
    for c in range(2):
        cs = slice(c * 256, (c + 1) * 256)
        a = proj(3 * WIDTH_A + c * 256)
        gate = proj(3 * WIDTH_A + CONV_CH + c * 256)
        u_ref[:, cs] = a * jax.nn.sigmoid(gate)
        qm_ref[:, cs] = head_norm(proj(3 * WIDTH_A + 2 * CONV_CH + c * 256), s128_ref, mg_ref[...]).astype(BF16)


def _in_proj(x, g1, w1, qg, kg, mg, tabs, seg64, seg128, tm, q_dtype):
    n = x.shape[0]
    t_rows = tabs[0].shape[0]
    assert n % tm == 0 and t_rows % tm == 0
    t_blocks = t_rows // tm
    row = lambda i: (i, 0)
    const = lambda i: (0, 0)
    tab = lambda i: (i % t_blocks, 0)
    return pl.pallas_call(
        _in_proj_kernel,
        grid=(n // tm,),
        in_specs=[
            pl.BlockSpec((tm, D_MODEL), row),
            pl.BlockSpec((1, D_MODEL), const),
            pl.BlockSpec((D_MODEL, W1_COLS), const),
            pl.BlockSpec((1, 256), const), pl.BlockSpec((1, 256), const), pl.BlockSpec((1, 256), const),
            pl.BlockSpec((tm, LANES), tab), pl.BlockSpec((tm, LANES), tab), pl.BlockSpec((tm, LANES), tab),
            pl.BlockSpec((256, 256), const), pl.BlockSpec((256, 256), const),
        ],
        out_specs=[
            pl.BlockSpec((tm, WIDTH_A), row), pl.BlockSpec((tm, WIDTH_A), row), pl.BlockSpec((tm, WIDTH_A), row),
            pl.BlockSpec((tm, CONV_CH), row), pl.BlockSpec((tm, MEM_WIDTH), row),
        ],
        out_shape=[
            jax.ShapeDtypeStruct((n, WIDTH_A), q_dtype), jax.ShapeDtypeStruct((n, WIDTH_A), F32),
            jax.ShapeDtypeStruct((n, WIDTH_A), F32), jax.ShapeDtypeStruct((n, CONV_CH), F32),
            jax.ShapeDtypeStruct((n, MEM_WIDTH), BF16),
        ],
        compiler_params=_params(("arbitrary",), 48),
        name="in_proj",
    )(x, g1, w1, qg, kg, mg, *tabs, seg64, seg128)


def _swa_prompt_kernel(q_ref, k_ref, v_ref, o_ref, lse_ref, q_s, k_s, v_s, o_s, l_s, *, dil, nsub):
    n = pl.program_id(1)
    span = SWA_BLOCK * dil
    cur = n % 2
    prv = 1 - cur
    for s in range(2):
        ls = slice(s * LANES, (s + 1) * LANES)
        q_s[s] = q_ref[:, ls]
        k_s[cur, s] = k_ref[:, ls]
        v_s[cur, s] = v_ref[:, ls]

    @pl.when(n == 0)
    def _():
        k_s[prv] = jnp.zeros(k_s.shape[1:], F32)
        v_s[prv] = jnp.zeros(v_s.shape[1:], F32)

    shape = (SWA_BLOCK, 2 * SWA_BLOCK)
    col = lax.broadcasted_iota(I32, shape, 1)
    off = lax.broadcasted_iota(I32, shape, 0) + SWA_BLOCK - col
    band = (off >= 0) & (off <= SWA_BLOCK)
    band_first = band & ((n > 0) | (col >= SWA_BLOCK))
    head = col // HEAD_DIM_A

    hms = [head == h for h in range(4)]

    def both(ref, lead, rows):
        return jnp.concatenate([ref[(*lead, s, rows, slice(None))] for s in range(2)], axis=1)

    def blocks(items):
        rows_of, masks, qs, kks, vvs = [], [], [], [], []
        for j, r in items:
            rows = pl.ds(j * span + r, SWA_BLOCK, stride=dil)
            before = (prv, pl.ds((nsub - 1) * span + r, SWA_BLOCK, stride=dil)) if j == 0 else \
                     (cur, pl.ds((j - 1) * span + r, SWA_BLOCK, stride=dil))
            rows_of.append(rows)
            masks.append(band_first if j == 0 else band)
            qs.append(both(q_s, (), rows).astype(BF16))
            kks.append(jnp.concatenate([both(k_s, before[:1], before[1]), both(k_s, (cur,), rows)],
                                       axis=0).astype(BF16))
            vvs.append(jnp.concatenate([both(v_s, before[:1], before[1]), both(v_s, (cur,), rows)],
                                       axis=0).astype(BF16))
        pairs = [(b, h) for b in range(len(items)) for h in range(4)]
        ss = [jnp.where(masks[b], _dot_nt(jnp.where(hms[h], qs[b], jnp.zeros_like(qs[b])), kks[b])
                        * (HEAD_DIM_A ** -0.5), NEG_INF) for b, h in pairs]
        ms = [jnp.max(s, axis=1, keepdims=True) for s in ss]
        ps = [jnp.exp(s - m) for s, m in zip(ss, ms)]
        sums = [jnp.sum(p, axis=1, keepdims=True) for p in ps]
        ohs = [_dot(p.astype(BF16), vvs[b]) for p, (b, h) in zip(ps, pairs)]
        for b in range(len(items)):
            o_acc = jnp.zeros(shape, F32)
            lse_acc = jnp.zeros(shape, F32)
            for h in range(4):
                i = 4 * b + h
                o_acc = jnp.where(hms[h], ohs[i] / sums[i], o_acc)
                lse_acc = jnp.where(hms[h], ms[i] + jnp.log(sums[i]), lse_acc)
            for s in range(2):
                ls = slice(s * LANES, (s + 1) * LANES)
                o_s[s, rows_of[b], :] = o_acc[:, ls]
                l_s[s, rows_of[b], :] = lse_acc[:, ls]

    if nsub == 1:
        def trip(i, carry):
            blocks([(0, 2 * i), (0, 2 * i + 1)])
            return carry

        lax.fori_loop(0, dil // 2, trip, 0)
    else:
        def trip(r, carry):
            for j0 in range(0, nsub, 2):
                blocks([(j0, r), (j0 + 1, r)])
            return carry

        lax.fori_loop(0, dil, trip, 0)
    for s in range(2):
        ls = slice(s * LANES, (s + 1) * LANES)
        o_ref[:, ls] = o_s[s]
        lse_ref[:, ls] = l_s[s]


def _swa_prompt(q, k, v, g, dil):
    b, s, _ = q.shape
    nsub = max(1, 4 // dil)
    t = nsub * dil * SWA_BLOCK
    assert s % t == 0
    inp = pl.BlockSpec((None, t, GROUP_W), lambda bi, n: (bi, n, g))
    out = pl.BlockSpec((None, t, GROUP_W), lambda bi, n: (bi, n, 0))
    slab = lambda *lead: pltpu.VMEM((*lead, 2, t, LANES), F32)
    o, lse = pl.pallas_call(
        functools.partial(_swa_prompt_kernel, dil=dil, nsub=nsub),
        grid=(b, s // t),
        in_specs=[inp, inp, inp],
        out_specs=[out, out],
        out_shape=[jax.ShapeDtypeStruct((b, s, GROUP_W), F32)] * 2,
        scratch_shapes=[slab(), slab(2), slab(2), slab(), slab()],
        compiler_params=_params(("arbitrary", "arbitrary"), 48),
        name=f"swa_prompt_d{dil}",
    )(q, k, v)
    return o.reshape(b * s, GROUP_W), lse.reshape(b * s, GROUP_W)


def _swa_sample_kernel(*refs, win, dil):
    def one(i, carry):
        _swa_sample_one(*(r.at[i] for r in refs), win=win, dil=dil)
        return carry

    lax.fori_loop(0, refs[0].shape[0], one, 0)


def _swa_sample_one(q_ref, kn_ref, vn_ref, kt_ref, vt_ref, o_ref, lse_ref, kto_ref, vto_ref, *, win, dil):
    t_new = q_ref.shape[0]
    n_tiles = win // LANES
    pad = jnp.zeros((LANES - t_new, GROUP_W), F32)
    kn = jnp.concatenate([kn_ref[...], pad], axis=0)
    vn = jnp.concatenate([vn_ref[...], pad], axis=0)

    def channel_major(a):
        return jnp.concatenate([a[:, :LANES].T, a[:, LANES:].T], axis=0)

    knt, vnt = channel_major(kn), channel_major(vn)
    lane = lax.broadcasted_iota(I32, (GROUP_W, LANES), 1)
    keep = lane < LANES - t_new
    for src_ref, new_t, dst_ref in ((kt_ref, knt, kto_ref), (vt_ref, vnt, vto_ref)):
        nxt = pltpu.roll(src_ref[:, 0:LANES], LANES - t_new, 1)
        for j in range(n_tiles):
            this = nxt
            following = new_t if j + 1 == n_tiles else src_ref[:, (j + 1) * LANES:(j + 2) * LANES]
            nxt = pltpu.roll(following, LANES - t_new, 1)
            dst_ref[:, j * LANES:(j + 1) * LANES] = jnp.where(keep, this, nxt)

    qf = q_ref[...].astype(F32)
    ch_head = lax.broadcasted_iota(I32, (t_new, GROUP_W), 1) // HEAD_DIM_A
    qm = jnp.concatenate([jnp.where(ch_head == h, qf, 0.0) for h in range(4)], axis=0).astype(BF16)
    nq = 4 * t_new
    scale = HEAD_DIM_A ** -0.5
    s_c = _dot(qm, kt_ref[...].astype(BF16)) * scale
    s_n = _dot(qm, knt.astype(BF16)) * scale
    t_c = lax.broadcasted_iota(I32, (nq, win), 0) & (t_new - 1)
    d_c = win + t_c - lax.broadcasted_iota(I32, (nq, win), 1)
    s_c = jnp.where((d_c <= win) & ((d_c & (dil - 1)) == 0), s_c, NEG_INF)
    t_n = lax.broadcasted_iota(I32, (nq, LANES), 0) & (t_new - 1)
    d_n = t_n - lax.broadcasted_iota(I32, (nq, LANES), 1)
    s_n = jnp.where((d_n >= 0) & ((d_n & (dil - 1)) == 0), s_n, NEG_INF)
    m = jnp.maximum(jnp.max(s_c, axis=1, keepdims=True), jnp.max(s_n, axis=1, keepdims=True))
    p_c = jnp.exp(s_c - m)
    p_n = jnp.exp(s_n - m)
    l = jnp.sum(p_c, axis=1, keepdims=True) + jnp.sum(p_n, axis=1, keepdims=True)
    o_all = (_dot_nt(p_c.astype(BF16), vt_ref[...].astype(BF16)) + _dot(p_n.astype(BF16), vn.astype(BF16))) / l
    lse_all = m + jnp.log(l)
    o_acc = jnp.zeros((t_new, GROUP_W), F32)
    lse_acc = jnp.zeros((t_new, GROUP_W), F32)
    for h in range(4):
        hm = ch_head == h
        o_acc = jnp.where(hm, o_all[h * t_new:(h + 1) * t_new, :], o_acc)
        lse_acc = jnp.where(hm, lse_all[h * t_new:(h + 1) * t_new, :], lse_acc)
    o_ref[...] = o_acc
    lse_ref[...] = lse_acc


def _swa_sample(q, k_new, v_new, kt_buf, vt_buf, g, win, dil):
    b, t, _ = q.shape
    assert t == 8 and win % LANES == 0 and dil & (dil - 1) == 0
    bb = max(1, 4096 // win)
    assert b % bb == 0
    new = pl.BlockSpec((bb, t, GROUP_W), lambda bi: (bi, 0, g))
    buf = pl.BlockSpec((bb, GROUP_W, win), lambda bi: (bi, 0, 0))
    out = pl.BlockSpec((bb, t, GROUP_W), lambda bi: (bi, 0, 0))
    o, lse, kto, vto = pl.pallas_call(
        functools.partial(_swa_sample_kernel, win=win, dil=dil),
        grid=(b // bb,),
        in_specs=[new, new, new, buf, buf],
        out_specs=[out, out, buf, buf],
        out_shape=[jax.ShapeDtypeStruct((b, t, GROUP_W), F32)] * 2
        + [jax.ShapeDtypeStruct((b, GROUP_W, win), F32)] * 2,
        compiler_params=_params(("arbitrary",), 48),
        name=f"swa_sample_w{win}",
    )(q, k_new, v_new, kt_buf, vt_buf)
    return o.reshape(b * t, GROUP_W), lse.reshape(b * t, GROUP_W), kto, vto


def _conv_kernel(halo_ref, cur_ref, w_ref, b_ref, g_ref, beta_ref, o_ref, ctx_ref, *, zero_first_halo, chunk):
    halo = halo_ref[...]
    if zero_first_halo:
        halo = jnp.where(pl.program_id(1) == 0, 0.0, halo)
    _conv_module(halo, cur_ref, w_ref, b_ref, g_ref, beta_ref, o_ref, ctx_ref, chunk)


def _conv_module(halo, cur_ref, w_ref, b_ref, g_ref, beta_ref, o_ref, ctx_ref, chunk):
    for rows in _conv_module_parts(halo, cur_ref, w_ref, b_ref, g_ref, beta_ref, o_ref, ctx_ref, chunk):
        rows()


def _conv_module_parts(halo, cur_ref, w_ref, b_ref, g_ref, beta_ref, o_ref, ctx_ref, chunk):
    tm = cur_ref.shape[0]
    ctx_ref[pl.ds(0, CONV_HALO), :] = halo
    ctx_ref[pl.ds(CONV_HALO, tm), :] = cur_ref[...]
    first = CONV_HALO - (CONV_WIDTH - 1)

    def rows_from(c0):
        acc = jnp.zeros((chunk, CONV_CH), F32) + b_ref[...]
        for phase in range(8):
            taps = range(phase, CONV_WIDTH, 8)
            start, shift = divmod(first + phase, 8)
            need = chunk + 8 * (len(taps) - 1)
            if shift == 0:
                window = ctx_ref[pl.ds(c0 + 8 * start, need), :]
            else:
                rows = ctx_ref[pl.ds(c0 + 8 * start, need + 8), :]
                window = pltpu.roll(rows, need + 8 - shift, 0)[0:need, :]
            for a, w in enumerate(taps):
                acc = acc + window[8 * a:8 * a + chunk, :] * w_ref[pl.ds(w, 1), :]
        mu = jnp.mean(acc, axis=-1, keepdims=True)
        xc = acc - mu
        var = jnp.mean(xc * xc, axis=-1, keepdims=True)
        y = xc * lax.rsqrt(var + EPS) * g_ref[...] + beta_ref[...]
        o_ref[pl.ds(c0, chunk), :] = (y * jax.nn.sigmoid(y)).astype(BF16)

    return [functools.partial(rows_from, c0) for c0 in range(0, tm, chunk)]


def _conv_branch(u, halo, conv_w, conv_b, cn_g, cn_b, tm, halo_from_u):
    b, t, _ = u.shape
    assert t % tm == 0
    ratio = tm // CONV_HALO if halo_from_u else 0
    halo_map = (lambda bi, i: (bi, jnp.maximum(i * ratio - 1, 0), 0)) if halo_from_u else (lambda bi, i: (bi, 0, 0))
    const = lambda bi, i: (0, 0)
    return pl.pallas_call(
        functools.partial(_conv_kernel, zero_first_halo=halo_from_u, chunk=min(tm, 64)),
        grid=(b, t // tm),
        in_specs=[
            pl.BlockSpec((None, CONV_HALO, CONV_CH), halo_map),
            pl.BlockSpec((None, tm, CONV_CH), lambda bi, i: (bi, i, 0)),
            pl.BlockSpec((CONV_WIDTH, CONV_CH), const),
            pl.BlockSpec((1, CONV_CH), const), pl.BlockSpec((1, CONV_CH), const), pl.BlockSpec((1, CONV_CH), const),
        ],
        out_specs=pl.BlockSpec((None, tm, CONV_CH), lambda bi, i: (bi, i, 0)),
        out_shape=jax.ShapeDtypeStruct((b, t, CONV_CH), BF16),
        scratch_shapes=[pltpu.VMEM((CONV_HALO + tm, CONV_CH), F32)],
        compiler_params=_params(("arbitrary", "arbitrary"), 32),
        name="conv_module",
    )(halo, u, conv_w, conv_b, cn_g, cn_b).reshape(b * t, CONV_CH)


def _mem_kv_kernel(mem_ref, g_ref, w_ref, kg_ref, k_ref, v_ref):
    hb = _rms(mem_ref[...], g_ref[...]).astype(BF16)
    for h in range(MEM_HEADS):
        cs = slice(h * MEM_HEAD_DIM, (h + 1) * MEM_HEAD_DIM)
        k_ref[:, cs] = _rms(_dot(hb, w_ref[:, cs]), kg_ref[...])
        v_ref[:, cs] = _dot(hb, w_ref[:, MEM_WIDTH + h * MEM_HEAD_DIM:MEM_WIDTH + (h + 1) * MEM_HEAD_DIM])


def _mem_kv(mem, g, w, kg):
    b = mem.shape[0]
    const = lambda bi: (0, 0)
    blk = pl.BlockSpec((None, MEM_LEN, MEM_WIDTH), lambda bi: (bi, 0, 0))
    return pl.pallas_call(
        _mem_kv_kernel,
        grid=(b,),
        in_specs=[pl.BlockSpec((None, MEM_LEN, D_MODEL), lambda bi: (bi, 0, 0)), pl.BlockSpec((1, D_MODEL), const),
                  pl.BlockSpec((D_MODEL, 2 * MEM_WIDTH), const), pl.BlockSpec((1, MEM_HEAD_DIM), const)],
        out_specs=[blk, blk],
        out_shape=[jax.ShapeDtypeStruct((b, MEM_LEN, MEM_WIDTH), F32)] * 2,
        compiler_params=_params(("arbitrary",), 32),
        name="mem_kv",
    )(mem, g, w, kg)


def _mem_attn_kernel(q_ref, k_ref, v_ref, o_ref):
    for h in range(MEM_HEADS):
        cs = slice(h * MEM_HEAD_DIM, (h + 1) * MEM_HEAD_DIM)
        s = _dot_nt(q_ref[:, cs], k_ref[:, cs].astype(BF16)) * (MEM_HEAD_DIM ** -0.5)
        m = jnp.max(s, axis=1, keepdims=True)
        p = jnp.exp(s - m)
        l = jnp.sum(p, axis=1, keepdims=True)
        o_ref[:, cs] = (_dot(p.astype(BF16), v_ref[:, cs].astype(BF16)) / l).astype(BF16)


def _mem_attn(q, k, v, tm):
    b, t, _ = q.shape
    assert t % tm == 0
    kv = pl.BlockSpec((None, MEM_LEN, MEM_WIDTH), lambda bi, i: (bi, 0, 0))
    qo = pl.BlockSpec((None, tm, MEM_WIDTH), lambda bi, i: (bi, i, 0))
    return pl.pallas_call(
        _mem_attn_kernel,
        grid=(b, t // tm),
        in_specs=[qo, kv, kv],
        out_specs=qo,
        out_shape=jax.ShapeDtypeStruct((b, t, MEM_WIDTH), BF16),
        compiler_params=_params(("arbitrary", "arbitrary"), 32),
        name="mem_attn",
    )(q, k, v).reshape(b * t, MEM_WIDTH)


def _mem_attn_rows_kernel(*refs):
    def one(i, carry):
        _mem_attn_rows_one(*(r.at[i] for r in refs))
        return carry

    lax.fori_loop(0, refs[0].shape[0], one, 0)


def _mem_attn_rows_one(q_ref, k_ref, v_ref, o_ref):
    t = q_ref.shape[0]
    qf = q_ref[...].astype(F32)
    qs = jnp.concatenate([qf[:, h * MEM_HEAD_DIM:(h + 1) * MEM_HEAD_DIM] for h in range(MEM_HEADS)], axis=0)
    s = _dot_nt(qs.astype(BF16), k_ref[...].astype(BF16)) * (MEM_HEAD_DIM ** -0.5)
    shape = s.shape
    same_head = (lax.broadcasted_iota(I32, shape, 0) // t) == (lax.broadcasted_iota(I32, shape, 1) & (MEM_HEADS - 1))
    s = jnp.where(same_head, s, NEG_INF)
    m = jnp.max(s, axis=1, keepdims=True)
    p = jnp.exp(s - m)
    l = jnp.sum(p, axis=1, keepdims=True)
    o = _dot(p.astype(BF16), v_ref[...].astype(BF16)) / l
    o_ref[...] = jnp.concatenate([o[h * t:(h + 1) * t, :] for h in range(MEM_HEADS)], axis=1).astype(BF16)


def _mem_attn_rows(q, k_rows, v_rows):
    b, t, _ = q.shape
    bb = 4
    assert b % bb == 0
    kv = pl.BlockSpec((bb, MEM_LEN * MEM_HEADS, MEM_HEAD_DIM), lambda bi: (bi, 0, 0))
    qo = pl.BlockSpec((bb, t, MEM_WIDTH), lambda bi: (bi, 0, 0))
    return pl.pallas_call(
        _mem_attn_rows_kernel,
        grid=(b // bb,),
        in_specs=[qo, kv, kv],
        out_specs=qo,
        out_shape=jax.ShapeDtypeStruct((b, t, MEM_WIDTH), BF16),
        compiler_params=_params(("arbitrary",), 32),
        name="mem_attn_rows",
    )(q, k_rows, v_rows).reshape(b * t, MEM_WIDTH)


def _merge_kernel(x_ref, o1_ref, o2_ref, o3_ref, l1_ref, l2_ref, l3_ref, ob_ref, om_ref, *rest):
    _merge_body((), x_ref, o1_ref, o2_ref, o3_ref, l1_ref, l2_ref, l3_ref, ob_ref, om_ref, *rest)


def _merge_conv_kernel(x_ref, o1_ref, o2_ref, o3_ref, l1_ref, l2_ref, l3_ref, halo_ref, u_ref, om_ref,
                       g1_ref, wg_ref, bg_ref, wa_ref, wb_ref, wm_ref, wo_ref, g2_ref, wrh_ref, wrc_ref, br_ref,
                       cw_ref, cb_ref, cg_ref, cbeta_ref, x1_ref, h2_ref, idx_ref, gw_ref,
                       mix_ref, ctx_ref, ob_ref, *, tiles_per_seq):
    halo = jnp.where(pl.program_id(0) % tiles_per_seq == 0, 0.0, halo_ref[...])
    conv_parts = _conv_module_parts(halo, u_ref, cw_ref, cb_ref, cg_ref, cbeta_ref, ob_ref, ctx_ref, 64)
    _merge_body(conv_parts, x_ref, o1_ref, o2_ref, o3_ref, l1_ref, l2_ref, l3_ref, ob_ref, om_ref,
                g1_ref, wg_ref, bg_ref, wa_ref, wb_ref, wm_ref, wo_ref, g2_ref, wrh_ref, wrc_ref, br_ref,
                x1_ref, h2_ref, idx_ref, gw_ref, mix_ref)


def _merge_body(conv_parts, x_ref, o1_ref, o2_ref, o3_ref, l1_ref, l2_ref, l3_ref, ob_ref, om_ref,
                g1_ref, wg_ref, bg_ref, wa_ref, wb_ref, wm_ref, wo_ref, g2_ref, wrh_ref, wrc_ref, br_ref,
                x1_ref, h2_ref, idx_ref, gw_ref, mix_ref):
    tm = x_ref.shape[0]
    x = x_ref[...]
    hb = _rms(x, g1_ref[...]).astype(BF16)
    la, lb, lc = l1_ref[...], l2_ref[...], l3_ref[...]
    m = jnp.maximum(la, jnp.maximum(lb, lc))
    ea, eb, ec = jnp.exp(la - m), jnp.exp(lb - m), jnp.exp(lc - m)
    oa = ((ea * o1_ref[...] + eb * o2_ref[...] + ec * o3_ref[...]) / (ea + eb + ec)).astype(BF16)
    om = om_ref[...]
    chunks = [slice(j * 256, (j + 1) * 256) for j in range(4)]

    def gate(branch, j):
        c0 = branch * D_MODEL + j * 256
        return jax.nn.sigmoid(_dot(hb, wg_ref[:, c0:c0 + 256]) + bg_ref[:, c0:c0 + 256])

    per_j = -(-len(conv_parts) // 4)
    partial, gate_b = [], []
    for j, cs in enumerate(chunks):
        for part in conv_parts[j * per_j:(j + 1) * per_j]:
            part()
        partial.append(gate(0, j) * _dot(oa, wa_ref[:, cs]) + gate(2, j) * _dot(om, wm_ref[:, cs]))
        gate_b.append(gate(1, j))
    ob = ob_ref[...]
    for j, cs in enumerate(chunks):
        mix_ref[:, cs] = (partial[j] + gate_b[j] * _dot(ob, wb_ref[:, cs])).astype(BF16)
    x1 = x + _dot(mix_ref[...], wo_ref[...])
    x1_ref[...] = x1
    h2 = _rms(x1, g2_ref[...])
    for j in range(ROW_TILE):
        h2_ref[pl.ds(j, tm, stride=ROW_TILE), :] = h2[:, j * LANES:(j + 1) * LANES]
    hi = h2.astype(BF16)
    lo = (h2 - hi.astype(F32)).astype(BF16)
    both = _dot(hi, wrc_ref[...])
    logits = both[:, :LANES] + both[:, LANES:] + _dot(lo, wrh_ref[...]) + br_ref[...]
    lane = lax.broadcasted_iota(I32, (tm, LANES), 1)
    logits = jnp.where(lane < N_EXPERTS, logits, NEG_INF)
    vals, idxs = [], []
    for _ in range(TOP_K):
        mk = jnp.max(logits, axis=1, keepdims=True)
        ik = jnp.min(jnp.where(logits == mk, lane, LANES), axis=1, keepdims=True)
        vals.append(mk)
        idxs.append(ik)
        logits = jnp.where(lane == ik, NEG_INF, logits)
    es = [jnp.exp(v - vals[0]) for v in vals]
    den = es[0] + es[1] + es[2] + es[3]
    l4 = lax.broadcasted_iota(I32, (tm, TOP_K), 1)
    idx_out = jnp.zeros((tm, TOP_K), I32)
    gw_out = jnp.zeros((tm, TOP_K), F32)
    for k in range(TOP_K):
        idx_out = jnp.where(l4 == k, idxs[k], idx_out)
        gw_out = jnp.where(l4 == k, es[k] / den, gw_out)
    idx_ref[...] = idx_out
    gw_ref[...] = gw_out


def _merge(x, os_, lses, ob, om, g1, wg, bg, wa, wb, wm, wo, g2, wrh, wrc, br, tm, conv=None):
    n = x.shape[0]
    assert n % tm == 0
    row = lambda i: (i, 0)
    const = lambda i: (0, 0)
    rows = lambda w: pl.BlockSpec((tm, w), row)
    full = lambda a: pl.BlockSpec(a.shape, const)
    weights = (g1, wg, bg, wa, wb, wm, wo, g2, wrh, wrc, br)
    scratch = [pltpu.VMEM((tm, D_MODEL), BF16)]
    if conv is None:
        body, b_specs, b_args, extra = _merge_kernel, [rows(CONV_CH)], (ob,), ()
    else:
        *conv_params, seq_len = conv
        assert seq_len % tm == 0 and tm % CONV_HALO == 0
        ratio = tm // CONV_HALO
        body = functools.partial(_merge_conv_kernel, tiles_per_seq=seq_len // tm)
        b_specs = [pl.BlockSpec((CONV_HALO, CONV_CH), lambda i: (jnp.maximum(i * ratio - 1, 0), 0)), rows(CONV_CH)]
        b_args, extra = (ob, ob), tuple(conv_params)
        scratch += [pltpu.VMEM((CONV_HALO + tm, CONV_CH), F32), pltpu.VMEM((tm, CONV_CH), BF16)]
    return pl.pallas_call(
        body,
        grid=(n // tm,),
        in_specs=[rows(D_MODEL)] + [rows(GROUP_W)] * 6 + b_specs + [rows(MEM_WIDTH)]
        + [full(a) for a in weights + extra],
        out_specs=[rows(D_MODEL), pl.BlockSpec((tm * ROW_TILE, LANES), row), rows(TOP_K), rows(TOP_K)],
        out_shape=[jax.ShapeDtypeStruct((n, D_MODEL), F32), jax.ShapeDtypeStruct((n * ROW_TILE, LANES), F32),
                   jax.ShapeDtypeStruct((n, TOP_K), I32), jax.ShapeDtypeStruct((n, TOP_K), F32)],
        scratch_shapes=scratch,
        compiler_params=_params(("arbitrary",), 56),
        name="merge_router",
    )(x, *os_, *lses, *b_args, om, *weights, *extra)


def _active_part(refs, bounds, i):
    value = refs[0][...]
    for ref, (first, _) in zip(refs[1:], bounds[1:]):
        value = jnp.where(i >= first, ref[...], value)
    return value


def _route_kernel(*refs, block_rows, bounds):
    idx_refs, (dest_ref, bexp_ref, seg_ref, cnt_ref, carry_ref) = refs[:len(bounds)], refs[len(bounds):]
    phase, i = pl.program_id(0), pl.program_id(1)
    tq = idx_refs[0].shape[0]
    nb = bexp_ref.shape[0]
    idx = _active_part(idx_refs, bounds, i)
    lane = lax.broadcasted_iota(I32, (tq, LANES), 1)
    member = jnp.zeros((tq, LANES), F32)
    for k in range(TOP_K):
        member = member + (lane == idx[:, k:k + 1]).astype(F32)
    tile_counts = jnp.sum(member, axis=0, keepdims=True)

    @pl.when(phase == 0)
    def _():
        @pl.when(i == 0)
        def _():
            cnt_ref[...] = jnp.zeros_like(cnt_ref)

        cnt_ref[...] += tile_counts
        dest_ref[...] = jnp.zeros_like(dest_ref)
        bexp_ref[...] = jnp.zeros_like(bexp_ref)
        seg_ref[...] = jnp.zeros_like(seg_ref)

    @pl.when(phase == 1)
    def _():
        @pl.when(i == 0)
        def _():
            carry_ref[...] = jnp.zeros_like(carry_ref)

        counts = jnp.broadcast_to(cnt_ref[...], (8, LANES))
        padded = jnp.floor((counts + (block_rows - 1)) * (1.0 / block_rows)) * block_rows
        lane8 = lax.broadcasted_iota(I32, (8, LANES), 1)
        ends = padded
        for s in (1, 2, 4, 8, 16, 32, 64):
            ends = ends + jnp.where(lane8 >= s, pltpu.roll(ends, s, 1), 0.0)
        starts = (ends - padded)[0:1, :]
        r = lax.broadcasted_iota(I32, (tq, tq), 0)
        c = lax.broadcasted_iota(I32, (tq, tq), 1)
        earlier = _dot((c < r).astype(BF16), member.astype(BF16)) + carry_ref[...]
        carry_ref[...] += tile_counts
        base = earlier + starts
        l4 = lax.broadcasted_iota(I32, (tq, TOP_K), 1)
        dest = jnp.zeros((tq, TOP_K), F32)
        for k in range(TOP_K):
            dk = jnp.sum(jnp.where(lane == idx[:, k:k + 1], base, 0.0), axis=1, keepdims=True)
            dest = jnp.where(l4 == k, dk, dest)
        dest_ref[...] = dest.astype(I32)
        first_row = (lax.broadcasted_iota(I32, (nb, LANES), 0) * block_rows).astype(F32)
        lane_nb = lax.broadcasted_iota(I32, (nb, LANES), 1)
        done = jnp.where((ends[0:1, :] <= first_row) & (lane_nb < N_EXPERTS), 1.0, 0.0)
        bexp_ref[...] = jnp.minimum(jnp.sum(done, axis=1, keepdims=True), N_EXPERTS - 1.0).astype(I32)
        row8 = lax.broadcasted_iota(I32, (8, LANES), 0)
        seg = jnp.where(row8 == 0, ends - padded + counts, jnp.where(row8 == 1, ends, 0.0))
        seg_ref[...] = seg.astype(I32)


def _route(idx_parts, block_rows, n_blocks, tq):
    specs, bounds, steps = _part_specs(idx_parts, (tq, TOP_K))
    n = steps * tq
    nb_pad = -(-n_blocks // 8) * 8
    dest, bexp, seg = pl.pallas_call(
        functools.partial(_route_kernel, block_rows=block_rows, bounds=bounds),
        grid=(2, steps),
        in_specs=specs,
        out_specs=[pl.BlockSpec((tq, TOP_K), lambda p, i: (i * p, 0)),
                   pl.BlockSpec((nb_pad, 1), lambda p, i: (0, 0)),
                   pl.BlockSpec((8, LANES), lambda p, i: (0, 0))],
        out_shape=[jax.ShapeDtypeStruct((n, TOP_K), I32), jax.ShapeDtypeStruct((nb_pad, 1), I32),
                   jax.ShapeDtypeStruct((8, LANES), I32)],
        scratch_shapes=[pltpu.VMEM((1, LANES), F32), pltpu.VMEM((1, LANES), F32)],
        compiler_params=_params(("arbitrary", "arbitrary"), 40),
        name="route",
    )(*idx_parts)
    return dest.reshape(n * TOP_K), bexp.reshape(nb_pad)[:n_blocks], seg


def _tile_rows(index):
    if isinstance(index, int):
        return pl.ds(index * ROW_TILE, ROW_TILE)
    return pl.ds(pl.multiple_of(index * ROW_TILE, ROW_TILE), ROW_TILE)


def _part_specs(parts, block):
    specs, bounds, first = [], [], 0
    for a in parts:
        assert a.shape[0] % block[0] == 0
        count = a.shape[0] // block[0]
        specs.append(pl.BlockSpec(block, lambda *g, first=first, count=count: (jnp.clip(g[-1] - first, 0, count - 1), 0)))
        bounds.append((first, count))
        first += count
    return specs, tuple(bounds), first


def _dispatch_kernel(seg_ref, dest_ref, *refs, bounds):
    h_refs, (xs_ref, zero_ref, sem, zero_sem) = refs[:len(bounds)], refs[len(bounds):]
    tm = h_refs[0].shape[0] // ROW_TILE
    i = pl.program_id(0)

    @pl.when(pl.program_id(0) == 0)
    def _():
        zero_ref[...] = jnp.zeros_like(zero_ref)

        def pad_copies(e):
            slot, length = seg_ref[0, e], seg_ref[1, e] - seg_ref[0, e]
            pieces = []
            size = zero_ref.shape[0] // ROW_TILE
            while size >= 1:
                present = (length & size) != 0
                rows = pl.ds(pl.multiple_of(slot * ROW_TILE, ROW_TILE), size * ROW_TILE)
                pieces.append((present, pltpu.make_async_copy(zero_ref.at[pl.ds(0, size * ROW_TILE)],
                                                              xs_ref.at[rows], zero_sem)))
                slot = slot + jnp.where(present, size, 0)
                size //= 2
            return pieces

        def start_expert(e):
            for present, piece in pad_copies(e):
                pl.when(present)(piece.start)

        def wait_expert(e):
            for present, piece in pad_copies(e):
                pl.when(present)(piece.wait)

        def expert(e, carry):
            start_expert(e)
            wait_expert(e - 1)
            return carry

        start_expert(0)
        lax.fori_loop(1, N_EXPERTS, expert, 0)
        wait_expert(N_EXPERTS - 1)

    for h_ref, (first, count) in zip(h_refs, bounds):
        @pl.when((i >= first) & (i < first + count))
        def _(h_ref=h_ref):
            def issue(r, carry):
                for k in range(TOP_K):
                    pltpu.make_async_copy(h_ref.at[_tile_rows(r)], xs_ref.at[_tile_rows(dest_ref[r * TOP_K + k])],
                                          sem).start(priority=k % 2)
                return carry

            for r in range(tm):
                issue(r, 0)

    for k in range(TOP_K):
        pltpu.make_async_copy(h_refs[0], xs_ref.at[pl.ds(0, tm * ROW_TILE)], sem).wait()


def _dispatch(h2t_parts, dest, seg, n_slots, tm, block_rows):
    specs, bounds, steps = _part_specs(h2t_parts, (tm * ROW_TILE, LANES))
    assert block_rows & (block_rows - 1) == 0
    return pl.pallas_call(
        functools.partial(_dispatch_kernel, bounds=bounds),
        grid=(steps,),
        in_specs=[pl.BlockSpec(memory_space=pltpu.SMEM),
                  pl.BlockSpec((tm * TOP_K,), lambda i: (i,), memory_space=pltpu.SMEM)] + specs,
        out_specs=pl.BlockSpec(memory_space=pl.ANY),
        out_shape=jax.ShapeDtypeStruct((n_slots * ROW_TILE, LANES), F32),
        scratch_shapes=[pltpu.VMEM((block_rows // 2 * ROW_TILE, LANES), F32), pltpu.SemaphoreType.DMA,
                        pltpu.SemaphoreType.DMA],
        compiler_params=_params(("arbitrary",), 32),
        name="dispatch",
    )(seg, dest, *h2t_parts)


def _moe_kernel(be_ref, seg_ref, xs_ref, wg_ref, bg_ref, wu_ref, bu_ref, wd_ref, bd_ref, yb_ref, wg_s, wu_s, wd_s):
    i = pl.program_id(0)
    bm = xs_ref.shape[0] // ROW_TILE
    used_rows = seg_ref[LANES + N_EXPERTS - 1]

    @pl.when(i * bm < used_rows)
    def _():
        changed = (i == 0) | (be_ref[i] != be_ref[jnp.maximum(i - 1, 0)])

        @pl.when(changed)
        def _():
            for r0 in range(0, D_MODEL, 128):
                rs = pl.ds(r0, 128)
                wg_s[rs, :] = wg_ref[rs, :].astype(BF16)
                wu_s[rs, :] = wu_ref[rs, :].astype(BF16)
                wd_s[rs, :] = wd_ref[rs, :].astype(BF16)

        xb = jnp.concatenate([xs_ref[pl.ds(j, bm, stride=ROW_TILE), :] for j in range(ROW_TILE)],
                             axis=1).astype(BF16)
        chunks = [slice(c * 256, (c + 1) * 256) for c in range(D_FF // 256)]
        gs = [_dot(xb, wg_s[:, cs]) + bg_ref[:, cs] for cs in chunks]
        us = [_dot(xb, wu_s[:, cs]) + bu_ref[:, cs] for cs in chunks]
        acts = []
        for g, u in zip(gs, us):
            g = jnp.minimum(g, SWIGLU_LIMIT)
            u = jnp.clip(u, -SWIGLU_LIMIT, SWIGLU_LIMIT)
            acts.append((g * jax.nn.sigmoid(SWIGLU_ALPHA * g) * (u + 1.0)).astype(BF16))
        acc = jnp.zeros((bm, D_MODEL), F32) + bd_ref[...]
        for act, cs in zip(acts, chunks):
            acc = acc + _dot(act, wd_s[cs, :])
        for j in range(ROW_TILE):
            yb_ref[pl.ds(j, bm, stride=ROW_TILE), :] = acc[:, j * LANES:(j + 1) * LANES]


def _moe(xs, bexp, seg, wg, bg, wu, bu, wd, bd, bm):
    n_slots = xs.shape[0] // ROW_TILE
    assert n_slots % bm == 0
    wspec = pl.BlockSpec((None, D_MODEL, D_FF), lambda i, be, sg: (be[i], 0, 0))
    bspec = pl.BlockSpec((None, 1, D_FF), lambda i, be, sg: (be[i], 0, 0))
    rows = pl.BlockSpec((bm * ROW_TILE, LANES), lambda i, be, sg: (i, 0))
    grid_spec = pltpu.PrefetchScalarGridSpec(
        num_scalar_prefetch=2,
        grid=(n_slots // bm,),
        in_specs=[rows, wspec, bspec, wspec, bspec, wspec, bspec],
        out_specs=rows,
        scratch_shapes=[pltpu.VMEM((D_MODEL, D_FF), BF16)] * 3,
    )
    return pl.pallas_call(
        _moe_kernel,
        grid_spec=grid_spec,
        out_shape=jax.ShapeDtypeStruct((n_slots * ROW_TILE, LANES), F32),
        compiler_params=_params(("arbitrary",), 56),
        name="moe_experts",
    )(bexp, seg.reshape(-1), xs, wg, bg.reshape(N_EXPERTS, 1, D_FF), wu, bu.reshape(N_EXPERTS, 1, D_FF),
      wd, bd.reshape(N_EXPERTS, 1, D_MODEL))


def _combine_kernel(dest_ref, dest_next_ref, yb_ref, *refs, bounds):
    nparts = len(bounds)
    x1_refs, gw_refs, y_refs, (rows_ref, sems) = (refs[:nparts], refs[nparts:2 * nparts], refs[2 * nparts:3 * nparts],
                                                  refs[3 * nparts:])
    tm = x1_refs[0].shape[0]
    i = pl.program_id(0)
    slot = i % 2

    def issue(dref, buf, unrolled):
        def body(r, carry):
            for k in range(TOP_K):
                pltpu.make_async_copy(yb_ref.at[_tile_rows(dref[r * TOP_K + k])], rows_ref.at[buf, k, _tile_rows(r)],
                                      sems.at[buf]).start(priority=k % 2)
            return carry

        if unrolled:
            for r in range(tm):
                body(r, 0)
        else:
            lax.fori_loop(0, tm, body, 0, unroll=4)

    @pl.when(i == 0)
    def _():
        issue(dest_ref, slot, False)

    @pl.when(i + 1 < pl.num_programs(0))
    def _():
        issue(dest_next_ref, 1 - slot, True)

    for k in range(TOP_K):
        pltpu.make_async_copy(yb_ref.at[pl.ds(0, tm * ROW_TILE)], rows_ref.at[slot, k], sems.at[slot]).wait()
    gw = _active_part(gw_refs, bounds, i)
    for x1_ref, y_ref, (first, count) in zip(x1_refs, y_refs, bounds):
        @pl.when((i >= first) & (i < first + count))
        def _(x1_ref=x1_ref, y_ref=y_ref):
            for j in range(ROW_TILE):
                ls = slice(j * LANES, (j + 1) * LANES)
                y = x1_ref[:, ls]
                for k in range(TOP_K):
                    y = y + gw[:, k:k + 1] * rows_ref[slot, k, pl.ds(j, tm, stride=ROW_TILE), :]
                y_ref[:, ls] = y


def _combine(x1_parts, gw_parts, dest, yb, tm):
    specs, bounds, steps = _part_specs(x1_parts, (tm, D_MODEL))
    gw_specs, gw_bounds, _ = _part_specs(gw_parts, (tm, TOP_K))
    assert gw_bounds == bounds
    return pl.pallas_call(
        functools.partial(_combine_kernel, bounds=bounds),
        grid=(steps,),
        in_specs=[pl.BlockSpec((tm * TOP_K,), lambda i: (i,), memory_space=pltpu.SMEM),
                  pl.BlockSpec((tm * TOP_K,), lambda i: (jnp.minimum(i + 1, steps - 1),), memory_space=pltpu.SMEM),
                  pl.BlockSpec(memory_space=pl.ANY)] + specs + gw_specs,
        out_specs=specs,
        out_shape=[jax.ShapeDtypeStruct(a.shape, F32) for a in x1_parts],
        scratch_shapes=[pltpu.VMEM((2, TOP_K, tm * ROW_TILE, LANES), F32), pltpu.SemaphoreType.DMA((2,))],
        compiler_params=_params(("arbitrary",), 32),
        name="combine",
    )(dest, dest, yb, *x1_parts, *gw_parts)


def _rope_tables(positions):
    half = ROPE_DIM // 2
    inv = jnp.power(jnp.float32(ROPE_THETA), -jnp.arange(half, dtype=F32) / half)
    ang = positions.astype(F32)[:, None] * inv[None, :]
    cos, sin = jnp.cos(ang), jnp.sin(ang)
    t = positions.shape[0]
    z8 = jnp.zeros((t, half), F32)
    rest0 = jnp.zeros((t, HEAD_DIM_A - ROPE_DIM), F32)
    a = jnp.concatenate([cos, cos, jnp.ones((t, HEAD_DIM_A - ROPE_DIM), F32)], axis=1)
    bp = jnp.concatenate([z8, sin, rest0], axis=1)
    bm = jnp.concatenate([-sin, z8, rest0], axis=1)
    return tuple(jnp.tile(m, (1, LANES // HEAD_DIM_A)) for m in (a, bp, bm))


def _segment_mean_matrix(seg):
    i = jnp.arange(256)
    return jnp.where((i[:, None] // seg) == (i[None, :] // seg), 1.0 / seg, 0.0).astype(BF16)


def _layer_weights(l, norm1_g, w_in, b_gates, qn_a, kn_a, conv_w, conv_b, cn_g, cn_b, qn_m,
                   w_proj_a, w_proj_b, w_proj_m, w_out, norm2_g, w_router, b_router,
                   w_gate, b_gate, w_up, b_up, w_down, b_down):
    wr = jnp.pad(w_router[l], ((0, 0), (0, LANES - N_EXPERTS)))
    wrh = wr.astype(BF16)
    return dict(
        g1=norm1_g[l][None, :],
        w1=w_in[l][:, :W1_COLS].astype(BF16),
        wgates=w_in[l][:, W1_COLS:].astype(BF16),
        bgates=b_gates[l][None, :],
        qg=jnp.tile(qn_a[l], 256 // HEAD_DIM_A)[None, :],
        kg=jnp.tile(kn_a[l], 256 // HEAD_DIM_A)[None, :],
        mg=jnp.tile(qn_m[l], 256 // MEM_HEAD_DIM)[None, :],
        conv_w=conv_w[l], conv_b=conv_b[l][None, :], cn_g=cn_g[l][None, :], cn_b=cn_b[l][None, :],
        wa=w_proj_a[l].astype(BF16), wb=w_proj_b[l].astype(BF16), wm=w_proj_m[l].astype(BF16),
        wo=w_out[l].astype(BF16), g2=norm2_g[l][None, :],
        wrh=wrh, wrc=jnp.concatenate([wrh, (wr - wrh.astype(F32)).astype(BF16)], axis=1),
        br=jnp.pad(b_router[l], (0, LANES - N_EXPERTS))[None, :],
        w_gate=w_gate[l], b_gate=b_gate[l], w_up=w_up[l], b_up=b_up[l], w_down=w_down[l], b_down=b_down[l],
        seg64=_segment_mean_matrix(HEAD_DIM_A), seg128=_segment_mean_matrix(MEM_HEAD_DIM),
    )


def _merge_group(x2d, os_, lses, ob, om, w, tm, conv_seq_len=None):
    conv = None if conv_seq_len is None else (w["conv_w"], w["conv_b"], w["cn_g"], w["cn_b"], conv_seq_len)
    return _merge(x2d, os_, lses, ob, om, w["g1"], w["wgates"], w["bgates"], w["wa"], w["wb"], w["wm"],
                  w["wo"], w["g2"], w["wrh"], w["wrc"], w["br"], tm, conv)


MOE_ROWS = 512
ROUTE_TILE = 1024
DISPATCH_TILE = 512
COMBINE_TILE = 256


def _moe_groups(merged, w):
    x1s, h2ts, idxs, gws = zip(*merged)
    n = sum(a.shape[0] for a in idxs)
    n_blocks = -(-(n * TOP_K) // MOE_ROWS) + N_EXPERTS
    dest, bexp, seg = _route(idxs, MOE_ROWS, n_blocks, ROUTE_TILE)
    xs = _dispatch(h2ts, dest, seg, n_blocks * MOE_ROWS, DISPATCH_TILE, MOE_ROWS)
    yb = _moe(xs, bexp, seg, w["w_gate"], w["b_gate"], w["w_up"], w["b_up"], w["w_down"], w["b_down"], MOE_ROWS)
    return _combine(x1s, gws, dest, yb, COMBINE_TILE)


def _prompt_layer(x, mem, w, mem_norm_g, w_mem_kv, kn_m):
    b, s, _ = x.shape
    n = b * s
    x2d = x.reshape(n, D_MODEL)
    tabs = _rope_tables(jnp.arange(s, dtype=I32))
    q, k, v, u, qm = _in_proj(x2d, w["g1"], w["w1"], w["qg"], w["kg"], w["mg"], tabs, w["seg64"], w["seg128"], 512, F32)
    q3, k3, v3 = (t.reshape(b, s, WIDTH_A) for t in (q, k, v))
    os_, lses, caches = [], [], []
    for g, (win, dil) in enumerate(SWA_GROUPS):
        o, lse = _swa_prompt(q3, k3, v3, g, dil)
        os_.append(o)
        lses.append(lse)
        cs = slice(g * GROUP_W, (g + 1) * GROUP_W)
        caches += [k3[:, s - win:, cs].reshape(b, win, 4, HEAD_DIM_A), v3[:, s - win:, cs].reshape(b, win, 4, HEAD_DIM_A)]
    u3 = u.reshape(b, s, CONV_CH)
    mk, mv = _mem_kv(mem, mem_norm_g[None, :], w_mem_kv.astype(BF16), kn_m[None, :])
    om = _mem_attn(qm.reshape(b, s, MEM_WIDTH), mk, mv, 512)
    merged = _merge_group(x2d, os_, lses, u, om, w, 512, conv_seq_len=s)
    state = caches + [u3[:, s - (CONV_WIDTH - 1):], mk.reshape(b, MEM_LEN, MEM_HEADS, MEM_HEAD_DIM),
                      mv.reshape(b, MEM_LEN, MEM_HEADS, MEM_HEAD_DIM)]
    return merged, state


def _sample_layer(x, mem_k, mem_v, bufs, conv_state, w):
    b, t, _ = x.shape
    n = b * t
    x2d = x.reshape(n, D_MODEL)
    tabs = _rope_tables(jnp.tile(PAST_LEN + jnp.arange(t, dtype=I32), b))
    q, k, v, u, qm = _in_proj(x2d, w["g1"], w["w1"], w["qg"], w["kg"], w["mg"], tabs, w["seg64"], w["seg128"], 256, BF16)
    q3, k3, v3 = (a.reshape(b, t, WIDTH_A) for a in (q, k, v))
    os_, lses, caches = [], [], []
    for g, (win, dil) in enumerate(SWA_GROUPS):
        to_cm = lambda a: jnp.transpose(a, (0, 2, 3, 1)).reshape(b, GROUP_W, win)
        from_cm = lambda a: jnp.transpose(a.reshape(b, 4, HEAD_DIM_A, win), (0, 3, 1, 2))
        o, lse, kto, vto = _swa_sample(q3, k3, v3, to_cm(bufs[2 * g]), to_cm(bufs[2 * g + 1]), g, win, dil)
        os_.append(o)
        lses.append(lse)
        caches += [from_cm(kto), from_cm(vto)]
    u3 = u.reshape(b, t, CONV_CH)
    halo = jnp.pad(conv_state, ((0, 0), (CONV_HALO - (CONV_WIDTH - 1), 0), (0, 0)))
    ob = _conv_branch(u3, halo, w["conv_w"], w["conv_b"], w["cn_g"], w["cn_b"], t, False)
    om = _mem_attn_rows(qm.reshape(b, t, MEM_WIDTH), mem_k.reshape(b, MEM_LEN * MEM_HEADS, MEM_HEAD_DIM),
                        mem_v.reshape(b, MEM_LEN * MEM_HEADS, MEM_HEAD_DIM))
    merged = _merge_group(x2d, os_, lses, ob, om, w, 256)
    new_conv = jnp.concatenate([conv_state, u3], axis=1)[:, t:]
    return merged, caches + [new_conv]


def kernel(x_prompt, x_sample, mem_prompt, cache_swa_k_w128, cache_swa_v_w128, cache_swa_k_w512, cache_swa_v_w512, cache_swa_k_w2048, cache_swa_v_w2048, state_conv, cache_mem_k, cache_mem_v, norm1_g, w_in, b_gates, qn_a, kn_a, conv_w, conv_b, cn_g, cn_b, mem_norm_g, w_mem_kv, qn_m, kn_m, w_proj_a, w_proj_b, w_proj_m, w_out, norm2_g, w_router, b_router, w_gate, b_gate, w_up, b_up, w_down, b_down):
    depth = norm1_g.shape[0]
    y_p, y_s = x_prompt, x_sample
    st_p, st_s = [], []
    for l in range(depth):
        w = _layer_weights(l, norm1_g, w_in, b_gates, qn_a, kn_a, conv_w, conv_b, cn_g, cn_b, qn_m,
                           w_proj_a, w_proj_b, w_proj_m, w_out, norm2_g, w_router, b_router,
                           w_gate, b_gate, w_up, b_up, w_down, b_down)
        merged_p, state_p = _prompt_layer(y_p, mem_prompt, w, mem_norm_g[l], w_mem_kv[l], kn_m[l])
        bufs = (cache_swa_k_w128[l], cache_swa_v_w128[l], cache_swa_k_w512[l], cache_swa_v_w512[l],
                cache_swa_k_w2048[l], cache_swa_v_w2048[l])
        merged_s, state_s = _sample_layer(y_s, cache_mem_k[l], cache_mem_v[l], bufs, state_conv[l], w)
        y_p2d, y_s2d = _moe_groups((merged_p, merged_s), w)
        y_p, y_s = y_p2d.reshape(y_p.shape), y_s2d.reshape(y_s.shape)
        st_p.append(state_p)
        st_s.append(state_s)
    outs_p = [jnp.stack(a) for a in zip(*st_p)]
    outs_s = [jnp.stack(a) for a in zip(*st_s)]
    return (y_p, y_s, *outs_p, *outs_s)
```

```python
import functools

import jax
import jax.numpy as jnp
from jax import lax
from jax.experimental import pallas as pl
from jax.experimental.pallas import tpu as pltpu

F32 = jnp.float32
BF16 = jnp.bfloat16
I32 = jnp.int32

D_MODEL = 1024
HEAD_DIM_A = 64
GROUP_W = 256
N_GROUPS = 3
WIDTH_A = N_GROUPS * GROUP_W
SWA_GROUPS = ((128, 1), (512, 4), (2048, 16))
SWA_BLOCK = 128
ROPE_DIM = 16
ROPE_THETA = 500000.0
CONV_CH = 512
CONV_WIDTH = 31
CONV_HALO = 32
MEM_LEN = 256
MEM_HEADS = 4
MEM_HEAD_DIM = 128
MEM_WIDTH = 512
N_EXPERTS = 32
TOP_K = 4
D_FF = 1024
SWIGLU_LIMIT = 7.0
SWIGLU_ALPHA = 1.702
EPS = 1e-6
PAST_LEN = 8192
LANES = 128
ROW_TILE = D_MODEL // LANES
W1_COLS = 3 * WIDTH_A + 2 * CONV_CH + MEM_WIDTH
NEG_INF = float("-inf")
MIB = 2 ** 20


def _params(semantics, vmem_mib):
    return pltpu.CompilerParams(dimension_semantics=semantics, vmem_limit_bytes=vmem_mib * MIB)


def _rms(x, gain):
    return x * lax.rsqrt(jnp.mean(x * x, axis=-1, keepdims=True) + EPS) * gain


def _dot(a, b):
    return jnp.dot(a, b, preferred_element_type=F32)


def _dot_nt(a, b):
    return lax.dot_general(a, b, (((1,), (1,)), ((), ())), preferred_element_type=F32)


def _in_proj_kernel(x_ref, g1_ref, w_ref, qg_ref, kg_ref, mg_ref, ra_ref, rp_ref, rm_ref, s64_ref, s128_ref,
                    q_ref, k_ref, v_ref, u_ref, qm_ref):
    hb = _rms(x_ref[...], g1_ref[...]).astype(BF16)
    ra, rp, rm = ra_ref[...], rp_ref[...], rm_ref[...]

    def proj(c0):
        return _dot(hb, w_ref[:, c0:c0 + 256])

    def head_norm(p, seg_ref, gain):
        ms = _dot((p * p).astype(BF16), seg_ref[...])
        return p * lax.rsqrt(ms + EPS) * gain

    def rope(p):
        outs = []
        for j in range(2):
            pj = p[:, j * LANES:(j + 1) * LANES]
            outs.append(pj * ra + pltpu.roll(pj, 8, 1) * rp + pltpu.roll(pj, LANES - 8, 1) * rm)
        return jnp.concatenate(outs, axis=1)

    for c in range(3):
        cs = slice(c * 256, (c + 1) * 256)
        q_ref[:, cs] = rope(head_norm(proj(c * 256), s64_ref, qg_ref[...])).astype(q_ref.dtype)
        k_ref[:, cs] = rope(head_norm(proj(WIDTH_A + c * 256), s64_ref, kg_ref[...]))
        v_ref[:, cs] = proj(2 * WIDTH_A + c * 256)
    for c in range(2):
        cs = slice(c * 256, (c + 1) * 256)
        a = proj(3 * WIDTH_A + c * 256)
        gate = proj(3 * WIDTH_A + CONV_CH + c * 256)
        u_ref[:, cs] = a * jax.nn.sigmoid(gate)
        qm_ref[:, cs] = head_norm(proj(3 * WIDTH_A + 2 * CONV_CH + c * 256), s128_ref, mg_ref[...]).astype(BF16)


def _in_proj(x, g1, w1, qg, kg, mg, tabs, seg64, seg128, tm, q_dtype):
    n = x.shape[0]
    t_rows = tabs[0].shape[0]
    assert n % tm == 0 and t_rows % tm == 0
    t_blocks = t_rows // tm
    row = lambda i: (i, 0)
    const = lambda i: (0, 0)
    tab = lambda i: (i % t_blocks, 0)
    return pl.pallas_call(
        _in_proj_kernel,
        grid=(n // tm,),
        in_specs=[
            pl.BlockSpec((tm, D_MODEL), row),
            pl.BlockSpec((1, D_MODEL), const),
            pl.BlockSpec((D_MODEL, W1_COLS), const),
            pl.BlockSpec((1, 256), const), pl.BlockSpec((1, 256), const), pl.BlockSpec((1, 256), const),
            pl.BlockSpec((tm, LANES), tab), pl.BlockSpec((tm, LANES), tab), pl.BlockSpec((tm, LANES), tab),
            pl.BlockSpec((256, 256), const), pl.BlockSpec((256, 256), const),
        ],
        out_specs=[
            pl.BlockSpec((tm, WIDTH_A), row), pl.BlockSpec((tm, WIDTH_A), row), pl.BlockSpec((tm, WIDTH_A), row),
            pl.BlockSpec((tm, CONV_CH), row), pl.BlockSpec((tm, MEM_WIDTH), row),
        ],
        out_shape=[
            jax.ShapeDtypeStruct((n, WIDTH_A), q_dtype), jax.ShapeDtypeStruct((n, WIDTH_A), F32),
            jax.ShapeDtypeStruct((n, WIDTH_A), F32), jax.ShapeDtypeStruct((n, CONV_CH), F32),
            jax.ShapeDtypeStruct((n, MEM_WIDTH), BF16),
        ],
        compiler_params=_params(("arbitrary",), 48),
        name="in_proj",
    )(x, g1, w1, qg, kg, mg, *tabs, seg64, seg128)


def _swa_prompt_kernel(q_ref, k_ref, v_ref, o_ref, lse_ref, q_s, k_s, v_s, o_s, l_s, *, dil, nsub):
    n = pl.program_id(1)
    span = SWA_BLOCK * dil
    cur = n % 2
    prv = 1 - cur
    for s in range(2):
        ls = slice(s * LANES, (s + 1) * LANES)
        q_s[s] = q_ref[:, ls]
        k_s[cur, s] = k_ref[:, ls]
        v_s[cur, s] = v_ref[:, ls]

    @pl.when(n == 0)
    def _():
        k_s[prv] = jnp.zeros(k_s.shape[1:], F32)
        v_s[prv] = jnp.zeros(v_s.shape[1:], F32)

    shape = (SWA_BLOCK, 2 * SWA_BLOCK)
    col = lax.broadcasted_iota(I32, shape, 1)
    off = lax.broadcasted_iota(I32, shape, 0) + SWA_BLOCK - col
    band = (off >= 0) & (off <= SWA_BLOCK)
    band_first = band & ((n > 0) | (col >= SWA_BLOCK))
    head = col // HEAD_DIM_A

    hms = [head == h for h in range(4)]

    def both(ref, lead, rows):
        return jnp.concatenate([ref[(*lead, s, rows, slice(None))] for s in range(2)], axis=1)

    def blocks(items):
        rows_of, masks, qs, kks, vvs = [], [], [], [], []
        for j, r in items:
            rows = pl.ds(j * span + r, SWA_BLOCK, stride=dil)
            before = (prv, pl.ds((nsub - 1) * span + r, SWA_BLOCK, stride=dil)) if j == 0 else \
                     (cur, pl.ds((j - 1) * span + r, SWA_BLOCK, stride=dil))
            rows_of.append(rows)
            masks.append(band_first if j == 0 else band)
            qs.append(both(q_s, (), rows).astype(BF16))
            kks.append(jnp.concatenate([both(k_s, before[:1], before[1]), both(k_s, (cur,), rows)],
                                       axis=0).astype(BF16))
            vvs.append(jnp.concatenate([both(v_s, before[:1], before[1]), both(v_s, (cur,), rows)],
                                       axis=0).astype(BF16))
        pairs = [(b, h) for b in range(len(items)) for h in range(4)]
        head_rows = [slice(h * SWA_BLOCK, (h + 1) * SWA_BLOCK) for h in range(4)]
        stacked = [_dot_nt(jnp.concatenate([jnp.where(hm, q, jnp.zeros_like(q)) for hm in hms], axis=0), kk)
                   for q, kk in zip(qs, kks)]
        ss = [jnp.where(masks[b], stacked[b][head_rows[h], :] * (HEAD_DIM_A ** -0.5), NEG_INF) for b, h in pairs]
        ms = [jnp.max(s, axis=1, keepdims=True) for s in ss]
        ps = [jnp.exp(s - m) for s, m in zip(ss, ms)]
        sums = [jnp.sum(p, axis=1, keepdims=True) for p in ps]
        outs = [_dot(jnp.concatenate([ps[4 * b + h].astype(BF16) for h in range(4)], axis=0), vvs[b])
                for b in range(len(items))]
        ohs = [outs[b][head_rows[h], :] for b, h in pairs]
        for b in range(len(items)):
            o_acc = jnp.zeros(shape, F32)
            lse_acc = jnp.zeros(shape, F32)
            for h in range(4):
                i = 4 * b + h
                o_acc = jnp.where(hms[h], ohs[i] / sums[i], o_acc)
                lse_acc = jnp.where(hms[h], ms[i] + jnp.log(sums[i]), lse_acc)
            for s in range(2):
                ls = slice(s * LANES, (s + 1) * LANES)
                o_s[s, rows_of[b], :] = o_acc[:, ls]
                l_s[s, rows_of[b], :] = lse_acc[:, ls]

    if nsub == 1:
        def trip(i, carry):
            blocks([(0, 2 * i), (0, 2 * i + 1)])
            return carry

        lax.fori_loop(0, dil // 2, trip, 0)
    else:
        def trip(r, carry):
            for j0 in range(0, nsub, 2):
                blocks([(j0, r), (j0 + 1, r)])
            return carry

        lax.fori_loop(0, dil, trip, 0)
    for s in range(2):
        ls = slice(s * LANES, (s + 1) * LANES)
        o_ref[:, ls] = o_s[s]
        lse_ref[:, ls] = l_s[s]


def _swa_prompt(q, k, v, g, dil):
    b, s, _ = q.shape
    nsub = max(1, 4 // dil)
    t = nsub * dil * SWA_BLOCK
    assert s % t == 0
    inp = pl.BlockSpec((None, t, GROUP_W), lambda bi, n: (bi, n, g))
    out = pl.BlockSpec((None, t, GROUP_W), lambda bi, n: (bi, n, 0))
    slab = lambda *lead: pltpu.VMEM((*lead, 2, t, LANES), F32)
    o, lse = pl.pallas_call(
        functools.partial(_swa_prompt_kernel, dil=dil, nsub=nsub),
        grid=(b, s // t),
        in_specs=[inp, inp, inp],
        out_specs=[out, out],
        out_shape=[jax.ShapeDtypeStruct((b, s, GROUP_W), F32)] * 2,
        scratch_shapes=[slab(), slab(2), slab(2), slab(), slab()],
        compiler_params=_params(("arbitrary", "arbitrary"), 48),
        name=f"swa_prompt_d{dil}",
    )(q, k, v)
    return o.reshape(b * s, GROUP_W), lse.reshape(b * s, GROUP_W)


def _swa_sample_kernel(*refs, win, dil):
    def one(i, carry):
        _swa_sample_one(*(r.at[i] for r in refs), win=win, dil=dil)
        return carry

    lax.fori_loop(0, refs[0].shape[0], one, 0)


def _swa_sample_one(q_ref, kn_ref, vn_ref, kt_ref, vt_ref, o_ref, lse_ref, kto_ref, vto_ref, *, win, dil):
    t_new = q_ref.shape[0]
    n_tiles = win // LANES
    pad = jnp.zeros((LANES - t_new, GROUP_W), F32)
    kn = jnp.concatenate([kn_ref[...], pad], axis=0)
    vn = jnp.concatenate([vn_ref[...], pad], axis=0)

    def channel_major(a):
        return jnp.concatenate([a[:, :LANES].T, a[:, LANES:].T], axis=0)

    knt, vnt = channel_major(kn), channel_major(vn)
    lane = lax.broadcasted_iota(I32, (GROUP_W, LANES), 1)
    keep = lane < LANES - t_new
    for src_ref, new_t, dst_ref in ((kt_ref, knt, kto_ref), (vt_ref, vnt, vto_ref)):
        nxt = pltpu.roll(src_ref[:, 0:LANES], LANES - t_new, 1)
        for j in range(n_tiles):
            this = nxt
            following = new_t if j + 1 == n_tiles else src_ref[:, (j + 1) * LANES:(j + 2) * LANES]
            nxt = pltpu.roll(following, LANES - t_new, 1)
            dst_ref[:, j * LANES:(j + 1) * LANES] = jnp.where(keep, this, nxt)

    qf = q_ref[...].astype(F32)
    ch_head = lax.broadcasted_iota(I32, (t_new, GROUP_W), 1) // HEAD_DIM_A
    qm = jnp.concatenate([jnp.where(ch_head == h, qf, 0.0) for h in range(4)], axis=0).astype(BF16)
    nq = 4 * t_new
    scale = HEAD_DIM_A ** -0.5
    s_c = _dot(qm, kt_ref[...].astype(BF16)) * scale
    s_n = _dot(qm, knt.astype(BF16)) * scale
    t_c = lax.broadcasted_iota(I32, (nq, win), 0) & (t_new - 1)
    d_c = win + t_c - lax.broadcasted_iota(I32, (nq, win), 1)
    s_c = jnp.where((d_c <= win) & ((d_c & (dil - 1)) == 0), s_c, NEG_INF)
    t_n = lax.broadcasted_iota(I32, (nq, LANES), 0) & (t_new - 1)
    d_n = t_n - lax.broadcasted_iota(I32, (nq, LANES), 1)
    s_n = jnp.where((d_n >= 0) & ((d_n & (dil - 1)) == 0), s_n, NEG_INF)
    m = jnp.maximum(jnp.max(s_c, axis=1, keepdims=True), jnp.max(s_n, axis=1, keepdims=True))
    p_c = jnp.exp(s_c - m)
    p_n = jnp.exp(s_n - m)
    l = jnp.sum(p_c, axis=1, keepdims=True) + jnp.sum(p_n, axis=1, keepdims=True)
    o_all = (_dot_nt(p_c.astype(BF16), vt_ref[...].astype(BF16)) + _dot(p_n.astype(BF16), vn.astype(BF16))) / l
    lse_all = m + jnp.log(l)
    o_acc = jnp.zeros((t_new, GROUP_W), F32)
    lse_acc = jnp.zeros((t_new, GROUP_W), F32)
    for h in range(4):
        hm = ch_head == h
        o_acc = jnp.where(hm, o_all[h * t_new:(h + 1) * t_new, :], o_acc)
        lse_acc = jnp.where(hm, lse_all[h * t_new:(h + 1) * t_new, :], lse_acc)
    o_ref[...] = o_acc
    lse_ref[...] = lse_acc


def _swa_sample(q, k_new, v_new, kt_buf, vt_buf, g, win, dil):
    b, t, _ = q.shape
    assert t == 8 and win % LANES == 0 and dil & (dil - 1) == 0
    bb = max(1, 4096 // win)
    assert b % bb == 0
    new = pl.BlockSpec((bb, t, GROUP_W), lambda bi: (bi, 0, g))
    buf = pl.BlockSpec((bb, GROUP_W, win), lambda bi: (bi, 0, 0))
    out = pl.BlockSpec((bb, t, GROUP_W), lambda bi: (bi, 0, 0))
    o, lse, kto, vto = pl.pallas_call(
        functools.partial(_swa_sample_kernel, win=win, dil=dil),
        grid=(b // bb,),
        in_specs=[new, new, new, buf, buf],
        out_specs=[out, out, buf, buf],
        out_shape=[jax.ShapeDtypeStruct((b, t, GROUP_W), F32)] * 2
        + [jax.ShapeDtypeStruct((b, GROUP_W, win), F32)] * 2,
        compiler_params=_params(("arbitrary",), 48),
        name=f"swa_sample_w{win}",
    )(q, k_new, v_new, kt_buf, vt_buf)
    return o.reshape(b * t, GROUP_W), lse.reshape(b * t, GROUP_W), kto, vto


def _conv_kernel(halo_ref, cur_ref, w_ref, b_ref, g_ref, beta_ref, o_ref, ctx_ref, *, zero_first_halo, chunk):
    halo = halo_ref[...]
    if zero_first_halo:
        halo = jnp.where(pl.program_id(1) == 0, 0.0, halo)
    _conv_module(halo, cur_ref, w_ref, b_ref, g_ref, beta_ref, o_ref, ctx_ref, chunk)


def _conv_module(halo, cur_ref, w_ref, b_ref, g_ref, beta_ref, o_ref, ctx_ref, chunk):
    for rows in _conv_module_parts(halo, cur_ref, w_ref, b_ref, g_ref, beta_ref, o_ref, ctx_ref, chunk):
        rows()


def _conv_module_parts(halo, cur_ref, w_ref, b_ref, g_ref, beta_ref, o_ref, ctx_ref, chunk):
    tm = cur_ref.shape[0]
    ctx_ref[pl.ds(0, CONV_HALO), :] = halo
    ctx_ref[pl.ds(CONV_HALO, tm), :] = cur_ref[...]
    first = CONV_HALO - (CONV_WIDTH - 1)

    def rows_from(c0):
        acc = jnp.zeros((chunk, CONV_CH), F32) + b_ref[...]
        for phase in range(8):
            taps = range(phase, CONV_WIDTH, 8)
            start, shift = divmod(first + phase, 8)
            need = chunk + 8 * (len(taps) - 1)
            if shift == 0:
                window = ctx_ref[pl.ds(c0 + 8 * start, need), :]
            else:
                rows = ctx_ref[pl.ds(c0 + 8 * start, need + 8), :]
                window = pltpu.roll(rows, need + 8 - shift, 0)[0:need, :]
            for a, w in enumerate(taps):
                acc = acc + window[8 * a:8 * a + chunk, :] * w_ref[pl.ds(w, 1), :]
        mu = jnp.mean(acc, axis=-1, keepdims=True)
        xc = acc - mu
        var = jnp.mean(xc * xc, axis=-1, keepdims=True)
        y = xc * lax.rsqrt(var + EPS) * g_ref[...] + beta_ref[...]
        o_ref[pl.ds(c0, chunk), :] = (y * jax.nn.sigmoid(y)).astype(BF16)

    return [functools.partial(rows_from, c0) for c0 in range(0, tm, chunk)]


def _conv_branch(u, halo, conv_w, conv_b, cn_g, cn_b, tm, halo_from_u):
    b, t, _ = u.shape
    assert t % tm == 0
    ratio = tm // CONV_HALO if halo_from_u else 0
    halo_map = (lambda bi, i: (bi, jnp.maximum(i * ratio - 1, 0), 0)) if halo_from_u else (lambda bi, i: (bi, 0, 0))
    const = lambda bi, i: (0, 0)
    return pl.pallas_call(
        functools.partial(_conv_kernel, zero_first_halo=halo_from_u, chunk=min(tm, 64)),
        grid=(b, t // tm),
        in_specs=[
            pl.BlockSpec((None, CONV_HALO, CONV_CH), halo_map),
            pl.BlockSpec((None, tm, CONV_CH), lambda bi, i: (bi, i, 0)),
            pl.BlockSpec((CONV_WIDTH, CONV_CH), const),
            pl.BlockSpec((1, CONV_CH), const), pl.BlockSpec((1, CONV_CH), const), pl.BlockSpec((1, CONV_CH), const),
        ],
        out_specs=pl.BlockSpec((None, tm, CONV_CH), lambda bi, i: (bi, i, 0)),
        out_shape=jax.ShapeDtypeStruct((b, t, CONV_CH), BF16),
        scratch_shapes=[pltpu.VMEM((CONV_HALO + tm, CONV_CH), F32)],
        compiler_params=_params(("arbitrary", "arbitrary"), 32),
        name="conv_module",
    )(halo, u, conv_w, conv_b, cn_g, cn_b).reshape(b * t, CONV_CH)


def _mem_kv_kernel(mem_ref, g_ref, w_ref, kg_ref, k_ref, v_ref):
    hb = _rms(mem_ref[...], g_ref[...]).astype(BF16)
    for h in range(MEM_HEADS):
        cs = slice(h * MEM_HEAD_DIM, (h + 1) * MEM_HEAD_DIM)
        k_ref[:, cs] = _rms(_dot(hb, w_ref[:, cs]), kg_ref[...])
        v_ref[:, cs] = _dot(hb, w_ref[:, MEM_WIDTH + h * MEM_HEAD_DIM:MEM_WIDTH + (h + 1) * MEM_HEAD_DIM])


def _mem_kv(mem, g, w, kg):
    b = mem.shape[0]
    const = lambda bi: (0, 0)
    blk = pl.BlockSpec((None, MEM_LEN, MEM_WIDTH), lambda bi: (bi, 0, 0))
    return pl.pallas_call(
        _mem_kv_kernel,
        grid=(b,),
        in_specs=[pl.BlockSpec((None, MEM_LEN, D_MODEL), lambda bi: (bi, 0, 0)), pl.BlockSpec((1, D_MODEL), const),
                  pl.BlockSpec((D_MODEL, 2 * MEM_WIDTH), const), pl.BlockSpec((1, MEM_HEAD_DIM), const)],
        out_specs=[blk, blk],
        out_shape=[jax.ShapeDtypeStruct((b, MEM_LEN, MEM_WIDTH), F32)] * 2,
        compiler_params=_params(("arbitrary",), 32),
        name="mem_kv",
    )(mem, g, w, kg)


def _mem_attn_kernel(q_ref, k_ref, v_ref, o_ref):
    for h in range(MEM_HEADS):
        cs = slice(h * MEM_HEAD_DIM, (h + 1) * MEM_HEAD_DIM)
        s = _dot_nt(q_ref[:, cs], k_ref[:, cs].astype(BF16)) * (MEM_HEAD_DIM ** -0.5)
        m = jnp.max(s, axis=1, keepdims=True)
        p = jnp.exp(s - m)
        l = jnp.sum(p, axis=1, keepdims=True)
        o_ref[:, cs] = (_dot(p.astype(BF16), v_ref[:, cs].astype(BF16)) / l).astype(BF16)


def _mem_attn(q, k, v, tm):
    b, t, _ = q.shape
    assert t % tm == 0
    kv = pl.BlockSpec((None, MEM_LEN, MEM_WIDTH), lambda bi, i: (bi, 0, 0))
    qo = pl.BlockSpec((None, tm, MEM_WIDTH), lambda bi, i: (bi, i, 0))
    return pl.pallas_call(
        _mem_attn_kernel,
        grid=(b, t // tm),
        in_specs=[qo, kv, kv],
        out_specs=qo,
        out_shape=jax.ShapeDtypeStruct((b, t, MEM_WIDTH), BF16),
        compiler_params=_params(("arbitrary", "arbitrary"), 32),
        name="mem_attn",
    )(q, k, v).reshape(b * t, MEM_WIDTH)


def _mem_attn_rows_kernel(*refs):
    def one(i, carry):
        _mem_attn_rows_one(*(r.at[i] for r in refs))
        return carry

    lax.fori_loop(0, refs[0].shape[0], one, 0)


def _mem_attn_rows_one(q_ref, k_ref, v_ref, o_ref):
    t = q_ref.shape[0]
    qf = q_ref[...].astype(F32)
    qs = jnp.concatenate([qf[:, h * MEM_HEAD_DIM:(h + 1) * MEM_HEAD_DIM] for h in range(MEM_HEADS)], axis=0)
    s = _dot_nt(qs.astype(BF16), k_ref[...].astype(BF16)) * (MEM_HEAD_DIM ** -0.5)
    shape = s.shape
    same_head = (lax.broadcasted_iota(I32, shape, 0) // t) == (lax.broadcasted_iota(I32, shape, 1) & (MEM_HEADS - 1))
    s = jnp.where(same_head, s, NEG_INF)
    m = jnp.max(s, axis=1, keepdims=True)
    p = jnp.exp(s - m)
    l = jnp.sum(p, axis=1, keepdims=True)
    o = _dot(p.astype(BF16), v_ref[...].astype(BF16)) / l
    o_ref[...] = jnp.concatenate([o[h * t:(h + 1) * t, :] for h in range(MEM_HEADS)], axis=1).astype(BF16)


def _mem_attn_rows(q, k_rows, v_rows):
    b, t, _ = q.shape
    bb = 4
    assert b % bb == 0
    kv = pl.BlockSpec((bb, MEM_LEN * MEM_HEADS, MEM_HEAD_DIM), lambda bi: (bi, 0, 0))
    qo = pl.BlockSpec((bb, t, MEM_WIDTH), lambda bi: (bi, 0, 0))
    return pl.pallas_call(
        _mem_attn_rows_kernel,
        grid=(b // bb,),
        in_specs=[qo, kv, kv],
        out_specs=qo,
        out_shape=jax.ShapeDtypeStruct((b, t, MEM_WIDTH), BF16),
        compiler_params=_params(("arbitrary",), 32),
        name="mem_attn_rows",
    )(q, k_rows, v_rows).reshape(b * t, MEM_WIDTH)


def _merge_kernel(x_ref, o1_ref, o2_ref, o3_ref, l1_ref, l2_ref, l3_ref, ob_ref, om_ref, *rest):
    _merge_body((), x_ref, o1_ref, o2_ref, o3_ref, l1_ref, l2_ref, l3_ref, ob_ref, om_ref, *rest)


def _merge_conv_kernel(x_ref, o1_ref, o2_ref, o3_ref, l1_ref, l2_ref, l3_ref, halo_ref, u_ref, om_ref,
                       g1_ref, wg_ref, bg_ref, wa_ref, wb_ref, wm_ref, wo_ref, g2_ref, wrh_ref, wrc_ref, br_ref,
                       cw_ref, cb_ref, cg_ref, cbeta_ref, x1_ref, h2_ref, idx_ref, gw_ref,
                       mix_ref, ctx_ref, ob_ref, *, tiles_per_seq):
    halo = jnp.where(pl.program_id(0) % tiles_per_seq == 0, 0.0, halo_ref[...])
    conv_parts = _conv_module_parts(halo, u_ref, cw_ref, cb_ref, cg_ref, cbeta_ref, ob_ref, ctx_ref, 64)
    _merge_body(conv_parts, x_ref, o1_ref, o2_ref, o3_ref, l1_ref, l2_ref, l3_ref, ob_ref, om_ref,
                g1_ref, wg_ref, bg_ref, wa_ref, wb_ref, wm_ref, wo_ref, g2_ref, wrh_ref, wrc_ref, br_ref,
                x1_ref, h2_ref, idx_ref, gw_ref, mix_ref)


def _merge_body(conv_parts, x_ref, o1_ref, o2_ref, o3_ref, l1_ref, l2_ref, l3_ref, ob_ref, om_ref,
                g1_ref, wg_ref, bg_ref, wa_ref, wb_ref, wm_ref, wo_ref, g2_ref, wrh_ref, wrc_ref, br_ref,
                x1_ref, h2_ref, idx_ref, gw_ref, mix_ref):
    tm = x_ref.shape[0]
    x = x_ref[...]
    hb = _rms(x, g1_ref[...]).astype(BF16)
    la, lb, lc = l1_ref[...], l2_ref[...], l3_ref[...]
    m = jnp.maximum(la, jnp.maximum(lb, lc))
    ea, eb, ec = jnp.exp(la - m), jnp.exp(lb - m), jnp.exp(lc - m)
    oa = ((ea * o1_ref[...] + eb * o2_ref[...] + ec * o3_ref[...]) / (ea + eb + ec)).astype(BF16)
    om = om_ref[...]
    chunks = [slice(j * 256, (j + 1) * 256) for j in range(4)]

    def gate(branch, j):
        c0 = branch * D_MODEL + j * 256
        return jax.nn.sigmoid(_dot(hb, wg_ref[:, c0:c0 + 256]) + bg_ref[:, c0:c0 + 256])

    per_j = -(-len(conv_parts) // 4)
    partial, gate_b = [], []
    for j, cs in enumerate(chunks):
        for part in conv_parts[j * per_j:(j + 1) * per_j]:
            part()
        partial.append(gate(0, j) * _dot(oa, wa_ref[:, cs]) + gate(2, j) * _dot(om, wm_ref[:, cs]))
        gate_b.append(gate(1, j))
    ob = ob_ref[...]
    for j, cs in enumerate(chunks):
        mix_ref[:, cs] = (partial[j] + gate_b[j] * _dot(ob, wb_ref[:, cs])).astype(BF16)
    x1 = x + _dot(mix_ref[...], wo_ref[...])
    x1_ref[...] = x1
    h2 = _rms(x1, g2_ref[...])
    for j in range(ROW_TILE):
        h2_ref[pl.ds(j, tm, stride=ROW_TILE), :] = h2[:, j * LANES:(j + 1) * LANES]
    hi = h2.astype(BF16)
    lo = (h2 - hi.astype(F32)).astype(BF16)
    both = _dot(hi, wrc_ref[...])
    logits = both[:, :LANES] + both[:, LANES:] + _dot(lo, wrh_ref[...]) + br_ref[...]
    lane = lax.broadcasted_iota(I32, (tm, LANES), 1)
    logits = jnp.where(lane < N_EXPERTS, logits, NEG_INF)
    vals, idxs = [], []
    for _ in range(TOP_K):
        mk = jnp.max(logits, axis=1, keepdims=True)
        ik = jnp.min(jnp.where(logits == mk, lane, LANES), axis=1, keepdims=True)
        vals.append(mk)
        idxs.append(ik)
        logits = jnp.where(lane == ik, NEG_INF, logits)
    es = [jnp.exp(v - vals[0]) for v in vals]
    den = es[0] + es[1] + es[2] + es[3]
    l4 = lax.broadcasted_iota(I32, (tm, TOP_K), 1)
    idx_out = jnp.zeros((tm, TOP_K), I32)
    gw_out = jnp.zeros((tm, TOP_K), F32)
    for k in range(TOP_K):
        idx_out = jnp.where(l4 == k, idxs[k], idx_out)
        gw_out = jnp.where(l4 == k, es[k] / den, gw_out)
    idx_ref[...] = idx_out
    gw_ref[...] = gw_out


def _merge(x, os_, lses, ob, om, g1, wg, bg, wa, wb, wm, wo, g2, wrh, wrc, br, tm, conv=None):
    n = x.shape[0]
    assert n % tm == 0
    row = lambda i: (i, 0)
    const = lambda i: (0, 0)
    rows = lambda w: pl.BlockSpec((tm, w), row)
    full = lambda a: pl.BlockSpec(a.shape, const)
    weights = (g1, wg, bg, wa, wb, wm, wo, g2, wrh, wrc, br)
    scratch = [pltpu.VMEM((tm, D_MODEL), BF16)]
    if conv is None:
        body, b_specs, b_args, extra = _merge_kernel, [rows(CONV_CH)], (ob,), ()
    else:
        *conv_params, seq_len = conv
        assert seq_len % tm == 0 and tm % CONV_HALO == 0
        ratio = tm // CONV_HALO
        body = functools.partial(_merge_conv_kernel, tiles_per_seq=seq_len // tm)
        b_specs = [pl.BlockSpec((CONV_HALO, CONV_CH), lambda i: (jnp.maximum(i * ratio - 1, 0), 0)), rows(CONV_CH)]
        b_args, extra = (ob, ob), tuple(conv_params)
        scratch += [pltpu.VMEM((CONV_HALO + tm, CONV_CH), F32), pltpu.VMEM((tm, CONV_CH), BF16)]
    return pl.pallas_call(
        body,
        grid=(n // tm,),
        in_specs=[rows(D_MODEL)] + [rows(GROUP_W)] * 6 + b_specs + [rows(MEM_WIDTH)]
        + [full(a) for a in weights + extra],
        out_specs=[rows(D_MODEL), pl.BlockSpec((tm * ROW_TILE, LANES), row), rows(TOP_K), rows(TOP_K)],
        out_shape=[jax.ShapeDtypeStruct((n, D_MODEL), F32), jax.ShapeDtypeStruct((n * ROW_TILE, LANES), F32),
                   jax.ShapeDtypeStruct((n, TOP_K), I32), jax.ShapeDtypeStruct((n, TOP_K), F32)],
        scratch_shapes=scratch,
        compiler_params=_params(("arbitrary",), 56),
        name="merge_router",
    )(x, *os_, *lses, *b_args, om, *weights, *extra)


def _active_part(refs, bounds, i):
    value = refs[0][...]
    for ref, (first, _) in zip(refs[1:], bounds[1:]):
        value = jnp.where(i >= first, ref[...], value)
    return value


def _route_kernel(*refs, block_rows, bounds):
    idx_refs, (dest_ref, bexp_ref, seg_ref, cnt_ref, carry_ref) = refs[:len(bounds)], refs[len(bounds):]
    phase, i = pl.program_id(0), pl.program_id(1)
    tq = idx_refs[0].shape[0]
    nb = bexp_ref.shape[0]
    idx = _active_part(idx_refs, bounds, i)
    lane = lax.broadcasted_iota(I32, (tq, LANES), 1)
    member = jnp.zeros((tq, LANES), F32)
    for k in range(TOP_K):
        member = member + (lane == idx[:, k:k + 1]).astype(F32)
    tile_counts = jnp.sum(member, axis=0, keepdims=True)

    @pl.when(phase == 0)
    def _():
        @pl.when(i == 0)
        def _():
            cnt_ref[...] = jnp.zeros_like(cnt_ref)

        cnt_ref[...] += tile_counts
        dest_ref[...] = jnp.zeros_like(dest_ref)
        bexp_ref[...] = jnp.zeros_like(bexp_ref)
        seg_ref[...] = jnp.zeros_like(seg_ref)

    @pl.when(phase == 1)
    def _():
        @pl.when(i == 0)
        def _():
            carry_ref[...] = jnp.zeros_like(carry_ref)

        counts = jnp.broadcast_to(cnt_ref[...], (8, LANES))
        padded = jnp.floor((counts + (block_rows - 1)) * (1.0 / block_rows)) * block_rows
        lane8 = lax.broadcasted_iota(I32, (8, LANES), 1)
        ends = padded
        for s in (1, 2, 4, 8, 16, 32, 64):
            ends = ends + jnp.where(lane8 >= s, pltpu.roll(ends, s, 1), 0.0)
        starts = (ends - padded)[0:1, :]
        r = lax.broadcasted_iota(I32, (tq, tq), 0)
        c = lax.broadcasted_iota(I32, (tq, tq), 1)
        earlier = _dot((c < r).astype(BF16), member.astype(BF16)) + carry_ref[...]
        carry_ref[...] += tile_counts
        base = earlier + starts
        l4 = lax.broadcasted_iota(I32, (tq, TOP_K), 1)
        dest = jnp.zeros((tq, TOP_K), F32)
        for k in range(TOP_K):
            dk = jnp.sum(jnp.where(lane == idx[:, k:k + 1], base, 0.0), axis=1, keepdims=True)
            dest = jnp.where(l4 == k, dk, dest)
        dest_ref[...] = dest.astype(I32)
        first_row = (lax.broadcasted_iota(I32, (nb, LANES), 0) * block_rows).astype(F32)
        lane_nb = lax.broadcasted_iota(I32, (nb, LANES), 1)
        done = jnp.where((ends[0:1, :] <= first_row) & (lane_nb < N_EXPERTS), 1.0, 0.0)
        bexp_ref[...] = jnp.minimum(jnp.sum(done, axis=1, keepdims=True), N_EXPERTS - 1.0).astype(I32)
        row8 = lax.broadcasted_iota(I32, (8, LANES), 0)
        seg = jnp.where(row8 == 0, ends - padded + counts, jnp.where(row8 == 1, ends, 0.0))
        seg_ref[...] = seg.astype(I32)


def _route(idx_parts, block_rows, n_blocks, tq):
    specs, bounds, steps = _part_specs(idx_parts, (tq, TOP_K))
    n = steps * tq
    nb_pad = -(-n_blocks // 8) * 8
    dest, bexp, seg = pl.pallas_call(
        functools.partial(_route_kernel, block_rows=block_rows, bounds=bounds),
        grid=(2, steps),
        in_specs=specs,
        out_specs=[pl.BlockSpec((tq, TOP_K), lambda p, i: (i * p, 0)),
                   pl.BlockSpec((nb_pad, 1), lambda p, i: (0, 0)),
                   pl.BlockSpec((8, LANES), lambda p, i: (0, 0))],
        out_shape=[jax.ShapeDtypeStruct((n, TOP_K), I32), jax.ShapeDtypeStruct((nb_pad, 1), I32),
                   jax.ShapeDtypeStruct((8, LANES), I32)],
        scratch_shapes=[pltpu.VMEM((1, LANES), F32), pltpu.VMEM((1, LANES), F32)],
        compiler_params=_params(("arbitrary", "arbitrary"), 40),
        name="route",
    )(*idx_parts)
    return dest.reshape(n * TOP_K), bexp.reshape(nb_pad)[:n_blocks], seg


def _tile_rows(index):
    if isinstance(index, int):
        return pl.ds(index * ROW_TILE, ROW_TILE)
    return pl.ds(pl.multiple_of(index * ROW_TILE, ROW_TILE), ROW_TILE)


def _part_specs(parts, block):
    specs, bounds, first = [], [], 0
    for a in parts:
        assert a.shape[0] % block[0] == 0
        count = a.shape[0] // block[0]
        specs.append(pl.BlockSpec(block, lambda *g, first=first, count=count: (jnp.clip(g[-1] - first, 0, count - 1), 0)))
        bounds.append((first, count))
        first += count
    return specs, tuple(bounds), first


def _dispatch_kernel(seg_ref, dest_ref, *refs, bounds):
    h_refs, (xs_ref, zero_ref, sem, zero_sem) = refs[:len(bounds)], refs[len(bounds):]
    tm = h_refs[0].shape[0] // ROW_TILE
    i = pl.program_id(0)

    @pl.when(pl.program_id(0) == 0)
    def _():
        zero_ref[...] = jnp.zeros_like(zero_ref)

        def pad_copies(e):
            slot, length = seg_ref[0, e], seg_ref[1, e] - seg_ref[0, e]
            pieces = []
            size = zero_ref.shape[0] // ROW_TILE
            while size >= 1:
                present = (length & size) != 0
                rows = pl.ds(pl.multiple_of(slot * ROW_TILE, ROW_TILE), size * ROW_TILE)
                pieces.append((present, pltpu.make_async_copy(zero_ref.at[pl.ds(0, size * ROW_TILE)],
                                                              xs_ref.at[rows], zero_sem)))
                slot = slot + jnp.where(present, size, 0)
                size //= 2
            return pieces

        def start_expert(e):
            for present, piece in pad_copies(e):
                pl.when(present)(piece.start)

        def wait_expert(e):
            for present, piece in pad_copies(e):
                pl.when(present)(piece.wait)

        def expert(e, carry):
            start_expert(e)
            wait_expert(e - 1)
            return carry

        start_expert(0)
        lax.fori_loop(1, N_EXPERTS, expert, 0)
        wait_expert(N_EXPERTS - 1)

        piece = zero_ref.shape[0]

        def tail_copy(p):
            return pltpu.make_async_copy(zero_ref, xs_ref.at[pl.ds(pl.multiple_of(p * piece, piece), piece)], zero_sem)

        first_tail = seg_ref[1, N_EXPERTS - 1] * ROW_TILE // piece
        n_pieces = xs_ref.shape[0] // piece
        lax.fori_loop(first_tail, n_pieces, lambda p, c: (tail_copy(p).start(), c)[1], 0)
        lax.fori_loop(first_tail, n_pieces, lambda p, c: (tail_copy(p).wait(), c)[1], 0)

    for h_ref, (first, count) in zip(h_refs, bounds):
        @pl.when((i >= first) & (i < first + count))
        def _(h_ref=h_ref):
            def issue(r, carry):
                for k in range(TOP_K):
                    pltpu.make_async_copy(h_ref.at[_tile_rows(r)], xs_ref.at[_tile_rows(dest_ref[r * TOP_K + k])],
                                          sem).start(priority=k % 2)
                return carry

            for r in range(tm):
                issue(r, 0)

    for k in range(TOP_K):
        pltpu.make_async_copy(h_refs[0], xs_ref.at[pl.ds(0, tm * ROW_TILE)], sem).wait()


def _dispatch(h2t_parts, dest, seg, n_slots, tm, block_rows):
    specs, bounds, steps = _part_specs(h2t_parts, (tm * ROW_TILE, LANES))
    assert block_rows & (block_rows - 1) == 0
    return pl.pallas_call(
        functools.partial(_dispatch_kernel, bounds=bounds),
        grid=(steps,),
        in_specs=[pl.BlockSpec(memory_space=pltpu.SMEM),
                  pl.BlockSpec((tm * TOP_K,), lambda i: (i,), memory_space=pltpu.SMEM)] + specs,
        out_specs=pl.BlockSpec(memory_space=pl.ANY),
        out_shape=jax.ShapeDtypeStruct((n_slots * ROW_TILE, LANES), F32),
        scratch_shapes=[pltpu.VMEM((block_rows // 2 * ROW_TILE, LANES), F32), pltpu.SemaphoreType.DMA,
                        pltpu.SemaphoreType.DMA],
        compiler_params=_params(("arbitrary",), 32),
        name="dispatch",
    )(seg, dest, *h2t_parts)


def _moe_kernel(be_ref, seg_ref, xs_ref, wg_ref, bg_ref, wu_ref, bu_ref, wd_ref, bd_ref, yb_ref, wg_s, wu_s, wd_s):
    i = pl.program_id(0)
    bm = xs_ref.shape[0] // ROW_TILE
    used_rows = seg_ref[LANES + N_EXPERTS - 1]

    @pl.when(i * bm >= used_rows)
    def _():
        yb_ref[...] = jnp.zeros_like(yb_ref)

    @pl.when(i * bm < used_rows)
    def _():
        changed = (i == 0) | (be_ref[i] != be_ref[jnp.maximum(i - 1, 0)])

        @pl.when(changed)
        def _():
            for r0 in range(0, D_MODEL, 128):
                rs = pl.ds(r0, 128)
                wg_s[rs, :] = wg_ref[rs, :].astype(BF16)
                wu_s[rs, :] = wu_ref[rs, :].astype(BF16)
                wd_s[rs, :] = wd_ref[rs, :].astype(BF16)

        xb = jnp.concatenate([xs_ref[pl.ds(j, bm, stride=ROW_TILE), :] for j in range(ROW_TILE)],
                             axis=1).astype(BF16)
        chunks = [slice(c * 256, (c + 1) * 256) for c in range(D_FF // 256)]
        gs = [_dot(xb, wg_s[:, cs]) + bg_ref[:, cs] for cs in chunks]
        us = [_dot(xb, wu_s[:, cs]) + bu_ref[:, cs] for cs in chunks]
        acts = []
        for g, u in zip(gs, us):
            g = jnp.minimum(g, SWIGLU_LIMIT)
            u = jnp.clip(u, -SWIGLU_LIMIT, SWIGLU_LIMIT)
            acts.append((g * jax.nn.sigmoid(SWIGLU_ALPHA * g) * (u + 1.0)).astype(BF16))
        acc = jnp.zeros((bm, D_MODEL), F32) + bd_ref[...]
        for act, cs in zip(acts, chunks):
            acc = acc + _dot(act, wd_s[cs, :])
        for j in range(ROW_TILE):
            yb_ref[pl.ds(j, bm, stride=ROW_TILE), :] = acc[:, j * LANES:(j + 1) * LANES]


def _moe(xs, bexp, seg, wg, bg, wu, bu, wd, bd, bm):
    n_slots = xs.shape[0] // ROW_TILE
    assert n_slots % bm == 0
    wspec = pl.BlockSpec((None, D_MODEL, D_FF), lambda i, be, sg: (be[i], 0, 0))
    bspec = pl.BlockSpec((None, 1, D_FF), lambda i, be, sg: (be[i], 0, 0))
    rows = pl.BlockSpec((bm * ROW_TILE, LANES), lambda i, be, sg: (i, 0))
    grid_spec = pltpu.PrefetchScalarGridSpec(
        num_scalar_prefetch=2,
        grid=(n_slots // bm,),
        in_specs=[rows, wspec, bspec, wspec, bspec, wspec, bspec],
        out_specs=rows,
        scratch_shapes=[pltpu.VMEM((D_MODEL, D_FF), BF16)] * 3,
    )
    return pl.pallas_call(
        _moe_kernel,
        grid_spec=grid_spec,
        out_shape=jax.ShapeDtypeStruct((n_slots * ROW_TILE, LANES), F32),
        compiler_params=_params(("arbitrary",), 56),
        name="moe_experts",
    )(bexp, seg.reshape(-1), xs, wg, bg.reshape(N_EXPERTS, 1, D_FF), wu, bu.reshape(N_EXPERTS, 1, D_FF),
      wd, bd.reshape(N_EXPERTS, 1, D_MODEL))


def _combine_kernel(dest_ref, dest_next_ref, yb_ref, *refs, bounds):
    nparts = len(bounds)
    x1_refs, gw_refs, y_refs, (rows_ref, sems) = (refs[:nparts], refs[nparts:2 * nparts], refs[2 * nparts:3 * nparts],
                                                  refs[3 * nparts:])
    tm = x1_refs[0].shape[0]
    i = pl.program_id(0)
    slot = i % 2

    def issue(dref, buf, unrolled):
        def body(r, carry):
            for k in range(TOP_K):
                pltpu.make_async_copy(yb_ref.at[_tile_rows(dref[r * TOP_K + k])], rows_ref.at[buf, k, _tile_rows(r)],
                                      sems.at[buf]).start(priority=k % 2)
            return carry

        if unrolled:
            for r in range(tm):
                body(r, 0)
        else:
            lax.fori_loop(0, tm, body, 0, unroll=4)

    @pl.when(i == 0)
    def _():
        issue(dest_ref, slot, False)

    @pl.when(i + 1 < pl.num_programs(0))
    def _():
        issue(dest_next_ref, 1 - slot, True)

    for k in range(TOP_K):
        pltpu.make_async_copy(yb_ref.at[pl.ds(0, tm * ROW_TILE)], rows_ref.at[slot, k], sems.at[slot]).wait()
    gw = _active_part(gw_refs, bounds, i)
    for x1_ref, y_ref, (first, count) in zip(x1_refs, y_refs, bounds):
        @pl.when((i >= first) & (i < first + count))
        def _(x1_ref=x1_ref, y_ref=y_ref):
            for j in range(ROW_TILE):
                ls = slice(j * LANES, (j + 1) * LANES)
                y = x1_ref[:, ls]
                for k in range(TOP_K):
                    y = y + gw[:, k:k + 1] * rows_ref[slot, k, pl.ds(j, tm, stride=ROW_TILE), :]
                y_ref[:, ls] = y


def _combine(x1_parts, gw_parts, dest, yb, tm):
    specs, bounds, steps = _part_specs(x1_parts, (tm, D_MODEL))
    gw_specs, gw_bounds, _ = _part_specs(gw_parts, (tm, TOP_K))
    assert gw_bounds == bounds
    return pl.pallas_call(
        functools.partial(_combine_kernel, bounds=bounds),
        grid=(steps,),
        in_specs=[pl.BlockSpec((tm * TOP_K,), lambda i: (i,), memory_space=pltpu.SMEM),
                  pl.BlockSpec((tm * TOP_K,), lambda i: (jnp.minimum(i + 1, steps - 1),), memory_space=pltpu.SMEM),
                  pl.BlockSpec(memory_space=pl.ANY)] + specs + gw_specs,
        out_specs=specs,
        out_shape=[jax.ShapeDtypeStruct(a.shape, F32) for a in x1_parts],
        scratch_shapes=[pltpu.VMEM((2, TOP_K, tm * ROW_TILE, LANES), F32), pltpu.SemaphoreType.DMA((2,))],
        compiler_params=_params(("arbitrary",), 32),
        name="combine",
    )(dest, dest, yb, *x1_parts, *gw_parts)


def _rope_tables(positions):
    half = ROPE_DIM // 2
    inv = jnp.power(jnp.float32(ROPE_THETA), -jnp.arange(half, dtype=F32) / half)
    ang = positions.astype(F32)[:, None] * inv[None, :]
    cos, sin = jnp.cos(ang), jnp.sin(ang)
    t = positions.shape[0]
    z8 = jnp.zeros((t, half), F32)
    rest0 = jnp.zeros((t, HEAD_DIM_A - ROPE_DIM), F32)
    a = jnp.concatenate([cos, cos, jnp.ones((t, HEAD_DIM_A - ROPE_DIM), F32)], axis=1)
    bp = jnp.concatenate([z8, sin, rest0], axis=1)
    bm = jnp.concatenate([-sin, z8, rest0], axis=1)
    return tuple(jnp.tile(m, (1, LANES // HEAD_DIM_A)) for m in (a, bp, bm))


def _segment_mean_matrix(seg):
    i = jnp.arange(256)
    return jnp.where((i[:, None] // seg) == (i[None, :] // seg), 1.0 / seg, 0.0).astype(BF16)


def _layer_weights(l, norm1_g, w_in, b_gates, qn_a, kn_a, conv_w, conv_b, cn_g, cn_b, qn_m,
                   w_proj_a, w_proj_b, w_proj_m, w_out, norm2_g, w_router, b_router,
                   w_gate, b_gate, w_up, b_up, w_down, b_down):
    wr = jnp.pad(w_router[l], ((0, 0), (0, LANES - N_EXPERTS)))
    wrh = wr.astype(BF16)
    return dict(
        g1=norm1_g[l][None, :],
        w1=w_in[l][:, :W1_COLS].astype(BF16),
        wgates=w_in[l][:, W1_COLS:].astype(BF16),
        bgates=b_gates[l][None, :],
        qg=jnp.tile(qn_a[l], 256 // HEAD_DIM_A)[None, :],
        kg=jnp.tile(kn_a[l], 256 // HEAD_DIM_A)[None, :],
        mg=jnp.tile(qn_m[l], 256 // MEM_HEAD_DIM)[None, :],
        conv_w=conv_w[l], conv_b=conv_b[l][None, :], cn_g=cn_g[l][None, :], cn_b=cn_b[l][None, :],
        wa=w_proj_a[l].astype(BF16), wb=w_proj_b[l].astype(BF16), wm=w_proj_m[l].astype(BF16),
        wo=w_out[l].astype(BF16), g2=norm2_g[l][None, :],
        wrh=wrh, wrc=jnp.concatenate([wrh, (wr - wrh.astype(F32)).astype(BF16)], axis=1),
        br=jnp.pad(b_router[l], (0, LANES - N_EXPERTS))[None, :],
        w_gate=w_gate[l], b_gate=b_gate[l], w_up=w_up[l], b_up=b_up[l], w_down=w_down[l], b_down=b_down[l],
        seg64=_segment_mean_matrix(HEAD_DIM_A), seg128=_segment_mean_matrix(MEM_HEAD_DIM),
    )


def _merge_group(x2d, os_, lses, ob, om, w, tm, conv_seq_len=None):
    conv = None if conv_seq_len is None else (w["conv_w"], w["conv_b"], w["cn_g"], w["cn_b"], conv_seq_len)
    return _merge(x2d, os_, lses, ob, om, w["g1"], w["wgates"], w["bgates"], w["wa"], w["wb"], w["wm"],
                  w["wo"], w["g2"], w["wrh"], w["wrc"], w["br"], tm, conv)


MOE_ROWS = 512
ROUTE_TILE = 1024
DISPATCH_TILE = 512
COMBINE_TILE = 256


def _moe_groups(merged, w):
    x1s, h2ts, idxs, gws = zip(*merged)
    n = sum(a.shape[0] for a in idxs)
    n_blocks = -(-(n * TOP_K) // MOE_ROWS) + N_EXPERTS
    dest, bexp, seg = _route(idxs, MOE_ROWS, n_blocks, ROUTE_TILE)
    xs = _dispatch(h2ts, dest, seg, n_blocks * MOE_ROWS, DISPATCH_TILE, MOE_ROWS)
    yb = _moe(xs, bexp, seg, w["w_gate"], w["b_gate"], w["w_up"], w["b_up"], w["w_down"], w["b_down"], MOE_ROWS)
    return _combine(x1s, gws, dest, yb, COMBINE_TILE)


def _prompt_layer(x, mem, w, mem_norm_g, w_mem_kv, kn_m):
    b, s, _ = x.shape
    n = b * s
    x2d = x.reshape(n, D_MODEL)
    tabs = _rope_tables(jnp.arange(s, dtype=I32))
    q, k, v, u, qm = _in_proj(x2d, w["g1"], w["w1"], w["qg"], w["kg"], w["mg"], tabs, w["seg64"], w["seg128"], 512, F32)
    q3, k3, v3 = (t.reshape(b, s, WIDTH_A) for t in (q, k, v))
    os_, lses, caches = [], [], []
    for g, (win, dil) in enumerate(SWA_GROUPS):
        o, lse = _swa_prompt(q3, k3, v3, g, dil)
        os_.append(o)
        lses.append(lse)
        cs = slice(g * GROUP_W, (g + 1) * GROUP_W)
        caches += [k3[:, s - win:, cs].reshape(b, win, 4, HEAD_DIM_A), v3[:, s - win:, cs].reshape(b, win, 4, HEAD_DIM_A)]
    u3 = u.reshape(b, s, CONV_CH)
    mk, mv = _mem_kv(mem, mem_norm_g[None, :], w_mem_kv.astype(BF16), kn_m[None, :])
    om = _mem_attn(qm.reshape(b, s, MEM_WIDTH), mk, mv, 512)
    merged = _merge_group(x2d, os_, lses, u, om, w, 512, conv_seq_len=s)
    state = caches + [u3[:, s - (CONV_WIDTH - 1):], mk.reshape(b, MEM_LEN, MEM_HEADS, MEM_HEAD_DIM),
                      mv.reshape(b, MEM_LEN, MEM_HEADS, MEM_HEAD_DIM)]
    return merged, state


def _sample_layer(x, mem_k, mem_v, bufs, conv_state, w):
    b, t, _ = x.shape
    n = b * t
    x2d = x.reshape(n, D_MODEL)
    tabs = _rope_tables(jnp.tile(PAST_LEN + jnp.arange(t, dtype=I32), b))
    q, k, v, u, qm = _in_proj(x2d, w["g1"], w["w1"], w["qg"], w["kg"], w["mg"], tabs, w["seg64"], w["seg128"], 256, BF16)
    q3, k3, v3 = (a.reshape(b, t, WIDTH_A) for a in (q, k, v))
    os_, lses, caches = [], [], []
    for g, (win, dil) in enumerate(SWA_GROUPS):
        to_cm = lambda a: jnp.transpose(a, (0, 2, 3, 1)).reshape(b, GROUP_W, win)
        from_cm = lambda a: jnp.transpose(a.reshape(b, 4, HEAD_DIM_A, win), (0, 3, 1, 2))
        o, lse, kto, vto = _swa_sample(q3, k3, v3, to_cm(bufs[2 * g]), to_cm(bufs[2 * g + 1]), g, win, dil)
        os_.append(o)
        lses.append(lse)
        caches += [from_cm(kto), from_cm(vto)]
    u3 = u.reshape(b, t, CONV_CH)
    halo = jnp.pad(conv_state, ((0, 0), (CONV_HALO - (CONV_WIDTH - 1), 0), (0, 0)))
    ob = _conv_branch(u3, halo, w["conv_w"], w["conv_b"], w["cn_g"], w["cn_b"], t, False)
    om = _mem_attn_rows(qm.reshape(b, t, MEM_WIDTH), mem_k.reshape(b, MEM_LEN * MEM_HEADS, MEM_HEAD_DIM),
                        mem_v.reshape(b, MEM_LEN * MEM_HEADS, MEM_HEAD_DIM))
    merged = _merge_group(x2d, os_, lses, ob, om, w, 256)
    new_conv = jnp.concatenate([conv_state, u3], axis=1)[:, t:]
    return merged, caches + [new_conv]


def kernel(x_prompt, x_sample, mem_prompt, cache_swa_k_w128, cache_swa_v_w128, cache_swa_k_w512, cache_swa_v_w512, cache_swa_k_w2048, cache_swa_v_w2048, state_conv, cache_mem_k, cache_mem_v, norm1_g, w_in, b_gates, qn_a, kn_a, conv_w, conv_b, cn_g, cn_b, mem_norm_g, w_mem_kv, qn_m, kn_m, w_proj_a, w_proj_b, w_proj_m, w_out, norm2_g, w_router, b_router, w_gate, b_gate, w_up, b_up, w_down, b_down):
    depth = norm1_g.shape[0]
    y_p, y_s = x_prompt, x_sample
    st_p, st_s = [], []
    for l in range(depth):
        w = _layer_weights(l, norm1_g, w_in, b_gates, qn_a, kn_a, conv_w, conv_b, cn_g, cn_b, qn_m,
                           w_proj_a, w_proj_b, w_proj_m, w_out, norm2_g, w_router, b_router,
                           w_gate, b_gate, w_up, b_up, w_down, b_down)
        merged_p, state_p = _prompt_layer(y_p, mem_prompt, w, mem_norm_g[l], w_mem_kv[l], kn_m[l])
        bufs = (cache_swa_k_w128[l], cache_swa_v_w128[l], cache_swa_k_w512[l], cache_swa_v_w512[l],
                cache_swa_k_w2048[l], cache_swa_v_w2048[l])
        merged_s, state_s = _sample_layer(y_s, cache_mem_k[l], cache_mem_v[l], bufs, state_conv[l], w)
        y_p2d, y_s2d = _moe_groups((merged_p, merged_s), w)
        y_p, y_s = y_p2d.reshape(y_p.shape), y_s2d.reshape(y_s.shape)
        st_p.append(state_p)
        st_s.append(state_s)
    outs_p = [jnp.stack(a) for a in zip(*st_p)]
    outs_s = [jnp.stack(a) for a in zip(*st_s)]
    return (y_p, y_s, *outs_p, *outs_s)
```
